```python
import math
import jax, jax.numpy as jnp
from jax import lax
import numpy as np

D_MODEL = 2048
BATCH = 2
SEQ = 4096
DEPTH = 4
DEC_BATCH = 8
DEC_SEQ = 1
PAST_LEN = 16384
PAGE_SIZE = 128

N_MIXERS = 4
EPS = 1e-6
NEG = -1e30
CONV_W = 4

S5_WIDTH = D_MODEL
S5_GROUP = 16
S5_GROUPS = S5_WIDTH // S5_GROUP
S5_STATE = 64
S5_CHUNK = 256

M2_WIDTH = 2 * D_MODEL
M2_HEADDIM = 64
M2_HEADS = M2_WIDTH // M2_HEADDIM
M2_STATE = 128
M2_GROUPS = 8
M2_CONV_DIM = M2_WIDTH + 2 * M2_GROUPS * M2_STATE
M2_CHUNK = 128

GD_DK = 128
GD_DV = 128
GD_QK_HEADS = D_MODEL // GD_DK
GD_V_HEADS = 2 * GD_QK_HEADS
GD_WIDTH = GD_V_HEADS * GD_DV
GD_CONV_DIM = 2 * GD_QK_HEADS * GD_DK + GD_WIDTH
GD_CHUNK = 64

AT_DIM = 128
AT_HEADS = D_MODEL // AT_DIM
AT_WIDTH = AT_HEADS * AT_DIM
IDX_HEADS = 16
IDX_DIM = 128
TOPK_MAX = 256
AT_QBLOCK = 64
REL_BUCKETS = 32
REL_MAX_DIST = 128

kernel_name = 'hybrid_s5_ssd_gdn_dsa_step'


def _rmsnorm(x, g):
    xf = x.astype(jnp.float32)
    return xf * lax.rsqrt(jnp.mean(xf * xf, axis=-1, keepdims=True) + EPS) * g.astype(jnp.float32)


def _l2norm(x):
    return x * lax.rsqrt(jnp.sum(x * x, axis=-1, keepdims=True) + EPS)


def _to_chunks(t, c):
    b, l = t.shape[:2]
    return t.reshape(b, l // c, c, *t.shape[2:]).swapaxes(0, 1)


def _from_chunks(t):
    n, b, c = t.shape[:3]
    return t.swapaxes(0, 1).reshape(b, n * c, *t.shape[3:])


def _causal_conv(x, buf, w, b=None):
    ch = x.shape[-1]
    xp = jnp.concatenate([buf.astype(x.dtype), x], axis=1)
    y = lax.conv_general_dilated(xp, w.astype(x.dtype)[:, None, :], window_strides=(1,), padding='VALID',
                                 dimension_numbers=('NWC', 'WIO', 'NWC'), feature_group_count=ch)
    if b is not None:
        y = y + b
    return y, xp[:, -(CONV_W - 1):]


def _modulate(x, c, g, w_mod, b_mod):
    mod = c.astype(jnp.float32) @ w_mod + b_mod
    shift, scale, gate = jnp.split(mod, 3, axis=-1)
    return _rmsnorm(x, g) * (1.0 + scale[:, None]) + shift[:, None], gate[:, None]


def _cmul(ar, ai, br, bi):
    return ar * br - ai * bi, ar * bi + ai * br


def _s5_combine(e1, e2):
    a1r, a1i, b1r, b1i = e1
    a2r, a2i, b2r, b2i = e2
    ar, ai = _cmul(a2r, a2i, a1r, a1i)
    br, bi = _cmul(a2r, a2i, b1r, b1i)
    return ar, ai, br + b2r, bi + b2i


def _mixer_s5(h, h_re, h_im, w_in, lam_re, lam_im, log_dt, b_re, b_im, c_re, c_im, d_skip, w_glu, b_glu, w_out):
    f = jnp.float32
    bn, l, _ = h.shape
    uz = h @ w_in
    u, z = uz[..., :S5_WIDTH], uz[..., S5_WIDTH:]
    lr, li = lam_re.astype(f), lam_im.astype(f)
    dt = jnp.exp(log_dt.astype(f))[:, None]
    ldr, ldi = lr * dt, li * dt
    ab_re, ab_im = jnp.exp(ldr) * jnp.cos(ldi), jnp.exp(ldr) * jnp.sin(ldi)
    den = lr * lr + li * li
    nr, ni = ab_re - 1.0, ab_im
    fr, fi = (nr * lr + ni * li) / den, (ni * lr - nr * li) / den
    bb_re, bb_im = _cmul(fr[..., None], fi[..., None], b_re.astype(f), b_im.astype(f))
    c = math.gcd(l, S5_CHUNK)
    kk = jnp.arange(1, c + 1, dtype=f)[:, None, None]
    pmag = jnp.exp(kk * ldr)
    pw_re, pw_im = pmag * jnp.cos(kk * ldi), pmag * jnp.sin(kk * ldi)
    cr, ci = c_re.astype(f), c_im.astype(f)
    dsk = d_skip.astype(f).reshape(S5_GROUPS, S5_GROUP)

    def body(carry, u_c):
        hr, hi = carry
        bu_re = jnp.einsum('bcgk,gpk->bcgp', u_c, bb_re)
        bu_im = jnp.einsum('bcgk,gpk->bcgp', u_c, bb_im)
        a_re = jnp.broadcast_to(ab_re, bu_re.shape)
        a_im = jnp.broadcast_to(ab_im, bu_im.shape)
        _, _, sr, si = lax.associative_scan(_s5_combine, (a_re, a_im, bu_re, bu_im), axis=1)
        pr, pi = _cmul(pw_re, pw_im, hr[:, None], hi[:, None])
        sr, si = sr + pr, si + pi
        y = jnp.einsum('bcgp,gkp->bcgk', sr, cr) - jnp.einsum('bcgp,gkp->bcgk', si, ci) + dsk * u_c
        return (sr[:, -1], si[:, -1]), y

    ug = _to_chunks(u.reshape(bn, l, S5_GROUPS, S5_GROUP), c)
    (n_re, n_im), ys = lax.scan(body, (h_re.astype(f), h_im.astype(f)), ug)
    y = jax.nn.gelu(_from_chunks(ys).reshape(bn, l, S5_WIDTH))
    y = y * jax.nn.sigmoid(y @ w_glu + b_glu)
    y = y * jax.nn.silu(z)
    return y @ w_out, n_re, n_im


def _mixer_mamba2(h, conv_buf, ssm, w_in, conv_w, conv_b, dt_bias, a_log, d_skip, norm_w, w_out):
    f = jnp.float32
    bn, l, _ = h.shape
    g_, r_, p_, n_ = M2_GROUPS, M2_HEADS // M2_GROUPS, M2_HEADDIM, M2_STATE
    proj = h @ w_in
    z = proj[..., :M2_WIDTH]
    xbc = proj[..., M2_WIDTH:M2_WIDTH + M2_CONV_DIM]
    dt_raw = proj[..., M2_WIDTH + M2_CONV_DIM:]
    xbc, new_buf = _causal_conv(xbc, conv_buf, conv_w, conv_b)
    xbc = jax.nn.silu(xbc)
    xh = xbc[..., :M2_WIDTH].reshape(bn, l, g_, r_, p_)
    bm = xbc[..., M2_WIDTH:M2_WIDTH + g_ * n_].reshape(bn, l, g_, n_)
    cm = xbc[..., M2_WIDTH + g_ * n_:].reshape(bn, l, g_, n_)
    dt = jax.nn.softplus(dt_raw + dt_bias).reshape(bn, l, g_, r_)
    la = -jnp.exp(a_log.astype(f)).reshape(g_, r_) * dt
    c = math.gcd(l, M2_CHUNK)
    tri = jnp.tri(c, dtype=bool)[None, :, :, None, None]

    def body(s, inp):
        xc, dtc, ac, bc, cc = inp
        cum = jnp.cumsum(ac, axis=1)
        seg = cum[:, :, None] - cum[:, None, :]
        dec = jnp.where(tri, jnp.exp(jnp.where(tri, seg, 0.0)), 0.0)
        xdt = xc * dtc[..., None]
        w = jnp.einsum('btgn,bsgn->btsg', cc, bc)[..., None] * dec
        y = jnp.einsum('btsgr,bsgrp->btgrp', w, xdt)
        y = y + jnp.einsum('btgn,bgrpn->btgrp', cc, s) * jnp.exp(cum)[..., None]
        wend = jnp.exp(cum[:, -1:] - cum)
        s_new = s * jnp.exp(cum[:, -1])[..., None, None] + jnp.einsum('bsgr,bsgrp,bsgn->bgrpn', wend, xdt, bc)
        return s_new, y

    s0 = ssm.astype(f).reshape(bn, g_, r_, p_, n_)
    s_fin, ys = lax.scan(body, s0, tuple(_to_chunks(t, c) for t in (xh, dt, la, bm, cm)))
    y = _from_chunks(ys) + d_skip.astype(f).reshape(g_, r_)[..., None] * xh
    y = _rmsnorm(y.reshape(bn, l, M2_WIDTH) * jax.nn.silu(z), norm_w)
    return y @ w_out, new_buf, s_fin.reshape(bn, M2_HEADS, p_, n_)


def _mixer_gdn(h, conv_buf, state, w_in, conv_w, a_log, dt_bias, norm_w, w_out):
    f = jnp.float32
    bn, l, _ = h.shape
    hk, hv = GD_QK_HEADS, GD_V_HEADS
    proj = h @ w_in
    qkv = proj[..., :GD_CONV_DIM]
    z = proj[..., GD_CONV_DIM:GD_CONV_DIM + GD_WIDTH].reshape(bn, l, hv, GD_DV)
    b_raw = proj[..., GD_CONV_DIM + GD_WIDTH:GD_CONV_DIM + GD_WIDTH + hv]
    a_raw = proj[..., GD_CONV_DIM + GD_WIDTH + hv:]
    qkv, new_buf = _causal_conv(qkv, conv_buf, conv_w)
    qkv = jax.nn.silu(qkv)
    q = qkv[..., :hk * GD_DK].reshape(bn, l, hk, GD_DK)
    k = qkv[..., hk * GD_DK:2 * hk * GD_DK].reshape(bn, l, hk, GD_DK)
    v = qkv[..., 2 * hk * GD_DK:].reshape(bn, l, hv, GD_DV)
    rep = hv // hk
    q = jnp.repeat(_l2norm(q), rep, axis=2) * GD_DK ** -0.5
    k = jnp.repeat(_l2norm(k), rep, axis=2)
    beta = jax.nn.sigmoid(b_raw)
    g = -jnp.exp(a_log.astype(f)) * jax.nn.softplus(a_raw + dt_bias)
    c = math.gcd(l, GD_CHUNK)
    incl = jnp.tri(c, dtype=bool)
    strict = jnp.tri(c, k=-1, dtype=bool)
    eye = jnp.eye(c, dtype=f)

    def body(s, inp):
        qc, kc, vc, gc, bc = [t.swapaxes(1, 2) for t in inp]
        gcum = jnp.cumsum(gc, axis=-1)
        seg = gcum[..., :, None] - gcum[..., None, :]
        dec = jnp.where(incl, jnp.exp(jnp.where(incl, seg, 0.0)), 0.0)
        kb = kc * bc[..., None]
        a = jnp.where(strict, jnp.einsum('bhtd,bhsd->bhts', kb, kc) * dec, 0.0)
        rhs = jnp.concatenate([vc * bc[..., None], kb * jnp.exp(gcum)[..., None]], axis=-1)
        sol = lax.linalg.triangular_solve(a + eye, rhs, left_side=True, lower=True, unit_diagonal=True)
        u, w = sol[..., :GD_DV], sol[..., GD_DV:]
        v_new = u - jnp.einsum('bhtd,bhde->bhte', w, s)
        o = jnp.einsum('bhtd,bhde->bhte', qc * jnp.exp(gcum)[..., None], s)
        o = o + jnp.einsum('bhts,bhse->bhte', jnp.einsum('bhtd,bhsd->bhts', qc, kc) * dec, v_new)
        glast = gcum[..., -1:]
        s_new = s * jnp.exp(glast)[..., None] + jnp.einsum('bhsd,bhse->bhde', kc * jnp.exp(glast - gcum)[..., None], v_new)
        return s_new, o.swapaxes(1, 2)

    s_fin, ys = lax.scan(body, state.astype(f), tuple(_to_chunks(t, c) for t in (q, k, v, g, beta)))
    o = _rmsnorm(_from_chunks(ys), norm_w) * jax.nn.silu(z)
    return o.reshape(bn, l, GD_WIDTH) @ w_out, new_buf, s_fin


def _t5_bucket(dist):
    exact = REL_BUCKETS // 2
    far = exact + (jnp.log(jnp.maximum(dist, 1).astype(jnp.float32) / exact)
                   / math.log(REL_MAX_DIST / exact) * (REL_BUCKETS - exact)).astype(jnp.int32)
    return jnp.where(dist < exact, dist, jnp.minimum(far, REL_BUCKETS - 1))


def _dsa_project(h, w_in):
    bn, l, _ = h.shape
    p = h @ w_in
    W = AT_WIDTH
    q = p[..., :W].reshape(bn, l, AT_HEADS, AT_DIM)
    k = p[..., W:2 * W].reshape(bn, l, AT_HEADS, AT_DIM)
    v = p[..., 2 * W:3 * W].reshape(bn, l, AT_HEADS, AT_DIM)
    z = p[..., 3 * W:4 * W]
    o = 4 * W
    qi = p[..., o:o + IDX_HEADS * IDX_DIM].reshape(bn, l, IDX_HEADS, IDX_DIM)
    ki = p[..., o + IDX_HEADS * IDX_DIM:o + IDX_HEADS * IDX_DIM + IDX_DIM]
    wi = p[..., o + IDX_HEADS * IDX_DIM + IDX_DIM:]
    return q, k, v, z, qi, ki, wi


def _index_scores(qi, wi, ki):
    rel = jax.nn.relu(jnp.einsum('bqhd,bsd->bqhs', qi, ki) * IDX_DIM ** -0.5)
    return jnp.einsum('bqh,bqhs->bqs', wi * IDX_HEADS ** -0.5, rel)


def _dsa_attend(q, qi, wi, qpos, ki_keys, gather_kv, k_sel, rel_bias):
    n_keys = ki_keys.shape[1]
    scores = _index_scores(qi, wi, ki_keys).astype(jnp.float32)
    adm = jnp.arange(n_keys)[None, :] <= qpos[:, None]
    scores = jnp.where(adm[None], scores, NEG)
    _, sel = lax.top_k(scores, k_sel)
    valid = sel <= qpos[None, :, None]
    kg, vg = gather_kv(sel)
    bias = rel_bias[_t5_bucket(jnp.maximum(qpos[None, :, None] - sel, 0))]
    logits = jnp.einsum('bqhd,bqkhd->bqhk', q, kg) * AT_DIM ** -0.5 + jnp.swapaxes(bias, 2, 3)
    logits = jnp.where(valid[:, :, None, :], logits.astype(jnp.float32), NEG)
    p = jax.nn.softmax(logits, axis=-1)
    return jnp.einsum('bqhk,bqkhd->bqhd', p, vg)


def _mixer_dsa_prompt(h, w_in, rel_bias, w_out):
    bn, l, _ = h.shape
    q, k, v, z, qi, ki, wi = _dsa_project(h, w_in)
    k_sel = min(TOPK_MAX, l // 4)
    qb = math.gcd(l, AT_QBLOCK)
    bidx = jnp.arange(bn)[:, None, None]

    def gather(sel):
        return k[bidx, sel], v[bidx, sel]

    def block(i):
        s0 = i * qb
        sl = lambda t: lax.dynamic_slice_in_dim(t, s0, qb, axis=1)
        qpos = s0 + jnp.arange(qb)
        return _dsa_attend(sl(q), sl(qi), sl(wi), qpos, ki, gather, k_sel, rel_bias)

    o = _from_chunks(lax.map(block, jnp.arange(l // qb))).reshape(bn, l, AT_WIDTH)
    return (o * jax.nn.silu(z)) @ w_out, k, v, ki


def _mixer_dsa_sample(h, cache_k, cache_v, cache_kidx, page_table, w_in, rel_bias, w_out):
    bn, l, _ = h.shape
    q, k, v, z, qi, ki, wi = _dsa_project(h, w_in)
    past = page_table.shape[1] * PAGE_SIZE
    k_sel = min(TOPK_MAX, (past + l) // 4)
    ki_past = cache_kidx[page_table].reshape(bn, past, IDX_DIM)
    ki_all = jnp.concatenate([ki_past.astype(ki.dtype), ki], axis=1)
    bidx = jnp.arange(bn)[:, None, None]

    def gather(sel):
        in_past = (sel < past)[..., None, None]
        pos = jnp.minimum(sel, past - 1)
        page = jnp.take_along_axis(page_table, (pos // PAGE_SIZE).reshape(bn, -1), axis=1).reshape(sel.shape)
        off = pos % PAGE_SIZE
        j = jnp.clip(sel - past, 0, l - 1)
        return (jnp.where(in_past, cache_k[page, off], k[bidx, j]),
                jnp.where(in_past, cache_v[page, off], v[bidx, j]))

    qpos = past + jnp.arange(l)
    o = _dsa_attend(q, qi, wi, qpos, ki_all, gather, k_sel, rel_bias).reshape(bn, l, AT_WIDTH)
    return (o * jax.nn.silu(z)) @ w_out, k, v, ki


def setup_inputs(seed: int = 0) -> dict:
    key = jax.random.key(seed)
    ks = iter(jax.random.split(key, 64))
    f = jnp.float32
    nrm = lambda shape, s=1.0: s * jax.random.normal(next(ks), shape, f)
    uni = lambda shape, lo, hi: jax.random.uniform(next(ks), shape, f, lo, hi)

    def dt_bias(n):
        dt = jnp.exp(uni((n,), math.log(1e-3), math.log(1e-1)))
        return dt + jnp.log(-jnp.expm1(-dt))

    n_pages = PAST_LEN // PAGE_SIZE
    n_pool = (5 * DEC_BATCH * n_pages + 3) // 4
    D = D_MODEL
    inp = {}
    inp['x_prompt'] = nrm((BATCH, SEQ, D))
    inp['x_sample'] = nrm((DEC_BATCH, DEC_SEQ, D))
    inp['state_s5_re'] = nrm((DEC_BATCH, S5_GROUPS, S5_STATE), 0.1)
    inp['state_s5_im'] = nrm((DEC_BATCH, S5_GROUPS, S5_STATE), 0.1)
    inp['state_m2_conv'] = nrm((DEC_BATCH, CONV_W - 1, M2_CONV_DIM))
    inp['state_m2_ssm'] = nrm((DEC_BATCH, M2_HEADS, M2_HEADDIM, M2_STATE), 0.1)
    inp['state_gd_conv'] = nrm((DEC_BATCH, CONV_W - 1, GD_CONV_DIM))
    inp['state_gd_ssm'] = nrm((DEC_BATCH, GD_V_HEADS, GD_DK, GD_DV), 0.1)
    inp['cache_k'] = nrm((n_pool, PAGE_SIZE, AT_HEADS, AT_DIM))
    inp['cache_v'] = nrm((n_pool, PAGE_SIZE, AT_HEADS, AT_DIM))
    inp['cache_kidx'] = nrm((n_pool, PAGE_SIZE, IDX_DIM))
    inp['page_table'] = jax.random.permutation(next(ks), n_pool)[:DEC_BATCH * n_pages].reshape(DEC_BATCH, n_pages).astype(jnp.int32)
    inp['c_prompt'] = nrm((BATCH, D))
    inp['c_sample'] = nrm((DEC_BATCH, D))
    inp['norm_g'] = 1.0 + nrm((DEPTH, D), 0.1)
    inp['w_mod'] = nrm((DEPTH, D, 3 * D), 0.5 * D ** -0.5)
    inp['b_mod'] = nrm((DEPTH, 3 * D), 0.02)
    inp['final_g'] = 1.0 + nrm((D,), 0.1)
    inp['s5_w_in'] = nrm((D, 2 * S5_WIDTH), D ** -0.5)
    inp['s5_lam_re'] = -0.5 + nrm((S5_GROUPS, S5_STATE), 0.01)
    inp['s5_lam_im'] = math.pi * jnp.arange(S5_STATE, dtype=f)[None, :] + nrm((S5_GROUPS, S5_STATE), 0.01)
    inp['s5_log_dt'] = uni((S5_GROUPS,), math.log(1e-3), math.log(1e-1))
    inp['s5_b_re'] = nrm((S5_GROUPS, S5_STATE, S5_GROUP), (2 * S5_GROUP) ** -0.5)
    inp['s5_b_im'] = nrm((S5_GROUPS, S5_STATE, S5_GROUP), (2 * S5_GROUP) ** -0.5)
    inp['s5_c_re'] = nrm((S5_GROUPS, S5_GROUP, S5_STATE), S5_STATE ** -0.5)
    inp['s5_c_im'] = nrm((S5_GROUPS, S5_GROUP, S5_STATE), S5_STATE ** -0.5)
    inp['s5_d'] = 1.0 + nrm((S5_WIDTH,), 0.1)
    inp['s5_w_glu'] = nrm((S5_WIDTH, S5_WIDTH), S5_WIDTH ** -0.5)
    inp['s5_b_glu'] = nrm((S5_WIDTH,), 0.02)
    inp['s5_w_out'] = nrm((S5_WIDTH, D), S5_WIDTH ** -0.5)
    inp['m2_w_in'] = nrm((D, M2_WIDTH + M2_CONV_DIM + M2_HEADS), D ** -0.5)
    inp['m2_conv_w'] = nrm((CONV_W, M2_CONV_DIM), CONV_W ** -0.5)
    inp['m2_conv_b'] = nrm((M2_CONV_DIM,), 0.02)
    inp['m2_dt_bias'] = dt_bias(M2_HEADS)
    inp['m2_a_log'] = jnp.log(uni((M2_HEADS,), 1.0, 16.0))
    inp['m2_d'] = 1.0 + nrm((M2_HEADS,), 0.1)
    inp['m2_norm'] = 1.0 + nrm((M2_WIDTH,), 0.1)
    inp['m2_w_out'] = nrm((M2_WIDTH, D), M2_WIDTH ** -0.5)
    inp['gd_w_in'] = nrm((D, GD_CONV_DIM + GD_WIDTH + 2 * GD_V_HEADS), D ** -0.5)
    inp['gd_conv_w'] = nrm((CONV_W, GD_CONV_DIM), CONV_W ** -0.5)
    inp['gd_a_log'] = jnp.log(uni((GD_V_HEADS,), 1.0, 16.0))
    inp['gd_dt_bias'] = dt_bias(GD_V_HEADS)
    inp['gd_norm'] = 1.0 + nrm((GD_DV,), 0.1)
    inp['gd_w_out'] = nrm((GD_WIDTH, D), GD_WIDTH ** -0.5)
    inp['at_w_in'] = nrm((D, 4 * AT_WIDTH + IDX_HEADS * IDX_DIM + IDX_DIM + IDX_HEADS), D ** -0.5)
    inp['rel_bias'] = nrm((REL_BUCKETS, AT_HEADS), 0.5)
    inp['at_w_out'] = nrm((AT_WIDTH, D), AT_WIDTH ** -0.5)
    return inp


def reference(x_prompt, x_sample, state_s5_re, state_s5_im, state_m2_conv, state_m2_ssm, state_gd_conv, state_gd_ssm,
              cache_k, cache_v, cache_kidx, page_table, c_prompt, c_sample, norm_g, w_mod, b_mod, final_g,
              s5_w_in, s5_lam_re, s5_lam_im, s5_log_dt, s5_b_re, s5_b_im, s5_c_re, s5_c_im, s5_d, s5_w_glu, s5_b_glu, s5_w_out,
              m2_w_in, m2_conv_w, m2_conv_b, m2_dt_bias, m2_a_log, m2_d, m2_norm, m2_w_out,
              gd_w_in, gd_conv_w, gd_a_log, gd_dt_bias, gd_norm, gd_w_out,
              at_w_in, rel_bias, at_w_out):
    f = jnp.float32
    xp = x_prompt.astype(f)
    xs = x_sample.astype(f)
    bp = xp.shape[0]
    s5w = (s5_w_in, s5_lam_re, s5_lam_im, s5_log_dt, s5_b_re, s5_b_im, s5_c_re, s5_c_im, s5_d, s5_w_glu, s5_b_glu, s5_w_out)
    m2w = (m2_w_in, m2_conv_w, m2_conv_b, m2_dt_bias, m2_a_log, m2_d, m2_norm, m2_w_out)
    gdw = (gd_w_in, gd_conv_w, gd_a_log, gd_dt_bias, gd_norm, gd_w_out)
    for i in range(DEPTH):
        kind = i % N_MIXERS
        hp, gp = _modulate(xp, c_prompt, norm_g[i], w_mod[i], b_mod[i])
        hs, gs = _modulate(xs, c_sample, norm_g[i], w_mod[i], b_mod[i])
        if kind == 0:
            z0 = jnp.zeros((bp, S5_GROUPS, S5_STATE), f)
            op, s5_re_p, s5_im_p = _mixer_s5(hp, z0, z0, *s5w)
            osm, s5_re_s, s5_im_s = _mixer_s5(hs, state_s5_re, state_s5_im, *s5w)
        elif kind == 1:
            op, m2_conv_p, m2_ssm_p = _mixer_mamba2(hp, jnp.zeros((bp, CONV_W - 1, M2_CONV_DIM), f),
                                                    jnp.zeros((bp, M2_HEADS, M2_HEADDIM, M2_STATE), f), *m2w)
            osm, m2_conv_s, m2_ssm_s = _mixer_mamba2(hs, state_m2_conv, state_m2_ssm, *m2w)
        elif kind == 2:
            op, gd_conv_p, gd_ssm_p = _mixer_gdn(hp, jnp.zeros((bp, CONV_W - 1, GD_CONV_DIM), f),
                                                 jnp.zeros((bp, GD_V_HEADS, GD_DK, GD_DV), f), *gdw)
            osm, gd_conv_s, gd_ssm_s = _mixer_gdn(hs, state_gd_conv, state_gd_ssm, *gdw)
        else:
            op, k_rows_p, v_rows_p, kidx_rows_p = _mixer_dsa_prompt(hp, at_w_in, rel_bias, at_w_out)
            osm, k_rows_s, v_rows_s, kidx_rows_s = _mixer_dsa_sample(hs, cache_k, cache_v, cache_kidx, page_table,
                                                                     at_w_in, rel_bias, at_w_out)
        xp = xp + gp * op
        xs = xs + gs * osm
    y_prompt = _rmsnorm(xp, final_g).astype(x_prompt.dtype)
    y_sample = _rmsnorm(xs, final_g).astype(x_sample.dtype)
    return (y_prompt, y_sample,
            s5_re_p, s5_im_p, s5_re_s, s5_im_s,
            m2_conv_p, m2_ssm_p, m2_conv_s, m2_ssm_s,
            gd_conv_p, gd_ssm_p, gd_conv_s, gd_ssm_s,
            k_rows_p, v_rows_p, kidx_rows_p, k_rows_s, v_rows_s, kidx_rows_s)
```

```python
import functools
import math

import numpy as np
import jax
import jax.numpy as jnp
from jax import lax
from jax.experimental import pallas as pl
from jax.experimental.pallas import tpu as pltpu

F32 = jnp.float32
BF16 = jnp.bfloat16

EPS = 1e-6
NEG = -1e30
CONV_W = 4
V7X_VMEM_LIMIT_BYTES = 56 * 1024 * 1024
LANES = 128
SUBLANES = 8

S5_GROUP = 16
S5_STATE = 64
S5_CHUNK = 256
S5_SEG = S5_CHUNK // SUBLANES
S5_BLK_CH = 256
S5_BLK_ST = 1024


def _params(sem):
    return pltpu.CompilerParams(dimension_semantics=sem, vmem_limit_bytes=V7X_VMEM_LIMIT_BYTES)


def _sigmoid(x):
    return 1.0 / (1.0 + jnp.exp(-x))


def _silu(x):
    return x * _sigmoid(x)


def _gelu(x):
    return 0.5 * x * (1.0 + jnp.tanh(math.sqrt(2.0 / math.pi) * (x + 0.044715 * (x * x * x))))


def _softplus(x):
    return jnp.maximum(x, 0.0) + jnp.log1p(jnp.exp(-jnp.abs(x)))


def _dot(a, b):
    return jnp.dot(a, b, preferred_element_type=F32)


def _dot_nt(a, b):
    return lax.dot_general(a, b, (((1,), (1,)), ((), ())), preferred_element_type=F32)


def _split3(x):
    hi = x.astype(BF16)
    r1 = x - hi.astype(F32)
    mid = r1.astype(BF16)
    lo = (r1 - mid.astype(F32)).astype(BF16)
    return hi, mid, lo


def _dot_exact_lhs(sel, x):
    hi, mid, lo = _split3(x)
    return _dot(sel, hi) + (_dot(sel, mid) + _dot(sel, lo))


def _dot_f32(a, b):
    ah, am, al = _split3(a)
    bh, bm, bl = _split3(b)
    small = _dot(am, bm) + _dot(ah, bl) + _dot(al, bh)
    return _dot(ah, bh) + (_dot(ah, bm) + _dot(am, bh) + small)


def _mm_kernel(*refs, n_a, n_e, prologue, epilogue):
    a_refs = refs[:n_a]
    w_ref = refs[n_a]
    e_refs = refs[n_a + 1:n_a + 1 + n_e]
    o_ref = refs[n_a + 1 + n_e]
    if prologue is None:
        a = a_refs[0][...]
    else:
        a_scr = refs[n_a + 2 + n_e]

        @pl.when(pl.program_id(1) == 0)
        def _():
            a_scr[...] = prologue(*[r[...] for r in a_refs]).astype(BF16)

        a = a_scr[...]
    acc = _dot(a, w_ref[...])
    o_ref[...] = epilogue(acc, *[r[...] for r in e_refs]).astype(o_ref.dtype)


def _fused_matmul(a_ins, w, e_ins, *, prologue, epilogue, out_dtype, tm, tn, rows_per_batch=None, name):
    m = next(item[1].shape[0] for item in a_ins if item[0] == 'row')
    k, n = w.shape
    tm = min(tm, m)
    assert m % tm == 0 and n % tn == 0
    rpb = rows_per_batch

    def bidx(i):
        return (i * tm) // rpb

    in_specs, args = [], []
    for item in a_ins:
        kind, arr = item[0], item[1]
        wd = item[2] if len(item) > 2 else arr.shape[-1]
        coff = item[3] if len(item) > 3 else 0
        if kind == 'row':
            in_specs.append(pl.BlockSpec((tm, wd), lambda i, j, coff=coff: (i, coff)))
        elif kind == 'vec':
            in_specs.append(pl.BlockSpec((1, wd), lambda i, j: (0, 0)))
        else:
            in_specs.append(pl.BlockSpec((None, 1, wd), lambda i, j: (bidx(i), 0, 0)))
        args.append(arr)
    in_specs.append(pl.BlockSpec((k, tn), lambda i, j: (0, j)))
    args.append(w)
    for item in e_ins:
        kind, arr = item[0], item[1]
        off = item[2] if len(item) > 2 else 0
        if kind == 'tile':
            in_specs.append(pl.BlockSpec((tm, tn), lambda i, j, off=off: (i, j + off)))
        elif kind == 'col':
            in_specs.append(pl.BlockSpec((1, tn), lambda i, j: (0, j)))
        else:
            in_specs.append(pl.BlockSpec((None, 1, tn), lambda i, j: (bidx(i), 0, j)))
        args.append(arr)
    scratch = [] if prologue is None else [pltpu.VMEM((tm, k), BF16)]
    kern = functools.partial(_mm_kernel, n_a=len(a_ins), n_e=len(e_ins), prologue=prologue, epilogue=epilogue)
    return pl.pallas_call(
        kern, out_shape=jax.ShapeDtypeStruct((m, n), out_dtype), grid=(m // tm, n // tn),
        in_specs=in_specs, out_specs=pl.BlockSpec((tm, tn), lambda i, j: (i, j)),
        scratch_shapes=scratch, compiler_params=_params(("parallel", "arbitrary")), name=name)(*args)


def _pad_cols(w, mult):
    n = w.shape[-1]
    npad = (-n) % mult
    if npad:
        w = jnp.pad(w, ((0, 0), (0, npad)))
    return w


def _modnorm_prologue(x, g, scale, shift):
    r = x * lax.rsqrt(jnp.mean(x * x, axis=-1, keepdims=True) + EPS) * g
    return r * (1.0 + scale) + shift


def _identity_epilogue(acc):
    return acc


def _residual_epilogue(acc, x, gate):
    return x + gate * acc


def _in_proj(x, g, scale, shift, w, *, batch_kind, rows_per_batch, tm, tn, name):
    kind = 'batch' if batch_kind else 'row'
    return _fused_matmul([('row', x), ('vec', g), (kind, scale), (kind, shift)], w, [],
                         prologue=_modnorm_prologue, epilogue=_identity_epilogue, out_dtype=F32,
                         tm=tm, tn=tn, rows_per_batch=rows_per_batch, name=name)


def _gate_proj_kernel(x_ref, g_ref, scale_ref, shift_ref, w_ref, o_ref):
    h = _modnorm_prologue(x_ref[...], g_ref[...], scale_ref[...], shift_ref[...])
    o_ref[...] = _dot_f32(h, w_ref[...])


def _gate_proj(x, g, scale, shift, w, *, batch_kind, rows_per_batch, tm, name):
    m, d = x.shape
    n = w.shape[1]
    tm = min(tm, m)
    if batch_kind:
        mod_spec = pl.BlockSpec((None, 1, d), lambda i: ((i * tm) // rows_per_batch, 0, 0))
    else:
        mod_spec = pl.BlockSpec((tm, d), lambda i: (i, 0))
    return pl.pallas_call(
        _gate_proj_kernel, out_shape=jax.ShapeDtypeStruct((m, n), F32), grid=(m // tm,),
        in_specs=[pl.BlockSpec((tm, d), lambda i: (i, 0)), pl.BlockSpec((1, d), lambda i: (0, 0)),
                  mod_spec, mod_spec, pl.BlockSpec((d, n), lambda i: (0, 0))],
        out_specs=pl.BlockSpec((tm, n), lambda i: (i, 0)),
        compiler_params=_params(("parallel",)), name=name)(x, g, scale, shift, w)


def _out_proj(a_ins, w, x, gate, *, batch_kind, rows_per_batch, tm, tn, name, prologue=None):
    kind = 'batchcol' if batch_kind else 'tile'
    return _fused_matmul(a_ins, w, [('tile', x), (kind, gate)], prologue=prologue, epilogue=_residual_epilogue,
                         out_dtype=F32, tm=tm, tn=tn, rows_per_batch=rows_per_batch, name=name)


def _mod_kernel(c_ref, w_ref, b_ref, o_ref):
    o_ref[...] = _dot(c_ref[...].astype(BF16), w_ref[...].astype(BF16)) + b_ref[...]


def _modulation(c_all, w_mod, b_mod, tn=512):
    depth, d, n = w_mod.shape
    rows = c_all.shape[0]
    return pl.pallas_call(
        _mod_kernel, out_shape=jax.ShapeDtypeStruct((depth, rows, n), F32), grid=(depth, n // tn),
        in_specs=[pl.BlockSpec((rows, d), lambda l, j: (0, 0)),
                  pl.BlockSpec((None, d, tn), lambda l, j: (l, 0, j)),
                  pl.BlockSpec((None, 1, tn), lambda l, j: (l, 0, j))],
        out_specs=pl.BlockSpec((None, rows, tn), lambda l, j: (l, 0, j)),
        compiler_params=_params(("parallel", "parallel")), name="adaln_modulation")(
            c_all, w_mod, b_mod.reshape(depth, 1, n))


def _rmsnorm_kernel(x_ref, g_ref, o_ref):
    x = x_ref[...]
    o_ref[...] = x * lax.rsqrt(jnp.mean(x * x, axis=-1, keepdims=True) + EPS) * g_ref[...]


def _final_norm(x, g, tm=512):
    m, d = x.shape
    tm = min(tm, m)
    return pl.pallas_call(
        _rmsnorm_kernel, out_shape=jax.ShapeDtypeStruct((m, d), F32), grid=(m // tm,),
        in_specs=[pl.BlockSpec((tm, d), lambda i: (i, 0)), pl.BlockSpec((1, d), lambda i: (0, 0))],
        out_specs=pl.BlockSpec((tm, d), lambda i: (i, 0)),
        compiler_params=_params(("parallel",)), name="final_rmsnorm")(x, g.reshape(1, d))


def _s5_tables(lam_re, lam_im, log_dt, b_re, b_im, c_re, c_im, d_skip):
    f = F32
    groups, p = lam_re.shape
    nblk = groups * S5_GROUP // S5_BLK_CH
    gpb = groups // nblk
    lr, li = lam_re.astype(f), lam_im.astype(f)
    dt = jnp.exp(log_dt.astype(f))[:, None]
    ldr, ldi = lr * dt, li * dt
    kk = jnp.arange(1, S5_SEG + 1, dtype=f)[:, None, None]
    pmag = jnp.exp(kk * ldr)
    pw_re, pw_im = pmag * jnp.cos(kk * ldi), pmag * jnp.sin(kk * ldi)
    ab_re, ab_im = jnp.exp(ldr) * jnp.cos(ldi), jnp.exp(ldr) * jnp.sin(ldi)
    den = lr * lr + li * li
    nr, ni = ab_re - 1.0, ab_im
    fr, fi = (nr * lr + ni * li) / den, (ni * lr - nr * li) / den
    bre, bim = b_re.astype(f), b_im.astype(f)
    bb_re = fr[..., None] * bre - fi[..., None] * bim
    bb_im = fr[..., None] * bim + fi[..., None] * bre
    eye = jnp.eye(gpb, dtype=f)

    def bd_in(bb):
        t = bb.reshape(nblk, gpb, p, S5_GROUP).transpose(0, 1, 3, 2)
        return jnp.einsum('bgkp,gh->bgkhp', t, eye).reshape(nblk, gpb * S5_GROUP, gpb * p).astype(BF16)

    def bd_out(c):
        t = c.astype(f).reshape(nblk, gpb, S5_GROUP, p).transpose(0, 1, 3, 2)
        return jnp.einsum('bgpk,gh->bgphk', t, eye).reshape(nblk, gpb * p, gpb * S5_GROUP).astype(BF16)

    def lanes(t):
        lead = t.shape[:-2]
        t = t.reshape(lead + (nblk, gpb * p))
        return jnp.moveaxis(t, -2, 0)

    return dict(
        bb_re=bd_in(bb_re), bb_im=bd_in(bb_im), c_re=bd_out(c_re), c_im=bd_out(c_im),
        ab_re=lanes(ab_re[None]), ab_im=lanes(ab_im[None]),
        pw_re=lanes(pw_re), pw_im=lanes(pw_im),
        d=d_skip.astype(f).reshape(1, -1), nblk=nblk)


def _s5_perm():
    pm = np.zeros((S5_CHUNK, S5_CHUNK), np.float32)
    r = np.arange(S5_CHUNK)
    pm[r, (r % SUBLANES) * S5_SEG + r // SUBLANES] = 1.0
    return jnp.asarray(pm, BF16), jnp.asarray(pm.T, BF16)


def _s5_scan_kernel(u_ref, pm_ref, pmt_ref, bbre_ref, bbim_ref, cre_ref, cim_ref, abre_ref, abim_ref,
                    pwre_ref, pwim_ref, d_ref, y_ref, sre_out, sim_out,
                    xre, xim, car_re, car_im, cin_re, cin_im, lend_re, lend_im):
    n = pl.program_id(2)
    nst = xre.shape[1]

    @pl.when(n == 0)
    def _():
        car_re[...] = jnp.zeros_like(car_re)
        car_im[...] = jnp.zeros_like(car_im)

    u = u_ref[...]
    up = _dot(pm_ref[...], u.astype(BF16)).astype(BF16)
    xre[...] = _dot(up, bbre_ref[...])
    xim[...] = _dot(up, bbim_ref[...])
    are = jnp.broadcast_to(abre_ref[...], (SUBLANES, nst))
    aim = jnp.broadcast_to(abim_ref[...], (SUBLANES, nst))
    sre = jnp.zeros((SUBLANES, nst), F32)
    sim = jnp.zeros((SUBLANES, nst), F32)
    for i in range(S5_SEG):
        r = slice(SUBLANES * i, SUBLANES * (i + 1))
        nre = are * sre - aim * sim + xre[r, :]
        nim = are * sim + aim * sre + xim[r, :]
        xre[r, :] = nre
        xim[r, :] = nim
        sre, sim = nre, nim
    lend_re[...] = sre
    lend_im[...] = sim
    a_re = pwre_ref[S5_SEG - 1:S5_SEG, :]
    a_im = pwim_ref[S5_SEG - 1:S5_SEG, :]
    cr, ci = car_re[...], car_im[...]
    for s in range(SUBLANES):
        cin_re[s:s + 1, :] = cr
        cin_im[s:s + 1, :] = ci
        lr, li = lend_re[s:s + 1, :], lend_im[s:s + 1, :]
        cr, ci = a_re * cr - a_im * ci + lr, a_re * ci + a_im * cr + li
    car_re[...] = cr
    car_im[...] = ci
    cinr, cini = cin_re[...], cin_im[...]
    for i in range(S5_SEG):
        r = slice(SUBLANES * i, SUBLANES * (i + 1))
        pr, pi_ = pwre_ref[i:i + 1, :], pwim_ref[i:i + 1, :]
        xre[r, :] = xre[r, :] + (pr * cinr - pi_ * cini)
        xim[r, :] = xim[r, :] + (pr * cini + pi_ * cinr)
    yp = _dot(xre[...].astype(BF16), cre_ref[...]) - _dot(xim[...].astype(BF16), cim_ref[...])
    hi = yp.astype(BF16)
    lo = (yp - hi.astype(F32)).astype(BF16)
    y = _dot(pmt_ref[...], hi) + _dot(pmt_ref[...], lo) + d_ref[...] * u
    y_ref[...] = _gelu(y)

    @pl.when(n == pl.num_programs(2) - 1)
    def _():
        sre_out[...] = cr
        sim_out[...] = ci


def _s5_scan(proj, tabs, batch, seq):
    nblk = tabs['nblk']
    nch = seq // S5_CHUNK
    pm, pmt = _s5_perm()
    nstate = nblk * S5_BLK_ST
    const3 = lambda shape: pl.BlockSpec((None,) + shape, lambda k, b, n: (k, 0, 0))
    y, sre, sim = pl.pallas_call(
        _s5_scan_kernel,
        out_shape=(jax.ShapeDtypeStruct((batch * seq, nblk * S5_BLK_CH), F32),
                   jax.ShapeDtypeStruct((batch, 1, nstate), F32),
                   jax.ShapeDtypeStruct((batch, 1, nstate), F32)),
        grid=(nblk, batch, nch),
        in_specs=[pl.BlockSpec((S5_CHUNK, S5_BLK_CH), lambda k, b, n: (b * nch + n, k)),
                  pl.BlockSpec((S5_CHUNK, S5_CHUNK), lambda k, b, n: (0, 0)),
                  pl.BlockSpec((S5_CHUNK, S5_CHUNK), lambda k, b, n: (0, 0)),
                  const3((S5_BLK_CH, S5_BLK_ST)), const3((S5_BLK_CH, S5_BLK_ST)),
                  const3((S5_BLK_ST, S5_BLK_CH)), const3((S5_BLK_ST, S5_BLK_CH)),
                  const3((1, S5_BLK_ST)), const3((1, S5_BLK_ST)),
                  const3((S5_SEG, S5_BLK_ST)), const3((S5_SEG, S5_BLK_ST)),
                  pl.BlockSpec((1, S5_BLK_CH), lambda k, b, n: (0, k))],
        out_specs=(pl.BlockSpec((S5_CHUNK, S5_BLK_CH), lambda k, b, n: (b * nch + n, k)),
                   pl.BlockSpec((None, 1, S5_BLK_ST), lambda k, b, n: (b, 0, k)),
                   pl.BlockSpec((None, 1, S5_BLK_ST), lambda k, b, n: (b, 0, k))),
        scratch_shapes=[pltpu.VMEM((S5_CHUNK, S5_BLK_ST), F32), pltpu.VMEM((S5_CHUNK, S5_BLK_ST), F32),
                        pltpu.VMEM((1, S5_BLK_ST), F32), pltpu.VMEM((1, S5_BLK_ST), F32),
                        pltpu.VMEM((SUBLANES, S5_BLK_ST), F32), pltpu.VMEM((SUBLANES, S5_BLK_ST), F32),
                        pltpu.VMEM((SUBLANES, S5_BLK_ST), F32), pltpu.VMEM((SUBLANES, S5_BLK_ST), F32)],
        compiler_params=_params(("parallel", "parallel", "arbitrary")), name="s5_scan")(
            proj, pm, pmt, tabs['bb_re'], tabs['bb_im'], tabs['c_re'], tabs['c_im'],
            tabs['ab_re'], tabs['ab_im'], tabs['pw_re'], tabs['pw_im'], tabs['d'])
    return y, sre, sim


def _s5_step_kernel(u_ref, hre_ref, him_ref, bbre_ref, bbim_ref, cre_ref, cim_ref, abre_ref, abim_ref, d_ref,
                    y_ref, sre_out, sim_out):
    u = u_ref[...]
    ub = u.astype(BF16)
    are, aim = abre_ref[...], abim_ref[...]
    hre, him = hre_ref[...], him_ref[...]
    sre = are * hre - aim * him + _dot(ub, bbre_ref[...])
    sim = are * him + aim * hre + _dot(ub, bbim_ref[...])
    sre_out[...] = sre
    sim_out[...] = sim
    y = _dot(sre.astype(BF16), cre_ref[...]) - _dot(sim.astype(BF16), cim_ref[...]) + d_ref[...] * u
    y_ref[...] = _gelu(y)


def _s5_step(proj, h_re, h_im, tabs):
    nblk = tabs['nblk']
    rows = proj.shape[0]
    nstate = nblk * S5_BLK_ST
    const3 = lambda shape: pl.BlockSpec((None,) + shape, lambda k: (k, 0, 0))
    lane_blk = lambda w: pl.BlockSpec((rows, w), lambda k: (0, k))
    return pl.pallas_call(
        _s5_step_kernel,
        out_shape=(jax.ShapeDtypeStruct((rows, nblk * S5_BLK_CH), F32),
                   jax.ShapeDtypeStruct((rows, nstate), F32), jax.ShapeDtypeStruct((rows, nstate), F32)),
        grid=(nblk,),
        in_specs=[lane_blk(S5_BLK_CH), lane_blk(S5_BLK_ST), lane_blk(S5_BLK_ST),
                  const3((S5_BLK_CH, S5_BLK_ST)), const3((S5_BLK_CH, S5_BLK_ST)),
                  const3((S5_BLK_ST, S5_BLK_CH)), const3((S5_BLK_ST, S5_BLK_CH)),
                  const3((1, S5_BLK_ST)), const3((1, S5_BLK_ST)),
                  pl.BlockSpec((1, S5_BLK_CH), lambda k: (0, k))],
        out_specs=(lane_blk(S5_BLK_CH), lane_blk(S5_BLK_ST), lane_blk(S5_BLK_ST)),
        compiler_params=_params(("parallel",)), name="s5_step")(
            proj, h_re.reshape(rows, nstate), h_im.reshape(rows, nstate),
            tabs['bb_re'], tabs['bb_im'], tabs['c_re'], tabs['c_im'], tabs['ab_re'], tabs['ab_im'], tabs['d'])


def _s5_weights(w_in, lam_re, lam_im, log_dt, b_re, b_im, c_re, c_im, d_skip, w_glu, b_glu, w_out):
    tabs = _s5_tables(lam_re, lam_im, log_dt, b_re, b_im, c_re, c_im, d_skip)
    return (w_in.astype(BF16), w_glu.astype(BF16), b_glu.astype(F32).reshape(1, -1), w_out.astype(BF16), tabs)


def _glu_epilogue(acc, gy, z, b):
    return gy * _sigmoid(acc + b) * _silu(z)


def _cast_prologue(a):
    return a


def _s5_layer(x, mods, h_state, w, *, prompt, batch, seq, tm):
    g, scale, shift, gate = mods
    w_in, w_glu, b_glu, w_out, tabs = w
    width = w_glu.shape[0]
    proj = _in_proj(x, g, scale, shift, w_in, batch_kind=prompt, rows_per_batch=seq, tm=tm, tn=512, name="s5_in_proj")
    if prompt:
        gy, sre, sim = _s5_scan(proj, tabs, batch, seq)
    else:
        gy, sre, sim = _s5_step(proj, h_state[0], h_state[1], tabs)
    tn = 512
    a = _fused_matmul([('row', gy)], w_glu, [('tile', gy), ('tile', proj, width // tn), ('col', b_glu)],
                      prologue=_cast_prologue, epilogue=_glu_epilogue, out_dtype=BF16, tm=tm, tn=tn, name="s5_glu")
    x_new = _out_proj([('row', a)], w_out, x, gate, batch_kind=prompt, rows_per_batch=seq, tm=tm, tn=512, name="s5_out_proj")
    return x_new, sre, sim


def _conv_silu_chunk(x_ref, w_ref, b_ref, pad_ref, first):
    c = x_ref.shape[0]

    @pl.when(first)
    def _():
        pad_ref[0:SUBLANES, :] = jnp.zeros((SUBLANES, pad_ref.shape[1]), F32)

    pad_ref[SUBLANES:SUBLANES + c, :] = x_ref[...]
    acc = w_ref[3:4, :] * pad_ref[SUBLANES:SUBLANES + c, :]
    for j in range(CONV_W - 1):
        off = SUBLANES - (CONV_W - 1) + j
        acc = acc + w_ref[j:j + 1, :] * pad_ref[off:off + c, :]
    if b_ref is not None:
        acc = acc + b_ref[...]
    pad_ref[0:SUBLANES, :] = pad_ref[c:c + SUBLANES, :]
    return _silu(acc)


def _conv_silu_step(x, taps_ref, w_ref, b_ref):
    acc = w_ref[3:4, :] * x
    for j in range(CONV_W - 1):
        acc = acc + w_ref[j:j + 1, :] * taps_ref[j]
    if b_ref is not None:
        acc = acc + b_ref[...]
    return _silu(acc)


def _tri_masks(c):
    t = lax.broadcasted_iota(jnp.int32, (c, c), 0)
    s = lax.broadcasted_iota(jnp.int32, (c, c), 1)
    return s <= t, s < t


def _pad_to_square_t(x, n):
    rows = x.shape[0]
    return jnp.concatenate([x, jnp.zeros((n - rows, n), x.dtype)], axis=0).T


M2_HEADDIM = 64
M2_STATE = 128
M2_GROUPS = 8
M2_CHUNK = 128


def _m2_ssd_kernel(x_ref, b_ref, c_ref, dt_ref, wx_ref, wb_ref, wc_ref, bx_ref, bb_ref, bc_ref,
                   dtb_ref, nega_ref, dsk_ref, y_ref, sout_ref, xpad, bpad, cpad, s_ref):
    n = pl.program_id(1)
    first = n == 0
    c = x_ref.shape[0]
    npairs = s_ref.shape[0]
    pairs_per_group = npairs // M2_GROUPS

    @pl.when(first)
    def _():
        s_ref[...] = jnp.zeros_like(s_ref)

    xs = _conv_silu_chunk(x_ref, wx_ref, bx_ref, xpad, first)
    bm = _conv_silu_chunk(b_ref, wb_ref, bb_ref, bpad, first).astype(BF16)
    cm = _conv_silu_chunk(c_ref, wc_ref, bc_ref, cpad, first).astype(BF16)
    dtv = _softplus(dt_ref[...] + dtb_ref[...])
    la = nega_ref[...] * dtv
    incl, _ = _tri_masks(c)
    tri = jnp.where(incl, 1.0, 0.0).astype(BF16)
    cum = _dot_exact_lhs(tri, la)
    cum_t = cum.T
    ecum_all = jnp.exp(cum)
    wend_all = jnp.exp(cum[c - 1:c, :] - cum)
    elast_t = jnp.exp(cum_t[:, c - 1:c])
    lane_first = lax.broadcasted_iota(jnp.int32, (c, LANES), 1) < M2_HEADDIM
    row_first = lax.broadcasted_iota(jnp.int32, (LANES, LANES), 0) < M2_HEADDIM

    for g in range(M2_GROUPS):
        bg = bm[:, g * M2_STATE:(g + 1) * M2_STATE]
        cg = cm[:, g * M2_STATE:(g + 1) * M2_STATE]
        gm = _dot_nt(cg, bg)
        for j in range(pairs_per_group):
            p = g * pairs_per_group + j
            ha, hb = 2 * p, 2 * p + 1
            xp = xs[:, p * LANES:(p + 1) * LANES]

            def decay_weights(h):
                seg = cum[:, h:h + 1] - cum_t[h:h + 1, :]
                dec = jnp.where(incl, jnp.exp(jnp.where(incl, seg, 0.0)), 0.0)
                return (gm * dec).astype(BF16)

            xdt = xp * jnp.where(lane_first, dtv[:, ha:ha + 1], dtv[:, hb:hb + 1])
            xdt_a = jnp.where(lane_first, xdt, 0.0)
            xdt_b = xdt - xdt_a
            y = _dot(decay_weights(ha), xdt_a.astype(BF16)) + _dot(decay_weights(hb), xdt_b.astype(BF16))
            sp = s_ref[p]
            y = y + _dot_nt(cg, sp.astype(BF16)) * jnp.where(lane_first, ecum_all[:, ha:ha + 1], ecum_all[:, hb:hb + 1])
            y_ref[:, p * LANES:(p + 1) * LANES] = y + dsk_ref[:, p * LANES:(p + 1) * LANES] * xp
            xw = xdt * jnp.where(lane_first, wend_all[:, ha:ha + 1], wend_all[:, hb:hb + 1])
            dmat = jnp.where(row_first, elast_t[ha:ha + 1, :], elast_t[hb:hb + 1, :])
            s_ref[p] = sp * dmat + _dot(xw.T.astype(BF16), bg)

    @pl.when(n == pl.num_programs(1) - 1)
    def _():
        sout_ref[...] = s_ref[...]


def _m2_ssd(proj, gates, mw, batch, seq):
    c = M2_CHUNK
    nch = seq // c
    width = mw['width']
    gs = M2_GROUPS * M2_STATE
    npairs = width // LANES
    xo, bo, co = width // width, (2 * width) // gs, (2 * width + gs) // gs
    row = lambda w, off: pl.BlockSpec((c, w), lambda b, n: (b * nch + n, off))
    par = lambda r, w, off: pl.BlockSpec((r, w), lambda b, n: (0, off))
    return pl.pallas_call(
        _m2_ssd_kernel,
        out_shape=(jax.ShapeDtypeStruct((batch * seq, width), F32),
                   jax.ShapeDtypeStruct((batch, npairs, LANES, M2_STATE), F32)),
        grid=(batch, nch),
        in_specs=[row(width, xo), row(gs, bo), row(gs, co), row(LANES, 0),
                  par(CONV_W, width, 0), par(CONV_W, gs, width // gs), par(CONV_W, gs, width // gs + 1),
                  par(1, width, 0), par(1, gs, width // gs), par(1, gs, width // gs + 1),
                  par(1, LANES, 0), par(1, LANES, 0), par(1, width, 0)],
        out_specs=(pl.BlockSpec((c, width), lambda b, n: (b * nch + n, 0)),
                   pl.BlockSpec((None, npairs, LANES, M2_STATE), lambda b, n: (b, 0, 0, 0))),
        scratch_shapes=[pltpu.VMEM((c + SUBLANES, width), F32), pltpu.VMEM((c + SUBLANES, gs), F32),
                        pltpu.VMEM((c + SUBLANES, gs), F32), pltpu.VMEM((npairs, LANES, M2_STATE), F32)],
        compiler_params=_params(("parallel", "arbitrary")), name="m2_ssd")(
            proj, proj, proj, gates, mw['conv_w'], mw['conv_w'], mw['conv_w'], mw['conv_b'], mw['conv_b'], mw['conv_b'],
            mw['dt_bias'], mw['neg_a'], mw['d_ch'])


def _m2_pre_step_kernel(xbc_ref, dt_ref, taps_ref, w_ref, b_ref, dtb_ref, nega_ref, act_ref, dtv_ref, dec_ref):
    act_ref[...] = _conv_silu_step(xbc_ref[...], taps_ref, w_ref, b_ref)
    dtv = _softplus(dt_ref[...] + dtb_ref[...])
    dtv_ref[...] = dtv
    dec_ref[...] = jnp.exp(nega_ref[...] * dtv)


def _m2_pre_step(proj, dt_raw, taps, mw):
    rows = proj.shape[0]
    width = mw['width']
    cdim = mw['conv_w'].shape[1]
    xbc = proj[:, width:width + cdim]
    return pl.pallas_call(
        _m2_pre_step_kernel,
        out_shape=(jax.ShapeDtypeStruct((rows, cdim), F32), jax.ShapeDtypeStruct((rows, LANES), F32),
                   jax.ShapeDtypeStruct((rows, LANES), F32)),
        name="m2_pre_step", compiler_params=pltpu.CompilerParams(vmem_limit_bytes=V7X_VMEM_LIMIT_BYTES))(
            xbc, dt_raw, taps, mw['conv_w'], mw['conv_b'], mw['dt_bias'], mw['neg_a'])


def _m2_state_step_kernel(x_ref, dtc_ref, decc_ref, b_ref, c_ref, dsk_ref, s_ref, y_ref, sout_ref):
    rows = x_ref.shape[0]
    pairs = x_ref.shape[1] // LANES
    bv = b_ref[...]
    cb = c_ref[...].astype(BF16)
    for j in range(pairs):
        sl = slice(j * LANES, (j + 1) * LANES)
        xp = x_ref[:, sl]
        xdt_t = _pad_to_square_t(xp * dtc_ref[:, sl], LANES)
        dec_t = _pad_to_square_t(decc_ref[:, sl], LANES)
        for b in range(rows):
            sp = s_ref[b, j]
            s_new = sp * dec_t[:, b:b + 1] + xdt_t[:, b:b + 1] * bv[b:b + 1, :]
            sout_ref[b, j] = s_new
            y_ref[b:b + 1, sl] = _dot_nt(cb[b:b + 1, :], s_new.astype(BF16)) + dsk_ref[:, sl] * xp[b:b + 1, :]


def _m2_state_step(act, dt_ch, dec_ch, ssm, mw):
    rows = act.shape[0]
    width = mw['width']
    gw = width // M2_GROUPS
    ppg = gw // LANES
    npairs = width // LANES
    gs = M2_GROUPS * M2_STATE
    s4 = ssm.reshape(rows, npairs, LANES, M2_STATE)
    blk = lambda w, base: pl.BlockSpec((rows, w), lambda g: (0, base + g))
    return pl.pallas_call(
        _m2_state_step_kernel,
        out_shape=(jax.ShapeDtypeStruct((rows, width), F32), jax.ShapeDtypeStruct(s4.shape, F32)),
        grid=(M2_GROUPS,),
        in_specs=[blk(gw, 0), blk(gw, 0), blk(gw, 0), blk(M2_STATE, width // M2_STATE),
                  blk(M2_STATE, (width + gs) // M2_STATE), pl.BlockSpec((1, gw), lambda g: (0, g)),
                  pl.BlockSpec((rows, ppg, LANES, M2_STATE), lambda g: (0, g, 0, 0))],
        out_specs=(blk(gw, 0), pl.BlockSpec((rows, ppg, LANES, M2_STATE), lambda g: (0, g, 0, 0))),
        compiler_params=_params(("parallel",)), name="m2_state_step")(
            act, dt_ch, dec_ch, act, act, mw['d_ch'], s4)


def _m2_weights(w_in, conv_w, conv_b, dt_bias, a_log, d_skip, norm_w, w_out):
    f = F32
    heads = dt_bias.shape[0]
    width = norm_w.shape[0]
    pad = LANES - heads
    cdim = conv_w.shape[1]
    return dict(
        w_in=_pad_cols(w_in[:, :width + cdim].astype(BF16), 512),
        w_gate=_pad_cols(w_in[:, width + cdim:].astype(f), LANES),
        conv_w=conv_w.astype(f), conv_b=conv_b.astype(f).reshape(1, -1),
        dt_bias=jnp.pad(dt_bias.astype(f), (0, pad)).reshape(1, LANES),
        neg_a=jnp.pad(-jnp.exp(a_log.astype(f)), (0, pad)).reshape(1, LANES),
        d_ch=jnp.repeat(d_skip.astype(f), width // heads).reshape(1, width),
        norm=norm_w.astype(f).reshape(1, width), w_out=w_out.astype(BF16), width=width, heads=heads)


def _gated_rmsnorm_prologue(y, z, w):
    v = y * _silu(z)
    return v * lax.rsqrt(jnp.mean(v * v, axis=-1, keepdims=True) + EPS) * w


def _m2_layer(x, mods, state, mw, *, prompt, batch, seq, tm):
    g, scale, shift, gate = mods
    width, heads = mw['width'], mw['heads']
    cdim = mw['conv_w'].shape[1]
    proj = _in_proj(x, g, scale, shift, mw['w_in'], batch_kind=prompt, rows_per_batch=seq, tm=tm, tn=512,
                    name="m2_in_proj")
    gates = _gate_proj(x, g, scale, shift, mw['w_gate'], batch_kind=prompt, rows_per_batch=seq, tm=tm,
                       name="m2_gate_proj")
    if prompt:
        y, ssm = _m2_ssd(proj, gates, mw, batch, seq)
        conv_new = proj.reshape(batch, seq, -1)[:, seq - (CONV_W - 1):, width:width + cdim]
    else:
        conv_buf, ssm_in = state
        taps = jnp.swapaxes(conv_buf.astype(F32), 0, 1)
        act, dtv, dec = _m2_pre_step(proj, gates, taps, mw)
        rep = width // heads
        dt_ch = jnp.repeat(dtv[:, :heads], rep, axis=1)
        dec_ch = jnp.repeat(dec[:, :heads], rep, axis=1)
        y, ssm = _m2_state_step(act, dt_ch, dec_ch, ssm_in.astype(F32), mw)
        conv_new = jnp.concatenate([conv_buf.astype(F32)[:, 1:], proj[:, None, width:width + cdim]], axis=1)
    x_new = _out_proj([('row', y), ('row', proj, width, 0), ('vec', mw['norm'])], mw['w_out'], x, gate,
                      batch_kind=prompt, rows_per_batch=seq, tm=min(tm, 256), tn=512, name="m2_out_proj",
                      prologue=_gated_rmsnorm_prologue)
    return x_new, conv_new, ssm


GD_DK = 128
GD_DV = 128
GD_CHUNK = 64


def _dot_3pass(a, b):
    ah = a.astype(BF16)
    al = (a - ah.astype(F32)).astype(BF16)
    bh = b.astype(BF16)
    bl = (b - bh.astype(F32)).astype(BF16)
    return _dot(ah, bh) + (_dot(ah, bl) + _dot(al, bh))


def _l2norm_rows(x):
    return x * lax.rsqrt(jnp.sum(x * x, axis=-1, keepdims=True) + EPS)


def _rmsnorm_rows(x, w):
    return x * lax.rsqrt(jnp.mean(x * x, axis=-1, keepdims=True) + EPS) * w


GD_INV_BASE = 16


def _unit_lower_inverse(a_strict):
    c = a_strict.shape[0]
    row = lax.broadcasted_iota(jnp.int32, (c, c), 0)
    col = lax.broadcasted_iota(jnp.int32, (c, c), 1)
    eye = jnp.where(row == col, 1.0, 0.0)
    blk = GD_INV_BASE
    shift = int(math.log2(blk))
    p = jnp.where((row >> shift) == (col >> shift), -a_strict, 0.0)
    t = eye + p
    for _ in range(shift - 1):
        p = _dot_3pass(p, p)
        t = t + _dot_3pass(t, p)
    while blk < c:
        below = jnp.logical_and((row >> (shift + 1)) == (col >> (shift + 1)), (row >> shift) != (col >> shift))
        b = jnp.where(below, a_strict, 0.0)
        t = t - _dot_3pass(_dot_3pass(t, b), t)
        blk *= 2
        shift += 1
    return t


def _gd_chunk_kernel(qkv_ref, z_ref, braw_ref, araw_ref, cw_ref, nega_ref, dtb_ref, nw_ref,
                     o_ref, sout_ref, pad, s_ref):
    n = pl.program_id(1)
    first = n == 0
    c = qkv_ref.shape[0]
    hv = s_ref.shape[0]
    hk = hv // 2
    rep = hv // hk

    @pl.when(first)
    def _():
        s_ref[...] = jnp.zeros_like(s_ref)

    qkv = _conv_silu_chunk(qkv_ref, cw_ref, None, pad, first)
    beta = _sigmoid(braw_ref[...])
    gl = nega_ref[...] * _softplus(araw_ref[...] + dtb_ref[...])
    incl, strict = _tri_masks(c)
    tri = jnp.where(incl, 1.0, 0.0).astype(BF16)
    gcum = _dot_exact_lhs(tri, gl)
    gcum_t = jnp.concatenate([gcum, jnp.zeros((LANES - c, LANES), F32)], axis=0).T
    egc = jnp.exp(gcum)
    eend = jnp.exp(gcum[c - 1:c, :] - gcum)
    elast = jnp.exp(gcum[c - 1:c, :])
    zpad = jnp.zeros((LANES - c, GD_DV), F32)
    nw = nw_ref[...]

    for kh in range(hk):
        q = _l2norm_rows(qkv[:, kh * GD_DK:(kh + 1) * GD_DK]) * (GD_DK ** -0.5)
        k = _l2norm_rows(qkv[:, (hk + kh) * GD_DK:(hk + kh + 1) * GD_DK])
        kb16 = k.astype(BF16)
        kk = _dot_nt(kb16, kb16)
        qk = _dot_nt(q.astype(BF16), kb16)
        for r in range(rep):
            h = kh * rep + r
            v = qkv[:, (2 * hk + h) * GD_DV:(2 * hk + h + 1) * GD_DV]
            col = gcum[:, h:h + 1]
            seg = col - gcum_t[h:h + 1, :c]
            dec = jnp.where(incl, jnp.exp(jnp.where(incl, seg, 0.0)), 0.0)
            bcol = beta[:, h:h + 1]
            a = jnp.where(strict, (bcol * kk) * dec, 0.0)
            tinv = _unit_lower_inverse(a)
            ecol = egc[:, h:h + 1]
            rhs = jnp.concatenate([v * bcol, (k * bcol) * ecol], axis=1)
            sol = _dot_3pass(tinv, rhs)
            u, w = sol[:, :GD_DV], sol[:, GD_DV:]
            s = s_ref[h]
            sb = s.astype(BF16)
            v_new = u - _dot(w.astype(BF16), sb)
            o = _dot((q * ecol).astype(BF16), sb) + _dot((qk * dec).astype(BF16), v_new.astype(BF16))
            kd_t = jnp.concatenate([k * eend[:, h:h + 1], zpad], axis=0).T
            vn_pad = jnp.concatenate([v_new, zpad], axis=0)
            s_ref[h] = s * elast[:, h:h + 1] + _dot(kd_t.astype(BF16), vn_pad.astype(BF16))
            zh = z_ref[:, h * GD_DV:(h + 1) * GD_DV]
            o_ref[:, h * GD_DV:(h + 1) * GD_DV] = (_rmsnorm_rows(o, nw) * _silu(zh)).astype(o_ref.dtype)

    @pl.when(n == pl.num_programs(1) - 1)
    def _():
        sout_ref[...] = s_ref[...]


def _gd_chunked(proj, gates, gw, batch, seq):
    c = GD_CHUNK
    nch = seq // c
    cdim, width, hv = gw['cdim'], gw['width'], gw['hv']
    row = lambda w, off: pl.BlockSpec((c, w), lambda b, n: (b * nch + n, off))
    par = lambda r, w: pl.BlockSpec((r, w), lambda b, n: (0, 0))
    return pl.pallas_call(
        _gd_chunk_kernel,
        out_shape=(jax.ShapeDtypeStruct((batch * seq, width), BF16),
                   jax.ShapeDtypeStruct((batch, hv, GD_DK, GD_DV), F32)),
        grid=(batch, nch),
        in_specs=[row(cdim, 0), row(width, cdim // width), row(LANES, 0), row(LANES, 1),
                  par(CONV_W, cdim), par(1, LANES), par(1, LANES), par(1, GD_DV)],
        out_specs=(pl.BlockSpec((c, width), lambda b, n: (b * nch + n, 0)),
                   pl.BlockSpec((None, hv, GD_DK, GD_DV), lambda b, n: (b, 0, 0, 0))),
        scratch_shapes=[pltpu.VMEM((c + SUBLANES, cdim), F32), pltpu.VMEM((hv, GD_DK, GD_DV), F32)],
        compiler_params=_params(("parallel", "arbitrary")), name="gd_chunked")(
            proj, proj, gates, gates, gw['conv_w'], gw['neg_a'], gw['dt_bias'], gw['norm'])


def _gd_pre_step_kernel(qkv_ref, braw_ref, araw_ref, taps_ref, cw_ref, nega_ref, dtb_ref,
                        q_ref, k_ref, v_ref, beta_ref, eg_ref):
    hk = q_ref.shape[1] // GD_DK
    act = _conv_silu_step(qkv_ref[...], taps_ref, cw_ref, None)
    for kh in range(hk):
        sl = slice(kh * GD_DK, (kh + 1) * GD_DK)
        q_ref[:, sl] = _l2norm_rows(act[:, kh * GD_DK:(kh + 1) * GD_DK]) * (GD_DK ** -0.5)
        k_ref[:, sl] = _l2norm_rows(act[:, (hk + kh) * GD_DK:(hk + kh + 1) * GD_DK])
    v_ref[...] = act[:, 2 * hk * GD_DK:]
    beta_ref[...] = _sigmoid(braw_ref[...])
    eg_ref[...] = jnp.exp(nega_ref[...] * _softplus(araw_ref[...] + dtb_ref[...]))


def _gd_pre_step(proj, gates, taps, gw):
    rows = proj.shape[0]
    cdim, width, hv = gw['cdim'], gw['width'], gw['hv']
    qk_w = (cdim - width) // 2
    sd = lambda w: jax.ShapeDtypeStruct((rows, w), F32)
    return pl.pallas_call(
        _gd_pre_step_kernel, out_shape=(sd(qk_w), sd(qk_w), sd(width), sd(LANES), sd(LANES)),
        name="gd_pre_step", compiler_params=pltpu.CompilerParams(vmem_limit_bytes=V7X_VMEM_LIMIT_BYTES))(
            proj[:, :cdim], gates[:, :LANES], gates[:, LANES:], taps,
            gw['conv_w'], gw['neg_a'], gw['dt_bias'])


def _gd_state_step_kernel(q_ref, k_ref, v_ref, beta_ref, eg_ref, z_ref, nw_ref, s_ref, o_ref, sout_ref):
    rows = q_ref.shape[0]
    nk = q_ref.shape[1] // GD_DK
    rep = (v_ref.shape[1] // GD_DV) // nk
    nw = nw_ref[...]
    zrows = jnp.zeros((SUBLANES - 2, GD_DK), F32)
    for kh in range(nk):
        ksl = slice(kh * GD_DK, (kh + 1) * GD_DK)
        q8, k8 = q_ref[:, ksl], k_ref[:, ksl]
        k_t = _pad_to_square_t(k8, GD_DK)
        for b in range(rows):
            qb, kb = q8[b:b + 1, :], k8[b:b + 1, :]
            kq = jnp.concatenate([kb, qb, zrows], axis=0).astype(BF16)
            qk = jnp.sum(qb * kb, axis=-1, keepdims=True)
            for r in range(rep):
                h = kh * rep + r
                vsl = slice(h * GD_DV, (h + 1) * GD_DV)
                s = s_ref[b, h]
                ks_qs = _dot(kq, s.astype(BF16))
                eg = eg_ref[b:b + 1, vsl]
                beta = beta_ref[b:b + 1, vsl]
                v_new = beta * (v_ref[b:b + 1, vsl] - eg * ks_qs[0:1, :])
                o = eg * ks_qs[1:2, :] + qk * v_new
                sout_ref[b, h] = s * eg[:, 0:1] + k_t[:, b:b + 1] * v_new
                o_ref[b:b + 1, vsl] = _rmsnorm_rows(o, nw) * _silu(z_ref[b:b + 1, vsl])


def _gd_state_step(proj, qn, kn, v, beta_ch, eg_ch, state, gw, heads_per_step=4):
    rows = qn.shape[0]
    cdim, width, hv = gw['cdim'], gw['width'], gw['hv']
    steps = hv // heads_per_step
    kw = qn.shape[1] // steps
    vw = width // steps
    blk = lambda w, base=0: pl.BlockSpec((rows, w), lambda g, base=base: (0, base + g))
    sspec = pl.BlockSpec((rows, heads_per_step, GD_DK, GD_DV), lambda g: (0, g, 0, 0))
    return pl.pallas_call(
        _gd_state_step_kernel,
        out_shape=(jax.ShapeDtypeStruct((rows, width), F32), jax.ShapeDtypeStruct(state.shape, F32)),
        grid=(steps,),
        in_specs=[blk(kw), blk(kw), blk(vw), blk(vw), blk(vw), blk(vw, cdim // vw),
                  pl.BlockSpec((1, GD_DV), lambda g: (0, 0)), sspec],
        out_specs=(blk(vw), sspec),
        compiler_params=_params(("parallel",)), name="gd_state_step")(
            qn, kn, v, beta_ch, eg_ch, proj, gw['norm'], state)


def _gd_weights(w_in, conv_w, a_log, dt_bias, norm_w, w_out):
    f = F32
    hv = a_log.shape[0]
    cdim = conv_w.shape[1]
    width = w_out.shape[0]
    pad = LANES - hv
    base = cdim + width
    zeros = jnp.zeros((w_in.shape[0], pad), w_in.dtype)
    w_gate = jnp.concatenate([w_in[:, base:base + hv], zeros, w_in[:, base + hv:], zeros], axis=1)
    return dict(
        w_in=_pad_cols(w_in[:, :base].astype(BF16), 512), w_gate=w_gate.astype(f), conv_w=conv_w.astype(f),
        neg_a=jnp.pad(-jnp.exp(a_log.astype(f)), (0, pad)).reshape(1, LANES),
        dt_bias=jnp.pad(dt_bias.astype(f), (0, pad)).reshape(1, LANES),
        norm=norm_w.astype(f).reshape(1, -1), w_out=w_out.astype(BF16), cdim=cdim, width=width, hv=hv)


def _gd_layer(x, mods, state, gw, *, prompt, batch, seq, tm):
    g, scale, shift, gate = mods
    cdim, width, hv = gw['cdim'], gw['width'], gw['hv']
    proj = _in_proj(x, g, scale, shift, gw['w_in'], batch_kind=prompt, rows_per_batch=seq, tm=tm, tn=512,
                    name="gd_in_proj")
    gates = _gate_proj(x, g, scale, shift, gw['w_gate'], batch_kind=prompt, rows_per_batch=seq, tm=tm,
                       name="gd_gate_proj")
    if prompt:
        a, ssm = _gd_chunked(proj, gates, gw, batch, seq)
        conv_new = proj.reshape(batch, seq, -1)[:, seq - (CONV_W - 1):, :cdim]
        x_new = _out_proj([('row', a)], gw['w_out'], x, gate, batch_kind=True, rows_per_batch=seq, tm=tm, tn=512,
                          name="gd_out_proj")
    else:
        conv_buf, ssm_in = state
        taps = jnp.swapaxes(conv_buf.astype(F32), 0, 1)
        qn, kn, v, beta, eg = _gd_pre_step(proj, gates, taps, gw)
        beta_ch = jnp.repeat(beta[:, :hv], GD_DV, axis=1)
        eg_ch = jnp.repeat(eg[:, :hv], GD_DV, axis=1)
        a, ssm = _gd_state_step(proj, qn, kn, v, beta_ch, eg_ch, ssm_in.astype(F32), gw)
        conv_new = jnp.concatenate([conv_buf.astype(F32)[:, 1:], proj[:, None, :cdim]], axis=1)
        x_new = _out_proj([('row', a)], gw['w_out'], x, gate, batch_kind=False, rows_per_batch=seq, tm=tm, tn=512,
                          name="gd_out_proj", prologue=_cast_prologue)
    return x_new, conv_new, ssm


AT_DIM = 128
IDX_DIM = 128
TOPK_MAX = 256
REL_BUCKETS = 32
REL_MAX_DIST = 128
AT_TILE = 256
INT32_MIN = -2 ** 31


def _bucket_starts():
    d = np.arange(0, REL_MAX_DIST + 1)
    exact = REL_BUCKETS // 2
    far = exact + (np.log(np.maximum(d, 1).astype(np.float32) / exact) / math.log(REL_MAX_DIST / exact)
                   * (REL_BUCKETS - exact)).astype(np.int32)
    bucket = np.where(d < exact, d, np.minimum(far, REL_BUCKETS - 1))
    assert np.all(np.diff(bucket) >= 0) and bucket[-1] == REL_BUCKETS - 1
    return [int(np.argmax(bucket >= b)) for b in range(REL_BUCKETS)]


def _bias_from_dist(dist, value_of_bucket):
    starts = _bucket_starts()
    val = value_of_bucket(REL_BUCKETS - 1)
    for b in range(REL_BUCKETS - 2, -1, -1):
        val = jnp.where(dist < starts[b + 1], value_of_bucket(b), val)
    return val


def _sort_key(x):
    x = jnp.where(x == 0.0, 0.0, x)
    b = pltpu.bitcast(x, jnp.int32)
    return jnp.where(b < 0, b ^ jnp.int32(0x7FFFFFFF), b)


def _kth_largest_key(count_ge, shape, k):
    def body(it, ans):
        cand = ans | jnp.left_shift(jnp.int32(1), 31 - it)
        cnt = count_ge(cand ^ jnp.int32(INT32_MIN))
        return jnp.where(cnt >= k, cand, ans)

    ans = lax.fori_loop(0, 32, body, jnp.zeros(shape, jnp.int32))
    return ans ^ jnp.int32(INT32_MIN)


def _relbias_tiles_kernel(rb_ref, o_ref):
    delta = pl.program_id(0) * AT_TILE
    h = pl.program_id(1)
    i = lax.broadcasted_iota(jnp.int32, (AT_TILE, AT_TILE), 0)
    j = lax.broadcasted_iota(jnp.int32, (AT_TILE, AT_TILE), 1)
    o_ref[...] = _bias_from_dist(delta + i - j, lambda b: rb_ref[b, h])


def _relbias_tiles(rel_bias):
    heads = rel_bias.shape[1]
    ntile = 3
    assert (ntile - 1) * AT_TILE - (AT_TILE - 1) >= REL_MAX_DIST
    return pl.pallas_call(
        _relbias_tiles_kernel, out_shape=jax.ShapeDtypeStruct((ntile, heads, AT_TILE, AT_TILE), F32),
        grid=(ntile, heads),
        in_specs=[pl.BlockSpec(memory_space=pltpu.SMEM)],
        out_specs=pl.BlockSpec((None, None, AT_TILE, AT_TILE), lambda d, h: (d, h, 0, 0)),
        compiler_params=_params(("parallel", "parallel")), name="at_relbias_tiles")(rel_bias.astype(F32))


def _at_index_kernel(qi_ref, wi_ref, ki_ref, o_ref, keys, *, k_sel, score_scale):
    qb = pl.program_id(1)
    tq = qi_ref.shape[0]
    nkb = keys.shape[0]
    tk = keys.shape[2]
    nh = qi_ref.shape[1] // IDX_DIM
    wsc = wi_ref[...] * score_scale
    qpos = qb * tq + lax.broadcasted_iota(jnp.int32, (tq, tk), 0)
    kloc = lax.broadcasted_iota(jnp.int32, (tq, tk), 1)
    neg_key = _sort_key(jnp.full((tq, tk), NEG, F32))

    for kb in range(nkb):
        @pl.when(kb <= qb)
        def _():
            kblk = ki_ref[kb * tk:(kb + 1) * tk, :].astype(BF16)
            sc = jnp.zeros((tq, tk), F32)
            for h in range(nh):
                d = _dot_nt(qi_ref[:, h * IDX_DIM:(h + 1) * IDX_DIM].astype(BF16), kblk)
                sc = sc + wsc[:, h:h + 1] * jnp.maximum(d, 0.0)
            adm = kb * tk + kloc <= qpos
            keys[kb] = _sort_key(jnp.where(adm, sc, NEG))

        @pl.when(kb > qb)
        def _():
            keys[kb] = neg_key

    def count_ge(t):
        acc = jnp.zeros((tq, tk), jnp.int32)
        for kb in range(nkb):
            acc = acc + jnp.where(keys[kb] >= t, 1, 0)
        return jnp.sum(acc, axis=1, keepdims=True)

    thr = _kth_largest_key(count_ge, (tq, 1), k_sel)
    n_ge = count_ge(thr)
    has_ties = jnp.max(n_ge) > k_sel

    @pl.when(jnp.logical_not(has_ties))
    def _():
        for kb in range(nkb):
            adm = kb * tk + kloc <= qpos
            sel = jnp.logical_and(keys[kb] >= thr, adm)
            o_ref[:, kb * tk:(kb + 1) * tk] = jnp.where(sel, 0.0, NEG).astype(o_ref.dtype)

    @pl.when(has_ties)
    def _():
        acc = jnp.zeros((tq, tk), jnp.int32)
        for kb in range(nkb):
            acc = acc + jnp.where(keys[kb] > thr, 1, 0)
        room = (k_sel - jnp.sum(acc, axis=1, keepdims=True)).astype(F32)
        upper = jnp.where(lax.broadcasted_iota(jnp.int32, (tk, tk), 0) <= lax.broadcasted_iota(jnp.int32, (tk, tk), 1),
                          1.0, 0.0).astype(BF16)
        seen = jnp.zeros((tq, 1), F32)
        for kb in range(nkb):
            key = keys[kb]
            eq = key == thr
            eqf = jnp.where(eq, 1.0, 0.0)
            rank = seen + _dot(eqf.astype(BF16), upper)
            seen = seen + jnp.sum(eqf, axis=1, keepdims=True)
            adm = kb * tk + kloc <= qpos
            sel = jnp.logical_and(jnp.logical_or(key > thr, jnp.logical_and(eq, rank <= room)), adm)
            o_ref[:, kb * tk:(kb + 1) * tk] = jnp.where(sel, 0.0, NEG).astype(o_ref.dtype)


def _at_index(proj, aw, batch, seq, k_sel):
    tq = tk = AT_TILE
    nq = seq // tq
    width = aw['width']
    nh = aw['idx_heads']
    qio = (4 * width) // (nh * IDX_DIM)
    kio = (4 * width + nh * IDX_DIM) // IDX_DIM
    kern = functools.partial(_at_index_kernel, k_sel=k_sel, score_scale=IDX_DIM ** -0.5 * nh ** -0.5)
    return pl.pallas_call(
        kern, out_shape=jax.ShapeDtypeStruct((batch * seq, seq), BF16), grid=(batch, nq),
        in_specs=[pl.BlockSpec((tq, nh * IDX_DIM), lambda b, q: (b * nq + q, qio)),
                  pl.BlockSpec((tq, LANES), lambda b, q: (b * nq + q, kio + 1)),
                  pl.BlockSpec((seq, IDX_DIM), lambda b, q: (b, kio))],
        out_specs=pl.BlockSpec((tq, seq), lambda b, q: (b * nq + q, 0)),
        scratch_shapes=[pltpu.VMEM((seq // tk, tq, tk), jnp.int32)],
        compiler_params=_params(("parallel", "parallel")), name="at_index")(proj, proj, proj)


def _at_attend_kernel(q_ref, k_ref, v_ref, z_ref, mask_ref, bias_ref, o_ref, acc, m_scr, l_scr, *, scale):
    qb, kb = pl.program_id(1), pl.program_id(2)
    nh = q_ref.shape[1] // AT_DIM

    @pl.when(kb == 0)
    def _():
        acc[...] = jnp.zeros_like(acc)
        m_scr[...] = jnp.full(m_scr.shape, NEG, F32)
        l_scr[...] = jnp.zeros_like(l_scr)

    @pl.when(kb <= qb)
    def _():
        madd = mask_ref[...].astype(F32)
        for h in range(nh):
            sl = slice(h * AT_DIM, (h + 1) * AT_DIM)
            s = _dot_nt((q_ref[:, sl] * scale).astype(BF16), k_ref[:, sl].astype(BF16)) + bias_ref[h] + madd
            m_old = m_scr[h]
            m_new = jnp.maximum(m_old, jnp.max(s, axis=1, keepdims=True))
            alpha = jnp.exp(m_old - m_new)
            p = jnp.where(madd < 0.0, 0.0, jnp.exp(s - m_new))
            l_scr[h] = alpha * l_scr[h] + jnp.sum(p, axis=1, keepdims=True)
            acc[:, sl] = alpha * acc[:, sl] + _dot(p.astype(BF16), v_ref[:, sl].astype(BF16))
            m_scr[h] = m_new

    @pl.when(kb == qb)
    def _():
        for h in range(nh):
            sl = slice(h * AT_DIM, (h + 1) * AT_DIM)
            o_ref[:, sl] = (acc[:, sl] / l_scr[h] * _silu(z_ref[:, sl])).astype(o_ref.dtype)


def _at_attend(proj, maskadd, tiles, aw, batch, seq):
    t = AT_TILE
    nq = seq // t
    width = aw['width']
    nh = width // AT_DIM
    kern = functools.partial(_at_attend_kernel, scale=AT_DIM ** -0.5)
    kc = lambda q, k: jnp.minimum(k, q)
    return pl.pallas_call(
        kern, out_shape=jax.ShapeDtypeStruct((batch * seq, width), BF16), grid=(batch, nq, nq),
        in_specs=[pl.BlockSpec((t, width), lambda b, q, k: (b * nq + q, 0)),
                  pl.BlockSpec((t, width), lambda b, q, k: (b * nq + kc(q, k), 1)),
                  pl.BlockSpec((t, width), lambda b, q, k: (b * nq + kc(q, k), 2)),
                  pl.BlockSpec((t, width), lambda b, q, k: (b * nq + q, 3)),
                  pl.BlockSpec((t, t), lambda b, q, k: (b * nq + q, kc(q, k))),
                  pl.BlockSpec((None, nh, t, t), lambda b, q, k: (jnp.clip(q - k, 0, 2), 0, 0, 0))],
        out_specs=pl.BlockSpec((t, width), lambda b, q, k: (b * nq + q, 0)),
        scratch_shapes=[pltpu.VMEM((t, width), F32), pltpu.VMEM((nh, t, 1), F32), pltpu.VMEM((nh, t, 1), F32)],
        compiler_params=_params(("parallel", "parallel", "arbitrary")), name="at_attend")(
            proj, proj, proj, proj, maskadd, tiles)


def _at_weights(w_in, rel_bias, w_out):
    width = w_out.shape[0]
    heads = rel_bias.shape[1]
    idx_heads = (w_in.shape[1] - 4 * width - IDX_DIM) // (IDX_DIM + 1)
    return dict(w_in=_pad_cols(w_in.astype(BF16), 512), rel_bias=rel_bias.astype(F32), w_out=w_out.astype(BF16),
                width=width, heads=heads, idx_heads=idx_heads)


def _at_rows(proj, aw, lead):
    width, heads = aw['width'], aw['heads']
    k = proj[:, width:2 * width].reshape(lead + (heads, AT_DIM))
    v = proj[:, 2 * width:3 * width].reshape(lead + (heads, AT_DIM))
    o = 4 * width + aw['idx_heads'] * IDX_DIM
    ki = proj[:, o:o + IDX_DIM].reshape(lead + (IDX_DIM,))
    return k, v, ki


def _at_layer_prompt(x, mods, aw, *, batch, seq, tm):
    g, scale, shift, gate = mods
    proj = _in_proj(x, g, scale, shift, aw['w_in'], batch_kind=True, rows_per_batch=seq, tm=tm, tn=512,
                    name="at_in_proj")
    k_sel = min(TOPK_MAX, seq // 4)
    maskadd = _at_index(proj, aw, batch, seq, k_sel)
    tiles = _relbias_tiles(aw['rel_bias'])
    a = _at_attend(proj, maskadd, tiles, aw, batch, seq)
    x_new = _out_proj([('row', a)], aw['w_out'], x, gate, batch_kind=True, rows_per_batch=seq, tm=tm, tn=512,
                      name="at_out_proj")
    return (x_new,) + _at_rows(proj, aw, (batch, seq))


def _at_page_scores_kernel(pt_ref, qi_ref, w_ref, kp_ref, o_ref):
    d = _dot_nt(qi_ref[...].astype(BF16), kp_ref[...].astype(BF16))
    o_ref[...] = jnp.sum(w_ref[...] * jnp.maximum(d, 0.0), axis=0, keepdims=True)


def _at_page_scores(qi3, w3, cache_kidx, page_table):
    rows, nh, _ = qi3.shape
    npages = page_table.shape[1]
    page = cache_kidx.shape[1]
    grid_spec = pltpu.PrefetchScalarGridSpec(
        num_scalar_prefetch=1, grid=(rows, npages),
        in_specs=[pl.BlockSpec((None, nh, IDX_DIM), lambda b, p, pt: (b, 0, 0)),
                  pl.BlockSpec((None, nh, IDX_DIM), lambda b, p, pt: (b, 0, 0)),
                  pl.BlockSpec((None, page, IDX_DIM), lambda b, p, pt: (pt[b, p], 0, 0))],
        out_specs=pl.BlockSpec((None, None, 1, page), lambda b, p, pt: (b, p, 0, 0)))
    return pl.pallas_call(
        _at_page_scores_kernel, out_shape=jax.ShapeDtypeStruct((rows, npages, 1, page), F32), grid_spec=grid_spec,
        compiler_params=_params(("parallel", "arbitrary")), name="at_page_scores")(page_table, qi3, w3, cache_kidx)


def _at_sample_select_kernel(sc_ref, qi_ref, w_ref, kin_ref, rbt_ref, mask_ref, newadd_ref, bias_ref, *, k_sel):
    rows, npages, page = sc_ref.shape
    upper = jnp.where(lax.broadcasted_iota(jnp.int32, (page, page), 0) <= lax.broadcasted_iota(jnp.int32, (page, page), 1),
                      1.0, 0.0).astype(BF16)
    lower = jnp.where(lax.broadcasted_iota(jnp.int32, (npages, npages), 1) < lax.broadcasted_iota(jnp.int32, (npages, npages), 0),
                      1.0, 0.0).astype(BF16)

    def total(x):
        return jnp.sum(jnp.sum(x, axis=1, keepdims=True), axis=0, keepdims=True)

    for b in range(rows):
        keys = _sort_key(sc_ref[b])
        dots = jnp.sum(qi_ref[b] * kin_ref[b:b + 1, :], axis=1, keepdims=True)
        s_new = jnp.sum(w_ref[b][:, 0:1] * jnp.maximum(dots, 0.0), axis=0, keepdims=True)
        key_new = _sort_key(s_new)

        def count_ge(t):
            return total(jnp.where(keys >= t, 1, 0)) + jnp.where(key_new >= t, 1, 0)

        thr = _kth_largest_key(count_ge, (1, 1), k_sel)
        n_gt = total(jnp.where(keys > thr, 1.0, 0.0)) + jnp.where(key_new > thr, 1.0, 0.0)
        room = k_sel - n_gt
        eq = keys == thr
        eqf = jnp.where(eq, 1.0, 0.0)
        row_cnt = jnp.broadcast_to(jnp.sum(eqf, axis=1, keepdims=True), (npages, page))
        rank = _dot(lower, row_cnt.astype(BF16)) + _dot(eqf.astype(BF16), upper)
        sel = jnp.logical_or(keys > thr, jnp.logical_and(eq, rank <= room))
        mask_ref[b] = jnp.where(sel, 0.0, NEG)
        sel_new = jnp.logical_or(key_new > thr, jnp.logical_and(key_new == thr, total(eqf) + 1.0 <= room))
        newadd_ref[b:b + 1, :] = jnp.broadcast_to(jnp.where(sel_new, 0.0, NEG), (1, page))

    nh = rbt_ref.shape[0]
    bias_ref[0] = jnp.broadcast_to(rbt_ref[:, REL_BUCKETS - 1:REL_BUCKETS], (nh, page))
    dist = page - lax.broadcasted_iota(jnp.int32, (nh, page), 1)
    bias_ref[1] = _bias_from_dist(dist, lambda bk: rbt_ref[:, bk:bk + 1])


def _at_sample_select(scores, qi3, w3, ki_new, rbt, k_sel):
    rows, npages, page = scores.shape
    assert page >= REL_MAX_DIST
    kern = functools.partial(_at_sample_select_kernel, k_sel=k_sel)
    return pl.pallas_call(
        kern, out_shape=(jax.ShapeDtypeStruct((rows, npages, page), F32), jax.ShapeDtypeStruct((rows, page), F32),
                         jax.ShapeDtypeStruct((2, rbt.shape[0], page), F32)),
        name="at_sample_select", compiler_params=pltpu.CompilerParams(vmem_limit_bytes=V7X_VMEM_LIMIT_BYTES))(
            scores, qi3, w3, ki_new, rbt)


def _at_sample_attend_kernel(pt_ref, q_ref, k_ref, v_ref, mask_ref, newadd_ref, bias_ref, rbt_ref, kn_ref, vn_ref,
                             z_ref, o_ref, qbd, acc, m_scr, l_scr, *, scale):
    p = pl.program_id(1)
    last = pl.num_programs(1) - 1
    nh, width = qbd.shape
    diag = (lax.broadcasted_iota(jnp.int32, (nh, width), 1) >> 7) == lax.broadcasted_iota(jnp.int32, (nh, width), 0)

    @pl.when(p == 0)
    def _():
        qbd[...] = jnp.where(diag, jnp.broadcast_to(q_ref[...] * scale, (nh, width)), 0.0).astype(BF16)
        acc[...] = jnp.zeros_like(acc)
        m_scr[...] = jnp.full(m_scr.shape, NEG, F32)
        l_scr[...] = jnp.zeros_like(l_scr)

    madd = mask_ref[...]
    s = _dot_nt(qbd[...], k_ref[...].astype(BF16)) + bias_ref[jnp.where(p == last, 1, 0)] + madd
    m_old = m_scr[...]
    m_new = jnp.maximum(m_old, jnp.max(s, axis=1, keepdims=True))
    alpha = jnp.exp(m_old - m_new)
    pr = jnp.where(madd < 0.0, 0.0, jnp.exp(s - m_new))
    l_scr[...] = alpha * l_scr[...] + jnp.sum(pr, axis=1, keepdims=True)
    acc[...] = alpha * acc[...] + _dot(pr.astype(BF16), v_ref[...].astype(BF16))
    m_scr[...] = m_new

    @pl.when(p == last)
    def _():
        nadd = newadd_ref[:, 0:1]
        s_n = jnp.sum(qbd[...].astype(F32) * kn_ref[...], axis=1, keepdims=True) + rbt_ref[:, 0:1] + nadd
        m_o = m_scr[...]
        m_n = jnp.maximum(m_o, s_n)
        al = jnp.exp(m_o - m_n)
        p_n = jnp.where(nadd < 0.0, 0.0, jnp.exp(s_n - m_n))
        l_n = al * l_scr[...] + p_n
        o = (al * acc[...] + p_n * vn_ref[...]) / l_n
        o_ref[...] = jnp.sum(jnp.where(diag, o, 0.0), axis=0, keepdims=True) * _silu(z_ref[...])


def _at_sample_attend(q3, cache_k, cache_v, page_table, maskadd, newadd, bias, rbt, kn3, vn3, z3):
    rows, _, width = q3.shape
    npages = page_table.shape[1]
    page = cache_k.shape[1]
    nh = rbt.shape[0]
    row3 = pl.BlockSpec((None, 1, width), lambda b, p, pt: (b, 0, 0))
    grid_spec = pltpu.PrefetchScalarGridSpec(
        num_scalar_prefetch=1, grid=(rows, npages),
        in_specs=[row3,
                  pl.BlockSpec((None, page, width), lambda b, p, pt: (pt[b, p], 0, 0)),
                  pl.BlockSpec((None, page, width), lambda b, p, pt: (pt[b, p], 0, 0)),
                  pl.BlockSpec((None, None, 1, page), lambda b, p, pt: (b, p, 0, 0)),
                  pl.BlockSpec((None, 1, page), lambda b, p, pt: (b, 0, 0)),
                  pl.BlockSpec((2, nh, page), lambda b, p, pt: (0, 0, 0)),
                  pl.BlockSpec(rbt.shape, lambda b, p, pt: (0, 0)),
                  row3, row3, row3],
        out_specs=row3,
        scratch_shapes=[pltpu.VMEM((nh, width), BF16), pltpu.VMEM((nh, width), F32),
                        pltpu.VMEM((nh, 1), F32), pltpu.VMEM((nh, 1), F32)])
    kern = functools.partial(_at_sample_attend_kernel, scale=AT_DIM ** -0.5)
    return pl.pallas_call(
        kern, out_shape=jax.ShapeDtypeStruct((rows, 1, width), F32), grid_spec=grid_spec,
        compiler_params=_params(("parallel", "arbitrary")), name="at_sample_attend")(
            page_table, q3, cache_k, cache_v, maskadd, newadd, bias, rbt, kn3, vn3, z3)


def _at_layer_sample(x, mods, cache_k, cache_v, cache_kidx, page_table, aw):
    g, scale, shift, gate = mods
    rows = x.shape[0]
    width, heads, nh = aw['width'], aw['heads'], aw['idx_heads']
    proj = _in_proj(x, g, scale, shift, aw['w_in'], batch_kind=False, rows_per_batch=1, tm=SUBLANES, tn=512,
                    name="at_in_proj")
    npool, page = cache_k.shape[:2]
    npages = page_table.shape[1]
    past = npages * page
    k_sel = min(TOPK_MAX, (past + 1) // 4)
    o = 4 * width
    qi3 = proj[:, o:o + nh * IDX_DIM].reshape(rows, nh, IDX_DIM)
    ki_new = proj[:, o + nh * IDX_DIM:o + nh * IDX_DIM + IDX_DIM]
    wi = proj[:, o + nh * IDX_DIM + IDX_DIM:o + nh * IDX_DIM + IDX_DIM + nh] * (IDX_DIM ** -0.5 * nh ** -0.5)
    w3 = jnp.broadcast_to(wi[:, :, None], (rows, nh, IDX_DIM))
    scores = _at_page_scores(qi3, w3, cache_kidx.astype(F32), page_table).reshape(rows, npages, page)
    rbt = aw['rel_bias'].T
    maskadd, newadd, bias = _at_sample_select(scores, qi3, w3, ki_new, rbt, k_sel)
    r3 = lambda t: t.reshape(rows, 1, width)
    a = _at_sample_attend(r3(proj[:, :width]), cache_k.astype(F32).reshape(npool, page, width),
                          cache_v.astype(F32).reshape(npool, page, width), page_table,
                          maskadd.reshape(rows, npages, 1, page), newadd.reshape(rows, 1, page), bias, rbt,
                          r3(proj[:, width:2 * width]), r3(proj[:, 2 * width:3 * width]),
                          r3(proj[:, 3 * width:4 * width]))
    x_new = _out_proj([('row', a.reshape(rows, width))], aw['w_out'], x, gate, batch_kind=False, rows_per_batch=1,
                      tm=SUBLANES, tn=512, name="at_out_proj", prologue=_cast_prologue)
    return (x_new,) + _at_rows(proj, aw, (rows, 1))


def kernel(x_prompt, x_sample, state_s5_re, state_s5_im, state_m2_conv, state_m2_ssm, state_gd_conv, state_gd_ssm, cache_k, cache_v, cache_kidx, page_table, c_prompt, c_sample, norm_g, w_mod, b_mod, final_g, s5_w_in, s5_lam_re, s5_lam_im, s5_log_dt, s5_b_re, s5_b_im, s5_c_re, s5_c_im, s5_d, s5_w_glu, s5_b_glu, s5_w_out, m2_w_in, m2_conv_w, m2_conv_b, m2_dt_bias, m2_a_log, m2_d, m2_norm, m2_w_out, gd_w_in, gd_conv_w, gd_a_log, gd_dt_bias, gd_norm, gd_w_out, at_w_in, rel_bias, at_w_out):
    f = F32
    bp, seq, d = x_prompt.shape
    bs = x_sample.shape[0]
    depth = norm_g.shape[0]
    xp = x_prompt.astype(f).reshape(bp * seq, d)
    xs = x_sample.astype(f).reshape(bs, d)

    pad_rows = (-(bs + bp)) % SUBLANES
    c_all = jnp.concatenate([c_sample.astype(f), c_prompt.astype(f), jnp.zeros((pad_rows, d), f)], axis=0)
    mod = _modulation(c_all, w_mod, b_mod)

    def mods(i, prompt):
        g = norm_g[i].astype(f).reshape(1, d)
        rows = mod[i, bs:bs + bp] if prompt else mod[i, :bs]
        shift, scale, gate = rows[:, :d], rows[:, d:2 * d], rows[:, 2 * d:]
        if prompt:
            return g, scale[:, None, :], shift[:, None, :], gate[:, None, :]
        return g, scale, shift, gate

    s5w = _s5_weights(s5_w_in, s5_lam_re, s5_lam_im, s5_log_dt, s5_b_re, s5_b_im, s5_c_re, s5_c_im, s5_d,
                      s5_w_glu, s5_b_glu, s5_w_out)
    tm_p = 512

    xp, s5_re_p, s5_im_p = _s5_layer(xp, mods(0, True), None, s5w, prompt=True, batch=bp, seq=seq, tm=tm_p)
    xs, s5_re_s, s5_im_s = _s5_layer(xs, mods(0, False), (state_s5_re, state_s5_im), s5w, prompt=False,
                                     batch=bs, seq=1, tm=SUBLANES)
    groups, nstate = state_s5_re.shape[1:]
    s5_re_p, s5_im_p = s5_re_p.reshape(bp, groups, nstate), s5_im_p.reshape(bp, groups, nstate)
    s5_re_s, s5_im_s = s5_re_s.reshape(bs, groups, nstate), s5_im_s.reshape(bs, groups, nstate)

    m2w = _m2_weights(m2_w_in, m2_conv_w, m2_conv_b, m2_dt_bias, m2_a_log, m2_d, m2_norm, m2_w_out)
    xp, m2_conv_p, m2_ssm_p = _m2_layer(xp, mods(1, True), None, m2w, prompt=True, batch=bp, seq=seq, tm=tm_p)
    xs, m2_conv_s, m2_ssm_s = _m2_layer(xs, mods(1, False), (state_m2_conv, state_m2_ssm), m2w, prompt=False,
                                        batch=bs, seq=1, tm=SUBLANES)
    m2_ssm_p = m2_ssm_p.reshape((bp,) + state_m2_ssm.shape[1:])
    m2_ssm_s = m2_ssm_s.reshape(state_m2_ssm.shape)

    gdw = _gd_weights(gd_w_in, gd_conv_w, gd_a_log, gd_dt_bias, gd_norm, gd_w_out)
    xp, gd_conv_p, gd_ssm_p = _gd_layer(xp, mods(2, True), None, gdw, prompt=True, batch=bp, seq=seq, tm=tm_p)
    xs, gd_conv_s, gd_ssm_s = _gd_layer(xs, mods(2, False), (state_gd_conv, state_gd_ssm), gdw, prompt=False,
                                        batch=bs, seq=1, tm=SUBLANES)

    atw = _at_weights(at_w_in, rel_bias, at_w_out)
    xp, k_rows_p, v_rows_p, kidx_rows_p = _at_layer_prompt(xp, mods(3, True), atw, batch=bp, seq=seq, tm=tm_p)
    xs, k_rows_s, v_rows_s, kidx_rows_s = _at_layer_sample(xs, mods(3, False), cache_k, cache_v, cache_kidx,
                                                           page_table, atw)

    y_prompt = _final_norm(xp, final_g).reshape(x_prompt.shape).astype(x_prompt.dtype)
    y_sample = _final_norm(xs, final_g).reshape(x_sample.shape).astype(x_sample.dtype)
    return (y_prompt, y_sample, s5_re_p, s5_im_p, s5_re_s, s5_im_s, m2_conv_p, m2_ssm_p, m2_conv_s, m2_ssm_s,
            gd_conv_p, gd_ssm_p, gd_conv_s, gd_ssm_s,
            k_rows_p, v_rows_p, kidx_rows_p, k_rows_s, v_rows_s, kidx_rows_s)
```

```python
import functools
import math

import numpy as np
import jax
import jax.numpy as jnp
from jax import lax
from jax.experimental import pallas as pl
from jax.experimental.pallas import tpu as pltpu

F32 = jnp.float32
BF16 = jnp.bfloat16

EPS = 1e-6
NEG = -1e30
CONV_W = 4
V7X_VMEM_LIMIT_BYTES = 56 * 1024 * 1024
LANES = 128
SUBLANES = 8
MM_TM = 1024
MM_TM_WIDE_ROWS = 512
MM_TN = 1024

S5_GROUP = 16
S5_STATE = 64
S5_CHUNK = 256
S5_SEG = S5_CHUNK // SUBLANES
S5_BLK_CH = 256
S5_BLK_ST = 1024


def _params(sem):
    return pltpu.CompilerParams(dimension_semantics=sem, vmem_limit_bytes=V7X_VMEM_LIMIT_BYTES)


def _sigmoid(x):
    return 1.0 / (1.0 + jnp.exp(-x))


def _silu(x):
    return x * _sigmoid(x)


def _gelu(x):
    return 0.5 * x * (1.0 + jnp.tanh(math.sqrt(2.0 / math.pi) * (x + 0.044715 * (x * x * x))))


def _softplus(x):
    return jnp.maximum(x, 0.0) + jnp.log1p(jnp.exp(-jnp.abs(x)))


def _dot(a, b):
    return jnp.dot(a, b, preferred_element_type=F32)


def _dot_nt(a, b):
    return lax.dot_general(a, b, (((1,), (1,)), ((), ())), preferred_element_type=F32)


def _split3(x):
    hi = x.astype(BF16)
    r1 = x - hi.astype(F32)
    mid = r1.astype(BF16)
    lo = (r1 - mid.astype(F32)).astype(BF16)
    return hi, mid, lo


def _dot_exact_lhs(sel, x):
    hi, mid, lo = _split3(x)
    return _dot(sel, hi) + (_dot(sel, mid) + _dot(sel, lo))


def _dot_f32(a, b):
    ah, am, al = _split3(a)
    bh, bm, bl = _split3(b)
    small = _dot(am, bm) + _dot(ah, bl) + _dot(al, bh)
    return _dot(ah, bh) + (_dot(ah, bm) + _dot(am, bh) + small)


def _mm_kernel(*refs, n_a, n_e, prologue, epilogue):
    a_refs = refs[:n_a]
    w_ref = refs[n_a]
    e_refs = refs[n_a + 1:n_a + 1 + n_e]
    o_ref = refs[n_a + 1 + n_e]
    if prologue is None:
        a = a_refs[0][...]
    else:
        a_scr = refs[n_a + 2 + n_e]

        @pl.when(pl.program_id(1) == 0)
        def _():
            a_scr[...] = prologue(*[r[...] for r in a_refs]).astype(BF16)

        a = a_scr[...]
    acc = _dot(a, w_ref[...])
    o_ref[...] = epilogue(acc, *[r[...] for r in e_refs]).astype(o_ref.dtype)


def _fused_matmul(a_ins, w, e_ins, *, prologue, epilogue, out_dtype, tm, tn, rows_per_batch=None, name):
    m = next(item[1].shape[0] for item in a_ins if item[0] == 'row')
    k, n = w.shape
    tm = min(tm, m)
    tn = next(t for t in (1024, 768, 512, 384, 256, 128) if t <= tn and n % t == 0)
    assert m % tm == 0
    rpb = rows_per_batch

    def bidx(i):
        return (i * tm) // rpb

    in_specs, args = [], []
    for item in a_ins:
        kind, arr = item[0], item[1]
        wd = item[2] if len(item) > 2 else arr.shape[-1]
        coff = item[3] if len(item) > 3 else 0
        if kind == 'row':
            in_specs.append(pl.BlockSpec((tm, wd), lambda i, j, coff=coff: (i, coff)))
        elif kind == 'vec':
            in_specs.append(pl.BlockSpec((1, wd), lambda i, j: (0, 0)))
        else:
            in_specs.append(pl.BlockSpec((None, 1, wd), lambda i, j: (bidx(i), 0, 0)))
        args.append(arr)
    in_specs.append(pl.BlockSpec((k, tn), lambda i, j: (0, j)))
    args.append(w)
    for item in e_ins:
        kind, arr = item[0], item[1]
        off = (item[2] if len(item) > 2 else 0) // tn
        if kind == 'tile':
            assert len(item) < 3 or item[2] % tn == 0
            in_specs.append(pl.BlockSpec((tm, tn), lambda i, j, off=off: (i, j + off)))
        elif kind == 'col':
            in_specs.append(pl.BlockSpec((1, tn), lambda i, j: (0, j)))
        else:
            in_specs.append(pl.BlockSpec((None, 1, tn), lambda i, j: (bidx(i), 0, j)))
        args.append(arr)
    scratch = [] if prologue is None else [pltpu.VMEM((tm, k), BF16)]
    kern = functools.partial(_mm_kernel, n_a=len(a_ins), n_e=len(e_ins), prologue=prologue, epilogue=epilogue)
    return pl.pallas_call(
        kern, out_shape=jax.ShapeDtypeStruct((m, n), out_dtype), grid=(m // tm, n // tn),
        in_specs=in_specs, out_specs=pl.BlockSpec((tm, tn), lambda i, j: (i, j)),
        scratch_shapes=scratch, compiler_params=_params(("parallel", "arbitrary")), name=name)(*args)


def _pad_cols(w, mult):
    n = w.shape[-1]
    npad = (-n) % mult
    if npad:
        w = jnp.pad(w, ((0, 0), (0, npad)))
    return w


def _modnorm_prologue(x, g, scale, shift):
    r = x * lax.rsqrt(jnp.mean(x * x, axis=-1, keepdims=True) + EPS) * g
    return r * (1.0 + scale) + shift


def _identity_epilogue(acc):
    return acc


def _residual_epilogue(acc, x, gate):
    return x + gate * acc


def _in_proj(x, g, scale, shift, w, *, batch_kind, rows_per_batch, tm, tn, name):
    kind = 'batch' if batch_kind else 'row'
    return _fused_matmul([('row', x), ('vec', g), (kind, scale), (kind, shift)], w, [],
                         prologue=_modnorm_prologue, epilogue=_identity_epilogue, out_dtype=F32,
                         tm=tm, tn=tn, rows_per_batch=rows_per_batch, name=name)


def _gate_proj_kernel(x_ref, g_ref, scale_ref, shift_ref, w_ref, o_ref):
    h = _modnorm_prologue(x_ref[...], g_ref[...], scale_ref[...], shift_ref[...])
    o_ref[...] = _dot_f32(h, w_ref[...])


def _gate_proj(x, g, scale, shift, w, *, batch_kind, rows_per_batch, tm, name):
    m, d = x.shape
    n = w.shape[1]
    tm = min(tm, m, MM_TM_WIDE_ROWS)
    if batch_kind:
        mod_spec = pl.BlockSpec((None, 1, d), lambda i: ((i * tm) // rows_per_batch, 0, 0))
    else:
        mod_spec = pl.BlockSpec((tm, d), lambda i: (i, 0))
    return pl.pallas_call(
        _gate_proj_kernel, out_shape=jax.ShapeDtypeStruct((m, n), F32), grid=(m // tm,),
        in_specs=[pl.BlockSpec((tm, d), lambda i: (i, 0)), pl.BlockSpec((1, d), lambda i: (0, 0)),
                  mod_spec, mod_spec, pl.BlockSpec((d, n), lambda i: (0, 0))],
        out_specs=pl.BlockSpec((tm, n), lambda i: (i, 0)),
        compiler_params=_params(("parallel",)), name=name)(x, g, scale, shift, w)


def _out_proj(a_ins, w, x, gate, *, batch_kind, rows_per_batch, tm, tn, name, prologue=None):
    kind = 'batchcol' if batch_kind else 'tile'
    return _fused_matmul(a_ins, w, [('tile', x), (kind, gate)], prologue=prologue, epilogue=_residual_epilogue,
                         out_dtype=F32, tm=tm, tn=tn, rows_per_batch=rows_per_batch, name=name)


def _mod_kernel(c_ref, w_ref, b_ref, o_ref):
    o_ref[...] = _dot(c_ref[...].astype(BF16), w_ref[...].astype(BF16)) + b_ref[...]


def _modulation(c_all, w_mod, b_mod, tn=512):
    depth, d, n = w_mod.shape
    rows = c_all.shape[0]
    return pl.pallas_call(
        _mod_kernel, out_shape=jax.ShapeDtypeStruct((depth, rows, n), F32), grid=(depth, n // tn),
        in_specs=[pl.BlockSpec((rows, d), lambda l, j: (0, 0)),
                  pl.BlockSpec((None, d, tn), lambda l, j: (l, 0, j)),
                  pl.BlockSpec((None, 1, tn), lambda l, j: (l, 0, j))],
        out_specs=pl.BlockSpec((None, rows, tn), lambda l, j: (l, 0, j)),
        compiler_params=_params(("parallel", "parallel")), name="adaln_modulation")(
            c_all, w_mod, b_mod.reshape(depth, 1, n))


def _rmsnorm_kernel(x_ref, g_ref, o_ref):
    x = x_ref[...]
    o_ref[...] = x * lax.rsqrt(jnp.mean(x * x, axis=-1, keepdims=True) + EPS) * g_ref[...]


def _final_norm(x, g, tm=512):
    m, d = x.shape
    tm = min(tm, m)
    return pl.pallas_call(
        _rmsnorm_kernel, out_shape=jax.ShapeDtypeStruct((m, d), F32), grid=(m // tm,),
        in_specs=[pl.BlockSpec((tm, d), lambda i: (i, 0)), pl.BlockSpec((1, d), lambda i: (0, 0))],
        out_specs=pl.BlockSpec((tm, d), lambda i: (i, 0)),
        compiler_params=_params(("parallel",)), name="final_rmsnorm")(x, g.reshape(1, d))


def _s5_tables(lam_re, lam_im, log_dt, b_re, b_im, c_re, c_im, d_skip):
    f = F32
    groups, p = lam_re.shape
    nblk = groups * S5_GROUP // S5_BLK_CH
    gpb = groups // nblk
    lr, li = lam_re.astype(f), lam_im.astype(f)
    dt = jnp.exp(log_dt.astype(f))[:, None]
    ldr, ldi = lr * dt, li * dt
    kk = jnp.arange(1, S5_SEG + 1, dtype=f)[:, None, None]
    pmag = jnp.exp(kk * ldr)
    pw_re, pw_im = pmag * jnp.cos(kk * ldi), pmag * jnp.sin(kk * ldi)
    ab_re, ab_im = jnp.exp(ldr) * jnp.cos(ldi), jnp.exp(ldr) * jnp.sin(ldi)
    den = lr * lr + li * li
    nr, ni = ab_re - 1.0, ab_im
    fr, fi = (nr * lr + ni * li) / den, (ni * lr - nr * li) / den
    bre, bim = b_re.astype(f), b_im.astype(f)
    bb_re = fr[..., None] * bre - fi[..., None] * bim
    bb_im = fr[..., None] * bim + fi[..., None] * bre
    eye = jnp.eye(gpb, dtype=f)

    def bd_in(bb):
        t = bb.reshape(nblk, gpb, p, S5_GROUP).transpose(0, 1, 3, 2)
        return jnp.einsum('bgkp,gh->bgkhp', t, eye).reshape(nblk, gpb * S5_GROUP, gpb * p).astype(BF16)

    def bd_out(c):
        t = c.astype(f).reshape(nblk, gpb, S5_GROUP, p).transpose(0, 1, 3, 2)
        return jnp.einsum('bgpk,gh->bgphk', t, eye).reshape(nblk, gpb * p, gpb * S5_GROUP).astype(BF16)

    def lanes(t):
        lead = t.shape[:-2]
        t = t.reshape(lead + (nblk, gpb * p))
        return jnp.moveaxis(t, -2, 0)

    return dict(
        bb_re=bd_in(bb_re), bb_im=bd_in(bb_im), c_re=bd_out(c_re), c_im=bd_out(c_im),
        ab_re=lanes(ab_re[None]), ab_im=lanes(ab_im[None]),
        pw_re=lanes(pw_re), pw_im=lanes(pw_im),
        d=d_skip.astype(f).reshape(1, -1), nblk=nblk)


def _s5_perm():
    pm = np.zeros((S5_CHUNK, S5_CHUNK), np.float32)
    r = np.arange(S5_CHUNK)
    pm[r, (r % SUBLANES) * S5_SEG + r // SUBLANES] = 1.0
    return jnp.asarray(pm, BF16), jnp.asarray(pm.T, BF16)


def _s5_scan_kernel(u_ref, pm_ref, pmt_ref, bbre_ref, bbim_ref, cre_ref, cim_ref, abre_ref, abim_ref,
                    pwre_ref, pwim_ref, d_ref, y_ref, sre_out, sim_out,
                    xre, xim, car_re, car_im, cin_re, cin_im, lend_re, lend_im):
    n = pl.program_id(2)
    nst = xre.shape[1]

    @pl.when(n == 0)
    def _():
        car_re[...] = jnp.zeros_like(car_re)
        car_im[...] = jnp.zeros_like(car_im)

    u = u_ref[...]
    up = _dot(pm_ref[...], u.astype(BF16)).astype(BF16)
    xre[...] = _dot(up, bbre_ref[...])
    xim[...] = _dot(up, bbim_ref[...])
    are = jnp.broadcast_to(abre_ref[...], (SUBLANES, nst))
    aim = jnp.broadcast_to(abim_ref[...], (SUBLANES, nst))
    sre = jnp.zeros((SUBLANES, nst), F32)
    sim = jnp.zeros((SUBLANES, nst), F32)
    for i in range(S5_SEG):
        r = slice(SUBLANES * i, SUBLANES * (i + 1))
        nre = are * sre - aim * sim + xre[r, :]
        nim = are * sim + aim * sre + xim[r, :]
        xre[r, :] = nre
        xim[r, :] = nim
        sre, sim = nre, nim
    lend_re[...] = sre
    lend_im[...] = sim
    a_re = pwre_ref[S5_SEG - 1:S5_SEG, :]
    a_im = pwim_ref[S5_SEG - 1:S5_SEG, :]
    cr, ci = car_re[...], car_im[...]
    for s in range(SUBLANES):
        cin_re[s:s + 1, :] = cr
        cin_im[s:s + 1, :] = ci
        lr, li = lend_re[s:s + 1, :], lend_im[s:s + 1, :]
        cr, ci = a_re * cr - a_im * ci + lr, a_re * ci + a_im * cr + li
    car_re[...] = cr
    car_im[...] = ci
    cinr, cini = cin_re[...], cin_im[...]
    for i in range(S5_SEG):
        r = slice(SUBLANES * i, SUBLANES * (i + 1))
        pr, pi_ = pwre_ref[i:i + 1, :], pwim_ref[i:i + 1, :]
        xre[r, :] = xre[r, :] + (pr * cinr - pi_ * cini)
        xim[r, :] = xim[r, :] + (pr * cini + pi_ * cinr)
    yp = _dot(xre[...].astype(BF16), cre_ref[...]) - _dot(xim[...].astype(BF16), cim_ref[...])
    hi = yp.astype(BF16)
    lo = (yp - hi.astype(F32)).astype(BF16)
    y = _dot(pmt_ref[...], hi) + _dot(pmt_ref[...], lo) + d_ref[...] * u
    y_ref[...] = _gelu(y)

    @pl.when(n == pl.num_programs(2) - 1)
    def _():
        sre_out[...] = cr
        sim_out[...] = ci


def _s5_scan(proj, tabs, batch, seq):
    nblk = tabs['nblk']
    nch = seq // S5_CHUNK
    pm, pmt = _s5_perm()
    nstate = nblk * S5_BLK_ST
    const3 = lambda shape: pl.BlockSpec((None,) + shape, lambda k, b, n: (k, 0, 0))
    y, sre, sim = pl.pallas_call(
        _s5_scan_kernel,
        out_shape=(jax.ShapeDtypeStruct((batch * seq, nblk * S5_BLK_CH), F32),
                   jax.ShapeDtypeStruct((batch, 1, nstate), F32),
                   jax.ShapeDtypeStruct((batch, 1, nstate), F32)),
        grid=(nblk, batch, nch),
        in_specs=[pl.BlockSpec((S5_CHUNK, S5_BLK_CH), lambda k, b, n: (b * nch + n, k)),
                  pl.BlockSpec((S5_CHUNK, S5_CHUNK), lambda k, b, n: (0, 0)),
                  pl.BlockSpec((S5_CHUNK, S5_CHUNK), lambda k, b, n: (0, 0)),
                  const3((S5_BLK_CH, S5_BLK_ST)), const3((S5_BLK_CH, S5_BLK_ST)),
                  const3((S5_BLK_ST, S5_BLK_CH)), const3((S5_BLK_ST, S5_BLK_CH)),
                  const3((1, S5_BLK_ST)), const3((1, S5_BLK_ST)),
                  const3((S5_SEG, S5_BLK_ST)), const3((S5_SEG, S5_BLK_ST)),
                  pl.BlockSpec((1, S5_BLK_CH), lambda k, b, n: (0, k))],
        out_specs=(pl.BlockSpec((S5_CHUNK, S5_BLK_CH), lambda k, b, n: (b * nch + n, k)),
                   pl.BlockSpec((None, 1, S5_BLK_ST), lambda k, b, n: (b, 0, k)),
                   pl.BlockSpec((None, 1, S5_BLK_ST), lambda k, b, n: (b, 0, k))),
        scratch_shapes=[pltpu.VMEM((S5_CHUNK, S5_BLK_ST), F32), pltpu.VMEM((S5_CHUNK, S5_BLK_ST), F32),
                        pltpu.VMEM((1, S5_BLK_ST), F32), pltpu.VMEM((1, S5_BLK_ST), F32),
                        pltpu.VMEM((SUBLANES, S5_BLK_ST), F32), pltpu.VMEM((SUBLANES, S5_BLK_ST), F32),
                        pltpu.VMEM((SUBLANES, S5_BLK_ST), F32), pltpu.VMEM((SUBLANES, S5_BLK_ST), F32)],
        compiler_params=_params(("parallel", "parallel", "arbitrary")), name="s5_scan")(
            proj, pm, pmt, tabs['bb_re'], tabs['bb_im'], tabs['c_re'], tabs['c_im'],
            tabs['ab_re'], tabs['ab_im'], tabs['pw_re'], tabs['pw_im'], tabs['d'])
    return y, sre, sim


def _s5_step_kernel(u_ref, hre_ref, him_ref, bbre_ref, bbim_ref, cre_ref, cim_ref, abre_ref, abim_ref, d_ref,
                    y_ref, sre_out, sim_out):
    u = u_ref[...]
    ub = u.astype(BF16)
    are, aim = abre_ref[...], abim_ref[...]
    hre, him = hre_ref[...], him_ref[...]
    sre = are * hre - aim * him + _dot(ub, bbre_ref[...])
    sim = are * him + aim * hre + _dot(ub, bbim_ref[...])
    sre_out[...] = sre
    sim_out[...] = sim
    y = _dot(sre.astype(BF16), cre_ref[...]) - _dot(sim.astype(BF16), cim_ref[...]) + d_ref[...] * u
    y_ref[...] = _gelu(y)


def _s5_step(proj, h_re, h_im, tabs):
    nblk = tabs['nblk']
    rows = proj.shape[0]
    nstate = nblk * S5_BLK_ST
    const3 = lambda shape: pl.BlockSpec((None,) + shape, lambda k: (k, 0, 0))
    lane_blk = lambda w: pl.BlockSpec((rows, w), lambda k: (0, k))
    return pl.pallas_call(
        _s5_step_kernel,
        out_shape=(jax.ShapeDtypeStruct((rows, nblk * S5_BLK_CH), F32),
                   jax.ShapeDtypeStruct((rows, nstate), F32), jax.ShapeDtypeStruct((rows, nstate), F32)),
        grid=(nblk,),
        in_specs=[lane_blk(S5_BLK_CH), lane_blk(S5_BLK_ST), lane_blk(S5_BLK_ST),
                  const3((S5_BLK_CH, S5_BLK_ST)), const3((S5_BLK_CH, S5_BLK_ST)),
                  const3((S5_BLK_ST, S5_BLK_CH)), const3((S5_BLK_ST, S5_BLK_CH)),
                  const3((1, S5_BLK_ST)), const3((1, S5_BLK_ST)),
                  pl.BlockSpec((1, S5_BLK_CH), lambda k: (0, k))],
        out_specs=(lane_blk(S5_BLK_CH), lane_blk(S5_BLK_ST), lane_blk(S5_BLK_ST)),
        compiler_params=_params(("parallel",)), name="s5_step")(
            proj, h_re.reshape(rows, nstate), h_im.reshape(rows, nstate),
            tabs['bb_re'], tabs['bb_im'], tabs['c_re'], tabs['c_im'], tabs['ab_re'], tabs['ab_im'], tabs['d'])


def _s5_weights(w_in, lam_re, lam_im, log_dt, b_re, b_im, c_re, c_im, d_skip, w_glu, b_glu, w_out):
    tabs = _s5_tables(lam_re, lam_im, log_dt, b_re, b_im, c_re, c_im, d_skip)
    return (w_in.astype(BF16), w_glu.astype(BF16), b_glu.astype(F32).reshape(1, -1), w_out.astype(BF16), tabs)


def _glu_epilogue(acc, gy, z, b):
    return gy * _sigmoid(acc + b) * _silu(z)


def _cast_prologue(a):
    return a


def _s5_layer(x, mods, h_state, w, *, prompt, batch, seq, tm):
    g, scale, shift, gate = mods
    w_in, w_glu, b_glu, w_out, tabs = w
    width = w_glu.shape[0]
    proj = _in_proj(x, g, scale, shift, w_in, batch_kind=prompt, rows_per_batch=seq, tm=tm, tn=MM_TN, name="s5_in_proj")
    if prompt:
        gy, sre, sim = _s5_scan(proj, tabs, batch, seq)
    else:
        gy, sre, sim = _s5_step(proj, h_state[0], h_state[1], tabs)
    a = _fused_matmul([('row', gy)], w_glu, [('tile', gy), ('tile', proj, width), ('col', b_glu)],
                      prologue=_cast_prologue, epilogue=_glu_epilogue, out_dtype=BF16, tm=min(tm, MM_TM_WIDE_ROWS),
                      tn=MM_TN, name="s5_glu")
    x_new = _out_proj([('row', a)], w_out, x, gate, batch_kind=prompt, rows_per_batch=seq, tm=tm, tn=MM_TN, name="s5_out_proj")
    return x_new, sre, sim


def _conv_silu_chunk(x_ref, w_ref, b_ref, pad_ref, first):
    c = x_ref.shape[0]

    @pl.when(first)
    def _():
        pad_ref[0:SUBLANES, :] = jnp.zeros((SUBLANES, pad_ref.shape[1]), F32)

    pad_ref[SUBLANES:SUBLANES + c, :] = x_ref[...]
    acc = w_ref[3:4, :] * pad_ref[SUBLANES:SUBLANES + c, :]
    for j in range(CONV_W - 1):
        off = SUBLANES - (CONV_W - 1) + j
        acc = acc + w_ref[j:j + 1, :] * pad_ref[off:off + c, :]
    if b_ref is not None:
        acc = acc + b_ref[...]
    pad_ref[0:SUBLANES, :] = pad_ref[c:c + SUBLANES, :]
    return _silu(acc)


def _conv_silu_step(x, taps_ref, w_ref, b_ref):
    acc = w_ref[3:4, :] * x
    for j in range(CONV_W - 1):
        acc = acc + w_ref[j:j + 1, :] * taps_ref[j]
    if b_ref is not None:
        acc = acc + b_ref[...]
    return _silu(acc)


def _tri_masks(c):
    t = lax.broadcasted_iota(jnp.int32, (c, c), 0)
    s = lax.broadcasted_iota(jnp.int32, (c, c), 1)
    return s <= t, s < t


def _pad_to_square_t(x, n):
    rows = x.shape[0]
    return jnp.concatenate([x, jnp.zeros((n - rows, n), x.dtype)], axis=0).T


M2_HEADDIM = 64
M2_STATE = 128
M2_GROUPS = 8
M2_CHUNK = 128


def _m2_ssd_kernel(x_ref, b_ref, c_ref, dt_ref, z_ref, wx_ref, wb_ref, wc_ref, bx_ref, bb_ref, bc_ref,
                   dtb_ref, nega_ref, dsk_ref, nw_ref, o_ref, sout_ref, xpad, bpad, cpad, s_ref, y_ref):
    n = pl.program_id(1)
    first = n == 0
    c = x_ref.shape[0]
    npairs = s_ref.shape[0]
    pairs_per_group = npairs // M2_GROUPS

    @pl.when(first)
    def _():
        s_ref[...] = jnp.zeros_like(s_ref)

    xs = _conv_silu_chunk(x_ref, wx_ref, bx_ref, xpad, first)
    bm = _conv_silu_chunk(b_ref, wb_ref, bb_ref, bpad, first).astype(BF16)
    cm = _conv_silu_chunk(c_ref, wc_ref, bc_ref, cpad, first).astype(BF16)
    dtv = _softplus(dt_ref[...] + dtb_ref[...])
    la = nega_ref[...] * dtv
    incl, _ = _tri_masks(c)
    tri = jnp.where(incl, 1.0, 0.0).astype(BF16)
    cum = _dot_exact_lhs(tri, la)
    cum_t = cum.T
    ecum_all = jnp.exp(cum)
    wend_all = jnp.exp(cum[c - 1:c, :] - cum)
    elast_t = jnp.exp(cum_t[:, c - 1:c])
    lane_first = lax.broadcasted_iota(jnp.int32, (c, LANES), 1) < M2_HEADDIM
    row_first = lax.broadcasted_iota(jnp.int32, (LANES, LANES), 0) < M2_HEADDIM

    for g in range(M2_GROUPS):
        bg = bm[:, g * M2_STATE:(g + 1) * M2_STATE]
        cg = cm[:, g * M2_STATE:(g + 1) * M2_STATE]
        gm = _dot_nt(cg, bg)
        for j in range(pairs_per_group):
            p = g * pairs_per_group + j
            ha, hb = 2 * p, 2 * p + 1
            xp = xs[:, p * LANES:(p + 1) * LANES]

            def decay_weights(h):
                seg = cum[:, h:h + 1] - cum_t[h:h + 1, :]
                dec = jnp.where(incl, jnp.exp(jnp.where(incl, seg, 0.0)), 0.0)
                return (gm * dec).astype(BF16)

            xdt = xp * jnp.where(lane_first, dtv[:, ha:ha + 1], dtv[:, hb:hb + 1])
            xdt_a = jnp.where(lane_first, xdt, 0.0)
            xdt_b = xdt - xdt_a
            y = _dot(decay_weights(ha), xdt_a.astype(BF16)) + _dot(decay_weights(hb), xdt_b.astype(BF16))
            sp = s_ref[p]
            y = y + _dot_nt(cg, sp.astype(BF16)) * jnp.where(lane_first, ecum_all[:, ha:ha + 1], ecum_all[:, hb:hb + 1])
            y_ref[:, p * LANES:(p + 1) * LANES] = y + dsk_ref[:, p * LANES:(p + 1) * LANES] * xp
            xw = xdt * jnp.where(lane_first, wend_all[:, ha:ha + 1], wend_all[:, hb:hb + 1])
            dmat = jnp.where(row_first, elast_t[ha:ha + 1, :], elast_t[hb:hb + 1, :])
            s_ref[p] = sp * dmat + _dot(xw.T.astype(BF16), bg)

    o_ref[...] = _gated_rmsnorm_prologue(y_ref[...], z_ref[...], nw_ref[...]).astype(o_ref.dtype)

    @pl.when(n == pl.num_programs(1) - 1)
    def _():
        sout_ref[...] = s_ref[...]


def _m2_ssd(proj, gates, mw, batch, seq):
    c = M2_CHUNK
    nch = seq // c
    width = mw['width']
    gs = M2_GROUPS * M2_STATE
    npairs = width // LANES
    xo, bo, co = width // width, (2 * width) // gs, (2 * width + gs) // gs
    row = lambda w, off: pl.BlockSpec((c, w), lambda b, n: (b * nch + n, off))
    par = lambda r, w, off: pl.BlockSpec((r, w), lambda b, n: (0, off))
    return pl.pallas_call(
        _m2_ssd_kernel,
        out_shape=(jax.ShapeDtypeStruct((batch * seq, width), BF16),
                   jax.ShapeDtypeStruct((batch, npairs, LANES, M2_STATE), F32)),
        grid=(batch, nch),
        in_specs=[row(width, xo), row(gs, bo), row(gs, co), row(LANES, 0), row(width, 0),
                  par(CONV_W, width, 0), par(CONV_W, gs, width // gs), par(CONV_W, gs, width // gs + 1),
                  par(1, width, 0), par(1, gs, width // gs), par(1, gs, width // gs + 1),
                  par(1, LANES, 0), par(1, LANES, 0), par(1, width, 0), par(1, width, 0)],
        out_specs=(pl.BlockSpec((c, width), lambda b, n: (b * nch + n, 0)),
                   pl.BlockSpec((None, npairs, LANES, M2_STATE), lambda b, n: (b, 0, 0, 0))),
        scratch_shapes=[pltpu.VMEM((c + SUBLANES, width), F32), pltpu.VMEM((c + SUBLANES, gs), F32),
                        pltpu.VMEM((c + SUBLANES, gs), F32), pltpu.VMEM((npairs, LANES, M2_STATE), F32),
                        pltpu.VMEM((c, width), F32)],
        compiler_params=_params(("parallel", "arbitrary")), name="m2_ssd")(
            proj, proj, proj, gates, proj, mw['conv_w'], mw['conv_w'], mw['conv_w'], mw['conv_b'], mw['conv_b'],
            mw['conv_b'], mw['dt_bias'], mw['neg_a'], mw['d_ch'], mw['norm'])


def _m2_pre_step_kernel(xbc_ref, dt_ref, taps_ref, w_ref, b_ref, dtb_ref, nega_ref, act_ref, dtv_ref, dec_ref):
    act_ref[...] = _conv_silu_step(xbc_ref[...], taps_ref, w_ref, b_ref)
    dtv = _softplus(dt_ref[...] + dtb_ref[...])
    dtv_ref[...] = dtv
    dec_ref[...] = jnp.exp(nega_ref[...] * dtv)


def _m2_pre_step(proj, dt_raw, taps, mw):
    rows = proj.shape[0]
    width = mw['width']
    cdim = mw['conv_w'].shape[1]
    xbc = proj[:, width:width + cdim]
    return pl.pallas_call(
        _m2_pre_step_kernel,
        out_shape=(jax.ShapeDtypeStruct((rows, cdim), F32), jax.ShapeDtypeStruct((rows, LANES), F32),
                   jax.ShapeDtypeStruct((rows, LANES), F32)),
        name="m2_pre_step", compiler_params=pltpu.CompilerParams(vmem_limit_bytes=V7X_VMEM_LIMIT_BYTES))(
            xbc, dt_raw, taps, mw['conv_w'], mw['conv_b'], mw['dt_bias'], mw['neg_a'])


def _m2_state_step_kernel(x_ref, dtc_ref, decc_ref, b_ref, c_ref, dsk_ref, s_ref, y_ref, sout_ref):
    rows = x_ref.shape[0]
    pairs = x_ref.shape[1] // LANES
    bv = b_ref[...]
    cb = c_ref[...].astype(BF16)
    for j in range(pairs):
        sl = slice(j * LANES, (j + 1) * LANES)
        xp = x_ref[:, sl]
        xdt_t = _pad_to_square_t(xp * dtc_ref[:, sl], LANES)
        dec_t = _pad_to_square_t(decc_ref[:, sl], LANES)
        for b in range(rows):
            sp = s_ref[b, j]
            s_new = sp * dec_t[:, b:b + 1] + xdt_t[:, b:b + 1] * bv[b:b + 1, :]
            sout_ref[b, j] = s_new
            y_ref[b:b + 1, sl] = _dot_nt(cb[b:b + 1, :], s_new.astype(BF16)) + dsk_ref[:, sl] * xp[b:b + 1, :]


def _m2_state_step(act, dt_ch, dec_ch, ssm, mw):
    rows = act.shape[0]
    width = mw['width']
    gw = width // M2_GROUPS
    ppg = gw // LANES
    npairs = width // LANES
    gs = M2_GROUPS * M2_STATE
    s4 = ssm.reshape(rows, npairs, LANES, M2_STATE)
    blk = lambda w, base: pl.BlockSpec((rows, w), lambda g: (0, base + g))
    return pl.pallas_call(
        _m2_state_step_kernel,
        out_shape=(jax.ShapeDtypeStruct((rows, width), F32), jax.ShapeDtypeStruct(s4.shape, F32)),
        grid=(M2_GROUPS,),
        in_specs=[blk(gw, 0), blk(gw, 0), blk(gw, 0), blk(M2_STATE, width // M2_STATE),
                  blk(M2_STATE, (width + gs) // M2_STATE), pl.BlockSpec((1, gw), lambda g: (0, g)),
                  pl.BlockSpec((rows, ppg, LANES, M2_STATE), lambda g: (0, g, 0, 0))],
        out_specs=(blk(gw, 0), pl.BlockSpec((rows, ppg, LANES, M2_STATE), lambda g: (0, g, 0, 0))),
        compiler_params=_params(("parallel",)), name="m2_state_step")(
            act, dt_ch, dec_ch, act, act, mw['d_ch'], s4)


def _m2_weights(w_in, conv_w, conv_b, dt_bias, a_log, d_skip, norm_w, w_out):
    f = F32
    heads = dt_bias.shape[0]
    width = norm_w.shape[0]
    pad = LANES - heads
    cdim = conv_w.shape[1]
    return dict(
        w_in=_pad_cols(w_in[:, :width + cdim].astype(BF16), 512),
        w_gate=_pad_cols(w_in[:, width + cdim:].astype(f), LANES),
        conv_w=conv_w.astype(f), conv_b=conv_b.astype(f).reshape(1, -1),
        dt_bias=jnp.pad(dt_bias.astype(f), (0, pad)).reshape(1, LANES),
        neg_a=jnp.pad(-jnp.exp(a_log.astype(f)), (0, pad)).reshape(1, LANES),
        d_ch=jnp.repeat(d_skip.astype(f), width // heads).reshape(1, width),
        norm=norm_w.astype(f).reshape(1, width), w_out=w_out.astype(BF16), width=width, heads=heads)


def _gated_rmsnorm_prologue(y, z, w):
    v = y * _silu(z)
    return v * lax.rsqrt(jnp.mean(v * v, axis=-1, keepdims=True) + EPS) * w


def _m2_layer(x, mods, state, mw, *, prompt, batch, seq, tm):
    g, scale, shift, gate = mods
    width, heads = mw['width'], mw['heads']
    cdim = mw['conv_w'].shape[1]
    proj = _in_proj(x, g, scale, shift, mw['w_in'], batch_kind=prompt, rows_per_batch=seq, tm=tm, tn=MM_TN,
                    name="m2_in_proj")
    gates = _gate_proj(x, g, scale, shift, mw['w_gate'], batch_kind=prompt, rows_per_batch=seq, tm=tm,
                       name="m2_gate_proj")
    if prompt:
        a, ssm = _m2_ssd(proj, gates, mw, batch, seq)
        conv_new = proj.reshape(batch, seq, -1)[:, seq - (CONV_W - 1):, width:width + cdim]
        x_new = _out_proj([('row', a)], mw['w_out'], x, gate, batch_kind=True, rows_per_batch=seq, tm=tm, tn=MM_TN,
                          name="m2_out_proj")
        return x_new, conv_new, ssm
    else:
        conv_buf, ssm_in = state
        taps = jnp.swapaxes(conv_buf.astype(F32), 0, 1)
        act, dtv, dec = _m2_pre_step(proj, gates, taps, mw)
        rep = width // heads
        dt_ch = jnp.repeat(dtv[:, :heads], rep, axis=1)
        dec_ch = jnp.repeat(dec[:, :heads], rep, axis=1)
        y, ssm = _m2_state_step(act, dt_ch, dec_ch, ssm_in.astype(F32), mw)
        conv_new = jnp.concatenate([conv_buf.astype(F32)[:, 1:], proj[:, None, width:width + cdim]], axis=1)
    x_new = _out_proj([('row', y), ('row', proj, width, 0), ('vec', mw['norm'])], mw['w_out'], x, gate,
                      batch_kind=prompt, rows_per_batch=seq, tm=min(tm, 256), tn=MM_TN, name="m2_out_proj",
                      prologue=_gated_rmsnorm_prologue)
    return x_new, conv_new, ssm


GD_DK = 128
GD_DV = 128
GD_CHUNK = 64


def _dot_3pass(a, b):
    ah = a.astype(BF16)
    al = (a - ah.astype(F32)).astype(BF16)
    bh = b.astype(BF16)
    bl = (b - bh.astype(F32)).astype(BF16)
    return _dot(ah, bh) + (_dot(ah, bl) + _dot(al, bh))


def _l2norm_rows(x):
    return x * lax.rsqrt(jnp.sum(x * x, axis=-1, keepdims=True) + EPS)


def _rmsnorm_rows(x, w):
    return x * lax.rsqrt(jnp.mean(x * x, axis=-1, keepdims=True) + EPS) * w


GD_INV_BASE = 16


def _bdot(a, b):
    return jnp.einsum('hmk,hkn->hmn', a, b, preferred_element_type=F32)


def _bdot_nt(a, b):
    return jnp.einsum('hmk,hnk->hmn', a, b, preferred_element_type=F32)


def _bdot_3pass(a, b):
    ah = a.astype(BF16)
    al = (a - ah.astype(F32)).astype(BF16)
    bh = b.astype(BF16)
    bl = (b - bh.astype(F32)).astype(BF16)
    return _bdot(ah, bh) + (_bdot(ah, bl) + _bdot(al, bh))


def _unit_lower_inverse(a_strict):
    c = a_strict.shape[-1]
    row = lax.broadcasted_iota(jnp.int32, (c, c), 0)
    col = lax.broadcasted_iota(jnp.int32, (c, c), 1)
    eye = jnp.where(row == col, 1.0, 0.0)
    blk = GD_INV_BASE
    shift = int(math.log2(blk))
    p = jnp.where((row >> shift) == (col >> shift), -a_strict, 0.0)
    t = eye + p
    for _ in range(shift - 1):
        p = _bdot_3pass(p, p)
        t = t + _bdot_3pass(t, p)
    while blk < c:
        below = jnp.logical_and((row >> (shift + 1)) == (col >> (shift + 1)), (row >> shift) != (col >> shift))
        b = jnp.where(below, a_strict, 0.0)
        t = t - _bdot_3pass(_bdot_3pass(t, b), t)
        blk *= 2
        shift += 1
    return t


def _gd_chunk_kernel(qkv_ref, z_ref, braw_ref, araw_ref, cw_ref, nega_ref, dtb_ref, nw_ref,
                     o_ref, sout_ref, pad, s_ref):
    n = pl.program_id(1)
    first = n == 0
    c = qkv_ref.shape[0]
    hv = s_ref.shape[0]
    hk = hv // 2
    rep = hv // hk

    @pl.when(first)
    def _():
        s_ref[...] = jnp.zeros_like(s_ref)

    qkv = _conv_silu_chunk(qkv_ref, cw_ref, None, pad, first)
    beta = _sigmoid(braw_ref[...])
    gl = nega_ref[...] * _softplus(araw_ref[...] + dtb_ref[...])
    incl, strict = _tri_masks(c)
    tri = jnp.where(incl, 1.0, 0.0).astype(BF16)
    gcum = _dot_exact_lhs(tri, gl)
    gcum_t = jnp.concatenate([gcum, jnp.zeros((LANES - c, LANES), F32)], axis=0).T

    heads = range(hv)
    per_value_head = lambda t: jnp.stack([t[h // rep] for h in heads])
    q3 = jnp.stack([qkv[:, i * GD_DK:(i + 1) * GD_DK] for i in range(hk)])
    k3 = jnp.stack([qkv[:, (hk + i) * GD_DK:(hk + i + 1) * GD_DK] for i in range(hk)])
    v3 = jnp.stack([qkv[:, (2 * hk + h) * GD_DV:(2 * hk + h + 1) * GD_DV] for h in heads])
    q3 = _l2norm_rows(q3) * (GD_DK ** -0.5)
    k3 = _l2norm_rows(k3)
    k3b = k3.astype(BF16)
    kk = per_value_head(_bdot_nt(k3b, k3b))
    qk = per_value_head(_bdot_nt(q3.astype(BF16), k3b))
    q_v, k_v = per_value_head(q3), per_value_head(k3)
    colv = jnp.stack([gcum[:, h:h + 1] for h in heads])
    rowv = jnp.stack([gcum_t[h:h + 1, :c] for h in heads])
    bcol = jnp.stack([beta[:, h:h + 1] for h in heads])
    glast = colv[:, c - 1:c, :]
    ecol = jnp.exp(colv)
    dec = jnp.where(incl, jnp.exp(jnp.where(incl, colv - rowv, 0.0)), 0.0)
    a = jnp.where(strict, (bcol * kk) * dec, 0.0)
    tinv = _unit_lower_inverse(a)
    rhs = jnp.concatenate([v3 * bcol, (k_v * bcol) * ecol], axis=-1)
    sol = _bdot_3pass(tinv, rhs)
    u, w = sol[:, :, :GD_DV], sol[:, :, GD_DV:]
    s = s_ref[...]
    sb = s.astype(BF16)
    v_new = u - _bdot(w.astype(BF16), sb)
    o = _bdot((q_v * ecol).astype(BF16), sb) + _bdot((qk * dec).astype(BF16), v_new.astype(BF16))
    zpad = jnp.zeros((hv, LANES - c, GD_DV), F32)
    kd_t = jnp.swapaxes(jnp.concatenate([k_v * jnp.exp(glast - colv), zpad], axis=1), 1, 2)
    vn_pad = jnp.concatenate([v_new, zpad], axis=1)
    s_ref[...] = s * jnp.exp(glast) + _bdot(kd_t.astype(BF16), vn_pad.astype(BF16))
    on = _rmsnorm_rows(o, nw_ref[...])
    for h in heads:
        sl = slice(h * GD_DV, (h + 1) * GD_DV)
        o_ref[:, sl] = (on[h] * _silu(z_ref[:, sl])).astype(o_ref.dtype)

    @pl.when(n == pl.num_programs(1) - 1)
    def _():
        sout_ref[...] = s_ref[...]


def _gd_chunked(proj, gates, gw, batch, seq):
    c = GD_CHUNK
    nch = seq // c
    cdim, width, hv = gw['cdim'], gw['width'], gw['hv']
    row = lambda w, off: pl.BlockSpec((c, w), lambda b, n: (b * nch + n, off))
    par = lambda r, w: pl.BlockSpec((r, w), lambda b, n: (0, 0))
    return pl.pallas_call(
        _gd_chunk_kernel,
        out_shape=(jax.ShapeDtypeStruct((batch * seq, width), BF16),
                   jax.ShapeDtypeStruct((batch, hv, GD_DK, GD_DV), F32)),
        grid=(batch, nch),
        in_specs=[row(cdim, 0), row(width, cdim // width), row(LANES, 0), row(LANES, 1),
                  par(CONV_W, cdim), par(1, LANES), par(1, LANES), par(1, GD_DV)],
        out_specs=(pl.BlockSpec((c, width), lambda b, n: (b * nch + n, 0)),
                   pl.BlockSpec((None, hv, GD_DK, GD_DV), lambda b, n: (b, 0, 0, 0))),
        scratch_shapes=[pltpu.VMEM((c + SUBLANES, cdim), F32), pltpu.VMEM((hv, GD_DK, GD_DV), F32)],
        compiler_params=_params(("parallel", "arbitrary")), name="gd_chunked")(
            proj, proj, gates, gates, gw['conv_w'], gw['neg_a'], gw['dt_bias'], gw['norm'])


def _gd_pre_step_kernel(qkv_ref, braw_ref, araw_ref, taps_ref, cw_ref, nega_ref, dtb_ref,
                        q_ref, k_ref, v_ref, beta_ref, eg_ref):
    hk = q_ref.shape[1] // GD_DK
    act = _conv_silu_step(qkv_ref[...], taps_ref, cw_ref, None)
    for kh in range(hk):
        sl = slice(kh * GD_DK, (kh + 1) * GD_DK)
        q_ref[:, sl] = _l2norm_rows(act[:, kh * GD_DK:(kh + 1) * GD_DK]) * (GD_DK ** -0.5)
        k_ref[:, sl] = _l2norm_rows(act[:, (hk + kh) * GD_DK:(hk + kh + 1) * GD_DK])
    v_ref[...] = act[:, 2 * hk * GD_DK:]
    beta_ref[...] = _sigmoid(braw_ref[...])
    eg_ref[...] = jnp.exp(nega_ref[...] * _softplus(araw_ref[...] + dtb_ref[...]))


def _gd_pre_step(proj, gates, taps, gw):
    rows = proj.shape[0]
    cdim, width, hv = gw['cdim'], gw['width'], gw['hv']
    qk_w = (cdim - width) // 2
    sd = lambda w: jax.ShapeDtypeStruct((rows, w), F32)
    return pl.pallas_call(
        _gd_pre_step_kernel, out_shape=(sd(qk_w), sd(qk_w), sd(width), sd(LANES), sd(LANES)),
        name="gd_pre_step", compiler_params=pltpu.CompilerParams(vmem_limit_bytes=V7X_VMEM_LIMIT_BYTES))(
            proj[:, :cdim], gates[:, :LANES], gates[:, LANES:], taps,
            gw['conv_w'], gw['neg_a'], gw['dt_bias'])


def _gd_state_step_kernel(q_ref, k_ref, v_ref, beta_ref, eg_ref, z_ref, nw_ref, s_ref, o_ref, sout_ref):
    rows = q_ref.shape[0]
    nk = q_ref.shape[1] // GD_DK
    rep = (v_ref.shape[1] // GD_DV) // nk
    nw = nw_ref[...]
    zrows = jnp.zeros((SUBLANES - 2, GD_DK), F32)
    for kh in range(nk):
        ksl = slice(kh * GD_DK, (kh + 1) * GD_DK)
        q8, k8 = q_ref[:, ksl], k_ref[:, ksl]
        k_t = _pad_to_square_t(k8, GD_DK)
        for b in range(rows):
            qb, kb = q8[b:b + 1, :], k8[b:b + 1, :]
            kq = jnp.concatenate([kb, qb, zrows], axis=0).astype(BF16)
            qk = jnp.sum(qb * kb, axis=-1, keepdims=True)
            for r in range(rep):
                h = kh * rep + r
                vsl = slice(h * GD_DV, (h + 1) * GD_DV)
                s = s_ref[b, h]
                ks_qs = _dot(kq, s.astype(BF16))
                eg = eg_ref[b:b + 1, vsl]
                beta = beta_ref[b:b + 1, vsl]
                v_new = beta * (v_ref[b:b + 1, vsl] - eg * ks_qs[0:1, :])
                o = eg * ks_qs[1:2, :] + qk * v_new
                sout_ref[b, h] = s * eg[:, 0:1] + k_t[:, b:b + 1] * v_new
                o_ref[b:b + 1, vsl] = _rmsnorm_rows(o, nw) * _silu(z_ref[b:b + 1, vsl])


def _gd_state_step(proj, qn, kn, v, beta_ch, eg_ch, state, gw, heads_per_step=4):
    rows = qn.shape[0]
    cdim, width, hv = gw['cdim'], gw['width'], gw['hv']
    steps = hv // heads_per_step
    kw = qn.shape[1] // steps
    vw = width // steps
    blk = lambda w, base=0: pl.BlockSpec((rows, w), lambda g, base=base: (0, base + g))
    sspec = pl.BlockSpec((rows, heads_per_step, GD_DK, GD_DV), lambda g: (0, g, 0, 0))
    return pl.pallas_call(
        _gd_state_step_kernel,
        out_shape=(jax.ShapeDtypeStruct((rows, width), F32), jax.ShapeDtypeStruct(state.shape, F32)),
        grid=(steps,),
        in_specs=[blk(kw), blk(kw), blk(vw), blk(vw), blk(vw), blk(vw, cdim // vw),
                  pl.BlockSpec((1, GD_DV), lambda g: (0, 0)), sspec],
        out_specs=(blk(vw), sspec),
        compiler_params=_params(("parallel",)), name="gd_state_step")(
            qn, kn, v, beta_ch, eg_ch, proj, gw['norm'], state)


def _gd_weights(w_in, conv_w, a_log, dt_bias, norm_w, w_out):
    f = F32
    hv = a_log.shape[0]
    cdim = conv_w.shape[1]
    width = w_out.shape[0]
    pad = LANES - hv
    base = cdim + width
    zeros = jnp.zeros((w_in.shape[0], pad), w_in.dtype)
    w_gate = jnp.concatenate([w_in[:, base:base + hv], zeros, w_in[:, base + hv:], zeros], axis=1)
    return dict(
        w_in=_pad_cols(w_in[:, :base].astype(BF16), 512), w_gate=w_gate.astype(f), conv_w=conv_w.astype(f),
        neg_a=jnp.pad(-jnp.exp(a_log.astype(f)), (0, pad)).reshape(1, LANES),
        dt_bias=jnp.pad(dt_bias.astype(f), (0, pad)).reshape(1, LANES),
        norm=norm_w.astype(f).reshape(1, -1), w_out=w_out.astype(BF16), cdim=cdim, width=width, hv=hv)


def _gd_layer(x, mods, state, gw, *, prompt, batch, seq, tm):
    g, scale, shift, gate = mods
    cdim, width, hv = gw['cdim'], gw['width'], gw['hv']
    proj = _in_proj(x, g, scale, shift, gw['w_in'], batch_kind=prompt, rows_per_batch=seq, tm=tm, tn=MM_TN,
                    name="gd_in_proj")
    gates = _gate_proj(x, g, scale, shift, gw['w_gate'], batch_kind=prompt, rows_per_batch=seq, tm=tm,
                       name="gd_gate_proj")
    if prompt:
        a, ssm = _gd_chunked(proj, gates, gw, batch, seq)
        conv_new = proj.reshape(batch, seq, -1)[:, seq - (CONV_W - 1):, :cdim]
        x_new = _out_proj([('row', a)], gw['w_out'], x, gate, batch_kind=True, rows_per_batch=seq, tm=tm, tn=MM_TN,
                          name="gd_out_proj")
    else:
        conv_buf, ssm_in = state
        taps = jnp.swapaxes(conv_buf.astype(F32), 0, 1)
        qn, kn, v, beta, eg = _gd_pre_step(proj, gates, taps, gw)
        beta_ch = jnp.repeat(beta[:, :hv], GD_DV, axis=1)
        eg_ch = jnp.repeat(eg[:, :hv], GD_DV, axis=1)
        a, ssm = _gd_state_step(proj, qn, kn, v, beta_ch, eg_ch, ssm_in.astype(F32), gw)
        conv_new = jnp.concatenate([conv_buf.astype(F32)[:, 1:], proj[:, None, :cdim]], axis=1)
        x_new = _out_proj([('row', a)], gw['w_out'], x, gate, batch_kind=False, rows_per_batch=seq, tm=tm, tn=MM_TN,
                          name="gd_out_proj", prologue=_cast_prologue)
    return x_new, conv_new, ssm


AT_DIM = 128
IDX_DIM = 128
TOPK_MAX = 256
REL_BUCKETS = 32
REL_MAX_DIST = 128
AT_TILE = 256
INT32_MIN = -2 ** 31
_NEG_BITS = int(np.float32(NEG).view(np.int32))
NEG_SORT_KEY = _NEG_BITS ^ 0x7FFFFFFF if _NEG_BITS < 0 else _NEG_BITS


def _bucket_starts():
    d = np.arange(0, REL_MAX_DIST + 1)
    exact = REL_BUCKETS // 2
    far = exact + (np.log(np.maximum(d, 1).astype(np.float32) / exact) / math.log(REL_MAX_DIST / exact)
                   * (REL_BUCKETS - exact)).astype(np.int32)
    bucket = np.where(d < exact, d, np.minimum(far, REL_BUCKETS - 1))
    assert np.all(np.diff(bucket) >= 0) and bucket[-1] == REL_BUCKETS - 1
    return [int(np.argmax(bucket >= b)) for b in range(REL_BUCKETS)]


def _bias_from_dist(dist, value_of_bucket):
    starts = _bucket_starts()
    val = value_of_bucket(REL_BUCKETS - 1)
    for b in range(REL_BUCKETS - 2, -1, -1):
        val = jnp.where(dist < starts[b + 1], value_of_bucket(b), val)
    return val


def _sort_key(x):
    x = jnp.where(x == 0.0, 0.0, x)
    b = pltpu.bitcast(x, jnp.int32)
    return jnp.where(b < 0, b ^ jnp.int32(0x7FFFFFFF), b)


def _kth_largest_key(count_ge, shape, k):
    def body(it, ans):
        cand = ans | jnp.left_shift(jnp.int32(1), 31 - it)
        cnt = count_ge(cand ^ jnp.int32(INT32_MIN))
        return jnp.where(cnt >= k, cand, ans)

    ans = lax.fori_loop(0, 32, body, jnp.zeros(shape, jnp.int32))
    return ans ^ jnp.int32(INT32_MIN)


def _relbias_tiles_kernel(rb_ref, o_ref):
    delta = pl.program_id(0) * AT_TILE
    h = pl.program_id(1)
    i = lax.broadcasted_iota(jnp.int32, (AT_TILE, AT_TILE), 0)
    j = lax.broadcasted_iota(jnp.int32, (AT_TILE, AT_TILE), 1)
    o_ref[...] = _bias_from_dist(delta + i - j, lambda b: rb_ref[b, h])


def _relbias_tiles(rel_bias):
    heads = rel_bias.shape[1]
    ntile = 3
    assert (ntile - 1) * AT_TILE - (AT_TILE - 1) >= REL_MAX_DIST
    return pl.pallas_call(
        _relbias_tiles_kernel, out_shape=jax.ShapeDtypeStruct((ntile, heads, AT_TILE, AT_TILE), F32),
        grid=(ntile, heads),
        in_specs=[pl.BlockSpec(memory_space=pltpu.SMEM)],
        out_specs=pl.BlockSpec((None, None, AT_TILE, AT_TILE), lambda d, h: (d, h, 0, 0)),
        compiler_params=_params(("parallel", "parallel")), name="at_relbias_tiles")(rel_bias.astype(F32))


def _at_index_kernel(qi_ref, wi_ref, ki_ref, o_ref, keys, cnt, *, k_sel, score_scale):
    qb = pl.program_id(1)
    tq = qi_ref.shape[0]
    nkb = keys.shape[0]
    tk = keys.shape[2]
    nh = qi_ref.shape[1] // IDX_DIM
    wsc = wi_ref[...] * score_scale
    qpos = qb * tq + lax.broadcasted_iota(jnp.int32, (tq, tk), 0)
    kloc = lax.broadcasted_iota(jnp.int32, (tq, tk), 1)
    neg_key = _sort_key(jnp.full((tq, tk), NEG, F32))

    for kb in range(nkb):
        @pl.when(kb <= qb)
        def _():
            kblk = ki_ref[kb * tk:(kb + 1) * tk, :].astype(BF16)
            sc = jnp.zeros((tq, tk), F32)
            for h in range(nh):
                d = _dot_nt(qi_ref[:, h * IDX_DIM:(h + 1) * IDX_DIM].astype(BF16), kblk)
                sc = sc + wsc[:, h:h + 1] * jnp.maximum(d, 0.0)
            adm = kb * tk + kloc <= qpos
            keys[kb] = _sort_key(jnp.where(adm, sc, NEG))

        @pl.when(kb > qb)
        def _():
            keys[kb] = neg_key

    def count_ge(t):
        cnt[...] = jnp.where(keys[0] >= t, 1, 0)
        for kb in range(1, nkb):
            @pl.when(kb <= qb)
            def _():
                cnt[...] += jnp.where(keys[kb] >= t, 1, 0)
        beyond = (nkb - 1 - qb) * tk
        return jnp.sum(cnt[...], axis=1, keepdims=True) + jnp.where(t <= NEG_SORT_KEY, beyond, 0)

    thr = _kth_largest_key(count_ge, (tq, 1), k_sel)
    n_ge = count_ge(thr)
    has_ties = jnp.max(n_ge) > k_sel

    @pl.when(jnp.logical_not(has_ties))
    def _():
        for kb in range(nkb):
            adm = kb * tk + kloc <= qpos
            sel = jnp.logical_and(keys[kb] >= thr, adm)
            o_ref[:, kb * tk:(kb + 1) * tk] = jnp.where(sel, 0.0, NEG).astype(o_ref.dtype)

    @pl.when(has_ties)
    def _():
        acc = jnp.zeros((tq, tk), jnp.int32)
        for kb in range(nkb):
            acc = acc + jnp.where(keys[kb] > thr, 1, 0)
        room = (k_sel - jnp.sum(acc, axis=1, keepdims=True)).astype(F32)
        upper = jnp.where(lax.broadcasted_iota(jnp.int32, (tk, tk), 0) <= lax.broadcasted_iota(jnp.int32, (tk, tk), 1),
                          1.0, 0.0).astype(BF16)
        seen = jnp.zeros((tq, 1), F32)
        for kb in range(nkb):
            key = keys[kb]
            eq = key == thr
            eqf = jnp.where(eq, 1.0, 0.0)
            rank = seen + _dot(eqf.astype(BF16), upper)
            seen = seen + jnp.sum(eqf, axis=1, keepdims=True)
            adm = kb * tk + kloc <= qpos
            sel = jnp.logical_and(jnp.logical_or(key > thr, jnp.logical_and(eq, rank <= room)), adm)
            o_ref[:, kb * tk:(kb + 1) * tk] = jnp.where(sel, 0.0, NEG).astype(o_ref.dtype)


def _at_index(proj, aw, batch, seq, k_sel):
    tq = tk = AT_TILE
    nq = seq // tq
    width = aw['width']
    nh = aw['idx_heads']
    qio = (4 * width) // (nh * IDX_DIM)
    kio = (4 * width + nh * IDX_DIM) // IDX_DIM
    kern = functools.partial(_at_index_kernel, k_sel=k_sel, score_scale=IDX_DIM ** -0.5 * nh ** -0.5)
    return pl.pallas_call(
        kern, out_shape=jax.ShapeDtypeStruct((batch * seq, seq), BF16), grid=(batch, nq),
        in_specs=[pl.BlockSpec((tq, nh * IDX_DIM), lambda b, q: (b * nq + q, qio)),
                  pl.BlockSpec((tq, LANES), lambda b, q: (b * nq + q, kio + 1)),
                  pl.BlockSpec((seq, IDX_DIM), lambda b, q: (b, kio))],
        out_specs=pl.BlockSpec((tq, seq), lambda b, q: (b * nq + q, 0)),
        scratch_shapes=[pltpu.VMEM((seq // tk, tq, tk), jnp.int32), pltpu.VMEM((tq, tk), jnp.int32)],
        compiler_params=_params(("parallel", "parallel")), name="at_index")(proj, proj, proj)


def _at_attend_kernel(q_ref, k_ref, v_ref, z_ref, mask_ref, bias_ref, o_ref, acc, m_scr, l_scr, *, scale):
    qb, kb = pl.program_id(1), pl.program_id(2)
    nh = q_ref.shape[1] // AT_DIM

    @pl.when(kb == 0)
    def _():
        acc[...] = jnp.zeros_like(acc)
        m_scr[...] = jnp.full(m_scr.shape, NEG, F32)
        l_scr[...] = jnp.zeros_like(l_scr)

    @pl.when(kb <= qb)
    def _():
        madd = mask_ref[...].astype(F32)
        for h in range(nh):
            sl = slice(h * AT_DIM, (h + 1) * AT_DIM)
            s = _dot_nt((q_ref[:, sl] * scale).astype(BF16), k_ref[:, sl].astype(BF16)) + bias_ref[h] + madd
            m_old = m_scr[h]
            m_new = jnp.maximum(m_old, jnp.max(s, axis=1, keepdims=True))
            alpha = jnp.exp(m_old - m_new)
            p = jnp.where(madd < 0.0, 0.0, jnp.exp(s - m_new))
            l_scr[h] = alpha * l_scr[h] + jnp.sum(p, axis=1, keepdims=True)
            acc[:, sl] = alpha * acc[:, sl] + _dot(p.astype(BF16), v_ref[:, sl].astype(BF16))
            m_scr[h] = m_new

    @pl.when(kb == qb)
    def _():
        for h in range(nh):
            sl = slice(h * AT_DIM, (h + 1) * AT_DIM)
            o_ref[:, sl] = (acc[:, sl] / l_scr[h] * _silu(z_ref[:, sl])).astype(o_ref.dtype)


def _at_attend(proj, maskadd, tiles, aw, batch, seq):
    t = AT_TILE
    nq = seq // t
    width = aw['width']
    nh = width // AT_DIM
    kern = functools.partial(_at_attend_kernel, scale=AT_DIM ** -0.5)
    kc = lambda q, k: jnp.minimum(k, q)
    return pl.pallas_call(
        kern, out_shape=jax.ShapeDtypeStruct((batch * seq, width), BF16), grid=(batch, nq, nq),
        in_specs=[pl.BlockSpec((t, width), lambda b, q, k: (b * nq + q, 0)),
                  pl.BlockSpec((t, width), lambda b, q, k: (b * nq + kc(q, k), 1)),
                  pl.BlockSpec((t, width), lambda b, q, k: (b * nq + kc(q, k), 2)),
                  pl.BlockSpec((t, width), lambda b, q, k: (b * nq + q, 3)),
                  pl.BlockSpec((t, t), lambda b, q, k: (b * nq + q, kc(q, k))),
                  pl.BlockSpec((None, nh, t, t), lambda b, q, k: (jnp.clip(q - k, 0, 2), 0, 0, 0))],
        out_specs=pl.BlockSpec((t, width), lambda b, q, k: (b * nq + q, 0)),
        scratch_shapes=[pltpu.VMEM((t, width), F32), pltpu.VMEM((nh, t, 1), F32), pltpu.VMEM((nh, t, 1), F32)],
        compiler_params=_params(("parallel", "parallel", "arbitrary")), name="at_attend")(
            proj, proj, proj, proj, maskadd, tiles)


def _at_weights(w_in, rel_bias, w_out):
    width = w_out.shape[0]
    heads = rel_bias.shape[1]
    idx_heads = (w_in.shape[1] - 4 * width - IDX_DIM) // (IDX_DIM + 1)
    return dict(w_in=_pad_cols(w_in.astype(BF16), 512), rel_bias=rel_bias.astype(F32), w_out=w_out.astype(BF16),
                width=width, heads=heads, idx_heads=idx_heads)


def _at_rows(proj, aw, lead):
    width, heads = aw['width'], aw['heads']
    k = proj[:, width:2 * width].reshape(lead + (heads, AT_DIM))
    v = proj[:, 2 * width:3 * width].reshape(lead + (heads, AT_DIM))
    o = 4 * width + aw['idx_heads'] * IDX_DIM
    ki = proj[:, o:o + IDX_DIM].reshape(lead + (IDX_DIM,))
    return k, v, ki


def _at_layer_prompt(x, mods, aw, *, batch, seq, tm):
    g, scale, shift, gate = mods
    proj = _in_proj(x, g, scale, shift, aw['w_in'], batch_kind=True, rows_per_batch=seq, tm=tm, tn=MM_TN,
                    name="at_in_proj")
    k_sel = min(TOPK_MAX, seq // 4)
    maskadd = _at_index(proj, aw, batch, seq, k_sel)
    tiles = _relbias_tiles(aw['rel_bias'])
    a = _at_attend(proj, maskadd, tiles, aw, batch, seq)
    x_new = _out_proj([('row', a)], aw['w_out'], x, gate, batch_kind=True, rows_per_batch=seq, tm=tm, tn=MM_TN,
                      name="at_out_proj")
    return (x_new,) + _at_rows(proj, aw, (batch, seq))


def _at_page_scores_kernel(pt_ref, qi_ref, w_ref, kp_ref, o_ref):
    d = _dot_nt(qi_ref[...].astype(BF16), kp_ref[...].astype(BF16))
    o_ref[...] = jnp.sum(w_ref[...] * jnp.maximum(d, 0.0), axis=0, keepdims=True)


def _at_page_scores(qi3, w3, cache_kidx, page_table):
    rows, nh, _ = qi3.shape
    npages = page_table.shape[1]
    page = cache_kidx.shape[1]
    grid_spec = pltpu.PrefetchScalarGridSpec(
        num_scalar_prefetch=1, grid=(rows, npages),
        in_specs=[pl.BlockSpec((None, nh, IDX_DIM), lambda b, p, pt: (b, 0, 0)),
                  pl.BlockSpec((None, nh, IDX_DIM), lambda b, p, pt: (b, 0, 0)),
                  pl.BlockSpec((None, page, IDX_DIM), lambda b, p, pt: (pt[b, p], 0, 0))],
        out_specs=pl.BlockSpec((None, None, 1, page), lambda b, p, pt: (b, p, 0, 0)))
    return pl.pallas_call(
        _at_page_scores_kernel, out_shape=jax.ShapeDtypeStruct((rows, npages, 1, page), F32), grid_spec=grid_spec,
        compiler_params=_params(("parallel", "arbitrary")), name="at_page_scores")(page_table, qi3, w3, cache_kidx)


def _at_sample_select_kernel(sc_ref, qi_ref, w_ref, kin_ref, rbt_ref, mask_ref, newadd_ref, bias_ref, *, k_sel):
    rows, npages, page = sc_ref.shape
    upper = jnp.where(lax.broadcasted_iota(jnp.int32, (page, page), 0) <= lax.broadcasted_iota(jnp.int32, (page, page), 1),
                      1.0, 0.0).astype(BF16)
    lower = jnp.where(lax.broadcasted_iota(jnp.int32, (npages, npages), 1) < lax.broadcasted_iota(jnp.int32, (npages, npages), 0),
                      1.0, 0.0).astype(BF16)

    def total(x):
        return jnp.sum(jnp.sum(x, axis=1, keepdims=True), axis=0, keepdims=True)

    for b in range(rows):
        keys = _sort_key(sc_ref[b])
        dots = jnp.sum(qi_ref[b] * kin_ref[b:b + 1, :], axis=1, keepdims=True)
        s_new = jnp.sum(w_ref[b][:, 0:1] * jnp.maximum(dots, 0.0), axis=0, keepdims=True)
        key_new = _sort_key(s_new)

        def count_ge(t):
            return total(jnp.where(keys >= t, 1, 0)) + jnp.where(key_new >= t, 1, 0)

        thr = _kth_largest_key(count_ge, (1, 1), k_sel)
        n_gt = total(jnp.where(keys > thr, 1.0, 0.0)) + jnp.where(key_new > thr, 1.0, 0.0)
        room = k_sel - n_gt
        eq = keys == thr
        eqf = jnp.where(eq, 1.0, 0.0)
        row_cnt = jnp.broadcast_to(jnp.sum(eqf, axis=1, keepdims=True), (npages, page))
        rank = _dot(lower, row_cnt.astype(BF16)) + _dot(eqf.astype(BF16), upper)
        sel = jnp.logical_or(keys > thr, jnp.logical_and(eq, rank <= room))
        mask_ref[b] = jnp.where(sel, 0.0, NEG)
        sel_new = jnp.logical_or(key_new > thr, jnp.logical_and(key_new == thr, total(eqf) + 1.0 <= room))
        newadd_ref[b:b + 1, :] = jnp.broadcast_to(jnp.where(sel_new, 0.0, NEG), (1, page))

    nh = rbt_ref.shape[0]
    bias_ref[0] = jnp.broadcast_to(rbt_ref[:, REL_BUCKETS - 1:REL_BUCKETS], (nh, page))
    dist = page - lax.broadcasted_iota(jnp.int32, (nh, page), 1)
    bias_ref[1] = _bias_from_dist(dist, lambda bk: rbt_ref[:, bk:bk + 1])


def _at_sample_select(scores, qi3, w3, ki_new, rbt, k_sel):
    rows, npages, page = scores.shape
    assert page >= REL_MAX_DIST
    kern = functools.partial(_at_sample_select_kernel, k_sel=k_sel)
    return pl.pallas_call(
        kern, out_shape=(jax.ShapeDtypeStruct((rows, npages, page), F32), jax.ShapeDtypeStruct((rows, page), F32),
                         jax.ShapeDtypeStruct((2, rbt.shape[0], page), F32)),
        name="at_sample_select", compiler_params=pltpu.CompilerParams(vmem_limit_bytes=V7X_VMEM_LIMIT_BYTES))(
            scores, qi3, w3, ki_new, rbt)


def _at_sample_attend_kernel(pt_ref, q_ref, k_ref, v_ref, mask_ref, newadd_ref, bias_ref, rbt_ref, kn_ref, vn_ref,
                             z_ref, o_ref, qs, acc, m_scr, l_scr, *, scale):
    p = pl.program_id(1)
    last = pl.num_programs(1) - 1
    page, nh, d = k_ref.shape
    ncol = page * nh
    own = (lax.broadcasted_iota(jnp.int32, (nh, ncol), 1) & (nh - 1)) == lax.broadcasted_iota(jnp.int32, (nh, ncol), 0)

    @pl.when(p == 0)
    def _():
        qs[...] = (q_ref[...] * scale).astype(BF16)
        acc[...] = jnp.zeros_like(acc)
        m_scr[...] = jnp.full(m_scr.shape, NEG, F32)
        l_scr[...] = jnp.zeros_like(l_scr)

    keep = jnp.logical_and(own, mask_ref[...] >= 0.0)
    kf = k_ref[...].reshape(ncol, d).astype(BF16)
    s = jnp.where(keep, _dot_nt(qs[...], kf) + bias_ref[jnp.where(p == last, 1, 0)], NEG)
    m_old = m_scr[...]
    m_new = jnp.maximum(m_old, jnp.max(s, axis=1, keepdims=True))
    alpha = jnp.exp(m_old - m_new)
    pr = jnp.where(keep, jnp.exp(s - m_new), 0.0)
    l_scr[...] = alpha * l_scr[...] + jnp.sum(pr, axis=1, keepdims=True)
    acc[...] = alpha * acc[...] + _dot(pr.astype(BF16), v_ref[...].reshape(ncol, d).astype(BF16))
    m_scr[...] = m_new

    @pl.when(p == last)
    def _():
        nadd = newadd_ref[:, 0:1]
        s_n = jnp.sum(qs[...].astype(F32) * kn_ref[...], axis=1, keepdims=True) + rbt_ref[:, 0:1] + nadd
        m_o = m_scr[...]
        m_n = jnp.maximum(m_o, s_n)
        al = jnp.exp(m_o - m_n)
        p_n = jnp.where(nadd < 0.0, 0.0, jnp.exp(s_n - m_n))
        l_n = al * l_scr[...] + p_n
        o_ref[...] = (al * acc[...] + p_n * vn_ref[...]) / l_n * _silu(z_ref[...])


def _at_sample_attend(q3, cache_k, cache_v, page_table, maskadd, newadd, bias, rbt, kn3, vn3, z3):
    rows, nh, d = q3.shape
    assert nh & (nh - 1) == 0
    npages = page_table.shape[1]
    page = cache_k.shape[1]
    ncol = page * nh
    row3 = pl.BlockSpec((None, nh, d), lambda b, p, pt: (b, 0, 0))
    grid_spec = pltpu.PrefetchScalarGridSpec(
        num_scalar_prefetch=1, grid=(rows, npages),
        in_specs=[row3,
                  pl.BlockSpec((None, page, nh, d), lambda b, p, pt: (pt[b, p], 0, 0, 0)),
                  pl.BlockSpec((None, page, nh, d), lambda b, p, pt: (pt[b, p], 0, 0, 0)),
                  pl.BlockSpec((None, None, 1, ncol), lambda b, p, pt: (b, p, 0, 0)),
                  pl.BlockSpec((None, 1, page), lambda b, p, pt: (b, 0, 0)),
                  pl.BlockSpec((2, nh, ncol), lambda b, p, pt: (0, 0, 0)),
                  pl.BlockSpec(rbt.shape, lambda b, p, pt: (0, 0)),
                  row3, row3, row3],
        out_specs=row3,
        scratch_shapes=[pltpu.VMEM((nh, d), BF16), pltpu.VMEM((nh, d), F32),
                        pltpu.VMEM((nh, 1), F32), pltpu.VMEM((nh, 1), F32)])
    kern = functools.partial(_at_sample_attend_kernel, scale=AT_DIM ** -0.5)
    return pl.pallas_call(
        kern, out_shape=jax.ShapeDtypeStruct((rows, nh, d), F32), grid_spec=grid_spec,
        compiler_params=_params(("parallel", "arbitrary")), name="at_sample_attend")(
            page_table, q3, cache_k, cache_v, maskadd, newadd, bias, rbt, kn3, vn3, z3)


def _at_layer_sample(x, mods, cache_k, cache_v, cache_kidx, page_table, aw):
    g, scale, shift, gate = mods
    rows = x.shape[0]
    width, heads, nh = aw['width'], aw['heads'], aw['idx_heads']
    proj = _in_proj(x, g, scale, shift, aw['w_in'], batch_kind=False, rows_per_batch=1, tm=SUBLANES, tn=MM_TN,
                    name="at_in_proj")
    npool, page = cache_k.shape[:2]
    npages = page_table.shape[1]
    past = npages * page
    k_sel = min(TOPK_MAX, (past + 1) // 4)
    o = 4 * width
    qi3 = proj[:, o:o + nh * IDX_DIM].reshape(rows, nh, IDX_DIM)
    ki_new = proj[:, o + nh * IDX_DIM:o + nh * IDX_DIM + IDX_DIM]
    wi = proj[:, o + nh * IDX_DIM + IDX_DIM:o + nh * IDX_DIM + IDX_DIM + nh] * (IDX_DIM ** -0.5 * nh ** -0.5)
    w3 = jnp.broadcast_to(wi[:, :, None], (rows, nh, IDX_DIM))
    scores = _at_page_scores(qi3, w3, cache_kidx.astype(F32), page_table).reshape(rows, npages, page)
    rbt = aw['rel_bias'].T
    maskadd, newadd, bias = _at_sample_select(scores, qi3, w3, ki_new, rbt, k_sel)
    r3 = lambda t: t.reshape(rows, heads, AT_DIM)
    mask_cols = jnp.repeat(maskadd, heads, axis=-1).reshape(rows, npages, 1, page * heads)
    bias_cols = jnp.repeat(bias, heads, axis=-1)
    a = _at_sample_attend(r3(proj[:, :width]), cache_k.astype(F32), cache_v.astype(F32), page_table,
                          mask_cols, newadd.reshape(rows, 1, page), bias_cols, rbt,
                          r3(proj[:, width:2 * width]), r3(proj[:, 2 * width:3 * width]),
                          r3(proj[:, 3 * width:4 * width]))
    x_new = _out_proj([('row', a.reshape(rows, width))], aw['w_out'], x, gate, batch_kind=False, rows_per_batch=1,
                      tm=SUBLANES, tn=MM_TN, name="at_out_proj", prologue=_cast_prologue)
    return (x_new,) + _at_rows(proj, aw, (rows, 1))


def kernel(x_prompt, x_sample, state_s5_re, state_s5_im, state_m2_conv, state_m2_ssm, state_gd_conv, state_gd_ssm, cache_k, cache_v, cache_kidx, page_table, c_prompt, c_sample, norm_g, w_mod, b_mod, final_g, s5_w_in, s5_lam_re, s5_lam_im, s5_log_dt, s5_b_re, s5_b_im, s5_c_re, s5_c_im, s5_d, s5_w_glu, s5_b_glu, s5_w_out, m2_w_in, m2_conv_w, m2_conv_b, m2_dt_bias, m2_a_log, m2_d, m2_norm, m2_w_out, gd_w_in, gd_conv_w, gd_a_log, gd_dt_bias, gd_norm, gd_w_out, at_w_in, rel_bias, at_w_out):
    f = F32
    bp, seq, d = x_prompt.shape
    bs = x_sample.shape[0]
    depth = norm_g.shape[0]
    xp = x_prompt.astype(f).reshape(bp * seq, d)
    xs = x_sample.astype(f).reshape(bs, d)

    pad_rows = (-(bs + bp)) % SUBLANES
    c_all = jnp.concatenate([c_sample.astype(f), c_prompt.astype(f), jnp.zeros((pad_rows, d), f)], axis=0)
    mod = _modulation(c_all, w_mod, b_mod)

    def mods(i, prompt):
        g = norm_g[i].astype(f).reshape(1, d)
        rows = mod[i, bs:bs + bp] if prompt else mod[i, :bs]
        shift, scale, gate = rows[:, :d], rows[:, d:2 * d], rows[:, 2 * d:]
        if prompt:
            return g, scale[:, None, :], shift[:, None, :], gate[:, None, :]
        return g, scale, shift, gate

    s5w = _s5_weights(s5_w_in, s5_lam_re, s5_lam_im, s5_log_dt, s5_b_re, s5_b_im, s5_c_re, s5_c_im, s5_d,
                      s5_w_glu, s5_b_glu, s5_w_out)
    tm_p = MM_TM

    xp, s5_re_p, s5_im_p = _s5_layer(xp, mods(0, True), None, s5w, prompt=True, batch=bp, seq=seq, tm=tm_p)
    xs, s5_re_s, s5_im_s = _s5_layer(xs, mods(0, False), (state_s5_re, state_s5_im), s5w, prompt=False,
                                     batch=bs, seq=1, tm=SUBLANES)
    groups, nstate = state_s5_re.shape[1:]
    s5_re_p, s5_im_p = s5_re_p.reshape(bp, groups, nstate), s5_im_p.reshape(bp, groups, nstate)
    s5_re_s, s5_im_s = s5_re_s.reshape(bs, groups, nstate), s5_im_s.reshape(bs, groups, nstate)

    m2w = _m2_weights(m2_w_in, m2_conv_w, m2_conv_b, m2_dt_bias, m2_a_log, m2_d, m2_norm, m2_w_out)
    xp, m2_conv_p, m2_ssm_p = _m2_layer(xp, mods(1, True), None, m2w, prompt=True, batch=bp, seq=seq, tm=tm_p)
    xs, m2_conv_s, m2_ssm_s = _m2_layer(xs, mods(1, False), (state_m2_conv, state_m2_ssm), m2w, prompt=False,
                                        batch=bs, seq=1, tm=SUBLANES)
    m2_ssm_p = m2_ssm_p.reshape((bp,) + state_m2_ssm.shape[1:])
    m2_ssm_s = m2_ssm_s.reshape(state_m2_ssm.shape)

    gdw = _gd_weights(gd_w_in, gd_conv_w, gd_a_log, gd_dt_bias, gd_norm, gd_w_out)
    xp, gd_conv_p, gd_ssm_p = _gd_layer(xp, mods(2, True), None, gdw, prompt=True, batch=bp, seq=seq, tm=tm_p)
    xs, gd_conv_s, gd_ssm_s = _gd_layer(xs, mods(2, False), (state_gd_conv, state_gd_ssm), gdw, prompt=False,
                                        batch=bs, seq=1, tm=SUBLANES)

    atw = _at_weights(at_w_in, rel_bias, at_w_out)
    xp, k_rows_p, v_rows_p, kidx_rows_p = _at_layer_prompt(xp, mods(3, True), atw, batch=bp, seq=seq, tm=tm_p)
    xs, k_rows_s, v_rows_s, kidx_rows_s = _at_layer_sample(xs, mods(3, False), cache_k, cache_v, cache_kidx,
                                                           page_table, atw)

    y_prompt = _final_norm(xp, final_g).reshape(x_prompt.shape).astype(x_prompt.dtype)
    y_sample = _final_norm(xs, final_g).reshape(x_sample.shape).astype(x_sample.dtype)
    return (y_prompt, y_sample, s5_re_p, s5_im_p, s5_re_s, s5_im_s, m2_conv_p, m2_ssm_p, m2_conv_s, m2_ssm_s,
            gd_conv_p, gd_ssm_p, gd_conv_s, gd_ssm_s,
            k_rows_p, v_rows_p, kidx_rows_p, k_rows_s, v_rows_s, kidx_rows_s)
```

```python
import functools
import math

import numpy as np
import jax
import jax.numpy as jnp
from jax import lax
from jax.experimental import pallas as pl
from jax.experimental.pallas import tpu as pltpu

F32 = jnp.float32
BF16 = jnp.bfloat16

EPS = 1e-6
NEG = -1e30
CONV_W = 4
V7X_VMEM_LIMIT_BYTES = 56 * 1024 * 1024
LANES = 128
SUBLANES = 8
MM_TM = 1024
MM_TM_WIDE_ROWS = 512
MM_TN = 1024

S5_GROUP = 16
S5_STATE = 64
S5_CHUNK = 256
S5_SEG = S5_CHUNK // SUBLANES
S5_BLK_CH = 256
S5_BLK_ST = 1024


def _params(sem):
    return pltpu.CompilerParams(dimension_semantics=sem, vmem_limit_bytes=V7X_VMEM_LIMIT_BYTES)


def _sigmoid(x):
    return 1.0 / (1.0 + jnp.exp(-x))


def _silu(x):
    return x * _sigmoid(x)


def _gelu(x):
    return 0.5 * x * (1.0 + jnp.tanh(math.sqrt(2.0 / math.pi) * (x + 0.044715 * (x * x * x))))


def _softplus(x):
    return jnp.maximum(x, 0.0) + jnp.log1p(jnp.exp(-jnp.abs(x)))


def _dot(a, b):
    return jnp.dot(a, b, preferred_element_type=F32)


def _dot_nt(a, b):
    return lax.dot_general(a, b, (((1,), (1,)), ((), ())), preferred_element_type=F32)


def _split3(x):
    hi = x.astype(BF16)
    r1 = x - hi.astype(F32)
    mid = r1.astype(BF16)
    lo = (r1 - mid.astype(F32)).astype(BF16)
    return hi, mid, lo


def _dot_exact_lhs(sel, x):
    hi, mid, lo = _split3(x)
    return _dot(sel, hi) + (_dot(sel, mid) + _dot(sel, lo))


def _dot_f32(a, b):
    ah, am, al = _split3(a)
    bh, bm, bl = _split3(b)
    small = _dot(am, bm) + _dot(ah, bl) + _dot(al, bh)
    return _dot(ah, bh) + (_dot(ah, bm) + _dot(am, bh) + small)


def _mm_kernel(*refs, n_a, n_e, prologue, epilogue, bf16_copy):
    a_refs = refs[:n_a]
    w_ref = refs[n_a]
    e_refs = refs[n_a + 1:n_a + 1 + n_e]
    o_ref = refs[n_a + 1 + n_e]
    n_out = 2 if bf16_copy else 1
    if prologue is None:
        a = a_refs[0][...]
    else:
        a_scr = refs[n_a + 1 + n_e + n_out]

        @pl.when(pl.program_id(1) == 0)
        def _():
            a_scr[...] = prologue(*[r[...] for r in a_refs]).astype(BF16)

        a = a_scr[...]
    acc = _dot(a, w_ref[...])
    out = epilogue(acc, *[r[...] for r in e_refs])
    o_ref[...] = out.astype(o_ref.dtype)
    if bf16_copy:
        refs[n_a + 2 + n_e][...] = out.astype(BF16)


def _fused_matmul(a_ins, w, e_ins, *, prologue, epilogue, out_dtype, tm, tn, rows_per_batch=None, name,
                  bf16_copy=False):
    m = next(item[1].shape[0] for item in a_ins if item[0] == 'row')
    k, n = w.shape
    tm = min(tm, m)
    tn = next(t for t in (1024, 768, 512, 384, 256, 128) if t <= tn and n % t == 0)
    assert m % tm == 0
    rpb = rows_per_batch

    def bidx(i):
        return (i * tm) // rpb

    in_specs, args = [], []
    for item in a_ins:
        kind, arr = item[0], item[1]
        wd = item[2] if len(item) > 2 else arr.shape[-1]
        coff = item[3] if len(item) > 3 else 0
        if kind == 'row':
            in_specs.append(pl.BlockSpec((tm, wd), lambda i, j, coff=coff: (i, coff)))
        elif kind == 'vec':
            in_specs.append(pl.BlockSpec((1, wd), lambda i, j: (0, 0)))
        else:
            in_specs.append(pl.BlockSpec((None, 1, wd), lambda i, j: (bidx(i), 0, 0)))
        args.append(arr)
    in_specs.append(pl.BlockSpec((k, tn), lambda i, j: (0, j)))
    args.append(w)
    for item in e_ins:
        kind, arr = item[0], item[1]
        off = (item[2] if len(item) > 2 else 0) // tn
        if kind == 'tile':
            assert len(item) < 3 or item[2] % tn == 0
            in_specs.append(pl.BlockSpec((tm, tn), lambda i, j, off=off: (i, j + off)))
        elif kind == 'col':
            in_specs.append(pl.BlockSpec((1, tn), lambda i, j: (0, j)))
        else:
            in_specs.append(pl.BlockSpec((None, 1, tn), lambda i, j: (bidx(i), 0, j)))
        args.append(arr)
    scratch = [] if prologue is None else [pltpu.VMEM((tm, k), BF16)]
    kern = functools.partial(_mm_kernel, n_a=len(a_ins), n_e=len(e_ins), prologue=prologue, epilogue=epilogue,
                             bf16_copy=bf16_copy)
    out_shape = jax.ShapeDtypeStruct((m, n), out_dtype)
    out_spec = pl.BlockSpec((tm, tn), lambda i, j: (i, j))
    if bf16_copy:
        out_shape, out_spec = (out_shape, jax.ShapeDtypeStruct((m, n), BF16)), (out_spec, out_spec)
    return pl.pallas_call(
        kern, out_shape=out_shape, grid=(m // tm, n // tn), in_specs=in_specs, out_specs=out_spec,
        scratch_shapes=scratch, compiler_params=_params(("parallel", "arbitrary")), name=name)(*args)


def _pad_cols(w, mult):
    n = w.shape[-1]
    npad = (-n) % mult
    if npad:
        w = jnp.pad(w, ((0, 0), (0, npad)))
    return w


def _modnorm_prologue(x, g, scale, shift):
    r = x * lax.rsqrt(jnp.mean(x * x, axis=-1, keepdims=True) + EPS) * g
    return r * (1.0 + scale) + shift


def _identity_epilogue(acc):
    return acc


def _residual_epilogue(acc, x, gate):
    return x + gate * acc


def _in_proj(x, g, scale, shift, w, *, batch_kind, rows_per_batch, tm, tn, name, bf16_copy=False):
    kind = 'batch' if batch_kind else 'row'
    return _fused_matmul([('row', x), ('vec', g), (kind, scale), (kind, shift)], w, [],
                         prologue=_modnorm_prologue, epilogue=_identity_epilogue, out_dtype=F32,
                         tm=tm, tn=tn, rows_per_batch=rows_per_batch, name=name, bf16_copy=bf16_copy)


def _gate_proj_kernel(x_ref, g_ref, scale_ref, shift_ref, w_ref, o_ref):
    h = _modnorm_prologue(x_ref[...], g_ref[...], scale_ref[...], shift_ref[...])
    o_ref[...] = _dot_f32(h, w_ref[...])


def _gate_proj(x, g, scale, shift, w, *, batch_kind, rows_per_batch, tm, name):
    m, d = x.shape
    n = w.shape[1]
    tm = min(tm, m, MM_TM_WIDE_ROWS)
    if batch_kind:
        mod_spec = pl.BlockSpec((None, 1, d), lambda i: ((i * tm) // rows_per_batch, 0, 0))
    else:
        mod_spec = pl.BlockSpec((tm, d), lambda i: (i, 0))
    return pl.pallas_call(
        _gate_proj_kernel, out_shape=jax.ShapeDtypeStruct((m, n), F32), grid=(m // tm,),
        in_specs=[pl.BlockSpec((tm, d), lambda i: (i, 0)), pl.BlockSpec((1, d), lambda i: (0, 0)),
                  mod_spec, mod_spec, pl.BlockSpec((d, n), lambda i: (0, 0))],
        out_specs=pl.BlockSpec((tm, n), lambda i: (i, 0)),
        compiler_params=_params(("parallel",)), name=name)(x, g, scale, shift, w)


def _out_proj(a_ins, w, x, gate, *, batch_kind, rows_per_batch, tm, tn, name, prologue=None):
    kind = 'batchcol' if batch_kind else 'tile'
    return _fused_matmul(a_ins, w, [('tile', x), (kind, gate)], prologue=prologue, epilogue=_residual_epilogue,
                         out_dtype=F32, tm=tm, tn=tn, rows_per_batch=rows_per_batch, name=name)


def _mod_kernel(c_ref, w_ref, b_ref, o_ref):
    o_ref[...] = _dot(c_ref[...].astype(BF16), w_ref[...].astype(BF16)) + b_ref[...]


def _modulation(c_all, w_mod, b_mod, tn=512):
    depth, d, n = w_mod.shape
    rows = c_all.shape[0]
    return pl.pallas_call(
        _mod_kernel, out_shape=jax.ShapeDtypeStruct((depth, rows, n), F32), grid=(depth, n // tn),
        in_specs=[pl.BlockSpec((rows, d), lambda l, j: (0, 0)),
                  pl.BlockSpec((None, d, tn), lambda l, j: (l, 0, j)),
                  pl.BlockSpec((None, 1, tn), lambda l, j: (l, 0, j))],
        out_specs=pl.BlockSpec((None, rows, tn), lambda l, j: (l, 0, j)),
        compiler_params=_params(("parallel", "parallel")), name="adaln_modulation")(
            c_all, w_mod, b_mod.reshape(depth, 1, n))


def _rmsnorm_kernel(x_ref, g_ref, o_ref):
    x = x_ref[...]
    o_ref[...] = x * lax.rsqrt(jnp.mean(x * x, axis=-1, keepdims=True) + EPS) * g_ref[...]


def _final_norm(x, g, tm=512):
    m, d = x.shape
    tm = min(tm, m)
    return pl.pallas_call(
        _rmsnorm_kernel, out_shape=jax.ShapeDtypeStruct((m, d), F32), grid=(m // tm,),
        in_specs=[pl.BlockSpec((tm, d), lambda i: (i, 0)), pl.BlockSpec((1, d), lambda i: (0, 0))],
        out_specs=pl.BlockSpec((tm, d), lambda i: (i, 0)),
        compiler_params=_params(("parallel",)), name="final_rmsnorm")(x, g.reshape(1, d))


def _s5_tables(lam_re, lam_im, log_dt, b_re, b_im, c_re, c_im, d_skip):
    f = F32
    groups, p = lam_re.shape
    nblk = groups * S5_GROUP // S5_BLK_CH
    gpb = groups // nblk
    lr, li = lam_re.astype(f), lam_im.astype(f)
    dt = jnp.exp(log_dt.astype(f))[:, None]
    ldr, ldi = lr * dt, li * dt
    kk = jnp.arange(1, S5_SEG + 1, dtype=f)[:, None, None]
    pmag = jnp.exp(kk * ldr)
    pw_re, pw_im = pmag * jnp.cos(kk * ldi), pmag * jnp.sin(kk * ldi)
    ab_re, ab_im = jnp.exp(ldr) * jnp.cos(ldi), jnp.exp(ldr) * jnp.sin(ldi)
    den = lr * lr + li * li
    nr, ni = ab_re - 1.0, ab_im
    fr, fi = (nr * lr + ni * li) / den, (ni * lr - nr * li) / den
    bre, bim = b_re.astype(f), b_im.astype(f)
    bb_re = fr[..., None] * bre - fi[..., None] * bim
    bb_im = fr[..., None] * bim + fi[..., None] * bre
    eye = jnp.eye(gpb, dtype=f)

    def bd_in(bb):
        t = bb.reshape(nblk, gpb, p, S5_GROUP).transpose(0, 1, 3, 2)
        return jnp.einsum('bgkp,gh->bgkhp', t, eye).reshape(nblk, gpb * S5_GROUP, gpb * p).astype(BF16)

    def bd_out(c):
        t = c.astype(f).reshape(nblk, gpb, S5_GROUP, p).transpose(0, 1, 3, 2)
        return jnp.einsum('bgpk,gh->bgphk', t, eye).reshape(nblk, gpb * p, gpb * S5_GROUP).astype(BF16)

    def lanes(t):
        lead = t.shape[:-2]
        t = t.reshape(lead + (nblk, gpb * p))
        return jnp.moveaxis(t, -2, 0)

    return dict(
        bb_re=bd_in(bb_re), bb_im=bd_in(bb_im), c_re=bd_out(c_re), c_im=bd_out(c_im),
        ab_re=lanes(ab_re[None]), ab_im=lanes(ab_im[None]),
        pw_re=lanes(pw_re), pw_im=lanes(pw_im),
        d=d_skip.astype(f).reshape(1, -1), nblk=nblk)


def _s5_perm():
    pm = np.zeros((S5_CHUNK, S5_CHUNK), np.float32)
    r = np.arange(S5_CHUNK)
    pm[r, (r % SUBLANES) * S5_SEG + r // SUBLANES] = 1.0
    return jnp.asarray(pm, BF16), jnp.asarray(pm.T, BF16)


def _s5_scan_kernel(u_ref, pm_ref, pmt_ref, bbre_ref, bbim_ref, cre_ref, cim_ref, abre_ref, abim_ref,
                    pwre_ref, pwim_ref, d_ref, y_ref, sre_out, sim_out,
                    xre, xim, car_re, car_im, cin_re, cin_im, lend_re, lend_im):
    n = pl.program_id(2)
    nst = xre.shape[1]

    @pl.when(n == 0)
    def _():
        car_re[...] = jnp.zeros_like(car_re)
        car_im[...] = jnp.zeros_like(car_im)

    u = u_ref[...]
    up = _dot(pm_ref[...], u.astype(BF16)).astype(BF16)
    xre[...] = _dot(up, bbre_ref[...])
    xim[...] = _dot(up, bbim_ref[...])
    are = jnp.broadcast_to(abre_ref[...], (SUBLANES, nst))
    aim = jnp.broadcast_to(abim_ref[...], (SUBLANES, nst))
    sre = jnp.zeros((SUBLANES, nst), F32)
    sim = jnp.zeros((SUBLANES, nst), F32)
    for i in range(S5_SEG):
        r = slice(SUBLANES * i, SUBLANES * (i + 1))
        nre = are * sre - aim * sim + xre[r, :]
        nim = are * sim + aim * sre + xim[r, :]
        xre[r, :] = nre
        xim[r, :] = nim
        sre, sim = nre, nim
    lend_re[...] = sre
    lend_im[...] = sim
    a_re = pwre_ref[S5_SEG - 1:S5_SEG, :]
    a_im = pwim_ref[S5_SEG - 1:S5_SEG, :]
    cr, ci = car_re[...], car_im[...]
    for s in range(SUBLANES):
        cin_re[s:s + 1, :] = cr
        cin_im[s:s + 1, :] = ci
        lr, li = lend_re[s:s + 1, :], lend_im[s:s + 1, :]
        cr, ci = a_re * cr - a_im * ci + lr, a_re * ci + a_im * cr + li
    car_re[...] = cr
    car_im[...] = ci
    cinr, cini = cin_re[...], cin_im[...]
    for i in range(S5_SEG):
        r = slice(SUBLANES * i, SUBLANES * (i + 1))
        pr, pi_ = pwre_ref[i:i + 1, :], pwim_ref[i:i + 1, :]
        xre[r, :] = xre[r, :] + (pr * cinr - pi_ * cini)
        xim[r, :] = xim[r, :] + (pr * cini + pi_ * cinr)
    yp = _dot(xre[...].astype(BF16), cre_ref[...]) - _dot(xim[...].astype(BF16), cim_ref[...])
    hi = yp.astype(BF16)
    lo = (yp - hi.astype(F32)).astype(BF16)
    y = _dot(pmt_ref[...], hi) + _dot(pmt_ref[...], lo) + d_ref[...] * u
    y_ref[...] = _gelu(y)

    @pl.when(n == pl.num_programs(2) - 1)
    def _():
        sre_out[...] = cr
        sim_out[...] = ci


def _s5_scan(proj, tabs, batch, seq):
    nblk = tabs['nblk']
    nch = seq // S5_CHUNK
    pm, pmt = _s5_perm()
    nstate = nblk * S5_BLK_ST
    const3 = lambda shape: pl.BlockSpec((None,) + shape, lambda k, b, n: (k, 0, 0))
    y, sre, sim = pl.pallas_call(
        _s5_scan_kernel,
        out_shape=(jax.ShapeDtypeStruct((batch * seq, nblk * S5_BLK_CH), F32),
                   jax.ShapeDtypeStruct((batch, 1, nstate), F32),
                   jax.ShapeDtypeStruct((batch, 1, nstate), F32)),
        grid=(nblk, batch, nch),
        in_specs=[pl.BlockSpec((S5_CHUNK, S5_BLK_CH), lambda k, b, n: (b * nch + n, k)),
                  pl.BlockSpec((S5_CHUNK, S5_CHUNK), lambda k, b, n: (0, 0)),
                  pl.BlockSpec((S5_CHUNK, S5_CHUNK), lambda k, b, n: (0, 0)),
                  const3((S5_BLK_CH, S5_BLK_ST)), const3((S5_BLK_CH, S5_BLK_ST)),
                  const3((S5_BLK_ST, S5_BLK_CH)), const3((S5_BLK_ST, S5_BLK_CH)),
                  const3((1, S5_BLK_ST)), const3((1, S5_BLK_ST)),
                  const3((S5_SEG, S5_BLK_ST)), const3((S5_SEG, S5_BLK_ST)),
                  pl.BlockSpec((1, S5_BLK_CH), lambda k, b, n: (0, k))],
        out_specs=(pl.BlockSpec((S5_CHUNK, S5_BLK_CH), lambda k, b, n: (b * nch + n, k)),
                   pl.BlockSpec((None, 1, S5_BLK_ST), lambda k, b, n: (b, 0, k)),
                   pl.BlockSpec((None, 1, S5_BLK_ST), lambda k, b, n: (b, 0, k))),
        scratch_shapes=[pltpu.VMEM((S5_CHUNK, S5_BLK_ST), F32), pltpu.VMEM((S5_CHUNK, S5_BLK_ST), F32),
                        pltpu.VMEM((1, S5_BLK_ST), F32), pltpu.VMEM((1, S5_BLK_ST), F32),
                        pltpu.VMEM((SUBLANES, S5_BLK_ST), F32), pltpu.VMEM((SUBLANES, S5_BLK_ST), F32),
                        pltpu.VMEM((SUBLANES, S5_BLK_ST), F32), pltpu.VMEM((SUBLANES, S5_BLK_ST), F32)],
        compiler_params=_params(("parallel", "parallel", "arbitrary")), name="s5_scan")(
            proj, pm, pmt, tabs['bb_re'], tabs['bb_im'], tabs['c_re'], tabs['c_im'],
            tabs['ab_re'], tabs['ab_im'], tabs['pw_re'], tabs['pw_im'], tabs['d'])
    return y, sre, sim


def _s5_step_kernel(u_ref, hre_ref, him_ref, bbre_ref, bbim_ref, cre_ref, cim_ref, abre_ref, abim_ref, d_ref,
                    y_ref, sre_out, sim_out):
    u = u_ref[...]
    ub = u.astype(BF16)
    are, aim = abre_ref[...], abim_ref[...]
    hre, him = hre_ref[...], him_ref[...]
    sre = are * hre - aim * him + _dot(ub, bbre_ref[...])
    sim = are * him + aim * hre + _dot(ub, bbim_ref[...])
    sre_out[...] = sre
    sim_out[...] = sim
    y = _dot(sre.astype(BF16), cre_ref[...]) - _dot(sim.astype(BF16), cim_ref[...]) + d_ref[...] * u
    y_ref[...] = _gelu(y)


def _s5_step(proj, h_re, h_im, tabs):
    nblk = tabs['nblk']
    rows = proj.shape[0]
    nstate = nblk * S5_BLK_ST
    const3 = lambda shape: pl.BlockSpec((None,) + shape, lambda k: (k, 0, 0))
    lane_blk = lambda w: pl.BlockSpec((rows, w), lambda k: (0, k))
    return pl.pallas_call(
        _s5_step_kernel,
        out_shape=(jax.ShapeDtypeStruct((rows, nblk * S5_BLK_CH), F32),
                   jax.ShapeDtypeStruct((rows, nstate), F32), jax.ShapeDtypeStruct((rows, nstate), F32)),
        grid=(nblk,),
        in_specs=[lane_blk(S5_BLK_CH), lane_blk(S5_BLK_ST), lane_blk(S5_BLK_ST),
                  const3((S5_BLK_CH, S5_BLK_ST)), const3((S5_BLK_CH, S5_BLK_ST)),
                  const3((S5_BLK_ST, S5_BLK_CH)), const3((S5_BLK_ST, S5_BLK_CH)),
                  const3((1, S5_BLK_ST)), const3((1, S5_BLK_ST)),
                  pl.BlockSpec((1, S5_BLK_CH), lambda k: (0, k))],
        out_specs=(lane_blk(S5_BLK_CH), lane_blk(S5_BLK_ST), lane_blk(S5_BLK_ST)),
        compiler_params=_params(("parallel",)), name="s5_step")(
            proj, h_re.reshape(rows, nstate), h_im.reshape(rows, nstate),
            tabs['bb_re'], tabs['bb_im'], tabs['c_re'], tabs['c_im'], tabs['ab_re'], tabs['ab_im'], tabs['d'])


def _s5_weights(w_in, lam_re, lam_im, log_dt, b_re, b_im, c_re, c_im, d_skip, w_glu, b_glu, w_out):
    tabs = _s5_tables(lam_re, lam_im, log_dt, b_re, b_im, c_re, c_im, d_skip)
    return (w_in.astype(BF16), w_glu.astype(BF16), b_glu.astype(F32).reshape(1, -1), w_out.astype(BF16), tabs)


def _glu_epilogue(acc, gy, z, b):
    return gy * _sigmoid(acc + b) * _silu(z)


def _cast_prologue(a):
    return a


def _s5_layer(x, mods, h_state, w, *, prompt, batch, seq, tm):
    g, scale, shift, gate = mods
    w_in, w_glu, b_glu, w_out, tabs = w
    width = w_glu.shape[0]
    proj = _in_proj(x, g, scale, shift, w_in, batch_kind=prompt, rows_per_batch=seq, tm=tm, tn=MM_TN, name="s5_in_proj")
    if prompt:
        gy, sre, sim = _s5_scan(proj, tabs, batch, seq)
    else:
        gy, sre, sim = _s5_step(proj, h_state[0], h_state[1], tabs)
    a = _fused_matmul([('row', gy)], w_glu, [('tile', gy), ('tile', proj, width), ('col', b_glu)],
                      prologue=_cast_prologue, epilogue=_glu_epilogue, out_dtype=BF16, tm=min(tm, MM_TM_WIDE_ROWS),
                      tn=MM_TN, name="s5_glu")
    x_new = _out_proj([('row', a)], w_out, x, gate, batch_kind=prompt, rows_per_batch=seq, tm=tm, tn=MM_TN, name="s5_out_proj")
    return x_new, sre, sim


def _conv_silu_chunk(x_ref, w_ref, b_ref, pad_ref, first):
    c = x_ref.shape[0]

    @pl.when(first)
    def _():
        pad_ref[0:SUBLANES, :] = jnp.zeros((SUBLANES, pad_ref.shape[1]), F32)

    pad_ref[SUBLANES:SUBLANES + c, :] = x_ref[...]
    acc = w_ref[3:4, :] * pad_ref[SUBLANES:SUBLANES + c, :]
    for j in range(CONV_W - 1):
        off = SUBLANES - (CONV_W - 1) + j
        acc = acc + w_ref[j:j + 1, :] * pad_ref[off:off + c, :]
    if b_ref is not None:
        acc = acc + b_ref[...]
    pad_ref[0:SUBLANES, :] = pad_ref[c:c + SUBLANES, :]
    return _silu(acc)


def _conv_silu_step(x, taps_ref, w_ref, b_ref):
    acc = w_ref[3:4, :] * x
    for j in range(CONV_W - 1):
        acc = acc + w_ref[j:j + 1, :] * taps_ref[j]
    if b_ref is not None:
        acc = acc + b_ref[...]
    return _silu(acc)


def _tri_masks(c):
    t = lax.broadcasted_iota(jnp.int32, (c, c), 0)
    s = lax.broadcasted_iota(jnp.int32, (c, c), 1)
    return s <= t, s < t


def _pad_to_square_t(x, n):
    rows = x.shape[0]
    return jnp.concatenate([x, jnp.zeros((n - rows, n), x.dtype)], axis=0).T


M2_HEADDIM = 64
M2_STATE = 128
M2_GROUPS = 8
M2_CHUNK = 128


def _m2_ssd_kernel(x_ref, b_ref, c_ref, dt_ref, z_ref, wx_ref, wb_ref, wc_ref, bx_ref, bb_ref, bc_ref,
                   dtb_ref, nega_ref, dsk_ref, nw_ref, o_ref, sout_ref, xpad, bpad, cpad, s_ref, y_ref):
    n = pl.program_id(1)
    first = n == 0
    c = x_ref.shape[0]
    npairs = s_ref.shape[0]
    pairs_per_group = npairs // M2_GROUPS

    @pl.when(first)
    def _():
        s_ref[...] = jnp.zeros_like(s_ref)

    xs = _conv_silu_chunk(x_ref, wx_ref, bx_ref, xpad, first)
    bm = _conv_silu_chunk(b_ref, wb_ref, bb_ref, bpad, first).astype(BF16)
    cm = _conv_silu_chunk(c_ref, wc_ref, bc_ref, cpad, first).astype(BF16)
    dtv = _softplus(dt_ref[...] + dtb_ref[...])
    la = nega_ref[...] * dtv
    incl, _ = _tri_masks(c)
    tri = jnp.where(incl, 1.0, 0.0).astype(BF16)
    cum = _dot_exact_lhs(tri, la)
    cum_t = cum.T
    ecum_all = jnp.exp(cum)
    wend_all = jnp.exp(cum[c - 1:c, :] - cum)
    elast_t = jnp.exp(cum_t[:, c - 1:c])
    lane_first = lax.broadcasted_iota(jnp.int32, (c, LANES), 1) < M2_HEADDIM
    row_first = lax.broadcasted_iota(jnp.int32, (LANES, LANES), 0) < M2_HEADDIM

    for g in range(M2_GROUPS):
        bg = bm[:, g * M2_STATE:(g + 1) * M2_STATE]
        cg = cm[:, g * M2_STATE:(g + 1) * M2_STATE]
        gm = _dot_nt(cg, bg)
        for j in range(pairs_per_group):
            p = g * pairs_per_group + j
            ha, hb = 2 * p, 2 * p + 1
            xp = xs[:, p * LANES:(p + 1) * LANES]

            def decay_weights(h):
                seg = cum[:, h:h + 1] - cum_t[h:h + 1, :]
                dec = jnp.where(incl, jnp.exp(jnp.where(incl, seg, 0.0)), 0.0)
                return (gm * dec).astype(BF16)

            xdt = xp * jnp.where(lane_first, dtv[:, ha:ha + 1], dtv[:, hb:hb + 1])
            xdt_a = jnp.where(lane_first, xdt, 0.0)
            xdt_b = xdt - xdt_a
            y = _dot(decay_weights(ha), xdt_a.astype(BF16)) + _dot(decay_weights(hb), xdt_b.astype(BF16))
            sp = s_ref[p]
            y = y + _dot_nt(cg, sp.astype(BF16)) * jnp.where(lane_first, ecum_all[:, ha:ha + 1], ecum_all[:, hb:hb + 1])
            y_ref[:, p * LANES:(p + 1) * LANES] = y + dsk_ref[:, p * LANES:(p + 1) * LANES] * xp
            xw = xdt * jnp.where(lane_first, wend_all[:, ha:ha + 1], wend_all[:, hb:hb + 1])
            dmat = jnp.where(row_first, elast_t[ha:ha + 1, :], elast_t[hb:hb + 1, :])
            s_ref[p] = sp * dmat + _dot(xw.T.astype(BF16), bg)

    o_ref[...] = _gated_rmsnorm_prologue(y_ref[...], z_ref[...], nw_ref[...]).astype(o_ref.dtype)

    @pl.when(n == pl.num_programs(1) - 1)
    def _():
        sout_ref[...] = s_ref[...]


def _m2_ssd(proj, gates, mw, batch, seq):
    c = M2_CHUNK
    nch = seq // c
    width = mw['width']
    gs = M2_GROUPS * M2_STATE
    npairs = width // LANES
    xo, bo, co = width // width, (2 * width) // gs, (2 * width + gs) // gs
    row = lambda w, off: pl.BlockSpec((c, w), lambda b, n: (b * nch + n, off))
    par = lambda r, w, off: pl.BlockSpec((r, w), lambda b, n: (0, off))
    return pl.pallas_call(
        _m2_ssd_kernel,
        out_shape=(jax.ShapeDtypeStruct((batch * seq, width), BF16),
                   jax.ShapeDtypeStruct((batch, npairs, LANES, M2_STATE), F32)),
        grid=(batch, nch),
        in_specs=[row(width, xo), row(gs, bo), row(gs, co), row(LANES, 0), row(width, 0),
                  par(CONV_W, width, 0), par(CONV_W, gs, width // gs), par(CONV_W, gs, width // gs + 1),
                  par(1, width, 0), par(1, gs, width // gs), par(1, gs, width // gs + 1),
                  par(1, LANES, 0), par(1, LANES, 0), par(1, width, 0), par(1, width, 0)],
        out_specs=(pl.BlockSpec((c, width), lambda b, n: (b * nch + n, 0)),
                   pl.BlockSpec((None, npairs, LANES, M2_STATE), lambda b, n: (b, 0, 0, 0))),
        scratch_shapes=[pltpu.VMEM((c + SUBLANES, width), F32), pltpu.VMEM((c + SUBLANES, gs), F32),
                        pltpu.VMEM((c + SUBLANES, gs), F32), pltpu.VMEM((npairs, LANES, M2_STATE), F32),
                        pltpu.VMEM((c, width), F32)],
        compiler_params=_params(("parallel", "arbitrary")), name="m2_ssd")(
            proj, proj, proj, gates, proj, mw['conv_w'], mw['conv_w'], mw['conv_w'], mw['conv_b'], mw['conv_b'],
            mw['conv_b'], mw['dt_bias'], mw['neg_a'], mw['d_ch'], mw['norm'])


def _m2_pre_step_kernel(xbc_ref, dt_ref, taps_ref, w_ref, b_ref, dtb_ref, nega_ref, act_ref, dtv_ref, dec_ref):
    act_ref[...] = _conv_silu_step(xbc_ref[...], taps_ref, w_ref, b_ref)
    dtv = _softplus(dt_ref[...] + dtb_ref[...])
    dtv_ref[...] = dtv
    dec_ref[...] = jnp.exp(nega_ref[...] * dtv)


def _m2_pre_step(proj, dt_raw, taps, mw):
    rows = proj.shape[0]
    width = mw['width']
    cdim = mw['conv_w'].shape[1]
    xbc = proj[:, width:width + cdim]
    return pl.pallas_call(
        _m2_pre_step_kernel,
        out_shape=(jax.ShapeDtypeStruct((rows, cdim), F32), jax.ShapeDtypeStruct((rows, LANES), F32),
                   jax.ShapeDtypeStruct((rows, LANES), F32)),
        name="m2_pre_step", compiler_params=pltpu.CompilerParams(vmem_limit_bytes=V7X_VMEM_LIMIT_BYTES))(
            xbc, dt_raw, taps, mw['conv_w'], mw['conv_b'], mw['dt_bias'], mw['neg_a'])


def _m2_state_step_kernel(x_ref, dtc_ref, decc_ref, b_ref, c_ref, dsk_ref, s_ref, y_ref, sout_ref):
    rows = x_ref.shape[0]
    pairs = x_ref.shape[1] // LANES
    bv = b_ref[...]
    cb = c_ref[...].astype(BF16)
    for j in range(pairs):
        sl = slice(j * LANES, (j + 1) * LANES)
        xp = x_ref[:, sl]
        xdt_t = _pad_to_square_t(xp * dtc_ref[:, sl], LANES)
        dec_t = _pad_to_square_t(decc_ref[:, sl], LANES)
        for b in range(rows):
            sp = s_ref[b, j]
            s_new = sp * dec_t[:, b:b + 1] + xdt_t[:, b:b + 1] * bv[b:b + 1, :]
            sout_ref[b, j] = s_new
            y_ref[b:b + 1, sl] = _dot_nt(cb[b:b + 1, :], s_new.astype(BF16)) + dsk_ref[:, sl] * xp[b:b + 1, :]


def _m2_state_step(act, dt_ch, dec_ch, ssm, mw):
    rows = act.shape[0]
    width = mw['width']
    gw = width // M2_GROUPS
    ppg = gw // LANES
    npairs = width // LANES
    gs = M2_GROUPS * M2_STATE
    s4 = ssm.reshape(rows, npairs, LANES, M2_STATE)
    blk = lambda w, base: pl.BlockSpec((rows, w), lambda g: (0, base + g))
    return pl.pallas_call(
        _m2_state_step_kernel,
        out_shape=(jax.ShapeDtypeStruct((rows, width), F32), jax.ShapeDtypeStruct(s4.shape, F32)),
        grid=(M2_GROUPS,),
        in_specs=[blk(gw, 0), blk(gw, 0), blk(gw, 0), blk(M2_STATE, width // M2_STATE),
                  blk(M2_STATE, (width + gs) // M2_STATE), pl.BlockSpec((1, gw), lambda g: (0, g)),
                  pl.BlockSpec((rows, ppg, LANES, M2_STATE), lambda g: (0, g, 0, 0))],
        out_specs=(blk(gw, 0), pl.BlockSpec((rows, ppg, LANES, M2_STATE), lambda g: (0, g, 0, 0))),
        compiler_params=_params(("parallel",)), name="m2_state_step")(
            act, dt_ch, dec_ch, act, act, mw['d_ch'], s4)


def _m2_weights(w_in, conv_w, conv_b, dt_bias, a_log, d_skip, norm_w, w_out):
    f = F32
    heads = dt_bias.shape[0]
    width = norm_w.shape[0]
    pad = LANES - heads
    cdim = conv_w.shape[1]
    return dict(
        w_in=_pad_cols(w_in[:, :width + cdim].astype(BF16), 512),
        w_gate=_pad_cols(w_in[:, width + cdim:].astype(f), LANES),
        conv_w=conv_w.astype(f), conv_b=conv_b.astype(f).reshape(1, -1),
        dt_bias=jnp.pad(dt_bias.astype(f), (0, pad)).reshape(1, LANES),
        neg_a=jnp.pad(-jnp.exp(a_log.astype(f)), (0, pad)).reshape(1, LANES),
        d_ch=jnp.repeat(d_skip.astype(f), width // heads).reshape(1, width),
        norm=norm_w.astype(f).reshape(1, width), w_out=w_out.astype(BF16), width=width, heads=heads)


def _gated_rmsnorm_prologue(y, z, w):
    v = y * _silu(z)
    return v * lax.rsqrt(jnp.mean(v * v, axis=-1, keepdims=True) + EPS) * w


def _m2_layer(x, mods, state, mw, *, prompt, batch, seq, tm):
    g, scale, shift, gate = mods
    width, heads = mw['width'], mw['heads']
    cdim = mw['conv_w'].shape[1]
    proj = _in_proj(x, g, scale, shift, mw['w_in'], batch_kind=prompt, rows_per_batch=seq, tm=tm, tn=MM_TN,
                    name="m2_in_proj")
    gates = _gate_proj(x, g, scale, shift, mw['w_gate'], batch_kind=prompt, rows_per_batch=seq, tm=tm,
                       name="m2_gate_proj")
    if prompt:
        a, ssm = _m2_ssd(proj, gates, mw, batch, seq)
        conv_new = proj.reshape(batch, seq, -1)[:, seq - (CONV_W - 1):, width:width + cdim]
        x_new = _out_proj([('row', a)], mw['w_out'], x, gate, batch_kind=True, rows_per_batch=seq, tm=tm, tn=MM_TN,
                          name="m2_out_proj")
        return x_new, conv_new, ssm
    else:
        conv_buf, ssm_in = state
        taps = jnp.swapaxes(conv_buf.astype(F32), 0, 1)
        act, dtv, dec = _m2_pre_step(proj, gates, taps, mw)
        rep = width // heads
        dt_ch = jnp.repeat(dtv[:, :heads], rep, axis=1)
        dec_ch = jnp.repeat(dec[:, :heads], rep, axis=1)
        y, ssm = _m2_state_step(act, dt_ch, dec_ch, ssm_in.astype(F32), mw)
        conv_new = jnp.concatenate([conv_buf.astype(F32)[:, 1:], proj[:, None, width:width + cdim]], axis=1)
    x_new = _out_proj([('row', y), ('row', proj, width, 0), ('vec', mw['norm'])], mw['w_out'], x, gate,
                      batch_kind=prompt, rows_per_batch=seq, tm=min(tm, 256), tn=MM_TN, name="m2_out_proj",
                      prologue=_gated_rmsnorm_prologue)
    return x_new, conv_new, ssm


GD_DK = 128
GD_DV = 128
GD_CHUNK = 64


def _dot_3pass(a, b):
    ah = a.astype(BF16)
    al = (a - ah.astype(F32)).astype(BF16)
    bh = b.astype(BF16)
    bl = (b - bh.astype(F32)).astype(BF16)
    return _dot(ah, bh) + (_dot(ah, bl) + _dot(al, bh))


def _l2norm_rows(x):
    return x * lax.rsqrt(jnp.sum(x * x, axis=-1, keepdims=True) + EPS)


def _rmsnorm_rows(x, w):
    return x * lax.rsqrt(jnp.mean(x * x, axis=-1, keepdims=True) + EPS) * w


GD_INV_BASE = 16


def _bdot(a, b):
    return jnp.einsum('hmk,hkn->hmn', a, b, preferred_element_type=F32)


def _bdot_nt(a, b):
    return jnp.einsum('hmk,hnk->hmn', a, b, preferred_element_type=F32)


def _bdot_3pass(a, b):
    ah = a.astype(BF16)
    al = (a - ah.astype(F32)).astype(BF16)
    bh = b.astype(BF16)
    bl = (b - bh.astype(F32)).astype(BF16)
    return _bdot(ah, bh) + (_bdot(ah, bl) + _bdot(al, bh))


def _unit_lower_inverse(a_strict):
    c = a_strict.shape[-1]
    row = lax.broadcasted_iota(jnp.int32, (c, c), 0)
    col = lax.broadcasted_iota(jnp.int32, (c, c), 1)
    eye = jnp.where(row == col, 1.0, 0.0)
    blk = GD_INV_BASE
    shift = int(math.log2(blk))
    p = jnp.where((row >> shift) == (col >> shift), -a_strict, 0.0)
    t = eye + p
    for _ in range(shift - 1):
        p = _bdot_3pass(p, p)
        t = t + _bdot_3pass(t, p)
    while blk < c:
        below = jnp.logical_and((row >> (shift + 1)) == (col >> (shift + 1)), (row >> shift) != (col >> shift))
        b = jnp.where(below, a_strict, 0.0)
        t = t - _bdot_3pass(_bdot_3pass(t, b), t)
        blk *= 2
        shift += 1
    return t


def _gd_chunk_kernel(qkv_ref, z_ref, braw_ref, araw_ref, cw_ref, nega_ref, dtb_ref, nw_ref,
                     o_ref, sout_ref, pad, s_ref):
    n = pl.program_id(1)
    first = n == 0
    c = qkv_ref.shape[0]
    hv = s_ref.shape[0]
    hk = hv // 2
    rep = hv // hk

    @pl.when(first)
    def _():
        s_ref[...] = jnp.zeros_like(s_ref)

    qkv = _conv_silu_chunk(qkv_ref, cw_ref, None, pad, first)
    beta = _sigmoid(braw_ref[...])
    gl = nega_ref[...] * _softplus(araw_ref[...] + dtb_ref[...])
    incl, strict = _tri_masks(c)
    tri = jnp.where(incl, 1.0, 0.0).astype(BF16)
    gcum = _dot_exact_lhs(tri, gl)
    gcum_t = jnp.concatenate([gcum, jnp.zeros((LANES - c, LANES), F32)], axis=0).T

    heads = range(hv)
    per_value_head = lambda t: jnp.stack([t[h // rep] for h in heads])
    q3 = jnp.stack([qkv[:, i * GD_DK:(i + 1) * GD_DK] for i in range(hk)])
    k3 = jnp.stack([qkv[:, (hk + i) * GD_DK:(hk + i + 1) * GD_DK] for i in range(hk)])
    v3 = jnp.stack([qkv[:, (2 * hk + h) * GD_DV:(2 * hk + h + 1) * GD_DV] for h in heads])
    q3 = _l2norm_rows(q3) * (GD_DK ** -0.5)
    k3 = _l2norm_rows(k3)
    k3b = k3.astype(BF16)
    kk = per_value_head(_bdot_nt(k3b, k3b))
    qk = per_value_head(_bdot_nt(q3.astype(BF16), k3b))
    q_v, k_v = per_value_head(q3), per_value_head(k3)
    colv = jnp.stack([gcum[:, h:h + 1] for h in heads])
    rowv = jnp.stack([gcum_t[h:h + 1, :c] for h in heads])
    bcol = jnp.stack([beta[:, h:h + 1] for h in heads])
    glast = colv[:, c - 1:c, :]
    ecol = jnp.exp(colv)
    dec = jnp.where(incl, jnp.exp(jnp.where(incl, colv - rowv, 0.0)), 0.0)
    a = jnp.where(strict, (bcol * kk) * dec, 0.0)
    tinv = _unit_lower_inverse(a)
    rhs = jnp.concatenate([v3 * bcol, (k_v * bcol) * ecol], axis=-1)
    sol = _bdot_3pass(tinv, rhs)
    u, w = sol[:, :, :GD_DV], sol[:, :, GD_DV:]
    s = s_ref[...]
    sb = s.astype(BF16)
    v_new = u - _bdot(w.astype(BF16), sb)
    o = _bdot((q_v * ecol).astype(BF16), sb) + _bdot((qk * dec).astype(BF16), v_new.astype(BF16))
    zpad = jnp.zeros((hv, LANES - c, GD_DV), F32)
    kd_t = jnp.swapaxes(jnp.concatenate([k_v * jnp.exp(glast - colv), zpad], axis=1), 1, 2)
    vn_pad = jnp.concatenate([v_new, zpad], axis=1)
    s_ref[...] = s * jnp.exp(glast) + _bdot(kd_t.astype(BF16), vn_pad.astype(BF16))
    on = _rmsnorm_rows(o, nw_ref[...])
    for h in heads:
        sl = slice(h * GD_DV, (h + 1) * GD_DV)
        o_ref[:, sl] = (on[h] * _silu(z_ref[:, sl])).astype(o_ref.dtype)

    @pl.when(n == pl.num_programs(1) - 1)
    def _():
        sout_ref[...] = s_ref[...]


def _gd_chunked(proj, gates, gw, batch, seq):
    c = GD_CHUNK
    nch = seq // c
    cdim, width, hv = gw['cdim'], gw['width'], gw['hv']
    row = lambda w, off: pl.BlockSpec((c, w), lambda b, n: (b * nch + n, off))
    par = lambda r, w: pl.BlockSpec((r, w), lambda b, n: (0, 0))
    return pl.pallas_call(
        _gd_chunk_kernel,
        out_shape=(jax.ShapeDtypeStruct((batch * seq, width), BF16),
                   jax.ShapeDtypeStruct((batch, hv, GD_DK, GD_DV), F32)),
        grid=(batch, nch),
        in_specs=[row(cdim, 0), row(width, cdim // width), row(LANES, 0), row(LANES, 1),
                  par(CONV_W, cdim), par(1, LANES), par(1, LANES), par(1, GD_DV)],
        out_specs=(pl.BlockSpec((c, width), lambda b, n: (b * nch + n, 0)),
                   pl.BlockSpec((None, hv, GD_DK, GD_DV), lambda b, n: (b, 0, 0, 0))),
        scratch_shapes=[pltpu.VMEM((c + SUBLANES, cdim), F32), pltpu.VMEM((hv, GD_DK, GD_DV), F32)],
        compiler_params=_params(("parallel", "arbitrary")), name="gd_chunked")(
            proj, proj, gates, gates, gw['conv_w'], gw['neg_a'], gw['dt_bias'], gw['norm'])


def _gd_pre_step_kernel(qkv_ref, braw_ref, araw_ref, taps_ref, cw_ref, nega_ref, dtb_ref,
                        q_ref, k_ref, v_ref, beta_ref, eg_ref):
    hk = q_ref.shape[1] // GD_DK
    act = _conv_silu_step(qkv_ref[...], taps_ref, cw_ref, None)
    for kh in range(hk):
        sl = slice(kh * GD_DK, (kh + 1) * GD_DK)
        q_ref[:, sl] = _l2norm_rows(act[:, kh * GD_DK:(kh + 1) * GD_DK]) * (GD_DK ** -0.5)
        k_ref[:, sl] = _l2norm_rows(act[:, (hk + kh) * GD_DK:(hk + kh + 1) * GD_DK])
    v_ref[...] = act[:, 2 * hk * GD_DK:]
    beta_ref[...] = _sigmoid(braw_ref[...])
    eg_ref[...] = jnp.exp(nega_ref[...] * _softplus(araw_ref[...] + dtb_ref[...]))


def _gd_pre_step(proj, gates, taps, gw):
    rows = proj.shape[0]
    cdim, width, hv = gw['cdim'], gw['width'], gw['hv']
    qk_w = (cdim - width) // 2
    sd = lambda w: jax.ShapeDtypeStruct((rows, w), F32)
    return pl.pallas_call(
        _gd_pre_step_kernel, out_shape=(sd(qk_w), sd(qk_w), sd(width), sd(LANES), sd(LANES)),
        name="gd_pre_step", compiler_params=pltpu.CompilerParams(vmem_limit_bytes=V7X_VMEM_LIMIT_BYTES))(
            proj[:, :cdim], gates[:, :LANES], gates[:, LANES:], taps,
            gw['conv_w'], gw['neg_a'], gw['dt_bias'])


def _gd_state_step_kernel(q_ref, k_ref, v_ref, beta_ref, eg_ref, z_ref, nw_ref, s_ref, o_ref, sout_ref):
    rows = q_ref.shape[0]
    nk = q_ref.shape[1] // GD_DK
    rep = (v_ref.shape[1] // GD_DV) // nk
    nw = nw_ref[...]
    zrows = jnp.zeros((SUBLANES - 2, GD_DK), F32)
    for kh in range(nk):
        ksl = slice(kh * GD_DK, (kh + 1) * GD_DK)
        q8, k8 = q_ref[:, ksl], k_ref[:, ksl]
        k_t = _pad_to_square_t(k8, GD_DK)
        for b in range(rows):
            qb, kb = q8[b:b + 1, :], k8[b:b + 1, :]
            kq = jnp.concatenate([kb, qb, zrows], axis=0).astype(BF16)
            qk = jnp.sum(qb * kb, axis=-1, keepdims=True)
            for r in range(rep):
                h = kh * rep + r
                vsl = slice(h * GD_DV, (h + 1) * GD_DV)
                s = s_ref[b, h]
                ks_qs = _dot(kq, s.astype(BF16))
                eg = eg_ref[b:b + 1, vsl]
                beta = beta_ref[b:b + 1, vsl]
                v_new = beta * (v_ref[b:b + 1, vsl] - eg * ks_qs[0:1, :])
                o = eg * ks_qs[1:2, :] + qk * v_new
                sout_ref[b, h] = s * eg[:, 0:1] + k_t[:, b:b + 1] * v_new
                o_ref[b:b + 1, vsl] = _rmsnorm_rows(o, nw) * _silu(z_ref[b:b + 1, vsl])


def _gd_state_step(proj, qn, kn, v, beta_ch, eg_ch, state, gw, heads_per_step=4):
    rows = qn.shape[0]
    cdim, width, hv = gw['cdim'], gw['width'], gw['hv']
    steps = hv // heads_per_step
    kw = qn.shape[1] // steps
    vw = width // steps
    blk = lambda w, base=0: pl.BlockSpec((rows, w), lambda g, base=base: (0, base + g))
    sspec = pl.BlockSpec((rows, heads_per_step, GD_DK, GD_DV), lambda g: (0, g, 0, 0))
    return pl.pallas_call(
        _gd_state_step_kernel,
        out_shape=(jax.ShapeDtypeStruct((rows, width), F32), jax.ShapeDtypeStruct(state.shape, F32)),
        grid=(steps,),
        in_specs=[blk(kw), blk(kw), blk(vw), blk(vw), blk(vw), blk(vw, cdim // vw),
                  pl.BlockSpec((1, GD_DV), lambda g: (0, 0)), sspec],
        out_specs=(blk(vw), sspec),
        compiler_params=_params(("parallel",)), name="gd_state_step")(
            qn, kn, v, beta_ch, eg_ch, proj, gw['norm'], state)


def _gd_weights(w_in, conv_w, a_log, dt_bias, norm_w, w_out):
    f = F32
    hv = a_log.shape[0]
    cdim = conv_w.shape[1]
    width = w_out.shape[0]
    pad = LANES - hv
    base = cdim + width
    zeros = jnp.zeros((w_in.shape[0], pad), w_in.dtype)
    w_gate = jnp.concatenate([w_in[:, base:base + hv], zeros, w_in[:, base + hv:], zeros], axis=1)
    return dict(
        w_in=_pad_cols(w_in[:, :base].astype(BF16), 512), w_gate=w_gate.astype(f), conv_w=conv_w.astype(f),
        neg_a=jnp.pad(-jnp.exp(a_log.astype(f)), (0, pad)).reshape(1, LANES),
        dt_bias=jnp.pad(dt_bias.astype(f), (0, pad)).reshape(1, LANES),
        norm=norm_w.astype(f).reshape(1, -1), w_out=w_out.astype(BF16), cdim=cdim, width=width, hv=hv)


def _gd_layer(x, mods, state, gw, *, prompt, batch, seq, tm):
    g, scale, shift, gate = mods
    cdim, width, hv = gw['cdim'], gw['width'], gw['hv']
    proj = _in_proj(x, g, scale, shift, gw['w_in'], batch_kind=prompt, rows_per_batch=seq, tm=tm, tn=MM_TN,
                    name="gd_in_proj")
    gates = _gate_proj(x, g, scale, shift, gw['w_gate'], batch_kind=prompt, rows_per_batch=seq, tm=tm,
                       name="gd_gate_proj")
    if prompt:
        a, ssm = _gd_chunked(proj, gates, gw, batch, seq)
        conv_new = proj.reshape(batch, seq, -1)[:, seq - (CONV_W - 1):, :cdim]
        x_new = _out_proj([('row', a)], gw['w_out'], x, gate, batch_kind=True, rows_per_batch=seq, tm=tm, tn=MM_TN,
                          name="gd_out_proj")
    else:
        conv_buf, ssm_in = state
        taps = jnp.swapaxes(conv_buf.astype(F32), 0, 1)
        qn, kn, v, beta, eg = _gd_pre_step(proj, gates, taps, gw)
        beta_ch = jnp.repeat(beta[:, :hv], GD_DV, axis=1)
        eg_ch = jnp.repeat(eg[:, :hv], GD_DV, axis=1)
        a, ssm = _gd_state_step(proj, qn, kn, v, beta_ch, eg_ch, ssm_in.astype(F32), gw)
        conv_new = jnp.concatenate([conv_buf.astype(F32)[:, 1:], proj[:, None, :cdim]], axis=1)
        x_new = _out_proj([('row', a)], gw['w_out'], x, gate, batch_kind=False, rows_per_batch=seq, tm=tm, tn=MM_TN,
                          name="gd_out_proj", prologue=_cast_prologue)
    return x_new, conv_new, ssm


AT_DIM = 128
IDX_DIM = 128
TOPK_MAX = 256
REL_BUCKETS = 32
REL_MAX_DIST = 128
AT_TILE = 256
INT32_MIN = -2 ** 31
_NEG_BITS = int(np.float32(NEG).view(np.int32))
NEG_SORT_KEY = _NEG_BITS ^ 0x7FFFFFFF if _NEG_BITS < 0 else _NEG_BITS


def _bucket_starts():
    d = np.arange(0, REL_MAX_DIST + 1)
    exact = REL_BUCKETS // 2
    far = exact + (np.log(np.maximum(d, 1).astype(np.float32) / exact) / math.log(REL_MAX_DIST / exact)
                   * (REL_BUCKETS - exact)).astype(np.int32)
    bucket = np.where(d < exact, d, np.minimum(far, REL_BUCKETS - 1))
    assert np.all(np.diff(bucket) >= 0) and bucket[-1] == REL_BUCKETS - 1
    return [int(np.argmax(bucket >= b)) for b in range(REL_BUCKETS)]


def _bias_from_dist(dist, value_of_bucket):
    starts = _bucket_starts()
    val = value_of_bucket(REL_BUCKETS - 1)
    for b in range(REL_BUCKETS - 2, -1, -1):
        val = jnp.where(dist < starts[b + 1], value_of_bucket(b), val)
    return val


def _sort_key(x):
    x = jnp.where(x == 0.0, 0.0, x)
    b = pltpu.bitcast(x, jnp.int32)
    return jnp.where(b < 0, b ^ jnp.int32(0x7FFFFFFF), b)


def _kth_largest_key(count_ge, shape, k):
    def body(it, ans):
        cand = ans | jnp.left_shift(jnp.int32(1), 31 - it)
        cnt = count_ge(cand ^ jnp.int32(INT32_MIN))
        return jnp.where(cnt >= k, cand, ans)

    ans = lax.fori_loop(0, 32, body, jnp.zeros(shape, jnp.int32))
    return ans ^ jnp.int32(INT32_MIN)


def _relbias_tiles_kernel(rb_ref, o_ref):
    delta = pl.program_id(0) * AT_TILE
    h = pl.program_id(1)
    i = lax.broadcasted_iota(jnp.int32, (AT_TILE, AT_TILE), 0)
    j = lax.broadcasted_iota(jnp.int32, (AT_TILE, AT_TILE), 1)
    o_ref[...] = _bias_from_dist(delta + i - j, lambda b: rb_ref[b, h])


def _relbias_tiles(rel_bias):
    heads = rel_bias.shape[1]
    ntile = 3
    assert (ntile - 1) * AT_TILE - (AT_TILE - 1) >= REL_MAX_DIST
    return pl.pallas_call(
        _relbias_tiles_kernel, out_shape=jax.ShapeDtypeStruct((ntile, heads, AT_TILE, AT_TILE), F32),
        grid=(ntile, heads),
        in_specs=[pl.BlockSpec(memory_space=pltpu.SMEM)],
        out_specs=pl.BlockSpec((None, None, AT_TILE, AT_TILE), lambda d, h: (d, h, 0, 0)),
        compiler_params=_params(("parallel", "parallel")), name="at_relbias_tiles")(rel_bias.astype(F32))


def _at_index_kernel(qi_ref, wi_ref, ki_ref, o_ref, keys, cnt, *, k_sel, score_scale):
    qb = pl.program_id(1)
    tq = qi_ref.shape[0]
    nkb = keys.shape[0]
    tk = keys.shape[2]
    nh = qi_ref.shape[1] // IDX_DIM
    wsc = wi_ref[...] * score_scale
    qpos = qb * tq + lax.broadcasted_iota(jnp.int32, (tq, tk), 0)
    kloc = lax.broadcasted_iota(jnp.int32, (tq, tk), 1)
    neg_key = _sort_key(jnp.full((tq, tk), NEG, F32))

    for kb in range(nkb):
        @pl.when(kb <= qb)
        def _():
            kblk = ki_ref[kb * tk:(kb + 1) * tk, :].astype(BF16)
            sc = jnp.zeros((tq, tk), F32)
            for h in range(nh):
                d = _dot_nt(qi_ref[:, h * IDX_DIM:(h + 1) * IDX_DIM].astype(BF16), kblk)
                sc = sc + wsc[:, h:h + 1] * jnp.maximum(d, 0.0)
            adm = kb * tk + kloc <= qpos
            keys[kb] = _sort_key(jnp.where(adm, sc, NEG))

        @pl.when(kb > qb)
        def _():
            keys[kb] = neg_key

    def count_ge(t):
        cnt[...] = jnp.where(keys[0] >= t, 1, 0)
        for kb in range(1, nkb):
            @pl.when(kb <= qb)
            def _():
                cnt[...] += jnp.where(keys[kb] >= t, 1, 0)
        beyond = (nkb - 1 - qb) * tk
        return jnp.sum(cnt[...], axis=1, keepdims=True) + jnp.where(t <= NEG_SORT_KEY, beyond, 0)

    thr = _kth_largest_key(count_ge, (tq, 1), k_sel)
    n_ge = count_ge(thr)
    has_ties = jnp.max(n_ge) > k_sel

    @pl.when(jnp.logical_not(has_ties))
    def _():
        for kb in range(nkb):
            adm = kb * tk + kloc <= qpos
            sel = jnp.logical_and(keys[kb] >= thr, adm)
            o_ref[kb] = jnp.where(sel, 0.0, MASKED).astype(o_ref.dtype)

    @pl.when(has_ties)
    def _():
        acc = jnp.zeros((tq, tk), jnp.int32)
        for kb in range(nkb):
            acc = acc + jnp.where(keys[kb] > thr, 1, 0)
        room = (k_sel - jnp.sum(acc, axis=1, keepdims=True)).astype(F32)
        upper = jnp.where(lax.broadcasted_iota(jnp.int32, (tk, tk), 0) <= lax.broadcasted_iota(jnp.int32, (tk, tk), 1),
                          1.0, 0.0).astype(BF16)
        seen = jnp.zeros((tq, 1), F32)
        for kb in range(nkb):
            key = keys[kb]
            eq = key == thr
            eqf = jnp.where(eq, 1.0, 0.0)
            rank = seen + _dot(eqf.astype(BF16), upper)
            seen = seen + jnp.sum(eqf, axis=1, keepdims=True)
            adm = kb * tk + kloc <= qpos
            sel = jnp.logical_and(jnp.logical_or(key > thr, jnp.logical_and(eq, rank <= room)), adm)
            o_ref[kb] = jnp.where(sel, 0.0, MASKED).astype(o_ref.dtype)


def _at_index(proj, aw, batch, seq, k_sel):
    tq = tk = AT_TILE
    nq = seq // tq
    width = aw['width']
    nh = aw['idx_heads']
    qio = (4 * width) // (nh * IDX_DIM)
    kio = (4 * width + nh * IDX_DIM) // IDX_DIM
    kern = functools.partial(_at_index_kernel, k_sel=k_sel, score_scale=IDX_DIM ** -0.5 * nh ** -0.5)
    return pl.pallas_call(
        kern, out_shape=jax.ShapeDtypeStruct((batch * nq, seq // tk, tq, tk), BF16), grid=(batch, nq),
        in_specs=[pl.BlockSpec((tq, nh * IDX_DIM), lambda b, q: (b * nq + q, qio)),
                  pl.BlockSpec((tq, LANES), lambda b, q: (b * nq + q, kio + 1)),
                  pl.BlockSpec((seq, IDX_DIM), lambda b, q: (b, kio))],
        out_specs=pl.BlockSpec((None, seq // tk, tq, tk), lambda b, q: (b * nq + q, 0, 0, 0)),
        scratch_shapes=[pltpu.VMEM((seq // tk, tq, tk), jnp.int32), pltpu.VMEM((tq, tk), jnp.int32)],
        compiler_params=_params(("parallel", "parallel")), name="at_index")(proj, proj, proj)


AT_HEAD_GROUP = 4
MASKED = 2.0 * NEG


def _at_attend_kernel(q_ref, k_ref, vt_ref, z_ref, mask_ref, bias_ref, o_ref, acc, m_scr, l_scr):
    qb = pl.program_id(2)
    t = q_ref.shape[0]
    hg = q_ref.shape[1] // AT_DIM
    acc[...] = jnp.zeros_like(acc)
    m_scr[...] = jnp.full(m_scr.shape, NEG, F32)
    l_scr[...] = jnp.zeros_like(l_scr)

    heads = [slice(h * AT_DIM, (h + 1) * AT_DIM) for h in range(hg)]
    q3t = jnp.stack([q_ref[:, sl].T for sl in heads]).astype(BF16)

    def key_tile(kb, carry):
        rows = pl.ds(pl.multiple_of(kb * t, t), t)
        k3 = jnp.stack([k_ref[rows, sl] for sl in heads])
        s_t = _bdot(k3, q3t) + bias_ref[jnp.minimum(qb - kb, 2)] + mask_ref[kb].astype(F32)
        m_old = m_scr[...]
        m_new = jnp.maximum(m_old, jnp.max(s_t, axis=1, keepdims=True))
        alpha = jnp.exp(m_old - m_new)
        p_t = jnp.exp(s_t - m_new)
        l_scr[...] = alpha * l_scr[...] + jnp.sum(p_t, axis=1, keepdims=True)
        acc[...] = alpha * acc[...] + _bdot(vt_ref[:, kb], p_t.astype(BF16))
        m_scr[...] = m_new
        return carry

    lax.fori_loop(0, qb + 1, key_tile, 0)
    o_t = acc[...] / l_scr[...]
    for h, sl in enumerate(heads):
        o_ref[:, sl] = (o_t[h].T * _silu(z_ref[:, sl])).astype(o_ref.dtype)


def _at_attend(proj, proj_bf, maskadd_t, tiles_t, aw, batch, seq):
    t = AT_TILE
    nq = seq // t
    width = aw['width']
    heads = width // AT_DIM
    hg = AT_HEAD_GROUP
    gw = hg * AT_DIM
    ng = width // gw
    v_t = proj_bf[:, 2 * width:3 * width].reshape(batch, nq, t, heads, AT_DIM).transpose(0, 3, 1, 4, 2)
    return pl.pallas_call(
        _at_attend_kernel, out_shape=jax.ShapeDtypeStruct((batch * seq, width), BF16), grid=(batch, ng, nq),
        in_specs=[pl.BlockSpec((t, gw), lambda b, g, q: (b * nq + q, g)),
                  pl.BlockSpec((seq, gw), lambda b, g, q: (b, ng + g)),
                  pl.BlockSpec((None, hg, nq, AT_DIM, t), lambda b, g, q: (b, g, 0, 0, 0)),
                  pl.BlockSpec((t, gw), lambda b, g, q: (b * nq + q, 3 * ng + g)),
                  pl.BlockSpec((None, nq, t, t), lambda b, g, q: (b * nq + q, 0, 0, 0)),
                  pl.BlockSpec((3, hg, t, t), lambda b, g, q: (0, g, 0, 0))],
        out_specs=pl.BlockSpec((t, gw), lambda b, g, q: (b * nq + q, g)),
        scratch_shapes=[pltpu.VMEM((hg, AT_DIM, t), F32), pltpu.VMEM((hg, 1, t), F32), pltpu.VMEM((hg, 1, t), F32)],
        compiler_params=_params(("parallel", "parallel", "arbitrary")), name="at_attend")(
            proj, proj_bf, v_t, proj, maskadd_t, tiles_t)


def _at_weights(w_in, rel_bias, w_out):
    width = w_out.shape[0]
    heads = rel_bias.shape[1]
    idx_heads = (w_in.shape[1] - 4 * width - IDX_DIM) // (IDX_DIM + 1)
    col_scale = jnp.where(jnp.arange(w_in.shape[1]) < width, AT_DIM ** -0.5, 1.0).astype(F32)
    return dict(w_in=_pad_cols((w_in.astype(F32) * col_scale).astype(BF16), 512), rel_bias=rel_bias.astype(F32),
                w_out=w_out.astype(BF16), width=width, heads=heads, idx_heads=idx_heads)


def _at_rows(proj, aw, lead):
    width, heads = aw['width'], aw['heads']
    k = proj[:, width:2 * width].reshape(lead + (heads, AT_DIM))
    v = proj[:, 2 * width:3 * width].reshape(lead + (heads, AT_DIM))
    o = 4 * width + aw['idx_heads'] * IDX_DIM
    ki = proj[:, o:o + IDX_DIM].reshape(lead + (IDX_DIM,))
    return k, v, ki


def _at_layer_prompt(x, mods, aw, *, batch, seq, tm):
    g, scale, shift, gate = mods
    proj, proj_bf = _in_proj(x, g, scale, shift, aw['w_in'], batch_kind=True, rows_per_batch=seq, tm=tm, tn=MM_TN,
                             name="at_in_proj", bf16_copy=True)
    k_sel = min(TOPK_MAX, seq // 4)
    maskadd = _at_index(proj, aw, batch, seq, k_sel)
    tiles = _relbias_tiles(aw['rel_bias'])
    a = _at_attend(proj, proj_bf, jnp.swapaxes(maskadd, 2, 3), jnp.swapaxes(tiles, 2, 3), aw, batch, seq)
    x_new = _out_proj([('row', a)], aw['w_out'], x, gate, batch_kind=True, rows_per_batch=seq, tm=tm, tn=MM_TN,
                      name="at_out_proj")
    return (x_new,) + _at_rows(proj, aw, (batch, seq))


def _at_page_scores_kernel(pt_ref, qi_ref, w_ref, kp_ref, o_ref):
    d = _dot_nt(qi_ref[...].astype(BF16), kp_ref[...].astype(BF16))
    o_ref[...] = jnp.sum(w_ref[...] * jnp.maximum(d, 0.0), axis=0, keepdims=True)


def _at_page_scores(qi3, w3, cache_kidx, page_table):
    rows, nh, _ = qi3.shape
    npages = page_table.shape[1]
    page = cache_kidx.shape[1]
    grid_spec = pltpu.PrefetchScalarGridSpec(
        num_scalar_prefetch=1, grid=(rows, npages),
        in_specs=[pl.BlockSpec((None, nh, IDX_DIM), lambda b, p, pt: (b, 0, 0)),
                  pl.BlockSpec((None, nh, IDX_DIM), lambda b, p, pt: (b, 0, 0)),
                  pl.BlockSpec((None, page, IDX_DIM), lambda b, p, pt: (pt[b, p], 0, 0))],
        out_specs=pl.BlockSpec((None, None, 1, page), lambda b, p, pt: (b, p, 0, 0)))
    return pl.pallas_call(
        _at_page_scores_kernel, out_shape=jax.ShapeDtypeStruct((rows, npages, 1, page), F32), grid_spec=grid_spec,
        compiler_params=_params(("parallel", "arbitrary")), name="at_page_scores")(page_table, qi3, w3, cache_kidx)


def _at_sample_select_kernel(sc_ref, qi_ref, w_ref, kin_ref, rbt_ref, mask_ref, newadd_ref, bias_ref, *, k_sel):
    rows, npages, page = sc_ref.shape
    upper = jnp.where(lax.broadcasted_iota(jnp.int32, (page, page), 0) <= lax.broadcasted_iota(jnp.int32, (page, page), 1),
                      1.0, 0.0).astype(BF16)
    lower = jnp.where(lax.broadcasted_iota(jnp.int32, (npages, npages), 1) < lax.broadcasted_iota(jnp.int32, (npages, npages), 0),
                      1.0, 0.0).astype(BF16)

    def total(x):
        return jnp.sum(jnp.sum(x, axis=1, keepdims=True), axis=0, keepdims=True)

    for b in range(rows):
        keys = _sort_key(sc_ref[b])
        dots = jnp.sum(qi_ref[b] * kin_ref[b:b + 1, :], axis=1, keepdims=True)
        s_new = jnp.sum(w_ref[b][:, 0:1] * jnp.maximum(dots, 0.0), axis=0, keepdims=True)
        key_new = _sort_key(s_new)

        def count_ge(t):
            return total(jnp.where(keys >= t, 1, 0)) + jnp.where(key_new >= t, 1, 0)

        thr = _kth_largest_key(count_ge, (1, 1), k_sel)
        n_gt = total(jnp.where(keys > thr, 1.0, 0.0)) + jnp.where(key_new > thr, 1.0, 0.0)
        room = k_sel - n_gt
        eq = keys == thr
        eqf = jnp.where(eq, 1.0, 0.0)
        row_cnt = jnp.broadcast_to(jnp.sum(eqf, axis=1, keepdims=True), (npages, page))
        rank = _dot(lower, row_cnt.astype(BF16)) + _dot(eqf.astype(BF16), upper)
        sel = jnp.logical_or(keys > thr, jnp.logical_and(eq, rank <= room))
        mask_ref[b] = jnp.where(sel, 0.0, NEG)
        sel_new = jnp.logical_or(key_new > thr, jnp.logical_and(key_new == thr, total(eqf) + 1.0 <= room))
        newadd_ref[b:b + 1, :] = jnp.broadcast_to(jnp.where(sel_new, 0.0, NEG), (1, page))

    nh = rbt_ref.shape[0]
    bias_ref[0] = jnp.broadcast_to(rbt_ref[:, REL_BUCKETS - 1:REL_BUCKETS], (nh, page))
    dist = page - lax.broadcasted_iota(jnp.int32, (nh, page), 1)
    bias_ref[1] = _bias_from_dist(dist, lambda bk: rbt_ref[:, bk:bk + 1])


def _at_sample_select(scores, qi3, w3, ki_new, rbt, k_sel):
    rows, npages, page = scores.shape
    assert page >= REL_MAX_DIST
    kern = functools.partial(_at_sample_select_kernel, k_sel=k_sel)
    return pl.pallas_call(
        kern, out_shape=(jax.ShapeDtypeStruct((rows, npages, page), F32), jax.ShapeDtypeStruct((rows, page), F32),
                         jax.ShapeDtypeStruct((2, rbt.shape[0], page), F32)),
        name="at_sample_select", compiler_params=pltpu.CompilerParams(vmem_limit_bytes=V7X_VMEM_LIMIT_BYTES))(
            scores, qi3, w3, ki_new, rbt)


def _at_sample_attend_kernel(pt_ref, q_ref, k_ref, v_ref, mask_ref, newadd_ref, bias_ref, rbt_ref, kn_ref, vn_ref,
                             z_ref, o_ref, qs, acc, m_scr, l_scr, *, scale):
    p = pl.program_id(1)
    last = pl.num_programs(1) - 1
    page, nh, d = k_ref.shape
    ncol = page * nh
    own = (lax.broadcasted_iota(jnp.int32, (nh, ncol), 1) & (nh - 1)) == lax.broadcasted_iota(jnp.int32, (nh, ncol), 0)

    @pl.when(p == 0)
    def _():
        qs[...] = (q_ref[...] * scale).astype(BF16)
        acc[...] = jnp.zeros_like(acc)
        m_scr[...] = jnp.full(m_scr.shape, NEG, F32)
        l_scr[...] = jnp.zeros_like(l_scr)

    keep = jnp.logical_and(own, mask_ref[...] >= 0.0)
    kf = k_ref[...].reshape(ncol, d).astype(BF16)
    s = jnp.where(keep, _dot_nt(qs[...], kf) + bias_ref[jnp.where(p == last, 1, 0)], NEG)
    m_old = m_scr[...]
    m_new = jnp.maximum(m_old, jnp.max(s, axis=1, keepdims=True))
    alpha = jnp.exp(m_old - m_new)
    pr = jnp.where(keep, jnp.exp(s - m_new), 0.0)
    l_scr[...] = alpha * l_scr[...] + jnp.sum(pr, axis=1, keepdims=True)
    acc[...] = alpha * acc[...] + _dot(pr.astype(BF16), v_ref[...].reshape(ncol, d).astype(BF16))
    m_scr[...] = m_new

    @pl.when(p == last)
    def _():
        nadd = newadd_ref[:, 0:1]
        s_n = jnp.sum(qs[...].astype(F32) * kn_ref[...], axis=1, keepdims=True) + rbt_ref[:, 0:1] + nadd
        m_o = m_scr[...]
        m_n = jnp.maximum(m_o, s_n)
        al = jnp.exp(m_o - m_n)
        p_n = jnp.where(nadd < 0.0, 0.0, jnp.exp(s_n - m_n))
        l_n = al * l_scr[...] + p_n
        o_ref[...] = (al * acc[...] + p_n * vn_ref[...]) / l_n * _silu(z_ref[...])


def _at_sample_attend(q3, cache_k, cache_v, page_table, maskadd, newadd, bias, rbt, kn3, vn3, z3):
    rows, nh, d = q3.shape
    assert nh & (nh - 1) == 0
    npages = page_table.shape[1]
    page = cache_k.shape[1]
    ncol = page * nh
    row3 = pl.BlockSpec((None, nh, d), lambda b, p, pt: (b, 0, 0))
    grid_spec = pltpu.PrefetchScalarGridSpec(
        num_scalar_prefetch=1, grid=(rows, npages),
        in_specs=[row3,
                  pl.BlockSpec((None, page, nh, d), lambda b, p, pt: (pt[b, p], 0, 0, 0)),
                  pl.BlockSpec((None, page, nh, d), lambda b, p, pt: (pt[b, p], 0, 0, 0)),
                  pl.BlockSpec((None, None, 1, ncol), lambda b, p, pt: (b, p, 0, 0)),
                  pl.BlockSpec((None, 1, page), lambda b, p, pt: (b, 0, 0)),
                  pl.BlockSpec((2, nh, ncol), lambda b, p, pt: (0, 0, 0)),
                  pl.BlockSpec(rbt.shape, lambda b, p, pt: (0, 0)),
                  row3, row3, row3],
        out_specs=row3,
        scratch_shapes=[pltpu.VMEM((nh, d), BF16), pltpu.VMEM((nh, d), F32),
                        pltpu.VMEM((nh, 1), F32), pltpu.VMEM((nh, 1), F32)])
    kern = functools.partial(_at_sample_attend_kernel, scale=1.0)
    return pl.pallas_call(
        kern, out_shape=jax.ShapeDtypeStruct((rows, nh, d), F32), grid_spec=grid_spec,
        compiler_params=_params(("parallel", "arbitrary")), name="at_sample_attend")(
            page_table, q3, cache_k, cache_v, maskadd, newadd, bias, rbt, kn3, vn3, z3)


def _at_layer_sample(x, mods, cache_k, cache_v, cache_kidx, page_table, aw):
    g, scale, shift, gate = mods
    rows = x.shape[0]
    width, heads, nh = aw['width'], aw['heads'], aw['idx_heads']
    proj = _in_proj(x, g, scale, shift, aw['w_in'], batch_kind=False, rows_per_batch=1, tm=SUBLANES, tn=MM_TN,
                    name="at_in_proj")
    npool, page = cache_k.shape[:2]
    npages = page_table.shape[1]
    past = npages * page
    k_sel = min(TOPK_MAX, (past + 1) // 4)
    o = 4 * width
    qi3 = proj[:, o:o + nh * IDX_DIM].reshape(rows, nh, IDX_DIM)
    ki_new = proj[:, o + nh * IDX_DIM:o + nh * IDX_DIM + IDX_DIM]
    wi = proj[:, o + nh * IDX_DIM + IDX_DIM:o + nh * IDX_DIM + IDX_DIM + nh] * (IDX_DIM ** -0.5 * nh ** -0.5)
    w3 = jnp.broadcast_to(wi[:, :, None], (rows, nh, IDX_DIM))
    scores = _at_page_scores(qi3, w3, cache_kidx.astype(F32), page_table).reshape(rows, npages, page)
    rbt = aw['rel_bias'].T
    maskadd, newadd, bias = _at_sample_select(scores, qi3, w3, ki_new, rbt, k_sel)
    r3 = lambda t: t.reshape(rows, heads, AT_DIM)
    mask_cols = jnp.repeat(maskadd, heads, axis=-1).reshape(rows, npages, 1, page * heads)
    bias_cols = jnp.repeat(bias, heads, axis=-1)
    a = _at_sample_attend(r3(proj[:, :width]), cache_k.astype(F32), cache_v.astype(F32), page_table,
                          mask_cols, newadd.reshape(rows, 1, page), bias_cols, rbt,
                          r3(proj[:, width:2 * width]), r3(proj[:, 2 * width:3 * width]),
                          r3(proj[:, 3 * width:4 * width]))
    x_new = _out_proj([('row', a.reshape(rows, width))], aw['w_out'], x, gate, batch_kind=False, rows_per_batch=1,
                      tm=SUBLANES, tn=MM_TN, name="at_out_proj", prologue=_cast_prologue)
    return (x_new,) + _at_rows(proj, aw, (rows, 1))


def kernel(x_prompt, x_sample, state_s5_re, state_s5_im, state_m2_conv, state_m2_ssm, state_gd_conv, state_gd_ssm, cache_k, cache_v, cache_kidx, page_table, c_prompt, c_sample, norm_g, w_mod, b_mod, final_g, s5_w_in, s5_lam_re, s5_lam_im, s5_log_dt, s5_b_re, s5_b_im, s5_c_re, s5_c_im, s5_d, s5_w_glu, s5_b_glu, s5_w_out, m2_w_in, m2_conv_w, m2_conv_b, m2_dt_bias, m2_a_log, m2_d, m2_norm, m2_w_out, gd_w_in, gd_conv_w, gd_a_log, gd_dt_bias, gd_norm, gd_w_out, at_w_in, rel_bias, at_w_out):
    f = F32
    bp, seq, d = x_prompt.shape
    bs = x_sample.shape[0]
    depth = norm_g.shape[0]
    xp = x_prompt.astype(f).reshape(bp * seq, d)
    xs = x_sample.astype(f).reshape(bs, d)

    pad_rows = (-(bs + bp)) % SUBLANES
    c_all = jnp.concatenate([c_sample.astype(f), c_prompt.astype(f), jnp.zeros((pad_rows, d), f)], axis=0)
    mod = _modulation(c_all, w_mod, b_mod)

    def mods(i, prompt):
        g = norm_g[i].astype(f).reshape(1, d)
        rows = mod[i, bs:bs + bp] if prompt else mod[i, :bs]
        shift, scale, gate = rows[:, :d], rows[:, d:2 * d], rows[:, 2 * d:]
        if prompt:
            return g, scale[:, None, :], shift[:, None, :], gate[:, None, :]
        return g, scale, shift, gate

    s5w = _s5_weights(s5_w_in, s5_lam_re, s5_lam_im, s5_log_dt, s5_b_re, s5_b_im, s5_c_re, s5_c_im, s5_d,
                      s5_w_glu, s5_b_glu, s5_w_out)
    tm_p = MM_TM

    xp, s5_re_p, s5_im_p = _s5_layer(xp, mods(0, True), None, s5w, prompt=True, batch=bp, seq=seq, tm=tm_p)
    xs, s5_re_s, s5_im_s = _s5_layer(xs, mods(0, False), (state_s5_re, state_s5_im), s5w, prompt=False,
                                     batch=bs, seq=1, tm=SUBLANES)
    groups, nstate = state_s5_re.shape[1:]
    s5_re_p, s5_im_p = s5_re_p.reshape(bp, groups, nstate), s5_im_p.reshape(bp, groups, nstate)
    s5_re_s, s5_im_s = s5_re_s.reshape(bs, groups, nstate), s5_im_s.reshape(bs, groups, nstate)

    m2w = _m2_weights(m2_w_in, m2_conv_w, m2_conv_b, m2_dt_bias, m2_a_log, m2_d, m2_norm, m2_w_out)
    xp, m2_conv_p, m2_ssm_p = _m2_layer(xp, mods(1, True), None, m2w, prompt=True, batch=bp, seq=seq, tm=tm_p)
    xs, m2_conv_s, m2_ssm_s = _m2_layer(xs, mods(1, False), (state_m2_conv, state_m2_ssm), m2w, prompt=False,
                                        batch=bs, seq=1, tm=SUBLANES)
    m2_ssm_p = m2_ssm_p.reshape((bp,) + state_m2_ssm.shape[1:])
    m2_ssm_s = m2_ssm_s.reshape(state_m2_ssm.shape)

    gdw = _gd_weights(gd_w_in, gd_conv_w, gd_a_log, gd_dt_bias, gd_norm, gd_w_out)
    xp, gd_conv_p, gd_ssm_p = _gd_layer(xp, mods(2, True), None, gdw, prompt=True, batch=bp, seq=seq, tm=tm_p)
    xs, gd_conv_s, gd_ssm_s = _gd_layer(xs, mods(2, False), (state_gd_conv, state_gd_ssm), gdw, prompt=False,
                                        batch=bs, seq=1, tm=SUBLANES)

    atw = _at_weights(at_w_in, rel_bias, at_w_out)
    xp, k_rows_p, v_rows_p, kidx_rows_p = _at_layer_prompt(xp, mods(3, True), atw, batch=bp, seq=seq, tm=tm_p)
    xs, k_rows_s, v_rows_s, kidx_rows_s = _at_layer_sample(xs, mods(3, False), cache_k, cache_v, cache_kidx,
                                                           page_table, atw)

    y_prompt = _final_norm(xp, final_g).reshape(x_prompt.shape).astype(x_prompt.dtype)
    y_sample = _final_norm(xs, final_g).reshape(x_sample.shape).astype(x_sample.dtype)
    return (y_prompt, y_sample, s5_re_p, s5_im_p, s5_re_s, s5_im_s, m2_conv_p, m2_ssm_p, m2_conv_s, m2_ssm_s,
            gd_conv_p, gd_ssm_p, gd_conv_s, gd_ssm_s,
            k_rows_p, v_rows_p, kidx_rows_p, k_rows_s, v_rows_s, kidx_rows_s)
```

```python
import functools
import math

import numpy as np
import jax
import jax.numpy as jnp
from jax import lax
from jax.experimental import pallas as pl
from jax.experimental.pallas import tpu as pltpu

F32 = jnp.float32
BF16 = jnp.bfloat16

EPS = 1e-6
NEG = -1e30
CONV_W = 4
V7X_VMEM_LIMIT_BYTES = 56 * 1024 * 1024
LANES = 128
SUBLANES = 8
MM_TM = 1024
MM_TM_WIDE_ROWS = 512
MM_TN = 1024

S5_GROUP = 16
S5_STATE = 64
S5_CHUNK = 256
S5_SEG = S5_CHUNK // SUBLANES
S5_BLK_CH = 256
S5_BLK_ST = 1024


def _params(sem):
    return pltpu.CompilerParams(dimension_semantics=sem, vmem_limit_bytes=V7X_VMEM_LIMIT_BYTES)


def _sigmoid(x):
    return 1.0 / (1.0 + jnp.exp(-x))


def _silu(x):
    return x * _sigmoid(x)


def _gelu(x):
    return 0.5 * x * (1.0 + jnp.tanh(math.sqrt(2.0 / math.pi) * (x + 0.044715 * (x * x * x))))


def _softplus(x):
    return jnp.maximum(x, 0.0) + jnp.log1p(jnp.exp(-jnp.abs(x)))


def _dot(a, b):
    return jnp.dot(a, b, preferred_element_type=F32)


def _dot_nt(a, b):
    return lax.dot_general(a, b, (((1,), (1,)), ((), ())), preferred_element_type=F32)


def _split3(x):
    hi = x.astype(BF16)
    r1 = x - hi.astype(F32)
    mid = r1.astype(BF16)
    lo = (r1 - mid.astype(F32)).astype(BF16)
    return hi, mid, lo


def _dot_exact_lhs(sel, x):
    hi, mid, lo = _split3(x)
    return _dot(sel, hi) + (_dot(sel, mid) + _dot(sel, lo))


def _dot_f32(a, b):
    ah, am, al = _split3(a)
    bh, bm, bl = _split3(b)
    small = _dot(am, bm) + _dot(ah, bl) + _dot(al, bh)
    return _dot(ah, bh) + (_dot(ah, bm) + _dot(am, bh) + small)


def _mm_kernel(*refs, n_a, n_e, prologue, epilogue, bf16_copy):
    a_refs = refs[:n_a]
    w_ref = refs[n_a]
    e_refs = refs[n_a + 1:n_a + 1 + n_e]
    o_ref = refs[n_a + 1 + n_e]
    n_out = 2 if bf16_copy else 1
    if prologue is None:
        a = a_refs[0][...]
    else:
        a_scr = refs[n_a + 1 + n_e + n_out]

        @pl.when(pl.program_id(1) == 0)
        def _():
            a_scr[...] = prologue(*[r[...] for r in a_refs]).astype(BF16)

        a = a_scr[...]
    acc = _dot(a, w_ref[...])
    out = epilogue(acc, *[r[...] for r in e_refs])
    o_ref[...] = out.astype(o_ref.dtype)
    if bf16_copy:
        refs[n_a + 2 + n_e][...] = out.astype(BF16)


def _fused_matmul(a_ins, w, e_ins, *, prologue, epilogue, out_dtype, tm, tn, rows_per_batch=None, name,
                  bf16_copy=False):
    m = next(item[1].shape[0] for item in a_ins if item[0] == 'row')
    k, n = w.shape
    tm = min(tm, m)
    tn = next(t for t in (1024, 768, 512, 384, 256, 128) if t <= tn and n % t == 0)
    assert m % tm == 0
    rpb = rows_per_batch

    def bidx(i):
        return (i * tm) // rpb

    in_specs, args = [], []
    for item in a_ins:
        kind, arr = item[0], item[1]
        wd = item[2] if len(item) > 2 else arr.shape[-1]
        coff = item[3] if len(item) > 3 else 0
        if kind == 'row':
            in_specs.append(pl.BlockSpec((tm, wd), lambda i, j, coff=coff: (i, coff)))
        elif kind == 'vec':
            in_specs.append(pl.BlockSpec((1, wd), lambda i, j: (0, 0)))
        else:
            in_specs.append(pl.BlockSpec((None, 1, wd), lambda i, j: (bidx(i), 0, 0)))
        args.append(arr)
    in_specs.append(pl.BlockSpec((k, tn), lambda i, j: (0, j)))
    args.append(w)
    for item in e_ins:
        kind, arr = item[0], item[1]
        off = (item[2] if len(item) > 2 else 0) // tn
        if kind == 'tile':
            assert len(item) < 3 or item[2] % tn == 0
            in_specs.append(pl.BlockSpec((tm, tn), lambda i, j, off=off: (i, j + off)))
        elif kind == 'col':
            in_specs.append(pl.BlockSpec((1, tn), lambda i, j: (0, j)))
        else:
            in_specs.append(pl.BlockSpec((None, 1, tn), lambda i, j: (bidx(i), 0, j)))
        args.append(arr)
    scratch = [] if prologue is None else [pltpu.VMEM((tm, k), BF16)]
    kern = functools.partial(_mm_kernel, n_a=len(a_ins), n_e=len(e_ins), prologue=prologue, epilogue=epilogue,
                             bf16_copy=bf16_copy)
    out_shape = jax.ShapeDtypeStruct((m, n), out_dtype)
    out_spec = pl.BlockSpec((tm, tn), lambda i, j: (i, j))
    if bf16_copy:
        out_shape, out_spec = (out_shape, jax.ShapeDtypeStruct((m, n), BF16)), (out_spec, out_spec)
    return pl.pallas_call(
        kern, out_shape=out_shape, grid=(m // tm, n // tn), in_specs=in_specs, out_specs=out_spec,
        scratch_shapes=scratch, compiler_params=_params(("parallel", "arbitrary")), name=name)(*args)


def _pad_cols(w, mult):
    n = w.shape[-1]
    npad = (-n) % mult
    if npad:
        w = jnp.pad(w, ((0, 0), (0, npad)))
    return w


def _modnorm_prologue(x, g, scale, shift):
    r = x * lax.rsqrt(jnp.mean(x * x, axis=-1, keepdims=True) + EPS) * g
    return r * (1.0 + scale) + shift


def _identity_epilogue(acc):
    return acc


def _residual_epilogue(acc, x, gate):
    return x + gate * acc


def _in_proj(x, g, scale, shift, w, *, batch_kind, rows_per_batch, tm, tn, name, bf16_copy=False):
    kind = 'batch' if batch_kind else 'row'
    return _fused_matmul([('row', x), ('vec', g), (kind, scale), (kind, shift)], w, [],
                         prologue=_modnorm_prologue, epilogue=_identity_epilogue, out_dtype=F32,
                         tm=tm, tn=tn, rows_per_batch=rows_per_batch, name=name, bf16_copy=bf16_copy)


def _gate_proj_kernel(x_ref, g_ref, scale_ref, shift_ref, w_ref, o_ref):
    h = _modnorm_prologue(x_ref[...], g_ref[...], scale_ref[...], shift_ref[...])
    o_ref[...] = _dot_f32(h, w_ref[...])


def _gate_proj(x, g, scale, shift, w, *, batch_kind, rows_per_batch, tm, name):
    m, d = x.shape
    n = w.shape[1]
    tm = min(tm, m, MM_TM_WIDE_ROWS)
    if batch_kind:
        mod_spec = pl.BlockSpec((None, 1, d), lambda i: ((i * tm) // rows_per_batch, 0, 0))
    else:
        mod_spec = pl.BlockSpec((tm, d), lambda i: (i, 0))
    return pl.pallas_call(
        _gate_proj_kernel, out_shape=jax.ShapeDtypeStruct((m, n), F32), grid=(m // tm,),
        in_specs=[pl.BlockSpec((tm, d), lambda i: (i, 0)), pl.BlockSpec((1, d), lambda i: (0, 0)),
                  mod_spec, mod_spec, pl.BlockSpec((d, n), lambda i: (0, 0))],
        out_specs=pl.BlockSpec((tm, n), lambda i: (i, 0)),
        compiler_params=_params(("parallel",)), name=name)(x, g, scale, shift, w)


def _out_proj(a_ins, w, x, gate, *, batch_kind, rows_per_batch, tm, tn, name, prologue=None):
    kind = 'batchcol' if batch_kind else 'tile'
    return _fused_matmul(a_ins, w, [('tile', x), (kind, gate)], prologue=prologue, epilogue=_residual_epilogue,
                         out_dtype=F32, tm=tm, tn=tn, rows_per_batch=rows_per_batch, name=name)


def _mod_kernel(c_ref, w_ref, b_ref, o_ref):
    o_ref[...] = _dot(c_ref[...].astype(BF16), w_ref[...].astype(BF16)) + b_ref[...]


def _modulation(c_all, w_mod, b_mod, tn=512):
    depth, d, n = w_mod.shape
    rows = c_all.shape[0]
    return pl.pallas_call(
        _mod_kernel, out_shape=jax.ShapeDtypeStruct((depth, rows, n), F32), grid=(depth, n // tn),
        in_specs=[pl.BlockSpec((rows, d), lambda l, j: (0, 0)),
                  pl.BlockSpec((None, d, tn), lambda l, j: (l, 0, j)),
                  pl.BlockSpec((None, 1, tn), lambda l, j: (l, 0, j))],
        out_specs=pl.BlockSpec((None, rows, tn), lambda l, j: (l, 0, j)),
        compiler_params=_params(("parallel", "parallel")), name="adaln_modulation")(
            c_all, w_mod, b_mod.reshape(depth, 1, n))


def _rmsnorm_kernel(x_ref, g_ref, o_ref):
    x = x_ref[...]
    o_ref[...] = x * lax.rsqrt(jnp.mean(x * x, axis=-1, keepdims=True) + EPS) * g_ref[...]


def _final_norm(x, g, tm=512):
    m, d = x.shape
    tm = min(tm, m)
    return pl.pallas_call(
        _rmsnorm_kernel, out_shape=jax.ShapeDtypeStruct((m, d), F32), grid=(m // tm,),
        in_specs=[pl.BlockSpec((tm, d), lambda i: (i, 0)), pl.BlockSpec((1, d), lambda i: (0, 0))],
        out_specs=pl.BlockSpec((tm, d), lambda i: (i, 0)),
        compiler_params=_params(("parallel",)), name="final_rmsnorm")(x, g.reshape(1, d))


def _s5_tables(lam_re, lam_im, log_dt, b_re, b_im, c_re, c_im, d_skip):
    f = F32
    groups, p = lam_re.shape
    nblk = groups * S5_GROUP // S5_BLK_CH
    gpb = groups // nblk
    lr, li = lam_re.astype(f), lam_im.astype(f)
    dt = jnp.exp(log_dt.astype(f))[:, None]
    ldr, ldi = lr * dt, li * dt
    kk = jnp.arange(1, S5_SEG + 1, dtype=f)[:, None, None]
    pmag = jnp.exp(kk * ldr)
    pw_re, pw_im = pmag * jnp.cos(kk * ldi), pmag * jnp.sin(kk * ldi)
    ab_re, ab_im = jnp.exp(ldr) * jnp.cos(ldi), jnp.exp(ldr) * jnp.sin(ldi)
    den = lr * lr + li * li
    nr, ni = ab_re - 1.0, ab_im
    fr, fi = (nr * lr + ni * li) / den, (ni * lr - nr * li) / den
    bre, bim = b_re.astype(f), b_im.astype(f)
    bb_re = fr[..., None] * bre - fi[..., None] * bim
    bb_im = fr[..., None] * bim + fi[..., None] * bre
    eye = jnp.eye(gpb, dtype=f)

    def bd_in(bb):
        t = bb.reshape(nblk, gpb, p, S5_GROUP).transpose(0, 1, 3, 2)
        return jnp.einsum('bgkp,gh->bgkhp', t, eye).reshape(nblk, gpb * S5_GROUP, gpb * p).astype(BF16)

    def bd_out(c):
        t = c.astype(f).reshape(nblk, gpb, S5_GROUP, p).transpose(0, 1, 3, 2)
        return jnp.einsum('bgpk,gh->bgphk', t, eye).reshape(nblk, gpb * p, gpb * S5_GROUP).astype(BF16)

    def lanes(t):
        lead = t.shape[:-2]
        t = t.reshape(lead + (nblk, gpb * p))
        return jnp.moveaxis(t, -2, 0)

    return dict(
        bb_re=bd_in(bb_re), bb_im=bd_in(bb_im), c_re=bd_out(c_re), c_im=bd_out(c_im),
        ab_re=lanes(ab_re[None]), ab_im=lanes(ab_im[None]),
        pw_re=lanes(pw_re), pw_im=lanes(pw_im),
        d=d_skip.astype(f).reshape(1, -1), nblk=nblk)


def _s5_perm():
    pm = np.zeros((S5_CHUNK, S5_CHUNK), np.float32)
    r = np.arange(S5_CHUNK)
    pm[r, (r % SUBLANES) * S5_SEG + r // SUBLANES] = 1.0
    return jnp.asarray(pm, BF16), jnp.asarray(pm.T, BF16)


def _s5_scan_kernel(u_ref, pm_ref, pmt_ref, bbre_ref, bbim_ref, cre_ref, cim_ref, abre_ref, abim_ref,
                    pwre_ref, pwim_ref, d_ref, y_ref, sre_out, sim_out,
                    xre, xim, car_re, car_im, cin_re, cin_im, lend_re, lend_im):
    n = pl.program_id(2)
    nst = xre.shape[1]

    @pl.when(n == 0)
    def _():
        car_re[...] = jnp.zeros_like(car_re)
        car_im[...] = jnp.zeros_like(car_im)

    u = u_ref[...]
    up = _dot(pm_ref[...], u.astype(BF16)).astype(BF16)
    xre[...] = _dot(up, bbre_ref[...])
    xim[...] = _dot(up, bbim_ref[...])
    are = jnp.broadcast_to(abre_ref[...], (SUBLANES, nst))
    aim = jnp.broadcast_to(abim_ref[...], (SUBLANES, nst))
    sre = jnp.zeros((SUBLANES, nst), F32)
    sim = jnp.zeros((SUBLANES, nst), F32)
    for i in range(S5_SEG):
        r = slice(SUBLANES * i, SUBLANES * (i + 1))
        nre = are * sre - aim * sim + xre[r, :]
        nim = are * sim + aim * sre + xim[r, :]
        xre[r, :] = nre
        xim[r, :] = nim
        sre, sim = nre, nim
    lend_re[...] = sre
    lend_im[...] = sim
    a_re = pwre_ref[S5_SEG - 1:S5_SEG, :]
    a_im = pwim_ref[S5_SEG - 1:S5_SEG, :]
    cr, ci = car_re[...], car_im[...]
    for s in range(SUBLANES):
        cin_re[s:s + 1, :] = cr
        cin_im[s:s + 1, :] = ci
        lr, li = lend_re[s:s + 1, :], lend_im[s:s + 1, :]
        cr, ci = a_re * cr - a_im * ci + lr, a_re * ci + a_im * cr + li
    car_re[...] = cr
    car_im[...] = ci
    cinr, cini = cin_re[...], cin_im[...]
    for i in range(S5_SEG):
        r = slice(SUBLANES * i, SUBLANES * (i + 1))
        pr, pi_ = pwre_ref[i:i + 1, :], pwim_ref[i:i + 1, :]
        xre[r, :] = xre[r, :] + (pr * cinr - pi_ * cini)
        xim[r, :] = xim[r, :] + (pr * cini + pi_ * cinr)
    yp = _dot(xre[...].astype(BF16), cre_ref[...]) - _dot(xim[...].astype(BF16), cim_ref[...])
    hi = yp.astype(BF16)
    lo = (yp - hi.astype(F32)).astype(BF16)
    y = _dot(pmt_ref[...], hi) + _dot(pmt_ref[...], lo) + d_ref[...] * u
    y_ref[...] = _gelu(y)

    @pl.when(n == pl.num_programs(2) - 1)
    def _():
        sre_out[...] = cr
        sim_out[...] = ci


def _s5_scan(proj, tabs, batch, seq):
    nblk = tabs['nblk']
    nch = seq // S5_CHUNK
    pm, pmt = _s5_perm()
    nstate = nblk * S5_BLK_ST
    const3 = lambda shape: pl.BlockSpec((None,) + shape, lambda k, b, n: (k, 0, 0))
    y, sre, sim = pl.pallas_call(
        _s5_scan_kernel,
        out_shape=(jax.ShapeDtypeStruct((batch * seq, nblk * S5_BLK_CH), F32),
                   jax.ShapeDtypeStruct((batch, 1, nstate), F32),
                   jax.ShapeDtypeStruct((batch, 1, nstate), F32)),
        grid=(nblk, batch, nch),
        in_specs=[pl.BlockSpec((S5_CHUNK, S5_BLK_CH), lambda k, b, n: (b * nch + n, k)),
                  pl.BlockSpec((S5_CHUNK, S5_CHUNK), lambda k, b, n: (0, 0)),
                  pl.BlockSpec((S5_CHUNK, S5_CHUNK), lambda k, b, n: (0, 0)),
                  const3((S5_BLK_CH, S5_BLK_ST)), const3((S5_BLK_CH, S5_BLK_ST)),
                  const3((S5_BLK_ST, S5_BLK_CH)), const3((S5_BLK_ST, S5_BLK_CH)),
                  const3((1, S5_BLK_ST)), const3((1, S5_BLK_ST)),
                  const3((S5_SEG, S5_BLK_ST)), const3((S5_SEG, S5_BLK_ST)),
                  pl.BlockSpec((1, S5_BLK_CH), lambda k, b, n: (0, k))],
        out_specs=(pl.BlockSpec((S5_CHUNK, S5_BLK_CH), lambda k, b, n: (b * nch + n, k)),
                   pl.BlockSpec((None, 1, S5_BLK_ST), lambda k, b, n: (b, 0, k)),
                   pl.BlockSpec((None, 1, S5_BLK_ST), lambda k, b, n: (b, 0, k))),
        scratch_shapes=[pltpu.VMEM((S5_CHUNK, S5_BLK_ST), F32), pltpu.VMEM((S5_CHUNK, S5_BLK_ST), F32),
                        pltpu.VMEM((1, S5_BLK_ST), F32), pltpu.VMEM((1, S5_BLK_ST), F32),
                        pltpu.VMEM((SUBLANES, S5_BLK_ST), F32), pltpu.VMEM((SUBLANES, S5_BLK_ST), F32),
                        pltpu.VMEM((SUBLANES, S5_BLK_ST), F32), pltpu.VMEM((SUBLANES, S5_BLK_ST), F32)],
        compiler_params=_params(("parallel", "parallel", "arbitrary")), name="s5_scan")(
            proj, pm, pmt, tabs['bb_re'], tabs['bb_im'], tabs['c_re'], tabs['c_im'],
            tabs['ab_re'], tabs['ab_im'], tabs['pw_re'], tabs['pw_im'], tabs['d'])
    return y, sre, sim


def _s5_step_kernel(u_ref, hre_ref, him_ref, bbre_ref, bbim_ref, cre_ref, cim_ref, abre_ref, abim_ref, d_ref,
                    y_ref, sre_out, sim_out):
    u = u_ref[...]
    ub = u.astype(BF16)
    are, aim = abre_ref[...], abim_ref[...]
    hre, him = hre_ref[...], him_ref[...]
    sre = are * hre - aim * him + _dot(ub, bbre_ref[...])
    sim = are * him + aim * hre + _dot(ub, bbim_ref[...])
    sre_out[...] = sre
    sim_out[...] = sim
    y = _dot(sre.astype(BF16), cre_ref[...]) - _dot(sim.astype(BF16), cim_ref[...]) + d_ref[...] * u
    y_ref[...] = _gelu(y)


def _s5_step(proj, h_re, h_im, tabs):
    nblk = tabs['nblk']
    rows = proj.shape[0]
    nstate = nblk * S5_BLK_ST
    const3 = lambda shape: pl.BlockSpec((None,) + shape, lambda k: (k, 0, 0))
    lane_blk = lambda w: pl.BlockSpec((rows, w), lambda k: (0, k))
    return pl.pallas_call(
        _s5_step_kernel,
        out_shape=(jax.ShapeDtypeStruct((rows, nblk * S5_BLK_CH), F32),
                   jax.ShapeDtypeStruct((rows, nstate), F32), jax.ShapeDtypeStruct((rows, nstate), F32)),
        grid=(nblk,),
        in_specs=[lane_blk(S5_BLK_CH), lane_blk(S5_BLK_ST), lane_blk(S5_BLK_ST),
                  const3((S5_BLK_CH, S5_BLK_ST)), const3((S5_BLK_CH, S5_BLK_ST)),
                  const3((S5_BLK_ST, S5_BLK_CH)), const3((S5_BLK_ST, S5_BLK_CH)),
                  const3((1, S5_BLK_ST)), const3((1, S5_BLK_ST)),
                  pl.BlockSpec((1, S5_BLK_CH), lambda k: (0, k))],
        out_specs=(lane_blk(S5_BLK_CH), lane_blk(S5_BLK_ST), lane_blk(S5_BLK_ST)),
        compiler_params=_params(("parallel",)), name="s5_step")(
            proj, h_re.reshape(rows, nstate), h_im.reshape(rows, nstate),
            tabs['bb_re'], tabs['bb_im'], tabs['c_re'], tabs['c_im'], tabs['ab_re'], tabs['ab_im'], tabs['d'])


def _s5_weights(w_in, lam_re, lam_im, log_dt, b_re, b_im, c_re, c_im, d_skip, w_glu, b_glu, w_out):
    tabs = _s5_tables(lam_re, lam_im, log_dt, b_re, b_im, c_re, c_im, d_skip)
    return (w_in.astype(BF16), w_glu.astype(BF16), b_glu.astype(F32).reshape(1, -1), w_out.astype(BF16), tabs)


def _glu_epilogue(acc, gy, z, b):
    return gy * _sigmoid(acc + b) * _silu(z)


def _cast_prologue(a):
    return a


def _s5_layer(x, mods, h_state, w, *, prompt, batch, seq, tm):
    g, scale, shift, gate = mods
    w_in, w_glu, b_glu, w_out, tabs = w
    width = w_glu.shape[0]
    proj = _in_proj(x, g, scale, shift, w_in, batch_kind=prompt, rows_per_batch=seq, tm=tm, tn=MM_TN, name="s5_in_proj")
    if prompt:
        gy, sre, sim = _s5_scan(proj, tabs, batch, seq)
    else:
        gy, sre, sim = _s5_step(proj, h_state[0], h_state[1], tabs)
    a = _fused_matmul([('row', gy)], w_glu, [('tile', gy), ('tile', proj, width), ('col', b_glu)],
                      prologue=_cast_prologue, epilogue=_glu_epilogue, out_dtype=BF16, tm=min(tm, MM_TM_WIDE_ROWS),
                      tn=MM_TN, name="s5_glu")
    x_new = _out_proj([('row', a)], w_out, x, gate, batch_kind=prompt, rows_per_batch=seq, tm=tm, tn=MM_TN, name="s5_out_proj")
    return x_new, sre, sim


def _conv_silu_chunk(x_ref, w_ref, b_ref, pad_ref, first):
    c = x_ref.shape[0]

    @pl.when(first)
    def _():
        pad_ref[0:SUBLANES, :] = jnp.zeros((SUBLANES, pad_ref.shape[1]), F32)

    pad_ref[SUBLANES:SUBLANES + c, :] = x_ref[...]
    acc = w_ref[3:4, :] * pad_ref[SUBLANES:SUBLANES + c, :]
    for j in range(CONV_W - 1):
        off = SUBLANES - (CONV_W - 1) + j
        acc = acc + w_ref[j:j + 1, :] * pad_ref[off:off + c, :]
    if b_ref is not None:
        acc = acc + b_ref[...]
    pad_ref[0:SUBLANES, :] = pad_ref[c:c + SUBLANES, :]
    return _silu(acc)


def _conv_silu_step(x, taps_ref, w_ref, b_ref):
    acc = w_ref[3:4, :] * x
    for j in range(CONV_W - 1):
        acc = acc + w_ref[j:j + 1, :] * taps_ref[j]
    if b_ref is not None:
        acc = acc + b_ref[...]
    return _silu(acc)


def _tri_masks(c):
    t = lax.broadcasted_iota(jnp.int32, (c, c), 0)
    s = lax.broadcasted_iota(jnp.int32, (c, c), 1)
    return s <= t, s < t


def _pad_to_square_t(x, n):
    rows = x.shape[0]
    return jnp.concatenate([x, jnp.zeros((n - rows, n), x.dtype)], axis=0).T


M2_HEADDIM = 64
M2_STATE = 128
M2_GROUPS = 8
M2_CHUNK = 128


def _m2_ssd_kernel(x_ref, b_ref, c_ref, dt_ref, z_ref, wx_ref, wb_ref, wc_ref, bx_ref, bb_ref, bc_ref,
                   dtb_ref, nega_ref, dsk_ref, nw_ref, o_ref, sout_ref, xpad, bpad, cpad, s_ref, y_ref):
    n = pl.program_id(1)
    first = n == 0
    c = x_ref.shape[0]
    npairs = s_ref.shape[0]
    pairs_per_group = npairs // M2_GROUPS

    @pl.when(first)
    def _():
        s_ref[...] = jnp.zeros_like(s_ref)

    xs = _conv_silu_chunk(x_ref, wx_ref, bx_ref, xpad, first)
    bm = _conv_silu_chunk(b_ref, wb_ref, bb_ref, bpad, first).astype(BF16)
    cm = _conv_silu_chunk(c_ref, wc_ref, bc_ref, cpad, first).astype(BF16)
    dtv = _softplus(dt_ref[...] + dtb_ref[...])
    la = nega_ref[...] * dtv
    incl, _ = _tri_masks(c)
    tri = jnp.where(incl, 1.0, 0.0).astype(BF16)
    cum = _dot_exact_lhs(tri, la)
    cum_t = cum.T
    ecum_all = jnp.exp(cum)
    wend_all = jnp.exp(cum[c - 1:c, :] - cum)
    elast_t = jnp.exp(cum_t[:, c - 1:c])
    lane_first = lax.broadcasted_iota(jnp.int32, (c, LANES), 1) < M2_HEADDIM
    row_first = lax.broadcasted_iota(jnp.int32, (LANES, LANES), 0) < M2_HEADDIM

    for g in range(M2_GROUPS):
        bg = bm[:, g * M2_STATE:(g + 1) * M2_STATE]
        cg = cm[:, g * M2_STATE:(g + 1) * M2_STATE]
        gm = _dot_nt(cg, bg)
        for j in range(pairs_per_group):
            p = g * pairs_per_group + j
            ha, hb = 2 * p, 2 * p + 1
            xp = xs[:, p * LANES:(p + 1) * LANES]

            def decay_weights(h):
                seg = cum[:, h:h + 1] - cum_t[h:h + 1, :]
                dec = jnp.where(incl, jnp.exp(jnp.where(incl, seg, 0.0)), 0.0)
                return (gm * dec).astype(BF16)

            xdt = xp * jnp.where(lane_first, dtv[:, ha:ha + 1], dtv[:, hb:hb + 1])
            xdt_a = jnp.where(lane_first, xdt, 0.0)
            xdt_b = xdt - xdt_a
            y = _dot(decay_weights(ha), xdt_a.astype(BF16)) + _dot(decay_weights(hb), xdt_b.astype(BF16))
            sp = s_ref[p]
            y = y + _dot_nt(cg, sp.astype(BF16)) * jnp.where(lane_first, ecum_all[:, ha:ha + 1], ecum_all[:, hb:hb + 1])
            y_ref[:, p * LANES:(p + 1) * LANES] = y + dsk_ref[:, p * LANES:(p + 1) * LANES] * xp
            xw = xdt * jnp.where(lane_first, wend_all[:, ha:ha + 1], wend_all[:, hb:hb + 1])
            dmat = jnp.where(row_first, elast_t[ha:ha + 1, :], elast_t[hb:hb + 1, :])
            s_ref[p] = sp * dmat + _dot(xw.T.astype(BF16), bg)

    o_ref[...] = _gated_rmsnorm_prologue(y_ref[...], z_ref[...], nw_ref[...]).astype(o_ref.dtype)

    @pl.when(n == pl.num_programs(1) - 1)
    def _():
        sout_ref[...] = s_ref[...]


def _m2_ssd(proj, gates, mw, batch, seq):
    c = M2_CHUNK
    nch = seq // c
    width = mw['width']
    gs = M2_GROUPS * M2_STATE
    npairs = width // LANES
    xo, bo, co = width // width, (2 * width) // gs, (2 * width + gs) // gs
    row = lambda w, off: pl.BlockSpec((c, w), lambda b, n: (b * nch + n, off))
    par = lambda r, w, off: pl.BlockSpec((r, w), lambda b, n: (0, off))
    return pl.pallas_call(
        _m2_ssd_kernel,
        out_shape=(jax.ShapeDtypeStruct((batch * seq, width), BF16),
                   jax.ShapeDtypeStruct((batch, npairs, LANES, M2_STATE), F32)),
        grid=(batch, nch),
        in_specs=[row(width, xo), row(gs, bo), row(gs, co), row(LANES, 0), row(width, 0),
                  par(CONV_W, width, 0), par(CONV_W, gs, width // gs), par(CONV_W, gs, width // gs + 1),
                  par(1, width, 0), par(1, gs, width // gs), par(1, gs, width // gs + 1),
                  par(1, LANES, 0), par(1, LANES, 0), par(1, width, 0), par(1, width, 0)],
        out_specs=(pl.BlockSpec((c, width), lambda b, n: (b * nch + n, 0)),
                   pl.BlockSpec((None, npairs, LANES, M2_STATE), lambda b, n: (b, 0, 0, 0))),
        scratch_shapes=[pltpu.VMEM((c + SUBLANES, width), F32), pltpu.VMEM((c + SUBLANES, gs), F32),
                        pltpu.VMEM((c + SUBLANES, gs), F32), pltpu.VMEM((npairs, LANES, M2_STATE), F32),
                        pltpu.VMEM((c, width), F32)],
        compiler_params=_params(("parallel", "arbitrary")), name="m2_ssd")(
            proj, proj, proj, gates, proj, mw['conv_w'], mw['conv_w'], mw['conv_w'], mw['conv_b'], mw['conv_b'],
            mw['conv_b'], mw['dt_bias'], mw['neg_a'], mw['d_ch'], mw['norm'])


def _m2_pre_step_kernel(xbc_ref, dt_ref, taps_ref, w_ref, b_ref, dtb_ref, nega_ref, act_ref, dtv_ref, dec_ref):
    act_ref[...] = _conv_silu_step(xbc_ref[...], taps_ref, w_ref, b_ref)
    dtv = _softplus(dt_ref[...] + dtb_ref[...])
    dtv_ref[...] = dtv
    dec_ref[...] = jnp.exp(nega_ref[...] * dtv)


def _m2_pre_step(proj, dt_raw, taps, mw):
    rows = proj.shape[0]
    width = mw['width']
    cdim = mw['conv_w'].shape[1]
    xbc = proj[:, width:width + cdim]
    return pl.pallas_call(
        _m2_pre_step_kernel,
        out_shape=(jax.ShapeDtypeStruct((rows, cdim), F32), jax.ShapeDtypeStruct((rows, LANES), F32),
                   jax.ShapeDtypeStruct((rows, LANES), F32)),
        name="m2_pre_step", compiler_params=pltpu.CompilerParams(vmem_limit_bytes=V7X_VMEM_LIMIT_BYTES))(
            xbc, dt_raw, taps, mw['conv_w'], mw['conv_b'], mw['dt_bias'], mw['neg_a'])


def _m2_state_step_kernel(x_ref, dtc_ref, decc_ref, b_ref, c_ref, dsk_ref, s_ref, y_ref, sout_ref):
    rows = x_ref.shape[0]
    pairs = x_ref.shape[1] // LANES
    bv = b_ref[...]
    cb = c_ref[...].astype(BF16)
    for j in range(pairs):
        sl = slice(j * LANES, (j + 1) * LANES)
        xp = x_ref[:, sl]
        xdt_t = _pad_to_square_t(xp * dtc_ref[:, sl], LANES)
        dec_t = _pad_to_square_t(decc_ref[:, sl], LANES)
        for b in range(rows):
            sp = s_ref[b, j]
            s_new = sp * dec_t[:, b:b + 1] + xdt_t[:, b:b + 1] * bv[b:b + 1, :]
            sout_ref[b, j] = s_new
            y_ref[b:b + 1, sl] = _dot_nt(cb[b:b + 1, :], s_new.astype(BF16)) + dsk_ref[:, sl] * xp[b:b + 1, :]


def _m2_state_step(act, dt_ch, dec_ch, ssm, mw):
    rows = act.shape[0]
    width = mw['width']
    gw = width // M2_GROUPS
    ppg = gw // LANES
    npairs = width // LANES
    gs = M2_GROUPS * M2_STATE
    s4 = ssm.reshape(rows, npairs, LANES, M2_STATE)
    blk = lambda w, base: pl.BlockSpec((rows, w), lambda g: (0, base + g))
    return pl.pallas_call(
        _m2_state_step_kernel,
        out_shape=(jax.ShapeDtypeStruct((rows, width), F32), jax.ShapeDtypeStruct(s4.shape, F32)),
        grid=(M2_GROUPS,),
        in_specs=[blk(gw, 0), blk(gw, 0), blk(gw, 0), blk(M2_STATE, width // M2_STATE),
                  blk(M2_STATE, (width + gs) // M2_STATE), pl.BlockSpec((1, gw), lambda g: (0, g)),
                  pl.BlockSpec((rows, ppg, LANES, M2_STATE), lambda g: (0, g, 0, 0))],
        out_specs=(blk(gw, 0), pl.BlockSpec((rows, ppg, LANES, M2_STATE), lambda g: (0, g, 0, 0))),
        compiler_params=_params(("parallel",)), name="m2_state_step")(
            act, dt_ch, dec_ch, act, act, mw['d_ch'], s4)


def _m2_weights(w_in, conv_w, conv_b, dt_bias, a_log, d_skip, norm_w, w_out):
    f = F32
    heads = dt_bias.shape[0]
    width = norm_w.shape[0]
    pad = LANES - heads
    cdim = conv_w.shape[1]
    return dict(
        w_in=_pad_cols(w_in[:, :width + cdim].astype(BF16), 512),
        w_gate=_pad_cols(w_in[:, width + cdim:].astype(f), LANES),
        conv_w=conv_w.astype(f), conv_b=conv_b.astype(f).reshape(1, -1),
        dt_bias=jnp.pad(dt_bias.astype(f), (0, pad)).reshape(1, LANES),
        neg_a=jnp.pad(-jnp.exp(a_log.astype(f)), (0, pad)).reshape(1, LANES),
        d_ch=jnp.repeat(d_skip.astype(f), width // heads).reshape(1, width),
        norm=norm_w.astype(f).reshape(1, width), w_out=w_out.astype(BF16), width=width, heads=heads)


def _gated_rmsnorm_prologue(y, z, w):
    v = y * _silu(z)
    return v * lax.rsqrt(jnp.mean(v * v, axis=-1, keepdims=True) + EPS) * w


def _m2_layer(x, mods, state, mw, *, prompt, batch, seq, tm):
    g, scale, shift, gate = mods
    width, heads = mw['width'], mw['heads']
    cdim = mw['conv_w'].shape[1]
    proj = _in_proj(x, g, scale, shift, mw['w_in'], batch_kind=prompt, rows_per_batch=seq, tm=tm, tn=MM_TN,
                    name="m2_in_proj")
    gates = _gate_proj(x, g, scale, shift, mw['w_gate'], batch_kind=prompt, rows_per_batch=seq, tm=tm,
                       name="m2_gate_proj")
    if prompt:
        a, ssm = _m2_ssd(proj, gates, mw, batch, seq)
        conv_new = proj.reshape(batch, seq, -1)[:, seq - (CONV_W - 1):, width:width + cdim]
        x_new = _out_proj([('row', a)], mw['w_out'], x, gate, batch_kind=True, rows_per_batch=seq, tm=tm, tn=MM_TN,
                          name="m2_out_proj")
        return x_new, conv_new, ssm
    else:
        conv_buf, ssm_in = state
        taps = jnp.swapaxes(conv_buf.astype(F32), 0, 1)
        act, dtv, dec = _m2_pre_step(proj, gates, taps, mw)
        rep = width // heads
        dt_ch = jnp.repeat(dtv[:, :heads], rep, axis=1)
        dec_ch = jnp.repeat(dec[:, :heads], rep, axis=1)
        y, ssm = _m2_state_step(act, dt_ch, dec_ch, ssm_in.astype(F32), mw)
        conv_new = jnp.concatenate([conv_buf.astype(F32)[:, 1:], proj[:, None, width:width + cdim]], axis=1)
    x_new = _out_proj([('row', y), ('row', proj, width, 0), ('vec', mw['norm'])], mw['w_out'], x, gate,
                      batch_kind=prompt, rows_per_batch=seq, tm=min(tm, 256), tn=MM_TN, name="m2_out_proj",
                      prologue=_gated_rmsnorm_prologue)
    return x_new, conv_new, ssm


GD_DK = 128
GD_DV = 128
GD_CHUNK = 64


def _dot_3pass(a, b):
    ah = a.astype(BF16)
    al = (a - ah.astype(F32)).astype(BF16)
    bh = b.astype(BF16)
    bl = (b - bh.astype(F32)).astype(BF16)
    return _dot(ah, bh) + (_dot(ah, bl) + _dot(al, bh))


def _l2norm_rows(x):
    return x * lax.rsqrt(jnp.sum(x * x, axis=-1, keepdims=True) + EPS)


def _rmsnorm_rows(x, w):
    return x * lax.rsqrt(jnp.mean(x * x, axis=-1, keepdims=True) + EPS) * w


GD_INV_BASE = 16


def _bdot(a, b):
    return jnp.einsum('hmk,hkn->hmn', a, b, preferred_element_type=F32)


def _bdot_nt(a, b):
    return jnp.einsum('hmk,hnk->hmn', a, b, preferred_element_type=F32)


def _bdot_3pass(a, b):
    ah = a.astype(BF16)
    al = (a - ah.astype(F32)).astype(BF16)
    bh = b.astype(BF16)
    bl = (b - bh.astype(F32)).astype(BF16)
    return _bdot(ah, bh) + (_bdot(ah, bl) + _bdot(al, bh))


def _unit_lower_inverse(a_strict):
    c = a_strict.shape[-1]
    row = lax.broadcasted_iota(jnp.int32, (c, c), 0)
    col = lax.broadcasted_iota(jnp.int32, (c, c), 1)
    eye = jnp.where(row == col, 1.0, 0.0)
    blk = GD_INV_BASE
    shift = int(math.log2(blk))
    p = jnp.where((row >> shift) == (col >> shift), -a_strict, 0.0)
    t = eye + p
    for _ in range(shift - 1):
        p = _bdot_3pass(p, p)
        t = t + _bdot_3pass(t, p)
    while blk < c:
        below = jnp.logical_and((row >> (shift + 1)) == (col >> (shift + 1)), (row >> shift) != (col >> shift))
        b = jnp.where(below, a_strict, 0.0)
        t = t - _bdot_3pass(_bdot_3pass(t, b), t)
        blk *= 2
        shift += 1
    return t


def _gd_chunk_kernel(qkv_ref, z_ref, braw_ref, araw_ref, cw_ref, nega_ref, dtb_ref, nw_ref,
                     o_ref, sout_ref, pad, s_ref):
    n = pl.program_id(1)
    first = n == 0
    c = qkv_ref.shape[0]
    hv = s_ref.shape[0]
    hk = hv // 2
    rep = hv // hk

    @pl.when(first)
    def _():
        s_ref[...] = jnp.zeros_like(s_ref)

    qkv = _conv_silu_chunk(qkv_ref, cw_ref, None, pad, first)
    beta = _sigmoid(braw_ref[...])
    gl = nega_ref[...] * _softplus(araw_ref[...] + dtb_ref[...])
    incl, strict = _tri_masks(c)
    tri = jnp.where(incl, 1.0, 0.0).astype(BF16)
    gcum = _dot_exact_lhs(tri, gl)
    gcum_t = jnp.concatenate([gcum, jnp.zeros((LANES - c, LANES), F32)], axis=0).T

    heads = range(hv)
    per_value_head = lambda t: jnp.stack([t[h // rep] for h in heads])
    q3 = jnp.stack([qkv[:, i * GD_DK:(i + 1) * GD_DK] for i in range(hk)])
    k3 = jnp.stack([qkv[:, (hk + i) * GD_DK:(hk + i + 1) * GD_DK] for i in range(hk)])
    v3 = jnp.stack([qkv[:, (2 * hk + h) * GD_DV:(2 * hk + h + 1) * GD_DV] for h in heads])
    q3 = _l2norm_rows(q3) * (GD_DK ** -0.5)
    k3 = _l2norm_rows(k3)
    k3b = k3.astype(BF16)
    kk = per_value_head(_bdot_nt(k3b, k3b))
    qk = per_value_head(_bdot_nt(q3.astype(BF16), k3b))
    q_v, k_v = per_value_head(q3), per_value_head(k3)
    colv = jnp.stack([gcum[:, h:h + 1] for h in heads])
    rowv = jnp.stack([gcum_t[h:h + 1, :c] for h in heads])
    bcol = jnp.stack([beta[:, h:h + 1] for h in heads])
    glast = colv[:, c - 1:c, :]
    ecol = jnp.exp(colv)
    dec = jnp.where(incl, jnp.exp(jnp.where(incl, colv - rowv, 0.0)), 0.0)
    a = jnp.where(strict, (bcol * kk) * dec, 0.0)
    tinv = _unit_lower_inverse(a)
    rhs = jnp.concatenate([v3 * bcol, (k_v * bcol) * ecol], axis=-1)
    sol = _bdot_3pass(tinv, rhs)
    u, w = sol[:, :, :GD_DV], sol[:, :, GD_DV:]
    s = s_ref[...]
    sb = s.astype(BF16)
    v_new = u - _bdot(w.astype(BF16), sb)
    o = _bdot((q_v * ecol).astype(BF16), sb) + _bdot((qk * dec).astype(BF16), v_new.astype(BF16))
    zpad = jnp.zeros((hv, LANES - c, GD_DV), F32)
    kd_t = jnp.swapaxes(jnp.concatenate([k_v * jnp.exp(glast - colv), zpad], axis=1), 1, 2)
    vn_pad = jnp.concatenate([v_new, zpad], axis=1)
    s_ref[...] = s * jnp.exp(glast) + _bdot(kd_t.astype(BF16), vn_pad.astype(BF16))
    on = _rmsnorm_rows(o, nw_ref[...])
    for h in heads:
        sl = slice(h * GD_DV, (h + 1) * GD_DV)
        o_ref[:, sl] = (on[h] * _silu(z_ref[:, sl])).astype(o_ref.dtype)

    @pl.when(n == pl.num_programs(1) - 1)
    def _():
        sout_ref[...] = s_ref[...]


def _gd_chunked(proj, gates, gw, batch, seq):
    c = GD_CHUNK
    nch = seq // c
    cdim, width, hv = gw['cdim'], gw['width'], gw['hv']
    row = lambda w, off: pl.BlockSpec((c, w), lambda b, n: (b * nch + n, off))
    par = lambda r, w: pl.BlockSpec((r, w), lambda b, n: (0, 0))
    return pl.pallas_call(
        _gd_chunk_kernel,
        out_shape=(jax.ShapeDtypeStruct((batch * seq, width), BF16),
                   jax.ShapeDtypeStruct((batch, hv, GD_DK, GD_DV), F32)),
        grid=(batch, nch),
        in_specs=[row(cdim, 0), row(width, cdim // width), row(LANES, 0), row(LANES, 1),
                  par(CONV_W, cdim), par(1, LANES), par(1, LANES), par(1, GD_DV)],
        out_specs=(pl.BlockSpec((c, width), lambda b, n: (b * nch + n, 0)),
                   pl.BlockSpec((None, hv, GD_DK, GD_DV), lambda b, n: (b, 0, 0, 0))),
        scratch_shapes=[pltpu.VMEM((c + SUBLANES, cdim), F32), pltpu.VMEM((hv, GD_DK, GD_DV), F32)],
        compiler_params=_params(("parallel", "arbitrary")), name="gd_chunked")(
            proj, proj, gates, gates, gw['conv_w'], gw['neg_a'], gw['dt_bias'], gw['norm'])


def _gd_pre_step_kernel(qkv_ref, braw_ref, araw_ref, taps_ref, cw_ref, nega_ref, dtb_ref,
                        q_ref, k_ref, v_ref, beta_ref, eg_ref):
    hk = q_ref.shape[1] // GD_DK
    act = _conv_silu_step(qkv_ref[...], taps_ref, cw_ref, None)
    for kh in range(hk):
        sl = slice(kh * GD_DK, (kh + 1) * GD_DK)
        q_ref[:, sl] = _l2norm_rows(act[:, kh * GD_DK:(kh + 1) * GD_DK]) * (GD_DK ** -0.5)
        k_ref[:, sl] = _l2norm_rows(act[:, (hk + kh) * GD_DK:(hk + kh + 1) * GD_DK])
    v_ref[...] = act[:, 2 * hk * GD_DK:]
    beta_ref[...] = _sigmoid(braw_ref[...])
    eg_ref[...] = jnp.exp(nega_ref[...] * _softplus(araw_ref[...] + dtb_ref[...]))


def _gd_pre_step(proj, gates, taps, gw):
    rows = proj.shape[0]
    cdim, width, hv = gw['cdim'], gw['width'], gw['hv']
    qk_w = (cdim - width) // 2
    sd = lambda w: jax.ShapeDtypeStruct((rows, w), F32)
    return pl.pallas_call(
        _gd_pre_step_kernel, out_shape=(sd(qk_w), sd(qk_w), sd(width), sd(LANES), sd(LANES)),
        name="gd_pre_step", compiler_params=pltpu.CompilerParams(vmem_limit_bytes=V7X_VMEM_LIMIT_BYTES))(
            proj[:, :cdim], gates[:, :LANES], gates[:, LANES:], taps,
            gw['conv_w'], gw['neg_a'], gw['dt_bias'])


def _gd_state_step_kernel(q_ref, k_ref, v_ref, beta_ref, eg_ref, z_ref, nw_ref, s_ref, o_ref, sout_ref):
    rows = q_ref.shape[0]
    nk = q_ref.shape[1] // GD_DK
    rep = (v_ref.shape[1] // GD_DV) // nk
    nw = nw_ref[...]
    zrows = jnp.zeros((SUBLANES - 2, GD_DK), F32)
    for kh in range(nk):
        ksl = slice(kh * GD_DK, (kh + 1) * GD_DK)
        q8, k8 = q_ref[:, ksl], k_ref[:, ksl]
        k_t = _pad_to_square_t(k8, GD_DK)
        for b in range(rows):
            qb, kb = q8[b:b + 1, :], k8[b:b + 1, :]
            kq = jnp.concatenate([kb, qb, zrows], axis=0).astype(BF16)
            qk = jnp.sum(qb * kb, axis=-1, keepdims=True)
            for r in range(rep):
                h = kh * rep + r
                vsl = slice(h * GD_DV, (h + 1) * GD_DV)
                s = s_ref[b, h]
                ks_qs = _dot(kq, s.astype(BF16))
                eg = eg_ref[b:b + 1, vsl]
                beta = beta_ref[b:b + 1, vsl]
                v_new = beta * (v_ref[b:b + 1, vsl] - eg * ks_qs[0:1, :])
                o = eg * ks_qs[1:2, :] + qk * v_new
                sout_ref[b, h] = s * eg[:, 0:1] + k_t[:, b:b + 1] * v_new
                o_ref[b:b + 1, vsl] = _rmsnorm_rows(o, nw) * _silu(z_ref[b:b + 1, vsl])


def _gd_state_step(proj, qn, kn, v, beta_ch, eg_ch, state, gw, heads_per_step=4):
    rows = qn.shape[0]
    cdim, width, hv = gw['cdim'], gw['width'], gw['hv']
    steps = hv // heads_per_step
    kw = qn.shape[1] // steps
    vw = width // steps
    blk = lambda w, base=0: pl.BlockSpec((rows, w), lambda g, base=base: (0, base + g))
    sspec = pl.BlockSpec((rows, heads_per_step, GD_DK, GD_DV), lambda g: (0, g, 0, 0))
    return pl.pallas_call(
        _gd_state_step_kernel,
        out_shape=(jax.ShapeDtypeStruct((rows, width), F32), jax.ShapeDtypeStruct(state.shape, F32)),
        grid=(steps,),
        in_specs=[blk(kw), blk(kw), blk(vw), blk(vw), blk(vw), blk(vw, cdim // vw),
                  pl.BlockSpec((1, GD_DV), lambda g: (0, 0)), sspec],
        out_specs=(blk(vw), sspec),
        compiler_params=_params(("parallel",)), name="gd_state_step")(
            qn, kn, v, beta_ch, eg_ch, proj, gw['norm'], state)


def _gd_weights(w_in, conv_w, a_log, dt_bias, norm_w, w_out):
    f = F32
    hv = a_log.shape[0]
    cdim = conv_w.shape[1]
    width = w_out.shape[0]
    pad = LANES - hv
    base = cdim + width
    zeros = jnp.zeros((w_in.shape[0], pad), w_in.dtype)
    w_gate = jnp.concatenate([w_in[:, base:base + hv], zeros, w_in[:, base + hv:], zeros], axis=1)
    return dict(
        w_in=_pad_cols(w_in[:, :base].astype(BF16), 512), w_gate=w_gate.astype(f), conv_w=conv_w.astype(f),
        neg_a=jnp.pad(-jnp.exp(a_log.astype(f)), (0, pad)).reshape(1, LANES),
        dt_bias=jnp.pad(dt_bias.astype(f), (0, pad)).reshape(1, LANES),
        norm=norm_w.astype(f).reshape(1, -1), w_out=w_out.astype(BF16), cdim=cdim, width=width, hv=hv)


def _gd_layer(x, mods, state, gw, *, prompt, batch, seq, tm):
    g, scale, shift, gate = mods
    cdim, width, hv = gw['cdim'], gw['width'], gw['hv']
    proj = _in_proj(x, g, scale, shift, gw['w_in'], batch_kind=prompt, rows_per_batch=seq, tm=tm, tn=MM_TN,
                    name="gd_in_proj")
    gates = _gate_proj(x, g, scale, shift, gw['w_gate'], batch_kind=prompt, rows_per_batch=seq, tm=tm,
                       name="gd_gate_proj")
    if prompt:
        a, ssm = _gd_chunked(proj, gates, gw, batch, seq)
        conv_new = proj.reshape(batch, seq, -1)[:, seq - (CONV_W - 1):, :cdim]
        x_new = _out_proj([('row', a)], gw['w_out'], x, gate, batch_kind=True, rows_per_batch=seq, tm=tm, tn=MM_TN,
                          name="gd_out_proj")
    else:
        conv_buf, ssm_in = state
        taps = jnp.swapaxes(conv_buf.astype(F32), 0, 1)
        qn, kn, v, beta, eg = _gd_pre_step(proj, gates, taps, gw)
        beta_ch = jnp.repeat(beta[:, :hv], GD_DV, axis=1)
        eg_ch = jnp.repeat(eg[:, :hv], GD_DV, axis=1)
        a, ssm = _gd_state_step(proj, qn, kn, v, beta_ch, eg_ch, ssm_in.astype(F32), gw)
        conv_new = jnp.concatenate([conv_buf.astype(F32)[:, 1:], proj[:, None, :cdim]], axis=1)
        x_new = _out_proj([('row', a)], gw['w_out'], x, gate, batch_kind=False, rows_per_batch=seq, tm=tm, tn=MM_TN,
                          name="gd_out_proj", prologue=_cast_prologue)
    return x_new, conv_new, ssm


AT_DIM = 128
IDX_DIM = 128
TOPK_MAX = 256
REL_BUCKETS = 32
REL_MAX_DIST = 128
AT_TILE = 256
INT32_MIN = -2 ** 31
_NEG_BITS = int(np.float32(NEG).view(np.int32))
NEG_SORT_KEY = _NEG_BITS ^ 0x7FFFFFFF if _NEG_BITS < 0 else _NEG_BITS


def _bucket_starts():
    d = np.arange(0, REL_MAX_DIST + 1)
    exact = REL_BUCKETS // 2
    far = exact + (np.log(np.maximum(d, 1).astype(np.float32) / exact) / math.log(REL_MAX_DIST / exact)
                   * (REL_BUCKETS - exact)).astype(np.int32)
    bucket = np.where(d < exact, d, np.minimum(far, REL_BUCKETS - 1))
    assert np.all(np.diff(bucket) >= 0) and bucket[-1] == REL_BUCKETS - 1
    return [int(np.argmax(bucket >= b)) for b in range(REL_BUCKETS)]


def _bias_from_dist(dist, value_of_bucket):
    starts = _bucket_starts()
    val = value_of_bucket(REL_BUCKETS - 1)
    for b in range(REL_BUCKETS - 2, -1, -1):
        val = jnp.where(dist < starts[b + 1], value_of_bucket(b), val)
    return val


def _sort_key(x):
    x = jnp.where(x == 0.0, 0.0, x)
    b = pltpu.bitcast(x, jnp.int32)
    return jnp.where(b < 0, b ^ jnp.int32(0x7FFFFFFF), b)


def _kth_largest_key(count_ge, shape, k):
    def body(it, ans):
        cand = ans | jnp.left_shift(jnp.int32(1), 31 - it)
        cnt = count_ge(cand ^ jnp.int32(INT32_MIN))
        return jnp.where(cnt >= k, cand, ans)

    ans = lax.fori_loop(0, 32, body, jnp.zeros(shape, jnp.int32))
    return ans ^ jnp.int32(INT32_MIN)


def _relbias_tiles_kernel(rb_ref, o_ref):
    delta = pl.program_id(0) * AT_TILE
    h = pl.program_id(1)
    i = lax.broadcasted_iota(jnp.int32, (AT_TILE, AT_TILE), 0)
    j = lax.broadcasted_iota(jnp.int32, (AT_TILE, AT_TILE), 1)
    o_ref[...] = _bias_from_dist(delta + i - j, lambda b: rb_ref[b, h])


def _relbias_tiles(rel_bias):
    heads = rel_bias.shape[1]
    ntile = 3
    assert (ntile - 1) * AT_TILE - (AT_TILE - 1) >= REL_MAX_DIST
    return pl.pallas_call(
        _relbias_tiles_kernel, out_shape=jax.ShapeDtypeStruct((ntile, heads, AT_TILE, AT_TILE), F32),
        grid=(ntile, heads),
        in_specs=[pl.BlockSpec(memory_space=pltpu.SMEM)],
        out_specs=pl.BlockSpec((None, None, AT_TILE, AT_TILE), lambda d, h: (d, h, 0, 0)),
        compiler_params=_params(("parallel", "parallel")), name="at_relbias_tiles")(rel_bias.astype(F32))


def _at_index_kernel(qi_ref, wi_ref, ki_ref, o_ref, keys, cnt, *, k_sel, score_scale):
    qb = pl.program_id(1)
    tq = qi_ref.shape[0]
    nkb = keys.shape[0]
    tk = keys.shape[2]
    nh = qi_ref.shape[1] // IDX_DIM
    wsc = wi_ref[...] * score_scale
    qpos = qb * tq + lax.broadcasted_iota(jnp.int32, (tq, tk), 0)
    kloc = lax.broadcasted_iota(jnp.int32, (tq, tk), 1)
    neg_key = _sort_key(jnp.full((tq, tk), NEG, F32))

    for kb in range(nkb):
        @pl.when(kb <= qb)
        def _():
            kblk = ki_ref[kb * tk:(kb + 1) * tk, :].astype(BF16)
            sc = jnp.zeros((tq, tk), F32)
            for h in range(nh):
                d = _dot_nt(qi_ref[:, h * IDX_DIM:(h + 1) * IDX_DIM].astype(BF16), kblk)
                sc = sc + wsc[:, h:h + 1] * jnp.maximum(d, 0.0)
            adm = kb * tk + kloc <= qpos
            keys[kb] = _sort_key(jnp.where(adm, sc, NEG))

        @pl.when(kb > qb)
        def _():
            keys[kb] = neg_key

    def count_ge(t):
        cnt[...] = jnp.where(keys[0] >= t, 1, 0)
        for kb in range(1, nkb):
            @pl.when(kb <= qb)
            def _():
                cnt[...] += jnp.where(keys[kb] >= t, 1, 0)
        beyond = (nkb - 1 - qb) * tk
        return jnp.sum(cnt[...], axis=1, keepdims=True) + jnp.where(t <= NEG_SORT_KEY, beyond, 0)

    thr = _kth_largest_key(count_ge, (tq, 1), k_sel)
    n_ge = count_ge(thr)
    has_ties = jnp.max(n_ge) > k_sel

    @pl.when(jnp.logical_not(has_ties))
    def _():
        for kb in range(nkb):
            adm = kb * tk + kloc <= qpos
            sel = jnp.logical_and(keys[kb] >= thr, adm)
            o_ref[kb] = jnp.where(sel, 0.0, MASKED).astype(o_ref.dtype)

    @pl.when(has_ties)
    def _():
        acc = jnp.zeros((tq, tk), jnp.int32)
        for kb in range(nkb):
            acc = acc + jnp.where(keys[kb] > thr, 1, 0)
        room = (k_sel - jnp.sum(acc, axis=1, keepdims=True)).astype(F32)
        upper = jnp.where(lax.broadcasted_iota(jnp.int32, (tk, tk), 0) <= lax.broadcasted_iota(jnp.int32, (tk, tk), 1),
                          1.0, 0.0).astype(BF16)
        seen = jnp.zeros((tq, 1), F32)
        for kb in range(nkb):
            key = keys[kb]
            eq = key == thr
            eqf = jnp.where(eq, 1.0, 0.0)
            rank = seen + _dot(eqf.astype(BF16), upper)
            seen = seen + jnp.sum(eqf, axis=1, keepdims=True)
            adm = kb * tk + kloc <= qpos
            sel = jnp.logical_and(jnp.logical_or(key > thr, jnp.logical_and(eq, rank <= room)), adm)
            o_ref[kb] = jnp.where(sel, 0.0, MASKED).astype(o_ref.dtype)


def _at_index(proj, aw, batch, seq, k_sel):
    tq = tk = AT_TILE
    nq = seq // tq
    width = aw['width']
    nh = aw['idx_heads']
    qio = (4 * width) // (nh * IDX_DIM)
    kio = (4 * width + nh * IDX_DIM) // IDX_DIM
    kern = functools.partial(_at_index_kernel, k_sel=k_sel, score_scale=IDX_DIM ** -0.5 * nh ** -0.5)
    return pl.pallas_call(
        kern, out_shape=jax.ShapeDtypeStruct((batch * nq, seq // tk, tq, tk), BF16), grid=(batch, nq),
        in_specs=[pl.BlockSpec((tq, nh * IDX_DIM), lambda b, q: (b * nq + q, qio)),
                  pl.BlockSpec((tq, LANES), lambda b, q: (b * nq + q, kio + 1)),
                  pl.BlockSpec((seq, IDX_DIM), lambda b, q: (b, kio))],
        out_specs=pl.BlockSpec((None, seq // tk, tq, tk), lambda b, q: (b * nq + q, 0, 0, 0)),
        scratch_shapes=[pltpu.VMEM((seq // tk, tq, tk), jnp.int32), pltpu.VMEM((tq, tk), jnp.int32)],
        compiler_params=_params(("parallel", "parallel")), name="at_index")(proj, proj, proj)


AT_HEAD_GROUP = 4
MASKED = 2.0 * NEG


def _at_attend_kernel(q_ref, k_ref, vt_ref, z_ref, mask_ref, bias_ref, o_ref, acc, m_scr, l_scr):
    qb = pl.program_id(2)
    t = q_ref.shape[0]
    hg = q_ref.shape[1] // AT_DIM
    acc[...] = jnp.zeros_like(acc)
    m_scr[...] = jnp.full(m_scr.shape, NEG, F32)
    l_scr[...] = jnp.zeros_like(l_scr)

    heads = [slice(h * AT_DIM, (h + 1) * AT_DIM) for h in range(hg)]
    q3t = jnp.stack([q_ref[:, sl].T for sl in heads]).astype(BF16)

    def key_tile(kb, carry):
        rows = pl.ds(pl.multiple_of(kb * t, t), t)
        k3 = jnp.stack([k_ref[rows, sl] for sl in heads])
        s_t = _bdot(k3, q3t) + bias_ref[jnp.minimum(qb - kb, 2)] + mask_ref[kb].astype(F32)
        m_old = m_scr[...]
        m_new = jnp.maximum(m_old, jnp.max(s_t, axis=1, keepdims=True))
        alpha = jnp.exp(m_old - m_new)
        p_t = jnp.exp(s_t - m_new)
        l_scr[...] = alpha * l_scr[...] + jnp.sum(p_t, axis=1, keepdims=True)
        acc[...] = alpha * acc[...] + _bdot(vt_ref[:, kb], p_t.astype(BF16))
        m_scr[...] = m_new
        return carry

    lax.fori_loop(0, qb + 1, key_tile, 0)
    o_t = acc[...] / l_scr[...]
    for h, sl in enumerate(heads):
        o_ref[:, sl] = (o_t[h].T * _silu(z_ref[:, sl])).astype(o_ref.dtype)


def _at_attend(proj, proj_bf, maskadd_t, tiles_t, aw, batch, seq):
    t = AT_TILE
    nq = seq // t
    width = aw['width']
    heads = width // AT_DIM
    hg = AT_HEAD_GROUP
    gw = hg * AT_DIM
    ng = width // gw
    v_t = proj_bf[:, 2 * width:3 * width].reshape(batch, nq, t, heads, AT_DIM).transpose(0, 3, 1, 4, 2)
    return pl.pallas_call(
        _at_attend_kernel, out_shape=jax.ShapeDtypeStruct((batch * seq, width), BF16), grid=(batch, ng, nq),
        in_specs=[pl.BlockSpec((t, gw), lambda b, g, q: (b * nq + q, g)),
                  pl.BlockSpec((seq, gw), lambda b, g, q: (b, ng + g)),
                  pl.BlockSpec((None, hg, nq, AT_DIM, t), lambda b, g, q: (b, g, 0, 0, 0)),
                  pl.BlockSpec((t, gw), lambda b, g, q: (b * nq + q, 3 * ng + g)),
                  pl.BlockSpec((None, nq, t, t), lambda b, g, q: (b * nq + q, 0, 0, 0)),
                  pl.BlockSpec((3, hg, t, t), lambda b, g, q: (0, g, 0, 0))],
        out_specs=pl.BlockSpec((t, gw), lambda b, g, q: (b * nq + q, g)),
        scratch_shapes=[pltpu.VMEM((hg, AT_DIM, t), F32), pltpu.VMEM((hg, 1, t), F32), pltpu.VMEM((hg, 1, t), F32)],
        compiler_params=_params(("parallel", "parallel", "arbitrary")), name="at_attend")(
            proj, proj_bf, v_t, proj, maskadd_t, tiles_t)


def _at_weights(w_in, rel_bias, w_out):
    width = w_out.shape[0]
    heads = rel_bias.shape[1]
    idx_heads = (w_in.shape[1] - 4 * width - IDX_DIM) // (IDX_DIM + 1)
    col_scale = jnp.where(jnp.arange(w_in.shape[1]) < width, AT_DIM ** -0.5, 1.0).astype(F32)
    return dict(w_in=_pad_cols((w_in.astype(F32) * col_scale).astype(BF16), 512), rel_bias=rel_bias.astype(F32),
                w_out=w_out.astype(BF16), width=width, heads=heads, idx_heads=idx_heads)


def _at_rows(proj, aw, lead):
    width, heads = aw['width'], aw['heads']
    k = proj[:, width:2 * width].reshape(lead + (heads, AT_DIM))
    v = proj[:, 2 * width:3 * width].reshape(lead + (heads, AT_DIM))
    o = 4 * width + aw['idx_heads'] * IDX_DIM
    ki = proj[:, o:o + IDX_DIM].reshape(lead + (IDX_DIM,))
    return k, v, ki


def _at_layer_prompt(x, mods, aw, *, batch, seq, tm):
    g, scale, shift, gate = mods
    proj, proj_bf = _in_proj(x, g, scale, shift, aw['w_in'], batch_kind=True, rows_per_batch=seq, tm=tm, tn=MM_TN,
                             name="at_in_proj", bf16_copy=True)
    k_sel = min(TOPK_MAX, seq // 4)
    maskadd = _at_index(proj, aw, batch, seq, k_sel)
    tiles = _relbias_tiles(aw['rel_bias'])
    a = _at_attend(proj, proj_bf, jnp.swapaxes(maskadd, 2, 3), jnp.swapaxes(tiles, 2, 3), aw, batch, seq)
    x_new = _out_proj([('row', a)], aw['w_out'], x, gate, batch_kind=True, rows_per_batch=seq, tm=tm, tn=MM_TN,
                      name="at_out_proj")
    return (x_new,) + _at_rows(proj, aw, (batch, seq))


AT_PAGES_PER_STEP = 16


def _at_page_scores_kernel(pt_ref, qi_ref, w_ref, kidx_ref, o_ref, kbuf, sem):
    b, j = pl.program_id(0), pl.program_id(1)
    nj = pl.num_programs(1)
    npg = kbuf.shape[1]
    step = b * nj + j
    last_step = pl.num_programs(0) * nj - 1

    def page_copy(s, i, slot):
        sb, sj = s // nj, s % nj
        return pltpu.make_async_copy(kidx_ref.at[pt_ref[sb, sj * npg + i]], kbuf.at[slot, i], sem.at[slot])

    def start_all(s, slot):
        for i in range(npg):
            page_copy(s, i, slot).start()

    slot = step % 2

    @pl.when(step == 0)
    def _():
        start_all(step, slot)

    @pl.when(step < last_step)
    def _():
        start_all(step + 1, 1 - slot)

    for i in range(npg):
        page_copy(step, i, slot).wait()
    qi = qi_ref[...].astype(BF16)
    w = w_ref[...]
    for i in range(npg):
        d = _dot_nt(qi, kbuf[slot, i].astype(BF16))
        o_ref[i:i + 1, :] = jnp.sum(w * jnp.maximum(d, 0.0), axis=0, keepdims=True)


def _at_page_scores(qi3, w3, cache_kidx, page_table):
    rows, nh, _ = qi3.shape
    npages = page_table.shape[1]
    page = cache_kidx.shape[1]
    npg = math.gcd(npages, AT_PAGES_PER_STEP)
    grid_spec = pltpu.PrefetchScalarGridSpec(
        num_scalar_prefetch=1, grid=(rows, npages // npg),
        in_specs=[pl.BlockSpec((None, nh, IDX_DIM), lambda b, j, pt: (b, 0, 0)),
                  pl.BlockSpec((None, nh, IDX_DIM), lambda b, j, pt: (b, 0, 0)),
                  pl.BlockSpec(memory_space=pl.ANY)],
        out_specs=pl.BlockSpec((None, npg, page), lambda b, j, pt: (b, j, 0)),
        scratch_shapes=[pltpu.VMEM((2, npg, page, IDX_DIM), F32), pltpu.SemaphoreType.DMA((2,))])
    return pl.pallas_call(
        _at_page_scores_kernel, out_shape=jax.ShapeDtypeStruct((rows, npages, page), F32), grid_spec=grid_spec,
        compiler_params=_params(("arbitrary", "arbitrary")), name="at_page_scores")(page_table, qi3, w3, cache_kidx)


def _at_sample_select_kernel(sc_ref, qi_ref, w_ref, kin_ref, gidx_ref, newadd_ref, rank_scr, *, k_sel):
    npages, page = sc_ref.shape
    upper = jnp.where(lax.broadcasted_iota(jnp.int32, (page, page), 0) <= lax.broadcasted_iota(jnp.int32, (page, page), 1),
                      1.0, 0.0).astype(BF16)
    lower = jnp.where(lax.broadcasted_iota(jnp.int32, (npages, npages), 1) < lax.broadcasted_iota(jnp.int32, (npages, npages), 0),
                      1.0, 0.0).astype(BF16)

    def total(x):
        return jnp.sum(jnp.sum(x, axis=1, keepdims=True), axis=0, keepdims=True)

    def position_rank(flags):
        row_cnt = jnp.broadcast_to(jnp.sum(flags, axis=1, keepdims=True), (npages, page))
        return _dot(lower, row_cnt.astype(BF16)) + _dot(flags.astype(BF16), upper)

    keys = _sort_key(sc_ref[...])
    dots = jnp.sum(qi_ref[...] * kin_ref[...], axis=1, keepdims=True)
    s_new = jnp.sum(w_ref[:, 0:1] * jnp.maximum(dots, 0.0), axis=0, keepdims=True)
    key_new = _sort_key(s_new)

    def count_ge(t):
        return total(jnp.where(keys >= t, 1, 0)) + jnp.where(key_new >= t, 1, 0)

    thr = _kth_largest_key(count_ge, (1, 1), k_sel)
    n_gt = total(jnp.where(keys > thr, 1.0, 0.0)) + jnp.where(key_new > thr, 1.0, 0.0)
    room = k_sel - n_gt
    eq = keys == thr
    eqf = jnp.where(eq, 1.0, 0.0)
    sel = jnp.logical_or(keys > thr, jnp.logical_and(eq, position_rank(eqf) <= room))
    sel_new = jnp.logical_or(key_new > thr, jnp.logical_and(key_new == thr, total(eqf) + 1.0 <= room))
    newadd_ref[...] = jnp.broadcast_to(jnp.where(sel_new, 0.0, NEG), (1, page))

    self_f = jnp.where(sel, 1.0, 0.0)
    rank_scr[...] = jnp.where(sel, position_rank(self_f) - 1.0, -1.0)
    jidx = lax.broadcasted_iota(jnp.int32, (k_sel, page), 0).astype(F32)
    lane = lax.broadcasted_iota(jnp.int32, (page, LANES), 1)
    pick = jnp.where(lane == 0, lax.broadcasted_iota(jnp.int32, (page, LANES), 0).astype(F32),
                     jnp.where(lane <= 2, 1.0, 0.0)).astype(BF16)
    out_lane = lax.broadcasted_iota(jnp.int32, (k_sel, LANES), 1)

    def add_page(r, acc):
        onehot = jnp.where(rank_scr[pl.ds(r, 1), :] == jidx, 1.0, 0.0).astype(BF16)
        return acc + _dot(onehot, pick) * jnp.where(out_lane == 1, lax.convert_element_type(r, F32), 1.0)

    gidx_ref[...] = lax.fori_loop(0, npages, add_page, jnp.zeros((k_sel, LANES), F32))


def _at_sample_select(scores, qi3, w3, ki_new, k_sel):
    rows, npages, page = scores.shape
    nh = qi3.shape[1]
    assert page == LANES and page >= REL_MAX_DIST
    kern = functools.partial(_at_sample_select_kernel, k_sel=k_sel)
    return pl.pallas_call(
        kern, out_shape=(jax.ShapeDtypeStruct((rows, k_sel, LANES), F32), jax.ShapeDtypeStruct((rows, 1, page), F32)),
        grid=(rows,),
        in_specs=[pl.BlockSpec((None, npages, page), lambda b: (b, 0, 0)),
                  pl.BlockSpec((None, nh, IDX_DIM), lambda b: (b, 0, 0)),
                  pl.BlockSpec((None, nh, IDX_DIM), lambda b: (b, 0, 0)),
                  pl.BlockSpec((None, 1, IDX_DIM), lambda b: (b, 0, 0))],
        out_specs=(pl.BlockSpec((None, k_sel, LANES), lambda b: (b, 0, 0)),
                   pl.BlockSpec((None, 1, page), lambda b: (b, 0, 0))),
        scratch_shapes=[pltpu.VMEM((npages, page), F32)],
        compiler_params=_params(("parallel",)), name="at_sample_select")(
            scores, qi3, w3, ki_new.reshape(rows, 1, IDX_DIM))


def _at_gather_attend_kernel(pt_ref, slot_ref, off_ref, q_ref, ck_ref, cv_ref, pos_ref, newadd_ref, rbt_ref,
                             kn_ref, vn_ref, z_ref, o_ref, kg, vg, sem, *, past):
    b = pl.program_id(0)
    nsel, nh, d = kg.shape
    ncol = nsel * nh

    def row_copies(j):
        page = pt_ref[b, slot_ref[b, j]]
        off = off_ref[b, j]
        return (pltpu.make_async_copy(ck_ref.at[page, off], kg.at[j], sem.at[0]),
                pltpu.make_async_copy(cv_ref.at[page, off], vg.at[j], sem.at[1]))

    def start(j, carry):
        for c in row_copies(j):
            c.start()
        return carry

    def wait(j, carry):
        for c in row_copies(j):
            c.wait()
        return carry

    lax.fori_loop(0, nsel, start, 0)
    qs = q_ref[...].astype(BF16)
    pos = pos_ref[...]
    own = (lax.broadcasted_iota(jnp.int32, (nh, ncol), 1) & (nh - 1)) == lax.broadcasted_iota(jnp.int32, (nh, ncol), 0)
    keep = jnp.logical_and(own, pos >= 0)
    bias = _bias_from_dist(jnp.maximum(past - pos, 0), lambda bk: rbt_ref[:, bk:bk + 1])
    nadd = newadd_ref[:, 0:1]
    s_n = jnp.sum(qs.astype(F32) * kn_ref[...], axis=1, keepdims=True) + rbt_ref[:, 0:1] + nadd
    lax.fori_loop(0, nsel, wait, 0)
    s = jnp.where(keep, _dot_nt(qs, kg[...].reshape(ncol, d).astype(BF16)) + bias, NEG)
    m = jnp.maximum(jnp.max(s, axis=1, keepdims=True), s_n)
    pr = jnp.where(keep, jnp.exp(s - m), 0.0)
    p_n = jnp.where(nadd < 0.0, 0.0, jnp.exp(s_n - m))
    l = jnp.sum(pr, axis=1, keepdims=True) + p_n
    o = _dot(pr.astype(BF16), vg[...].reshape(ncol, d).astype(BF16)) + p_n * vn_ref[...]
    o_ref[...] = o / l * _silu(z_ref[...])


def _at_gather_attend(q3, cache_k, cache_v, page_table, slot, off, pos_cols, newadd, rbt, kn3, vn3, z3):
    rows, nh, d = q3.shape
    assert nh & (nh - 1) == 0
    nsel = slot.shape[1]
    ncol = nsel * nh
    past = page_table.shape[1] * cache_k.shape[1]
    row3 = pl.BlockSpec((None, nh, d), lambda b, *_: (b, 0, 0))
    grid_spec = pltpu.PrefetchScalarGridSpec(
        num_scalar_prefetch=3, grid=(rows,),
        in_specs=[row3, pl.BlockSpec(memory_space=pl.ANY), pl.BlockSpec(memory_space=pl.ANY),
                  pl.BlockSpec((None, 1, ncol), lambda b, *_: (b, 0, 0)),
                  pl.BlockSpec((None, 1, newadd.shape[-1]), lambda b, *_: (b, 0, 0)),
                  pl.BlockSpec(rbt.shape, lambda b, *_: (0, 0)),
                  row3, row3, row3],
        out_specs=row3,
        scratch_shapes=[pltpu.VMEM((nsel, nh, d), F32), pltpu.VMEM((nsel, nh, d), F32), pltpu.SemaphoreType.DMA((2,))])
    kern = functools.partial(_at_gather_attend_kernel, past=past)
    return pl.pallas_call(
        kern, out_shape=jax.ShapeDtypeStruct((rows, nh, d), F32), grid_spec=grid_spec,
        compiler_params=_params(("arbitrary",)), name="at_gather_attend")(
            page_table, slot, off, q3, cache_k, cache_v, pos_cols, newadd, rbt, kn3, vn3, z3)


def _at_layer_sample(x, mods, cache_k, cache_v, cache_kidx, page_table, aw):
    g, scale, shift, gate = mods
    rows = x.shape[0]
    width, heads, nh = aw['width'], aw['heads'], aw['idx_heads']
    proj = _in_proj(x, g, scale, shift, aw['w_in'], batch_kind=False, rows_per_batch=1, tm=SUBLANES, tn=MM_TN,
                    name="at_in_proj")
    npool, page = cache_k.shape[:2]
    npages = page_table.shape[1]
    past = npages * page
    k_sel = min(TOPK_MAX, (past + 1) // 4)
    o = 4 * width
    qi3 = proj[:, o:o + nh * IDX_DIM].reshape(rows, nh, IDX_DIM)
    ki_new = proj[:, o + nh * IDX_DIM:o + nh * IDX_DIM + IDX_DIM]
    wi = proj[:, o + nh * IDX_DIM + IDX_DIM:o + nh * IDX_DIM + IDX_DIM + nh] * (IDX_DIM ** -0.5 * nh ** -0.5)
    w3 = jnp.broadcast_to(wi[:, :, None], (rows, nh, IDX_DIM))
    scores = _at_page_scores(qi3, w3, cache_kidx.astype(F32), page_table)
    rbt = aw['rel_bias'].T
    gidx, newadd = _at_sample_select(scores, qi3, w3, ki_new, k_sel)
    off = gidx[:, :, 0].astype(jnp.int32)
    slot = gidx[:, :, 1].astype(jnp.int32)
    pos = jnp.where(gidx[:, :, 2] > 0.5, slot * page + off, -1)
    pos_cols = jnp.repeat(pos, heads, axis=-1).reshape(rows, 1, k_sel * heads)
    r3 = lambda t: t.reshape(rows, heads, AT_DIM)
    a = _at_gather_attend(r3(proj[:, :width]), cache_k.astype(F32), cache_v.astype(F32), page_table, slot, off,
                          pos_cols, newadd, rbt, r3(proj[:, width:2 * width]), r3(proj[:, 2 * width:3 * width]),
                          r3(proj[:, 3 * width:4 * width]))
    x_new = _out_proj([('row', a.reshape(rows, width))], aw['w_out'], x, gate, batch_kind=False, rows_per_batch=1,
                      tm=SUBLANES, tn=MM_TN, name="at_out_proj", prologue=_cast_prologue)
    return (x_new,) + _at_rows(proj, aw, (rows, 1))


def kernel(x_prompt, x_sample, state_s5_re, state_s5_im, state_m2_conv, state_m2_ssm, state_gd_conv, state_gd_ssm, cache_k, cache_v, cache_kidx, page_table, c_prompt, c_sample, norm_g, w_mod, b_mod, final_g, s5_w_in, s5_lam_re, s5_lam_im, s5_log_dt, s5_b_re, s5_b_im, s5_c_re, s5_c_im, s5_d, s5_w_glu, s5_b_glu, s5_w_out, m2_w_in, m2_conv_w, m2_conv_b, m2_dt_bias, m2_a_log, m2_d, m2_norm, m2_w_out, gd_w_in, gd_conv_w, gd_a_log, gd_dt_bias, gd_norm, gd_w_out, at_w_in, rel_bias, at_w_out):
    f = F32
    bp, seq, d = x_prompt.shape
    bs = x_sample.shape[0]
    depth = norm_g.shape[0]
    xp = x_prompt.astype(f).reshape(bp * seq, d)
    xs = x_sample.astype(f).reshape(bs, d)

    pad_rows = (-(bs + bp)) % SUBLANES
    c_all = jnp.concatenate([c_sample.astype(f), c_prompt.astype(f), jnp.zeros((pad_rows, d), f)], axis=0)
    mod = _modulation(c_all, w_mod, b_mod)

    def mods(i, prompt):
        g = norm_g[i].astype(f).reshape(1, d)
        rows = mod[i, bs:bs + bp] if prompt else mod[i, :bs]
        shift, scale, gate = rows[:, :d], rows[:, d:2 * d], rows[:, 2 * d:]
        if prompt:
            return g, scale[:, None, :], shift[:, None, :], gate[:, None, :]
        return g, scale, shift, gate

    s5w = _s5_weights(s5_w_in, s5_lam_re, s5_lam_im, s5_log_dt, s5_b_re, s5_b_im, s5_c_re, s5_c_im, s5_d,
                      s5_w_glu, s5_b_glu, s5_w_out)
    tm_p = MM_TM

    xp, s5_re_p, s5_im_p = _s5_layer(xp, mods(0, True), None, s5w, prompt=True, batch=bp, seq=seq, tm=tm_p)
    xs, s5_re_s, s5_im_s = _s5_layer(xs, mods(0, False), (state_s5_re, state_s5_im), s5w, prompt=False,
                                     batch=bs, seq=1, tm=SUBLANES)
    groups, nstate = state_s5_re.shape[1:]
    s5_re_p, s5_im_p = s5_re_p.reshape(bp, groups, nstate), s5_im_p.reshape(bp, groups, nstate)
    s5_re_s, s5_im_s = s5_re_s.reshape(bs, groups, nstate), s5_im_s.reshape(bs, groups, nstate)

    m2w = _m2_weights(m2_w_in, m2_conv_w, m2_conv_b, m2_dt_bias, m2_a_log, m2_d, m2_norm, m2_w_out)
    xp, m2_conv_p, m2_ssm_p = _m2_layer(xp, mods(1, True), None, m2w, prompt=True, batch=bp, seq=seq, tm=tm_p)
    xs, m2_conv_s, m2_ssm_s = _m2_layer(xs, mods(1, False), (state_m2_conv, state_m2_ssm), m2w, prompt=False,
                                        batch=bs, seq=1, tm=SUBLANES)
    m2_ssm_p = m2_ssm_p.reshape((bp,) + state_m2_ssm.shape[1:])
    m2_ssm_s = m2_ssm_s.reshape(state_m2_ssm.shape)

    gdw = _gd_weights(gd_w_in, gd_conv_w, gd_a_log, gd_dt_bias, gd_norm, gd_w_out)
    xp, gd_conv_p, gd_ssm_p = _gd_layer(xp, mods(2, True), None, gdw, prompt=True, batch=bp, seq=seq, tm=tm_p)
    xs, gd_conv_s, gd_ssm_s = _gd_layer(xs, mods(2, False), (state_gd_conv, state_gd_ssm), gdw, prompt=False,
                                        batch=bs, seq=1, tm=SUBLANES)

    atw = _at_weights(at_w_in, rel_bias, at_w_out)
    xp, k_rows_p, v_rows_p, kidx_rows_p = _at_layer_prompt(xp, mods(3, True), atw, batch=bp, seq=seq, tm=tm_p)
    xs, k_rows_s, v_rows_s, kidx_rows_s = _at_layer_sample(xs, mods(3, False), cache_k, cache_v, cache_kidx,
                                                           page_table, atw)

    y_prompt = _final_norm(xp, final_g).reshape(x_prompt.shape).astype(x_prompt.dtype)
    y_sample = _final_norm(xs, final_g).reshape(x_sample.shape).astype(x_sample.dtype)
    return (y_prompt, y_sample, s5_re_p, s5_im_p, s5_re_s, s5_im_s, m2_conv_p, m2_ssm_p, m2_conv_s, m2_ssm_s,
            gd_conv_p, gd_ssm_p, gd_conv_s, gd_ssm_s,
            k_rows_p, v_rows_p, kidx_rows_p, k_rows_s, v_rows_s, kidx_rows_s)
```

```python
import functools
import math

import numpy as np
import jax
import jax.numpy as jnp
from jax import lax
from jax.experimental import pallas as pl
from jax.experimental.pallas import tpu as pltpu

F32 = jnp.float32
BF16 = jnp.bfloat16

EPS = 1e-6
NEG = -1e30
CONV_W = 4
V7X_VMEM_LIMIT_BYTES = 56 * 1024 * 1024
LANES = 128
SUBLANES = 8
MM_TM = 1024
MM_TM_WIDE_ROWS = 512
MM_TN = 1024

S5_GROUP = 16
S5_STATE = 64
S5_CHUNK = 256
S5_SEG = S5_CHUNK // SUBLANES
S5_BLK_CH = 256
S5_BLK_ST = 1024


def _params(sem):
    return pltpu.CompilerParams(dimension_semantics=sem, vmem_limit_bytes=V7X_VMEM_LIMIT_BYTES)


def _sigmoid(x):
    return 1.0 / (1.0 + jnp.exp(-x))


def _silu(x):
    return x * _sigmoid(x)


def _gelu(x):
    return 0.5 * x * (1.0 + jnp.tanh(math.sqrt(2.0 / math.pi) * (x + 0.044715 * (x * x * x))))


def _softplus(x):
    return jnp.maximum(x, 0.0) + jnp.log1p(jnp.exp(-jnp.abs(x)))


def _dot(a, b):
    return jnp.dot(a, b, preferred_element_type=F32)


def _dot_nt(a, b):
    return lax.dot_general(a, b, (((1,), (1,)), ((), ())), preferred_element_type=F32)


def _split3(x):
    hi = x.astype(BF16)
    r1 = x - hi.astype(F32)
    mid = r1.astype(BF16)
    lo = (r1 - mid.astype(F32)).astype(BF16)
    return hi, mid, lo


def _dot_exact_lhs(sel, x):
    hi, mid, lo = _split3(x)
    return _dot(sel, hi) + (_dot(sel, mid) + _dot(sel, lo))


def _dot_f32(a, b):
    ah, am, al = _split3(a)
    bh, bm, bl = _split3(b)
    small = _dot(am, bm) + _dot(ah, bl) + _dot(al, bh)
    return _dot(ah, bh) + (_dot(ah, bm) + _dot(am, bh) + small)


def _mm_kernel(*refs, n_a, n_e, prologue, epilogue, bf16_copy):
    a_refs = refs[:n_a]
    w_ref = refs[n_a]
    e_refs = refs[n_a + 1:n_a + 1 + n_e]
    o_ref = refs[n_a + 1 + n_e]
    n_out = 2 if bf16_copy else 1
    if prologue is None:
        a = a_refs[0][...]
    else:
        a_scr = refs[n_a + 1 + n_e + n_out]

        @pl.when(pl.program_id(1) == 0)
        def _():
            a_scr[...] = prologue(*[r[...] for r in a_refs]).astype(BF16)

        a = a_scr[...]
    acc = _dot(a, w_ref[...])
    out = epilogue(acc, *[r[...] for r in e_refs])
    o_ref[...] = out.astype(o_ref.dtype)
    if bf16_copy:
        refs[n_a + 2 + n_e][...] = out.astype(BF16)


def _fused_matmul(a_ins, w, e_ins, *, prologue, epilogue, out_dtype, tm, tn, rows_per_batch=None, name,
                  bf16_copy=False):
    m = next(item[1].shape[0] for item in a_ins if item[0] == 'row')
    k, n = w.shape
    tm = min(tm, m)
    tn = next(t for t in (1024, 768, 512, 384, 256, 128) if t <= tn and n % t == 0)
    assert m % tm == 0
    rpb = rows_per_batch

    def bidx(i):
        return (i * tm) // rpb

    in_specs, args = [], []
    for item in a_ins:
        kind, arr = item[0], item[1]
        wd = item[2] if len(item) > 2 else arr.shape[-1]
        coff = item[3] if len(item) > 3 else 0
        if kind == 'row':
            in_specs.append(pl.BlockSpec((tm, wd), lambda i, j, coff=coff: (i, coff)))
        elif kind == 'vec':
            in_specs.append(pl.BlockSpec((1, wd), lambda i, j: (0, 0)))
        else:
            in_specs.append(pl.BlockSpec((None, 1, wd), lambda i, j: (bidx(i), 0, 0)))
        args.append(arr)
    in_specs.append(pl.BlockSpec((k, tn), lambda i, j: (0, j)))
    args.append(w)
    for item in e_ins:
        kind, arr = item[0], item[1]
        off = (item[2] if len(item) > 2 else 0) // tn
        if kind == 'tile':
            assert len(item) < 3 or item[2] % tn == 0
            in_specs.append(pl.BlockSpec((tm, tn), lambda i, j, off=off: (i, j + off)))
        elif kind == 'col':
            in_specs.append(pl.BlockSpec((1, tn), lambda i, j: (0, j)))
        else:
            in_specs.append(pl.BlockSpec((None, 1, tn), lambda i, j: (bidx(i), 0, j)))
        args.append(arr)
    scratch = [] if prologue is None else [pltpu.VMEM((tm, k), BF16)]
    kern = functools.partial(_mm_kernel, n_a=len(a_ins), n_e=len(e_ins), prologue=prologue, epilogue=epilogue,
                             bf16_copy=bf16_copy)
    out_shape = jax.ShapeDtypeStruct((m, n), out_dtype)
    out_spec = pl.BlockSpec((tm, tn), lambda i, j: (i, j))
    if bf16_copy:
        out_shape, out_spec = (out_shape, jax.ShapeDtypeStruct((m, n), BF16)), (out_spec, out_spec)
    return pl.pallas_call(
        kern, out_shape=out_shape, grid=(m // tm, n // tn), in_specs=in_specs, out_specs=out_spec,
        scratch_shapes=scratch, compiler_params=_params(("parallel", "arbitrary")), name=name)(*args)


def _pad_cols(w, mult):
    n = w.shape[-1]
    npad = (-n) % mult
    if npad:
        w = jnp.pad(w, ((0, 0), (0, npad)))
    return w


def _modnorm_prologue(x, g, scale, shift):
    r = x * lax.rsqrt(jnp.mean(x * x, axis=-1, keepdims=True) + EPS) * g
    return r * (1.0 + scale) + shift


def _identity_epilogue(acc):
    return acc


def _residual_epilogue(acc, x, gate):
    return x + gate * acc


def _in_proj(x, g, scale, shift, w, *, batch_kind, rows_per_batch, tm, tn, name, bf16_copy=False):
    kind = 'batch' if batch_kind else 'row'
    return _fused_matmul([('row', x), ('vec', g), (kind, scale), (kind, shift)], w, [],
                         prologue=_modnorm_prologue, epilogue=_identity_epilogue, out_dtype=F32,
                         tm=tm, tn=tn, rows_per_batch=rows_per_batch, name=name, bf16_copy=bf16_copy)


def _gate_proj_kernel(x_ref, g_ref, scale_ref, shift_ref, w_ref, o_ref):
    h = _modnorm_prologue(x_ref[...], g_ref[...], scale_ref[...], shift_ref[...])
    o_ref[...] = _dot_f32(h, w_ref[...])


def _gate_proj(x, g, scale, shift, w, *, batch_kind, rows_per_batch, tm, name):
    m, d = x.shape
    n = w.shape[1]
    tm = min(tm, m, MM_TM_WIDE_ROWS)
    if batch_kind:
        mod_spec = pl.BlockSpec((None, 1, d), lambda i: ((i * tm) // rows_per_batch, 0, 0))
    else:
        mod_spec = pl.BlockSpec((tm, d), lambda i: (i, 0))
    return pl.pallas_call(
        _gate_proj_kernel, out_shape=jax.ShapeDtypeStruct((m, n), F32), grid=(m // tm,),
        in_specs=[pl.BlockSpec((tm, d), lambda i: (i, 0)), pl.BlockSpec((1, d), lambda i: (0, 0)),
                  mod_spec, mod_spec, pl.BlockSpec((d, n), lambda i: (0, 0))],
        out_specs=pl.BlockSpec((tm, n), lambda i: (i, 0)),
        compiler_params=_params(("parallel",)), name=name)(x, g, scale, shift, w)


def _out_proj(a_ins, w, x, gate, *, batch_kind, rows_per_batch, tm, tn, name, prologue=None):
    kind = 'batchcol' if batch_kind else 'tile'
    return _fused_matmul(a_ins, w, [('tile', x), (kind, gate)], prologue=prologue, epilogue=_residual_epilogue,
                         out_dtype=F32, tm=tm, tn=tn, rows_per_batch=rows_per_batch, name=name)


def _mod_kernel(c_ref, w_ref, b_ref, o_ref):
    o_ref[...] = _dot(c_ref[...].astype(BF16), w_ref[...].astype(BF16)) + b_ref[...]


def _modulation(c_all, w_mod, b_mod, tn=512):
    depth, d, n = w_mod.shape
    rows = c_all.shape[0]
    return pl.pallas_call(
        _mod_kernel, out_shape=jax.ShapeDtypeStruct((depth, rows, n), F32), grid=(depth, n // tn),
        in_specs=[pl.BlockSpec((rows, d), lambda l, j: (0, 0)),
                  pl.BlockSpec((None, d, tn), lambda l, j: (l, 0, j)),
                  pl.BlockSpec((None, 1, tn), lambda l, j: (l, 0, j))],
        out_specs=pl.BlockSpec((None, rows, tn), lambda l, j: (l, 0, j)),
        compiler_params=_params(("parallel", "parallel")), name="adaln_modulation")(
            c_all, w_mod, b_mod.reshape(depth, 1, n))


def _rmsnorm_kernel(x_ref, g_ref, o_ref):
    x = x_ref[...]
    o_ref[...] = x * lax.rsqrt(jnp.mean(x * x, axis=-1, keepdims=True) + EPS) * g_ref[...]


def _final_norm(x, g, tm=512):
    m, d = x.shape
    tm = min(tm, m)
    return pl.pallas_call(
        _rmsnorm_kernel, out_shape=jax.ShapeDtypeStruct((m, d), F32), grid=(m // tm,),
        in_specs=[pl.BlockSpec((tm, d), lambda i: (i, 0)), pl.BlockSpec((1, d), lambda i: (0, 0))],
        out_specs=pl.BlockSpec((tm, d), lambda i: (i, 0)),
        compiler_params=_params(("parallel",)), name="final_rmsnorm")(x, g.reshape(1, d))


def _s5_tables(lam_re, lam_im, log_dt, b_re, b_im, c_re, c_im, d_skip):
    f = F32
    groups, p = lam_re.shape
    nblk = groups * S5_GROUP // S5_BLK_CH
    gpb = groups // nblk
    lr, li = lam_re.astype(f), lam_im.astype(f)
    dt = jnp.exp(log_dt.astype(f))[:, None]
    ldr, ldi = lr * dt, li * dt
    kk = jnp.arange(1, S5_SEG + 1, dtype=f)[:, None, None]
    pmag = jnp.exp(kk * ldr)
    pw_re, pw_im = pmag * jnp.cos(kk * ldi), pmag * jnp.sin(kk * ldi)
    ab_re, ab_im = jnp.exp(ldr) * jnp.cos(ldi), jnp.exp(ldr) * jnp.sin(ldi)
    den = lr * lr + li * li
    nr, ni = ab_re - 1.0, ab_im
    fr, fi = (nr * lr + ni * li) / den, (ni * lr - nr * li) / den
    bre, bim = b_re.astype(f), b_im.astype(f)
    bb_re = fr[..., None] * bre - fi[..., None] * bim
    bb_im = fr[..., None] * bim + fi[..., None] * bre
    eye = jnp.eye(gpb, dtype=f)

    def bd_in(bb):
        t = bb.reshape(nblk, gpb, p, S5_GROUP).transpose(0, 1, 3, 2)
        return jnp.einsum('bgkp,gh->bgkhp', t, eye).reshape(nblk, gpb * S5_GROUP, gpb * p).astype(BF16)

    def bd_out(c):
        t = c.astype(f).reshape(nblk, gpb, S5_GROUP, p).transpose(0, 1, 3, 2)
        return jnp.einsum('bgpk,gh->bgphk', t, eye).reshape(nblk, gpb * p, gpb * S5_GROUP).astype(BF16)

    def lanes(t):
        lead = t.shape[:-2]
        t = t.reshape(lead + (nblk, gpb * p))
        return jnp.moveaxis(t, -2, 0)

    return dict(
        bb_re=bd_in(bb_re), bb_im=bd_in(bb_im), c_re=bd_out(c_re), c_im=bd_out(c_im),
        ab_re=lanes(ab_re[None]), ab_im=lanes(ab_im[None]),
        pw_re=lanes(pw_re), pw_im=lanes(pw_im),
        d=d_skip.astype(f).reshape(1, -1), nblk=nblk)


def _s5_perm():
    pm = np.zeros((S5_CHUNK, S5_CHUNK), np.float32)
    r = np.arange(S5_CHUNK)
    pm[r, (r % SUBLANES) * S5_SEG + r // SUBLANES] = 1.0
    return jnp.asarray(pm, BF16), jnp.asarray(pm.T, BF16)


def _s5_scan_kernel(u_ref, pm_ref, pmt_ref, bbre_ref, bbim_ref, cre_ref, cim_ref, abre_ref, abim_ref,
                    pwre_ref, pwim_ref, d_ref, y_ref, sre_out, sim_out,
                    xre, xim, car_re, car_im, cin_re, cin_im, lend_re, lend_im):
    n = pl.program_id(2)
    nst = xre.shape[1]

    @pl.when(n == 0)
    def _():
        car_re[...] = jnp.zeros_like(car_re)
        car_im[...] = jnp.zeros_like(car_im)

    u = u_ref[...]
    up = _dot(pm_ref[...], u.astype(BF16)).astype(BF16)
    xre[...] = _dot(up, bbre_ref[...])
    xim[...] = _dot(up, bbim_ref[...])
    are = jnp.broadcast_to(abre_ref[...], (SUBLANES, nst))
    aim = jnp.broadcast_to(abim_ref[...], (SUBLANES, nst))
    sre = jnp.zeros((SUBLANES, nst), F32)
    sim = jnp.zeros((SUBLANES, nst), F32)
    for i in range(S5_SEG):
        r = slice(SUBLANES * i, SUBLANES * (i + 1))
        nre = are * sre - aim * sim + xre[r, :]
        nim = are * sim + aim * sre + xim[r, :]
        xre[r, :] = nre
        xim[r, :] = nim
        sre, sim = nre, nim
    lend_re[...] = sre
    lend_im[...] = sim
    a_re = pwre_ref[S5_SEG - 1:S5_SEG, :]
    a_im = pwim_ref[S5_SEG - 1:S5_SEG, :]
    cr, ci = car_re[...], car_im[...]
    for s in range(SUBLANES):
        cin_re[s:s + 1, :] = cr
        cin_im[s:s + 1, :] = ci
        lr, li = lend_re[s:s + 1, :], lend_im[s:s + 1, :]
        cr, ci = a_re * cr - a_im * ci + lr, a_re * ci + a_im * cr + li
    car_re[...] = cr
    car_im[...] = ci
    cinr, cini = cin_re[...], cin_im[...]
    for i in range(S5_SEG):
        r = slice(SUBLANES * i, SUBLANES * (i + 1))
        pr, pi_ = pwre_ref[i:i + 1, :], pwim_ref[i:i + 1, :]
        xre[r, :] = xre[r, :] + (pr * cinr - pi_ * cini)
        xim[r, :] = xim[r, :] + (pr * cini + pi_ * cinr)
    yp = _dot(xre[...].astype(BF16), cre_ref[...]) - _dot(xim[...].astype(BF16), cim_ref[...])
    hi = yp.astype(BF16)
    lo = (yp - hi.astype(F32)).astype(BF16)
    y = _dot(pmt_ref[...], hi) + _dot(pmt_ref[...], lo) + d_ref[...] * u
    y_ref[...] = _gelu(y)

    @pl.when(n == pl.num_programs(2) - 1)
    def _():
        sre_out[...] = cr
        sim_out[...] = ci


def _s5_scan(proj, tabs, batch, seq):
    nblk = tabs['nblk']
    nch = seq // S5_CHUNK
    pm, pmt = _s5_perm()
    nstate = nblk * S5_BLK_ST
    const3 = lambda shape: pl.BlockSpec((None,) + shape, lambda k, b, n: (k, 0, 0))
    y, sre, sim = pl.pallas_call(
        _s5_scan_kernel,
        out_shape=(jax.ShapeDtypeStruct((batch * seq, nblk * S5_BLK_CH), F32),
                   jax.ShapeDtypeStruct((batch, 1, nstate), F32),
                   jax.ShapeDtypeStruct((batch, 1, nstate), F32)),
        grid=(nblk, batch, nch),
        in_specs=[pl.BlockSpec((S5_CHUNK, S5_BLK_CH), lambda k, b, n: (b * nch + n, k)),
                  pl.BlockSpec((S5_CHUNK, S5_CHUNK), lambda k, b, n: (0, 0)),
                  pl.BlockSpec((S5_CHUNK, S5_CHUNK), lambda k, b, n: (0, 0)),
                  const3((S5_BLK_CH, S5_BLK_ST)), const3((S5_BLK_CH, S5_BLK_ST)),
                  const3((S5_BLK_ST, S5_BLK_CH)), const3((S5_BLK_ST, S5_BLK_CH)),
                  const3((1, S5_BLK_ST)), const3((1, S5_BLK_ST)),
                  const3((S5_SEG, S5_BLK_ST)), const3((S5_SEG, S5_BLK_ST)),
                  pl.BlockSpec((1, S5_BLK_CH), lambda k, b, n: (0, k))],
        out_specs=(pl.BlockSpec((S5_CHUNK, S5_BLK_CH), lambda k, b, n: (b * nch + n, k)),
                   pl.BlockSpec((None, 1, S5_BLK_ST), lambda k, b, n: (b, 0, k)),
                   pl.BlockSpec((None, 1, S5_BLK_ST), lambda k, b, n: (b, 0, k))),
        scratch_shapes=[pltpu.VMEM((S5_CHUNK, S5_BLK_ST), F32), pltpu.VMEM((S5_CHUNK, S5_BLK_ST), F32),
                        pltpu.VMEM((1, S5_BLK_ST), F32), pltpu.VMEM((1, S5_BLK_ST), F32),
                        pltpu.VMEM((SUBLANES, S5_BLK_ST), F32), pltpu.VMEM((SUBLANES, S5_BLK_ST), F32),
                        pltpu.VMEM((SUBLANES, S5_BLK_ST), F32), pltpu.VMEM((SUBLANES, S5_BLK_ST), F32)],
        compiler_params=_params(("parallel", "parallel", "arbitrary")), name="s5_scan")(
            proj, pm, pmt, tabs['bb_re'], tabs['bb_im'], tabs['c_re'], tabs['c_im'],
            tabs['ab_re'], tabs['ab_im'], tabs['pw_re'], tabs['pw_im'], tabs['d'])
    return y, sre, sim


def _s5_step_kernel(u_ref, hre_ref, him_ref, bbre_ref, bbim_ref, cre_ref, cim_ref, abre_ref, abim_ref, d_ref,
                    y_ref, sre_out, sim_out):
    u = u_ref[...]
    ub = u.astype(BF16)
    are, aim = abre_ref[...], abim_ref[...]
    hre, him = hre_ref[...], him_ref[...]
    sre = are * hre - aim * him + _dot(ub, bbre_ref[...])
    sim = are * him + aim * hre + _dot(ub, bbim_ref[...])
    sre_out[...] = sre
    sim_out[...] = sim
    y = _dot(sre.astype(BF16), cre_ref[...]) - _dot(sim.astype(BF16), cim_ref[...]) + d_ref[...] * u
    y_ref[...] = _gelu(y)


def _s5_step(proj, h_re, h_im, tabs):
    nblk = tabs['nblk']
    rows = proj.shape[0]
    nstate = nblk * S5_BLK_ST
    const3 = lambda shape: pl.BlockSpec((None,) + shape, lambda k: (k, 0, 0))
    lane_blk = lambda w: pl.BlockSpec((rows, w), lambda k: (0, k))
    return pl.pallas_call(
        _s5_step_kernel,
        out_shape=(jax.ShapeDtypeStruct((rows, nblk * S5_BLK_CH), F32),
                   jax.ShapeDtypeStruct((rows, nstate), F32), jax.ShapeDtypeStruct((rows, nstate), F32)),
        grid=(nblk,),
        in_specs=[lane_blk(S5_BLK_CH), lane_blk(S5_BLK_ST), lane_blk(S5_BLK_ST),
                  const3((S5_BLK_CH, S5_BLK_ST)), const3((S5_BLK_CH, S5_BLK_ST)),
                  const3((S5_BLK_ST, S5_BLK_CH)), const3((S5_BLK_ST, S5_BLK_CH)),
                  const3((1, S5_BLK_ST)), const3((1, S5_BLK_ST)),
                  pl.BlockSpec((1, S5_BLK_CH), lambda k: (0, k))],
        out_specs=(lane_blk(S5_BLK_CH), lane_blk(S5_BLK_ST), lane_blk(S5_BLK_ST)),
        compiler_params=_params(("parallel",)), name="s5_step")(
            proj, h_re.reshape(rows, nstate), h_im.reshape(rows, nstate),
            tabs['bb_re'], tabs['bb_im'], tabs['c_re'], tabs['c_im'], tabs['ab_re'], tabs['ab_im'], tabs['d'])


def _s5_weights(w_in, lam_re, lam_im, log_dt, b_re, b_im, c_re, c_im, d_skip, w_glu, b_glu, w_out):
    tabs = _s5_tables(lam_re, lam_im, log_dt, b_re, b_im, c_re, c_im, d_skip)
    return (w_in.astype(BF16), w_glu.astype(BF16), b_glu.astype(F32).reshape(1, -1), w_out.astype(BF16), tabs)


def _glu_epilogue(acc, gy, z, b):
    return gy * _sigmoid(acc + b) * _silu(z)


def _cast_prologue(a):
    return a


def _s5_layer(x, mods, h_state, w, *, prompt, batch, seq, tm):
    g, scale, shift, gate = mods
    w_in, w_glu, b_glu, w_out, tabs = w
    width = w_glu.shape[0]
    proj = _in_proj(x, g, scale, shift, w_in, batch_kind=prompt, rows_per_batch=seq, tm=tm, tn=MM_TN, name="s5_in_proj")
    if prompt:
        gy, sre, sim = _s5_scan(proj, tabs, batch, seq)
    else:
        gy, sre, sim = _s5_step(proj, h_state[0], h_state[1], tabs)
    a = _fused_matmul([('row', gy)], w_glu, [('tile', gy), ('tile', proj, width), ('col', b_glu)],
                      prologue=_cast_prologue, epilogue=_glu_epilogue, out_dtype=BF16, tm=min(tm, MM_TM_WIDE_ROWS),
                      tn=MM_TN, name="s5_glu")
    x_new = _out_proj([('row', a)], w_out, x, gate, batch_kind=prompt, rows_per_batch=seq, tm=tm, tn=MM_TN, name="s5_out_proj")
    return x_new, sre, sim


def _conv_silu_chunk(x_ref, w_ref, b_ref, pad_ref, first):
    c = x_ref.shape[0]

    @pl.when(first)
    def _():
        pad_ref[0:SUBLANES, :] = jnp.zeros((SUBLANES, pad_ref.shape[1]), F32)

    pad_ref[SUBLANES:SUBLANES + c, :] = x_ref[...]
    acc = w_ref[3:4, :] * pad_ref[SUBLANES:SUBLANES + c, :]
    for j in range(CONV_W - 1):
        off = SUBLANES - (CONV_W - 1) + j
        acc = acc + w_ref[j:j + 1, :] * pad_ref[off:off + c, :]
    if b_ref is not None:
        acc = acc + b_ref[...]
    pad_ref[0:SUBLANES, :] = pad_ref[c:c + SUBLANES, :]
    return _silu(acc)


def _conv_silu_step(x, taps_ref, w_ref, b_ref):
    acc = w_ref[3:4, :] * x
    for j in range(CONV_W - 1):
        acc = acc + w_ref[j:j + 1, :] * taps_ref[j]
    if b_ref is not None:
        acc = acc + b_ref[...]
    return _silu(acc)


def _tri_masks(c):
    t = lax.broadcasted_iota(jnp.int32, (c, c), 0)
    s = lax.broadcasted_iota(jnp.int32, (c, c), 1)
    return s <= t, s < t


def _pad_to_square_t(x, n):
    rows = x.shape[0]
    return jnp.concatenate([x, jnp.zeros((n - rows, n), x.dtype)], axis=0).T


M2_HEADDIM = 64
M2_STATE = 128
M2_GROUPS = 8
M2_CHUNK = 128


def _m2_ssd_kernel(x_ref, b_ref, c_ref, dt_ref, z_ref, wx_ref, wb_ref, wc_ref, bx_ref, bb_ref, bc_ref,
                   dtb_ref, nega_ref, dsk_ref, nw_ref, o_ref, sout_ref, xpad, bpad, cpad, s_ref, y_ref):
    n = pl.program_id(1)
    first = n == 0
    c = x_ref.shape[0]
    npairs = s_ref.shape[0]
    pairs_per_group = npairs // M2_GROUPS

    @pl.when(first)
    def _():
        s_ref[...] = jnp.zeros_like(s_ref)

    xs = _conv_silu_chunk(x_ref, wx_ref, bx_ref, xpad, first)
    bm = _conv_silu_chunk(b_ref, wb_ref, bb_ref, bpad, first).astype(BF16)
    cm = _conv_silu_chunk(c_ref, wc_ref, bc_ref, cpad, first).astype(BF16)
    dtv = _softplus(dt_ref[...] + dtb_ref[...])
    la = nega_ref[...] * dtv
    incl, _ = _tri_masks(c)
    tri = jnp.where(incl, 1.0, 0.0).astype(BF16)
    cum = _dot_exact_lhs(tri, la)
    cum_t = cum.T
    ecum_all = jnp.exp(cum)
    wend_all = jnp.exp(cum[c - 1:c, :] - cum)
    elast_t = jnp.exp(cum_t[:, c - 1:c])
    lane_first = lax.broadcasted_iota(jnp.int32, (c, LANES), 1) < M2_HEADDIM
    row_first = lax.broadcasted_iota(jnp.int32, (LANES, LANES), 0) < M2_HEADDIM

    for g in range(M2_GROUPS):
        bg = bm[:, g * M2_STATE:(g + 1) * M2_STATE]
        cg = cm[:, g * M2_STATE:(g + 1) * M2_STATE]
        gm = _dot_nt(cg, bg)
        for j in range(pairs_per_group):
            p = g * pairs_per_group + j
            ha, hb = 2 * p, 2 * p + 1
            xp = xs[:, p * LANES:(p + 1) * LANES]

            def decay_weights(h):
                seg = cum[:, h:h + 1] - cum_t[h:h + 1, :]
                dec = jnp.where(incl, jnp.exp(jnp.where(incl, seg, 0.0)), 0.0)
                return (gm * dec).astype(BF16)

            xdt = xp * jnp.where(lane_first, dtv[:, ha:ha + 1], dtv[:, hb:hb + 1])
            xdt_a = jnp.where(lane_first, xdt, 0.0)
            xdt_b = xdt - xdt_a
            y = _dot(decay_weights(ha), xdt_a.astype(BF16)) + _dot(decay_weights(hb), xdt_b.astype(BF16))
            sp = s_ref[p]
            y = y + _dot_nt(cg, sp.astype(BF16)) * jnp.where(lane_first, ecum_all[:, ha:ha + 1], ecum_all[:, hb:hb + 1])
            y_ref[:, p * LANES:(p + 1) * LANES] = y + dsk_ref[:, p * LANES:(p + 1) * LANES] * xp
            xw = xdt * jnp.where(lane_first, wend_all[:, ha:ha + 1], wend_all[:, hb:hb + 1])
            dmat = jnp.where(row_first, elast_t[ha:ha + 1, :], elast_t[hb:hb + 1, :])
            s_ref[p] = sp * dmat + _dot(xw.T.astype(BF16), bg)

    o_ref[...] = _gated_rmsnorm_prologue(y_ref[...], z_ref[...], nw_ref[...]).astype(o_ref.dtype)

    @pl.when(n == pl.num_programs(1) - 1)
    def _():
        sout_ref[...] = s_ref[...]


def _m2_ssd(proj, gates, mw, batch, seq):
    c = M2_CHUNK
    nch = seq // c
    width = mw['width']
    gs = M2_GROUPS * M2_STATE
    npairs = width // LANES
    xo, bo, co = width // width, (2 * width) // gs, (2 * width + gs) // gs
    row = lambda w, off: pl.BlockSpec((c, w), lambda b, n: (b * nch + n, off))
    par = lambda r, w, off: pl.BlockSpec((r, w), lambda b, n: (0, off))
    return pl.pallas_call(
        _m2_ssd_kernel,
        out_shape=(jax.ShapeDtypeStruct((batch * seq, width), BF16),
                   jax.ShapeDtypeStruct((batch, npairs, LANES, M2_STATE), F32)),
        grid=(batch, nch),
        in_specs=[row(width, xo), row(gs, bo), row(gs, co), row(LANES, 0), row(width, 0),
                  par(CONV_W, width, 0), par(CONV_W, gs, width // gs), par(CONV_W, gs, width // gs + 1),
                  par(1, width, 0), par(1, gs, width // gs), par(1, gs, width // gs + 1),
                  par(1, LANES, 0), par(1, LANES, 0), par(1, width, 0), par(1, width, 0)],
        out_specs=(pl.BlockSpec((c, width), lambda b, n: (b * nch + n, 0)),
                   pl.BlockSpec((None, npairs, LANES, M2_STATE), lambda b, n: (b, 0, 0, 0))),
        scratch_shapes=[pltpu.VMEM((c + SUBLANES, width), F32), pltpu.VMEM((c + SUBLANES, gs), F32),
                        pltpu.VMEM((c + SUBLANES, gs), F32), pltpu.VMEM((npairs, LANES, M2_STATE), F32),
                        pltpu.VMEM((c, width), F32)],
        compiler_params=_params(("parallel", "arbitrary")), name="m2_ssd")(
            proj, proj, proj, gates, proj, mw['conv_w'], mw['conv_w'], mw['conv_w'], mw['conv_b'], mw['conv_b'],
            mw['conv_b'], mw['dt_bias'], mw['neg_a'], mw['d_ch'], mw['norm'])


def _m2_pre_step_kernel(xbc_ref, dt_ref, taps_ref, w_ref, b_ref, dtb_ref, nega_ref, act_ref, dtv_ref, dec_ref):
    act_ref[...] = _conv_silu_step(xbc_ref[...], taps_ref, w_ref, b_ref)
    dtv = _softplus(dt_ref[...] + dtb_ref[...])
    dtv_ref[...] = dtv
    dec_ref[...] = jnp.exp(nega_ref[...] * dtv)


def _m2_pre_step(proj, dt_raw, taps, mw):
    rows = proj.shape[0]
    width = mw['width']
    cdim = mw['conv_w'].shape[1]
    xbc = proj[:, width:width + cdim]
    return pl.pallas_call(
        _m2_pre_step_kernel,
        out_shape=(jax.ShapeDtypeStruct((rows, cdim), F32), jax.ShapeDtypeStruct((rows, LANES), F32),
                   jax.ShapeDtypeStruct((rows, LANES), F32)),
        name="m2_pre_step", compiler_params=pltpu.CompilerParams(vmem_limit_bytes=V7X_VMEM_LIMIT_BYTES))(
            xbc, dt_raw, taps, mw['conv_w'], mw['conv_b'], mw['dt_bias'], mw['neg_a'])


def _m2_state_step_kernel(x_ref, dtc_ref, decc_ref, b_ref, c_ref, dsk_ref, s_ref, y_ref, sout_ref):
    rows = x_ref.shape[0]
    pairs = x_ref.shape[1] // LANES
    bv = b_ref[...]
    cb = c_ref[...].astype(BF16)
    for j in range(pairs):
        sl = slice(j * LANES, (j + 1) * LANES)
        xp = x_ref[:, sl]
        xdt_t = _pad_to_square_t(xp * dtc_ref[:, sl], LANES)
        dec_t = _pad_to_square_t(decc_ref[:, sl], LANES)
        for b in range(rows):
            sp = s_ref[b, j]
            s_new = sp * dec_t[:, b:b + 1] + xdt_t[:, b:b + 1] * bv[b:b + 1, :]
            sout_ref[b, j] = s_new
            y_ref[b:b + 1, sl] = _dot_nt(cb[b:b + 1, :], s_new.astype(BF16)) + dsk_ref[:, sl] * xp[b:b + 1, :]


def _m2_state_step(act, dt_ch, dec_ch, ssm, mw):
    rows = act.shape[0]
    width = mw['width']
    gw = width // M2_GROUPS
    ppg = gw // LANES
    npairs = width // LANES
    gs = M2_GROUPS * M2_STATE
    s4 = ssm.reshape(rows, npairs, LANES, M2_STATE)
    blk = lambda w, base: pl.BlockSpec((rows, w), lambda g: (0, base + g))
    return pl.pallas_call(
        _m2_state_step_kernel,
        out_shape=(jax.ShapeDtypeStruct((rows, width), F32), jax.ShapeDtypeStruct(s4.shape, F32)),
        grid=(M2_GROUPS,),
        in_specs=[blk(gw, 0), blk(gw, 0), blk(gw, 0), blk(M2_STATE, width // M2_STATE),
                  blk(M2_STATE, (width + gs) // M2_STATE), pl.BlockSpec((1, gw), lambda g: (0, g)),
                  pl.BlockSpec((rows, ppg, LANES, M2_STATE), lambda g: (0, g, 0, 0))],
        out_specs=(blk(gw, 0), pl.BlockSpec((rows, ppg, LANES, M2_STATE), lambda g: (0, g, 0, 0))),
        compiler_params=_params(("parallel",)), name="m2_state_step")(
            act, dt_ch, dec_ch, act, act, mw['d_ch'], s4)


def _m2_weights(w_in, conv_w, conv_b, dt_bias, a_log, d_skip, norm_w, w_out):
    f = F32
    heads = dt_bias.shape[0]
    width = norm_w.shape[0]
    pad = LANES - heads
    cdim = conv_w.shape[1]
    return dict(
        w_in=_pad_cols(w_in[:, :width + cdim].astype(BF16), 512),
        w_gate=_pad_cols(w_in[:, width + cdim:].astype(f), LANES),
        conv_w=conv_w.astype(f), conv_b=conv_b.astype(f).reshape(1, -1),
        dt_bias=jnp.pad(dt_bias.astype(f), (0, pad)).reshape(1, LANES),
        neg_a=jnp.pad(-jnp.exp(a_log.astype(f)), (0, pad)).reshape(1, LANES),
        d_ch=jnp.repeat(d_skip.astype(f), width // heads).reshape(1, width),
        norm=norm_w.astype(f).reshape(1, width), w_out=w_out.astype(BF16), width=width, heads=heads)


def _gated_rmsnorm_prologue(y, z, w):
    v = y * _silu(z)
    return v * lax.rsqrt(jnp.mean(v * v, axis=-1, keepdims=True) + EPS) * w


def _m2_layer(x, mods, state, mw, *, prompt, batch, seq, tm):
    g, scale, shift, gate = mods
    width, heads = mw['width'], mw['heads']
    cdim = mw['conv_w'].shape[1]
    proj = _in_proj(x, g, scale, shift, mw['w_in'], batch_kind=prompt, rows_per_batch=seq, tm=tm, tn=MM_TN,
                    name="m2_in_proj")
    gates = _gate_proj(x, g, scale, shift, mw['w_gate'], batch_kind=prompt, rows_per_batch=seq, tm=tm,
                       name="m2_gate_proj")
    if prompt:
        a, ssm = _m2_ssd(proj, gates, mw, batch, seq)
        conv_new = proj.reshape(batch, seq, -1)[:, seq - (CONV_W - 1):, width:width + cdim]
        x_new = _out_proj([('row', a)], mw['w_out'], x, gate, batch_kind=True, rows_per_batch=seq, tm=tm, tn=MM_TN,
                          name="m2_out_proj")
        return x_new, conv_new, ssm
    else:
        conv_buf, ssm_in = state
        taps = jnp.swapaxes(conv_buf.astype(F32), 0, 1)
        act, dtv, dec = _m2_pre_step(proj, gates, taps, mw)
        rep = width // heads
        dt_ch = jnp.repeat(dtv[:, :heads], rep, axis=1)
        dec_ch = jnp.repeat(dec[:, :heads], rep, axis=1)
        y, ssm = _m2_state_step(act, dt_ch, dec_ch, ssm_in.astype(F32), mw)
        conv_new = jnp.concatenate([conv_buf.astype(F32)[:, 1:], proj[:, None, width:width + cdim]], axis=1)
    x_new = _out_proj([('row', y), ('row', proj, width, 0), ('vec', mw['norm'])], mw['w_out'], x, gate,
                      batch_kind=prompt, rows_per_batch=seq, tm=min(tm, 256), tn=MM_TN, name="m2_out_proj",
                      prologue=_gated_rmsnorm_prologue)
    return x_new, conv_new, ssm


GD_DK = 128
GD_DV = 128
GD_CHUNK = 64


def _dot_3pass(a, b):
    ah = a.astype(BF16)
    al = (a - ah.astype(F32)).astype(BF16)
    bh = b.astype(BF16)
    bl = (b - bh.astype(F32)).astype(BF16)
    return _dot(ah, bh) + (_dot(ah, bl) + _dot(al, bh))


def _l2norm_rows(x):
    return x * lax.rsqrt(jnp.sum(x * x, axis=-1, keepdims=True) + EPS)


def _rmsnorm_rows(x, w):
    return x * lax.rsqrt(jnp.mean(x * x, axis=-1, keepdims=True) + EPS) * w


GD_INV_BASE = 16


def _bdot(a, b):
    return jnp.einsum('hmk,hkn->hmn', a, b, preferred_element_type=F32)


def _bdot_nt(a, b):
    return jnp.einsum('hmk,hnk->hmn', a, b, preferred_element_type=F32)


def _split2(x):
    hi = x.astype(BF16)
    return hi, (x - hi.astype(F32)).astype(BF16)


def _bdot_3pass(a, b):
    (ah, al), (bh, bl) = a, b
    return _bdot(ah, bh) + (_bdot(ah, bl) + _bdot(al, bh))


def _unit_lower_inverse(a_strict):
    c = a_strict.shape[-1]
    row = lax.broadcasted_iota(jnp.int32, (c, c), 0)
    col = lax.broadcasted_iota(jnp.int32, (c, c), 1)
    eye = jnp.where(row == col, 1.0, 0.0)
    blk = GD_INV_BASE
    shift = int(math.log2(blk))
    p = jnp.where((row >> shift) == (col >> shift), -a_strict, 0.0)
    t = eye + p
    ps = _split2(p)
    for _ in range(shift - 1):
        p = _bdot_3pass(ps, ps)
        ps = _split2(p)
        t = t + _bdot_3pass(_split2(t), ps)
    while blk < c:
        below = jnp.logical_and((row >> (shift + 1)) == (col >> (shift + 1)), (row >> shift) != (col >> shift))
        ts = _split2(t)
        tb = _bdot_3pass(ts, _split2(jnp.where(below, a_strict, 0.0)))
        t = t - _bdot_3pass(_split2(tb), ts)
        blk *= 2
        shift += 1
    return t


def _gd_chunk_kernel(qkv_ref, z_ref, braw_ref, araw_ref, cw_ref, nega_ref, dtb_ref, nw_ref,
                     o_ref, sout_ref, pad, s_ref):
    n = pl.program_id(1)
    first = n == 0
    c = qkv_ref.shape[0]
    hv = s_ref.shape[0]
    hk = hv // 2
    rep = hv // hk

    @pl.when(first)
    def _():
        s_ref[...] = jnp.zeros_like(s_ref)

    qkv = _conv_silu_chunk(qkv_ref, cw_ref, None, pad, first)
    beta = _sigmoid(braw_ref[...])
    gl = nega_ref[...] * _softplus(araw_ref[...] + dtb_ref[...])
    incl, strict = _tri_masks(c)
    tri = jnp.where(incl, 1.0, 0.0).astype(BF16)
    gcum = _dot_exact_lhs(tri, gl)
    gcum_t = jnp.concatenate([gcum, jnp.zeros((LANES - c, LANES), F32)], axis=0).T

    heads = range(hv)
    per_value_head = lambda t: jnp.stack([t[h // rep] for h in heads])
    q3 = jnp.stack([qkv[:, i * GD_DK:(i + 1) * GD_DK] for i in range(hk)])
    k3 = jnp.stack([qkv[:, (hk + i) * GD_DK:(hk + i + 1) * GD_DK] for i in range(hk)])
    v3 = jnp.stack([qkv[:, (2 * hk + h) * GD_DV:(2 * hk + h + 1) * GD_DV] for h in heads])
    q3 = _l2norm_rows(q3) * (GD_DK ** -0.5)
    k3 = _l2norm_rows(k3)
    k3b = k3.astype(BF16)
    kk = per_value_head(_bdot_nt(k3b, k3b))
    qk = per_value_head(_bdot_nt(q3.astype(BF16), k3b))
    q_v, k_v = per_value_head(q3), per_value_head(k3)
    colv = jnp.stack([gcum[:, h:h + 1] for h in heads])
    rowv = jnp.stack([gcum_t[h:h + 1, :c] for h in heads])
    bcol = jnp.stack([beta[:, h:h + 1] for h in heads])
    glast = colv[:, c - 1:c, :]
    ecol = jnp.exp(colv)
    dec = jnp.where(incl, jnp.exp(jnp.where(incl, colv - rowv, 0.0)), 0.0)
    a = jnp.where(strict, (bcol * kk) * dec, 0.0)
    tinv = _unit_lower_inverse(a)
    rhs = jnp.concatenate([v3 * bcol, (k_v * bcol) * ecol], axis=-1)
    th, tl = _split2(tinv)
    rb = rhs.astype(BF16)
    sol = _bdot(th, rb) + _bdot(tl, rb)
    u, w = sol[:, :, :GD_DV], sol[:, :, GD_DV:]
    s = s_ref[...]
    sb = s.astype(BF16)
    v_new = u - _bdot(w.astype(BF16), sb)
    o = _bdot((q_v * ecol).astype(BF16), sb) + _bdot((qk * dec).astype(BF16), v_new.astype(BF16))
    zpad = jnp.zeros((hv, LANES - c, GD_DV), F32)
    kd_t = jnp.swapaxes(jnp.concatenate([k_v * jnp.exp(glast - colv), zpad], axis=1), 1, 2)
    vn_pad = jnp.concatenate([v_new, zpad], axis=1)
    s_ref[...] = s * jnp.exp(glast) + _bdot(kd_t.astype(BF16), vn_pad.astype(BF16))
    on = _rmsnorm_rows(o, nw_ref[...])
    for h in heads:
        sl = slice(h * GD_DV, (h + 1) * GD_DV)
        o_ref[:, sl] = (on[h] * _silu(z_ref[:, sl])).astype(o_ref.dtype)

    @pl.when(n == pl.num_programs(1) - 1)
    def _():
        sout_ref[...] = s_ref[...]


def _gd_chunked(proj, gates, gw, batch, seq):
    c = GD_CHUNK
    nch = seq // c
    cdim, width, hv = gw['cdim'], gw['width'], gw['hv']
    row = lambda w, off: pl.BlockSpec((c, w), lambda b, n: (b * nch + n, off))
    par = lambda r, w: pl.BlockSpec((r, w), lambda b, n: (0, 0))
    return pl.pallas_call(
        _gd_chunk_kernel,
        out_shape=(jax.ShapeDtypeStruct((batch * seq, width), BF16),
                   jax.ShapeDtypeStruct((batch, hv, GD_DK, GD_DV), F32)),
        grid=(batch, nch),
        in_specs=[row(cdim, 0), row(width, cdim // width), row(LANES, 0), row(LANES, 1),
                  par(CONV_W, cdim), par(1, LANES), par(1, LANES), par(1, GD_DV)],
        out_specs=(pl.BlockSpec((c, width), lambda b, n: (b * nch + n, 0)),
                   pl.BlockSpec((None, hv, GD_DK, GD_DV), lambda b, n: (b, 0, 0, 0))),
        scratch_shapes=[pltpu.VMEM((c + SUBLANES, cdim), F32), pltpu.VMEM((hv, GD_DK, GD_DV), F32)],
        compiler_params=_params(("parallel", "arbitrary")), name="gd_chunked")(
            proj, proj, gates, gates, gw['conv_w'], gw['neg_a'], gw['dt_bias'], gw['norm'])


def _gd_pre_step_kernel(qkv_ref, braw_ref, araw_ref, taps_ref, cw_ref, nega_ref, dtb_ref,
                        q_ref, k_ref, v_ref, beta_ref, eg_ref):
    hk = q_ref.shape[1] // GD_DK
    act = _conv_silu_step(qkv_ref[...], taps_ref, cw_ref, None)
    for kh in range(hk):
        sl = slice(kh * GD_DK, (kh + 1) * GD_DK)
        q_ref[:, sl] = _l2norm_rows(act[:, kh * GD_DK:(kh + 1) * GD_DK]) * (GD_DK ** -0.5)
        k_ref[:, sl] = _l2norm_rows(act[:, (hk + kh) * GD_DK:(hk + kh + 1) * GD_DK])
    v_ref[...] = act[:, 2 * hk * GD_DK:]
    beta_ref[...] = _sigmoid(braw_ref[...])
    eg_ref[...] = jnp.exp(nega_ref[...] * _softplus(araw_ref[...] + dtb_ref[...]))


def _gd_pre_step(proj, gates, taps, gw):
    rows = proj.shape[0]
    cdim, width, hv = gw['cdim'], gw['width'], gw['hv']
    qk_w = (cdim - width) // 2
    sd = lambda w: jax.ShapeDtypeStruct((rows, w), F32)
    return pl.pallas_call(
        _gd_pre_step_kernel, out_shape=(sd(qk_w), sd(qk_w), sd(width), sd(LANES), sd(LANES)),
        name="gd_pre_step", compiler_params=pltpu.CompilerParams(vmem_limit_bytes=V7X_VMEM_LIMIT_BYTES))(
            proj[:, :cdim], gates[:, :LANES], gates[:, LANES:], taps,
            gw['conv_w'], gw['neg_a'], gw['dt_bias'])


def _gd_state_step_kernel(q_ref, k_ref, v_ref, beta_ref, eg_ref, z_ref, nw_ref, s_ref, o_ref, sout_ref):
    rows = q_ref.shape[0]
    nk = q_ref.shape[1] // GD_DK
    rep = (v_ref.shape[1] // GD_DV) // nk
    nw = nw_ref[...]
    zrows = jnp.zeros((SUBLANES - 2, GD_DK), F32)
    for kh in range(nk):
        ksl = slice(kh * GD_DK, (kh + 1) * GD_DK)
        q8, k8 = q_ref[:, ksl], k_ref[:, ksl]
        k_t = _pad_to_square_t(k8, GD_DK)
        for b in range(rows):
            qb, kb = q8[b:b + 1, :], k8[b:b + 1, :]
            kq = jnp.concatenate([kb, qb, zrows], axis=0).astype(BF16)
            qk = jnp.sum(qb * kb, axis=-1, keepdims=True)
            for r in range(rep):
                h = kh * rep + r
                vsl = slice(h * GD_DV, (h + 1) * GD_DV)
                s = s_ref[b, h]
                ks_qs = _dot(kq, s.astype(BF16))
                eg = eg_ref[b:b + 1, vsl]
                beta = beta_ref[b:b + 1, vsl]
                v_new = beta * (v_ref[b:b + 1, vsl] - eg * ks_qs[0:1, :])
                o = eg * ks_qs[1:2, :] + qk * v_new
                sout_ref[b, h] = s * eg[:, 0:1] + k_t[:, b:b + 1] * v_new
                o_ref[b:b + 1, vsl] = _rmsnorm_rows(o, nw) * _silu(z_ref[b:b + 1, vsl])


def _gd_state_step(proj, qn, kn, v, beta_ch, eg_ch, state, gw, heads_per_step=4):
    rows = qn.shape[0]
    cdim, width, hv = gw['cdim'], gw['width'], gw['hv']
    steps = hv // heads_per_step
    kw = qn.shape[1] // steps
    vw = width // steps
    blk = lambda w, base=0: pl.BlockSpec((rows, w), lambda g, base=base: (0, base + g))
    sspec = pl.BlockSpec((rows, heads_per_step, GD_DK, GD_DV), lambda g: (0, g, 0, 0))
    return pl.pallas_call(
        _gd_state_step_kernel,
        out_shape=(jax.ShapeDtypeStruct((rows, width), F32), jax.ShapeDtypeStruct(state.shape, F32)),
        grid=(steps,),
        in_specs=[blk(kw), blk(kw), blk(vw), blk(vw), blk(vw), blk(vw, cdim // vw),
                  pl.BlockSpec((1, GD_DV), lambda g: (0, 0)), sspec],
        out_specs=(blk(vw), sspec),
        compiler_params=_params(("parallel",)), name="gd_state_step")(
            qn, kn, v, beta_ch, eg_ch, proj, gw['norm'], state)


def _gd_weights(w_in, conv_w, a_log, dt_bias, norm_w, w_out):
    f = F32
    hv = a_log.shape[0]
    cdim = conv_w.shape[1]
    width = w_out.shape[0]
    pad = LANES - hv
    base = cdim + width
    zeros = jnp.zeros((w_in.shape[0], pad), w_in.dtype)
    w_gate = jnp.concatenate([w_in[:, base:base + hv], zeros, w_in[:, base + hv:], zeros], axis=1)
    return dict(
        w_in=_pad_cols(w_in[:, :base].astype(BF16), 512), w_gate=w_gate.astype(f), conv_w=conv_w.astype(f),
        neg_a=jnp.pad(-jnp.exp(a_log.astype(f)), (0, pad)).reshape(1, LANES),
        dt_bias=jnp.pad(dt_bias.astype(f), (0, pad)).reshape(1, LANES),
        norm=norm_w.astype(f).reshape(1, -1), w_out=w_out.astype(BF16), cdim=cdim, width=width, hv=hv)


def _gd_layer(x, mods, state, gw, *, prompt, batch, seq, tm):
    g, scale, shift, gate = mods
    cdim, width, hv = gw['cdim'], gw['width'], gw['hv']
    proj = _in_proj(x, g, scale, shift, gw['w_in'], batch_kind=prompt, rows_per_batch=seq, tm=tm, tn=MM_TN,
                    name="gd_in_proj")
    gates = _gate_proj(x, g, scale, shift, gw['w_gate'], batch_kind=prompt, rows_per_batch=seq, tm=tm,
                       name="gd_gate_proj")
    if prompt:
        a, ssm = _gd_chunked(proj, gates, gw, batch, seq)
        conv_new = proj.reshape(batch, seq, -1)[:, seq - (CONV_W - 1):, :cdim]
        x_new = _out_proj([('row', a)], gw['w_out'], x, gate, batch_kind=True, rows_per_batch=seq, tm=tm, tn=MM_TN,
                          name="gd_out_proj")
    else:
        conv_buf, ssm_in = state
        taps = jnp.swapaxes(conv_buf.astype(F32), 0, 1)
        qn, kn, v, beta, eg = _gd_pre_step(proj, gates, taps, gw)
        beta_ch = jnp.repeat(beta[:, :hv], GD_DV, axis=1)
        eg_ch = jnp.repeat(eg[:, :hv], GD_DV, axis=1)
        a, ssm = _gd_state_step(proj, qn, kn, v, beta_ch, eg_ch, ssm_in.astype(F32), gw)
        conv_new = jnp.concatenate([conv_buf.astype(F32)[:, 1:], proj[:, None, :cdim]], axis=1)
        x_new = _out_proj([('row', a)], gw['w_out'], x, gate, batch_kind=False, rows_per_batch=seq, tm=tm, tn=MM_TN,
                          name="gd_out_proj", prologue=_cast_prologue)
    return x_new, conv_new, ssm


AT_DIM = 128
IDX_DIM = 128
TOPK_MAX = 256
REL_BUCKETS = 32
REL_MAX_DIST = 128
AT_TILE = 256
INT32_MIN = -2 ** 31
_NEG_BITS = int(np.float32(NEG).view(np.int32))
NEG_SORT_KEY = _NEG_BITS ^ 0x7FFFFFFF if _NEG_BITS < 0 else _NEG_BITS


def _bucket_starts():
    d = np.arange(0, REL_MAX_DIST + 1)
    exact = REL_BUCKETS // 2
    far = exact + (np.log(np.maximum(d, 1).astype(np.float32) / exact) / math.log(REL_MAX_DIST / exact)
                   * (REL_BUCKETS - exact)).astype(np.int32)
    bucket = np.where(d < exact, d, np.minimum(far, REL_BUCKETS - 1))
    assert np.all(np.diff(bucket) >= 0) and bucket[-1] == REL_BUCKETS - 1
    return [int(np.argmax(bucket >= b)) for b in range(REL_BUCKETS)]


def _bias_from_dist(dist, value_of_bucket):
    starts = _bucket_starts()
    val = value_of_bucket(REL_BUCKETS - 1)
    for b in range(REL_BUCKETS - 2, -1, -1):
        val = jnp.where(dist < starts[b + 1], value_of_bucket(b), val)
    return val


def _sort_key(x):
    x = jnp.where(x == 0.0, 0.0, x)
    b = pltpu.bitcast(x, jnp.int32)
    return jnp.where(b < 0, b ^ jnp.int32(0x7FFFFFFF), b)


def _kth_largest_key(count_ge, shape, k):
    def body(it, ans):
        cand = ans | jnp.left_shift(jnp.int32(1), 31 - it)
        cnt = count_ge(cand ^ jnp.int32(INT32_MIN))
        return jnp.where(cnt >= k, cand, ans)

    ans = lax.fori_loop(0, 32, body, jnp.zeros(shape, jnp.int32))
    return ans ^ jnp.int32(INT32_MIN)


def _relbias_tiles_kernel(rb_ref, o_ref):
    delta = pl.program_id(0) * AT_TILE
    h = pl.program_id(1)
    i = lax.broadcasted_iota(jnp.int32, (AT_TILE, AT_TILE), 0)
    j = lax.broadcasted_iota(jnp.int32, (AT_TILE, AT_TILE), 1)
    o_ref[...] = _bias_from_dist(delta + i - j, lambda b: rb_ref[b, h])


def _relbias_tiles(rel_bias):
    heads = rel_bias.shape[1]
    ntile = 3
    assert (ntile - 1) * AT_TILE - (AT_TILE - 1) >= REL_MAX_DIST
    return pl.pallas_call(
        _relbias_tiles_kernel, out_shape=jax.ShapeDtypeStruct((ntile, heads, AT_TILE, AT_TILE), F32),
        grid=(ntile, heads),
        in_specs=[pl.BlockSpec(memory_space=pltpu.SMEM)],
        out_specs=pl.BlockSpec((None, None, AT_TILE, AT_TILE), lambda d, h: (d, h, 0, 0)),
        compiler_params=_params(("parallel", "parallel")), name="at_relbias_tiles")(rel_bias.astype(F32))


def _at_index_kernel(qi_ref, wi_ref, ki_ref, o_ref, keys, cnt, *, k_sel, score_scale):
    qb = pl.program_id(1)
    tq = qi_ref.shape[0]
    nkb = keys.shape[0]
    tk = keys.shape[2]
    nh = qi_ref.shape[1] // IDX_DIM
    wsc = wi_ref[...] * score_scale
    qpos = qb * tq + lax.broadcasted_iota(jnp.int32, (tq, tk), 0)
    kloc = lax.broadcasted_iota(jnp.int32, (tq, tk), 1)
    neg_key = _sort_key(jnp.full((tq, tk), NEG, F32))

    for kb in range(nkb):
        @pl.when(kb <= qb)
        def _():
            kblk = ki_ref[kb * tk:(kb + 1) * tk, :].astype(BF16)
            sc = jnp.zeros((tq, tk), F32)
            for h in range(nh):
                d = _dot_nt(qi_ref[:, h * IDX_DIM:(h + 1) * IDX_DIM].astype(BF16), kblk)
                sc = sc + wsc[:, h:h + 1] * jnp.maximum(d, 0.0)
            adm = kb * tk + kloc <= qpos
            keys[kb] = _sort_key(jnp.where(adm, sc, NEG))

        @pl.when(kb > qb)
        def _():
            keys[kb] = neg_key

    def count_ge(t):
        cnt[...] = jnp.where(keys[0] >= t, 1, 0)
        for kb in range(1, nkb):
            @pl.when(kb <= qb)
            def _():
                cnt[...] += jnp.where(keys[kb] >= t, 1, 0)
        beyond = (nkb - 1 - qb) * tk
        return jnp.sum(cnt[...], axis=1, keepdims=True) + jnp.where(t <= NEG_SORT_KEY, beyond, 0)

    thr = _kth_largest_key(count_ge, (tq, 1), k_sel)
    n_ge = count_ge(thr)
    has_ties = jnp.max(n_ge) > k_sel

    @pl.when(jnp.logical_not(has_ties))
    def _():
        for kb in range(nkb):
            adm = kb * tk + kloc <= qpos
            sel = jnp.logical_and(keys[kb] >= thr, adm)
            o_ref[kb] = jnp.where(sel, 0.0, MASKED).T.astype(o_ref.dtype)

    @pl.when(has_ties)
    def _():
        acc = jnp.zeros((tq, tk), jnp.int32)
        for kb in range(nkb):
            acc = acc + jnp.where(keys[kb] > thr, 1, 0)
        room = (k_sel - jnp.sum(acc, axis=1, keepdims=True)).astype(F32)
        upper = jnp.where(lax.broadcasted_iota(jnp.int32, (tk, tk), 0) <= lax.broadcasted_iota(jnp.int32, (tk, tk), 1),
                          1.0, 0.0).astype(BF16)
        seen = jnp.zeros((tq, 1), F32)
        for kb in range(nkb):
            key = keys[kb]
            eq = key == thr
            eqf = jnp.where(eq, 1.0, 0.0)
            rank = seen + _dot(eqf.astype(BF16), upper)
            seen = seen + jnp.sum(eqf, axis=1, keepdims=True)
            adm = kb * tk + kloc <= qpos
            sel = jnp.logical_and(jnp.logical_or(key > thr, jnp.logical_and(eq, rank <= room)), adm)
            o_ref[kb] = jnp.where(sel, 0.0, MASKED).T.astype(o_ref.dtype)


def _at_index(proj, aw, batch, seq, k_sel):
    tq = tk = AT_TILE
    nq = seq // tq
    width = aw['width']
    nh = aw['idx_heads']
    qio = (4 * width) // (nh * IDX_DIM)
    kio = (4 * width + nh * IDX_DIM) // IDX_DIM
    kern = functools.partial(_at_index_kernel, k_sel=k_sel, score_scale=IDX_DIM ** -0.5 * nh ** -0.5)
    return pl.pallas_call(
        kern, out_shape=jax.ShapeDtypeStruct((batch * nq, seq // tk, tq, tk), BF16), grid=(batch, nq),
        in_specs=[pl.BlockSpec((tq, nh * IDX_DIM), lambda b, q: (b * nq + q, qio)),
                  pl.BlockSpec((tq, LANES), lambda b, q: (b * nq + q, kio + 1)),
                  pl.BlockSpec((seq, IDX_DIM), lambda b, q: (b, kio))],
        out_specs=pl.BlockSpec((None, seq // tk, tq, tk), lambda b, q: (b * nq + q, 0, 0, 0)),
        scratch_shapes=[pltpu.VMEM((seq // tk, tq, tk), jnp.int32), pltpu.VMEM((tq, tk), jnp.int32)],
        compiler_params=_params(("parallel", "parallel")), name="at_index")(proj, proj, proj)


AT_HEAD_GROUP = 4
MASKED = 2.0 * NEG


def _at_attend_kernel(q_ref, k_ref, vt_ref, z_ref, mask_ref, bias_ref, o_ref, acc, m_scr, l_scr):
    qb = pl.program_id(2)
    t = q_ref.shape[0]
    hg = q_ref.shape[1] // AT_DIM
    acc[...] = jnp.zeros_like(acc)
    m_scr[...] = jnp.full(m_scr.shape, NEG, F32)
    l_scr[...] = jnp.zeros_like(l_scr)

    heads = [slice(h * AT_DIM, (h + 1) * AT_DIM) for h in range(hg)]
    q3t = jnp.stack([q_ref[:, sl].T for sl in heads]).astype(BF16)

    def key_tile(kb, carry):
        rows = pl.ds(pl.multiple_of(kb * t, t), t)
        k3 = jnp.stack([k_ref[rows, sl] for sl in heads])
        s_t = _bdot(k3, q3t) + bias_ref[jnp.minimum(qb - kb, 2)] + mask_ref[kb].astype(F32)
        m_old = m_scr[...]
        m_new = jnp.maximum(m_old, jnp.max(s_t, axis=1, keepdims=True))
        alpha = jnp.exp(m_old - m_new)
        p_t = jnp.exp(s_t - m_new)
        l_scr[...] = alpha * l_scr[...] + jnp.sum(p_t, axis=1, keepdims=True)
        acc[...] = alpha * acc[...] + _bdot(vt_ref[:, kb], p_t.astype(BF16))
        m_scr[...] = m_new
        return carry

    lax.fori_loop(0, qb + 1, key_tile, 0)
    o_t = acc[...] / l_scr[...]
    for h, sl in enumerate(heads):
        o_ref[:, sl] = (o_t[h].T * _silu(z_ref[:, sl])).astype(o_ref.dtype)


def _at_attend(proj, proj_bf, maskadd_t, tiles_t, aw, batch, seq):
    t = AT_TILE
    nq = seq // t
    width = aw['width']
    heads = width // AT_DIM
    hg = AT_HEAD_GROUP
    gw = hg * AT_DIM
    ng = width // gw
    v_t = proj_bf[:, 2 * width:3 * width].reshape(batch, nq, t, heads, AT_DIM).transpose(0, 3, 1, 4, 2)
    return pl.pallas_call(
        _at_attend_kernel, out_shape=jax.ShapeDtypeStruct((batch * seq, width), BF16), grid=(batch, ng, nq),
        in_specs=[pl.BlockSpec((t, gw), lambda b, g, q: (b * nq + q, g)),
                  pl.BlockSpec((seq, gw), lambda b, g, q: (b, ng + g)),
                  pl.BlockSpec((None, hg, nq, AT_DIM, t), lambda b, g, q: (b, g, 0, 0, 0)),
                  pl.BlockSpec((t, gw), lambda b, g, q: (b * nq + q, 3 * ng + g)),
                  pl.BlockSpec((None, nq, t, t), lambda b, g, q: (b * nq + q, 0, 0, 0)),
                  pl.BlockSpec((3, hg, t, t), lambda b, g, q: (0, g, 0, 0))],
        out_specs=pl.BlockSpec((t, gw), lambda b, g, q: (b * nq + q, g)),
        scratch_shapes=[pltpu.VMEM((hg, AT_DIM, t), F32), pltpu.VMEM((hg, 1, t), F32), pltpu.VMEM((hg, 1, t), F32)],
        compiler_params=_params(("parallel", "parallel", "arbitrary")), name="at_attend")(
            proj, proj_bf, v_t, proj, maskadd_t, tiles_t)


def _at_weights(w_in, rel_bias, w_out):
    width = w_out.shape[0]
    heads = rel_bias.shape[1]
    idx_heads = (w_in.shape[1] - 4 * width - IDX_DIM) // (IDX_DIM + 1)
    col_scale = jnp.where(jnp.arange(w_in.shape[1]) < width, AT_DIM ** -0.5, 1.0).astype(F32)
    return dict(w_in=_pad_cols((w_in.astype(F32) * col_scale).astype(BF16), 512), rel_bias=rel_bias.astype(F32),
                w_out=w_out.astype(BF16), width=width, heads=heads, idx_heads=idx_heads)


def _at_rows_kernel(k_ref, v_ref, ki_ref, ko_ref, vo_ref, kio_ref):
    nh = ko_ref.shape[1]
    heads = lambda x: jnp.stack([x[:, h * AT_DIM:(h + 1) * AT_DIM] for h in range(nh)], axis=1)
    ko_ref[...] = heads(k_ref[...])
    vo_ref[...] = heads(v_ref[...])
    kio_ref[...] = ki_ref[...]


def _at_rows(proj, aw, lead, tm=256):
    width, heads = aw['width'], aw['heads']
    m = proj.shape[0]
    tm = min(tm, m)
    kio = (4 * width + aw['idx_heads'] * IDX_DIM) // IDX_DIM
    row4 = pl.BlockSpec((tm, heads, AT_DIM), lambda i: (i, 0, 0))
    k, v, ki = pl.pallas_call(
        _at_rows_kernel,
        out_shape=(jax.ShapeDtypeStruct((m, heads, AT_DIM), F32), jax.ShapeDtypeStruct((m, heads, AT_DIM), F32),
                   jax.ShapeDtypeStruct((m, IDX_DIM), F32)),
        grid=(m // tm,),
        in_specs=[pl.BlockSpec((tm, width), lambda i: (i, 1)), pl.BlockSpec((tm, width), lambda i: (i, 2)),
                  pl.BlockSpec((tm, IDX_DIM), lambda i: (i, kio))],
        out_specs=(row4, row4, pl.BlockSpec((tm, IDX_DIM), lambda i: (i, 0))),
        compiler_params=_params(("parallel",)), name="at_rows")(proj, proj, proj)
    return (k.reshape(lead + (heads, AT_DIM)), v.reshape(lead + (heads, AT_DIM)), ki.reshape(lead + (IDX_DIM,)))


def _at_layer_prompt(x, mods, aw, *, batch, seq, tm):
    g, scale, shift, gate = mods
    proj, proj_bf = _in_proj(x, g, scale, shift, aw['w_in'], batch_kind=True, rows_per_batch=seq, tm=tm, tn=MM_TN,
                             name="at_in_proj", bf16_copy=True)
    k_sel = min(TOPK_MAX, seq // 4)
    maskadd = _at_index(proj, aw, batch, seq, k_sel)
    tiles = _relbias_tiles(aw['rel_bias'])
    a = _at_attend(proj, proj_bf, maskadd, jnp.swapaxes(tiles, 2, 3), aw, batch, seq)
    x_new = _out_proj([('row', a)], aw['w_out'], x, gate, batch_kind=True, rows_per_batch=seq, tm=tm, tn=MM_TN,
                      name="at_out_proj")
    return (x_new,) + _at_rows(proj, aw, (batch, seq))


AT_PAGES_PER_STEP = 16


def _at_page_scores_kernel(pt_ref, qi_ref, w_ref, kidx_ref, o_ref, kbuf, sem):
    b, j = pl.program_id(0), pl.program_id(1)
    nj = pl.num_programs(1)
    npg = kbuf.shape[1]
    step = b * nj + j
    last_step = pl.num_programs(0) * nj - 1

    def page_copy(s, i, slot):
        sb, sj = s // nj, s % nj
        return pltpu.make_async_copy(kidx_ref.at[pt_ref[sb, sj * npg + i]], kbuf.at[slot, i], sem.at[slot])

    def start_all(s, slot):
        for i in range(npg):
            page_copy(s, i, slot).start()

    slot = step % 2

    @pl.when(step == 0)
    def _():
        start_all(step, slot)

    @pl.when(step < last_step)
    def _():
        start_all(step + 1, 1 - slot)

    for i in range(npg):
        page_copy(step, i, slot).wait()
    qi = qi_ref[...].astype(BF16)
    w = w_ref[...]
    for i in range(npg):
        d = _dot_nt(qi, kbuf[slot, i].astype(BF16))
        o_ref[i:i + 1, :] = jnp.sum(w * jnp.maximum(d, 0.0), axis=0, keepdims=True)


def _at_page_scores(qi3, w3, cache_kidx, page_table):
    rows, nh, _ = qi3.shape
    npages = page_table.shape[1]
    page = cache_kidx.shape[1]
    npg = math.gcd(npages, AT_PAGES_PER_STEP)
    grid_spec = pltpu.PrefetchScalarGridSpec(
        num_scalar_prefetch=1, grid=(rows, npages // npg),
        in_specs=[pl.BlockSpec((None, nh, IDX_DIM), lambda b, j, pt: (b, 0, 0)),
                  pl.BlockSpec((None, nh, IDX_DIM), lambda b, j, pt: (b, 0, 0)),
                  pl.BlockSpec(memory_space=pl.ANY)],
        out_specs=pl.BlockSpec((None, npg, page), lambda b, j, pt: (b, j, 0)),
        scratch_shapes=[pltpu.VMEM((2, npg, page, IDX_DIM), F32), pltpu.SemaphoreType.DMA((2,))])
    return pl.pallas_call(
        _at_page_scores_kernel, out_shape=jax.ShapeDtypeStruct((rows, npages, page), F32), grid_spec=grid_spec,
        compiler_params=_params(("arbitrary", "arbitrary")), name="at_page_scores")(page_table, qi3, w3, cache_kidx)


def _at_sample_select_kernel(sc_ref, qi_ref, w_ref, kin_ref, gidx_ref, newadd_ref, rank_scr, *, k_sel):
    npages, page = sc_ref.shape
    upper = jnp.where(lax.broadcasted_iota(jnp.int32, (page, page), 0) <= lax.broadcasted_iota(jnp.int32, (page, page), 1),
                      1.0, 0.0).astype(BF16)
    lower = jnp.where(lax.broadcasted_iota(jnp.int32, (npages, npages), 1) < lax.broadcasted_iota(jnp.int32, (npages, npages), 0),
                      1.0, 0.0).astype(BF16)

    def total(x):
        return jnp.sum(jnp.sum(x, axis=1, keepdims=True), axis=0, keepdims=True)

    def position_rank(flags):
        row_cnt = jnp.broadcast_to(jnp.sum(flags, axis=1, keepdims=True), (npages, page))
        return _dot(lower, row_cnt.astype(BF16)) + _dot(flags.astype(BF16), upper)

    keys = _sort_key(sc_ref[...])
    dots = jnp.sum(qi_ref[...] * kin_ref[...], axis=1, keepdims=True)
    s_new = jnp.sum(w_ref[:, 0:1] * jnp.maximum(dots, 0.0), axis=0, keepdims=True)
    key_new = _sort_key(s_new)

    def count_ge(t):
        return total(jnp.where(keys >= t, 1, 0)) + jnp.where(key_new >= t, 1, 0)

    thr = _kth_largest_key(count_ge, (1, 1), k_sel)
    n_gt = total(jnp.where(keys > thr, 1.0, 0.0)) + jnp.where(key_new > thr, 1.0, 0.0)
    room = k_sel - n_gt
    eq = keys == thr
    eqf = jnp.where(eq, 1.0, 0.0)
    sel = jnp.logical_or(keys > thr, jnp.logical_and(eq, position_rank(eqf) <= room))
    sel_new = jnp.logical_or(key_new > thr, jnp.logical_and(key_new == thr, total(eqf) + 1.0 <= room))
    newadd_ref[...] = jnp.broadcast_to(jnp.where(sel_new, 0.0, NEG), (1, page))

    self_f = jnp.where(sel, 1.0, 0.0)
    rank_scr[...] = jnp.where(sel, position_rank(self_f) - 1.0, -1.0)
    jidx = lax.broadcasted_iota(jnp.int32, (k_sel, page), 0).astype(F32)
    lane = lax.broadcasted_iota(jnp.int32, (page, LANES), 1)
    pick = jnp.where(lane == 0, lax.broadcasted_iota(jnp.int32, (page, LANES), 0).astype(F32),
                     jnp.where(lane <= 2, 1.0, 0.0)).astype(BF16)
    out_lane = lax.broadcasted_iota(jnp.int32, (k_sel, LANES), 1)

    def add_pages(g, acc):
        ranks = rank_scr[pl.ds(pl.multiple_of(g * SUBLANES, SUBLANES), SUBLANES), :]
        for i in range(SUBLANES):
            onehot = jnp.where(ranks[i:i + 1, :] == jidx, 1.0, 0.0).astype(BF16)
            slot = lax.convert_element_type(g * SUBLANES + i, F32)
            acc = acc + _dot(onehot, pick) * jnp.where(out_lane == 1, slot, 1.0)
        return acc

    assert npages % SUBLANES == 0
    gidx_ref[...] = lax.fori_loop(0, npages // SUBLANES, add_pages, jnp.zeros((k_sel, LANES), F32))


def _at_sample_select(scores, qi3, w3, ki_new, k_sel):
    rows, npages, page = scores.shape
    nh = qi3.shape[1]
    assert page == LANES and page >= REL_MAX_DIST
    kern = functools.partial(_at_sample_select_kernel, k_sel=k_sel)
    return pl.pallas_call(
        kern, out_shape=(jax.ShapeDtypeStruct((rows, k_sel, LANES), F32), jax.ShapeDtypeStruct((rows, 1, page), F32)),
        grid=(rows,),
        in_specs=[pl.BlockSpec((None, npages, page), lambda b: (b, 0, 0)),
                  pl.BlockSpec((None, nh, IDX_DIM), lambda b: (b, 0, 0)),
                  pl.BlockSpec((None, nh, IDX_DIM), lambda b: (b, 0, 0)),
                  pl.BlockSpec((None, 1, IDX_DIM), lambda b: (b, 0, 0))],
        out_specs=(pl.BlockSpec((None, k_sel, LANES), lambda b: (b, 0, 0)),
                   pl.BlockSpec((None, 1, page), lambda b: (b, 0, 0))),
        scratch_shapes=[pltpu.VMEM((npages, page), F32)],
        compiler_params=_params(("parallel",)), name="at_sample_select")(
            scores, qi3, w3, ki_new.reshape(rows, 1, IDX_DIM))


def _at_gather_attend_kernel(pt_ref, slot_ref, off_ref, q_ref, ck_ref, cv_ref, pos_ref, newadd_ref, rbt_ref,
                             kn_ref, vn_ref, z_ref, o_ref, kg, vg, sem, *, past):
    b = pl.program_id(0)
    nsel, nh, d = kg.shape
    ncol = nsel * nh

    def row_copies(j):
        page = pt_ref[b, slot_ref[b, j]]
        off = off_ref[b, j]
        return (pltpu.make_async_copy(ck_ref.at[page, off], kg.at[j], sem.at[0]),
                pltpu.make_async_copy(cv_ref.at[page, off], vg.at[j], sem.at[1]))

    def start(j, carry):
        for c in row_copies(j):
            c.start()
        return carry

    def wait(j, carry):
        for c in row_copies(j):
            c.wait()
        return carry

    lax.fori_loop(0, nsel, start, 0)
    qs = q_ref[...].astype(BF16)
    pos = pos_ref[...]
    own = (lax.broadcasted_iota(jnp.int32, (nh, ncol), 1) & (nh - 1)) == lax.broadcasted_iota(jnp.int32, (nh, ncol), 0)
    keep = jnp.logical_and(own, pos >= 0)
    bias = _bias_from_dist(jnp.maximum(past - pos, 0), lambda bk: rbt_ref[:, bk:bk + 1])
    nadd = newadd_ref[:, 0:1]
    s_n = jnp.sum(qs.astype(F32) * kn_ref[...], axis=1, keepdims=True) + rbt_ref[:, 0:1] + nadd
    lax.fori_loop(0, nsel, wait, 0)
    s = jnp.where(keep, _dot_nt(qs, kg[...].reshape(ncol, d).astype(BF16)) + bias, NEG)
    m = jnp.maximum(jnp.max(s, axis=1, keepdims=True), s_n)
    pr = jnp.where(keep, jnp.exp(s - m), 0.0)
    p_n = jnp.where(nadd < 0.0, 0.0, jnp.exp(s_n - m))
    l = jnp.sum(pr, axis=1, keepdims=True) + p_n
    o = _dot(pr.astype(BF16), vg[...].reshape(ncol, d).astype(BF16)) + p_n * vn_ref[...]
    o_ref[...] = o / l * _silu(z_ref[...])


def _at_gather_attend(q3, cache_k, cache_v, page_table, slot, off, pos_cols, newadd, rbt, kn3, vn3, z3):
    rows, nh, d = q3.shape
    assert nh & (nh - 1) == 0
    nsel = slot.shape[1]
    ncol = nsel * nh
    past = page_table.shape[1] * cache_k.shape[1]
    row3 = pl.BlockSpec((None, nh, d), lambda b, *_: (b, 0, 0))
    grid_spec = pltpu.PrefetchScalarGridSpec(
        num_scalar_prefetch=3, grid=(rows,),
        in_specs=[row3, pl.BlockSpec(memory_space=pl.ANY), pl.BlockSpec(memory_space=pl.ANY),
                  pl.BlockSpec((None, 1, ncol), lambda b, *_: (b, 0, 0)),
                  pl.BlockSpec((None, 1, newadd.shape[-1]), lambda b, *_: (b, 0, 0)),
                  pl.BlockSpec(rbt.shape, lambda b, *_: (0, 0)),
                  row3, row3, row3],
        out_specs=row3,
        scratch_shapes=[pltpu.VMEM((nsel, nh, d), F32), pltpu.VMEM((nsel, nh, d), F32), pltpu.SemaphoreType.DMA((2,))])
    kern = functools.partial(_at_gather_attend_kernel, past=past)
    return pl.pallas_call(
        kern, out_shape=jax.ShapeDtypeStruct((rows, nh, d), F32), grid_spec=grid_spec,
        compiler_params=_params(("arbitrary",)), name="at_gather_attend")(
            page_table, slot, off, q3, cache_k, cache_v, pos_cols, newadd, rbt, kn3, vn3, z3)


def _at_layer_sample(x, mods, cache_k, cache_v, cache_kidx, page_table, aw):
    g, scale, shift, gate = mods
    rows = x.shape[0]
    width, heads, nh = aw['width'], aw['heads'], aw['idx_heads']
    proj = _in_proj(x, g, scale, shift, aw['w_in'], batch_kind=False, rows_per_batch=1, tm=SUBLANES, tn=MM_TN,
                    name="at_in_proj")
    npool, page = cache_k.shape[:2]
    npages = page_table.shape[1]
    past = npages * page
    k_sel = min(TOPK_MAX, (past + 1) // 4)
    o = 4 * width
    qi3 = proj[:, o:o + nh * IDX_DIM].reshape(rows, nh, IDX_DIM)
    ki_new = proj[:, o + nh * IDX_DIM:o + nh * IDX_DIM + IDX_DIM]
    wi = proj[:, o + nh * IDX_DIM + IDX_DIM:o + nh * IDX_DIM + IDX_DIM + nh] * (IDX_DIM ** -0.5 * nh ** -0.5)
    w3 = jnp.broadcast_to(wi[:, :, None], (rows, nh, IDX_DIM))
    scores = _at_page_scores(qi3, w3, cache_kidx.astype(F32), page_table)
    rbt = aw['rel_bias'].T
    gidx, newadd = _at_sample_select(scores, qi3, w3, ki_new, k_sel)
    off = gidx[:, :, 0].astype(jnp.int32)
    slot = gidx[:, :, 1].astype(jnp.int32)
    pos = jnp.where(gidx[:, :, 2] > 0.5, slot * page + off, -1)
    pos_cols = jnp.repeat(pos, heads, axis=-1).reshape(rows, 1, k_sel * heads)
    r3 = lambda t: t.reshape(rows, heads, AT_DIM)
    a = _at_gather_attend(r3(proj[:, :width]), cache_k.astype(F32), cache_v.astype(F32), page_table, slot, off,
                          pos_cols, newadd, rbt, r3(proj[:, width:2 * width]), r3(proj[:, 2 * width:3 * width]),
                          r3(proj[:, 3 * width:4 * width]))
    x_new = _out_proj([('row', a.reshape(rows, width))], aw['w_out'], x, gate, batch_kind=False, rows_per_batch=1,
                      tm=SUBLANES, tn=MM_TN, name="at_out_proj", prologue=_cast_prologue)
    return (x_new,) + _at_rows(proj, aw, (rows, 1))


def kernel(x_prompt, x_sample, state_s5_re, state_s5_im, state_m2_conv, state_m2_ssm, state_gd_conv, state_gd_ssm, cache_k, cache_v, cache_kidx, page_table, c_prompt, c_sample, norm_g, w_mod, b_mod, final_g, s5_w_in, s5_lam_re, s5_lam_im, s5_log_dt, s5_b_re, s5_b_im, s5_c_re, s5_c_im, s5_d, s5_w_glu, s5_b_glu, s5_w_out, m2_w_in, m2_conv_w, m2_conv_b, m2_dt_bias, m2_a_log, m2_d, m2_norm, m2_w_out, gd_w_in, gd_conv_w, gd_a_log, gd_dt_bias, gd_norm, gd_w_out, at_w_in, rel_bias, at_w_out):
    f = F32
    bp, seq, d = x_prompt.shape
    bs = x_sample.shape[0]
    depth = norm_g.shape[0]
    xp = x_prompt.astype(f).reshape(bp * seq, d)
    xs = x_sample.astype(f).reshape(bs, d)

    pad_rows = (-(bs + bp)) % SUBLANES
    c_all = jnp.concatenate([c_sample.astype(f), c_prompt.astype(f), jnp.zeros((pad_rows, d), f)], axis=0)
    mod = _modulation(c_all, w_mod, b_mod)

    def mods(i, prompt):
        g = norm_g[i].astype(f).reshape(1, d)
        rows = mod[i, bs:bs + bp] if prompt else mod[i, :bs]
        shift, scale, gate = rows[:, :d], rows[:, d:2 * d], rows[:, 2 * d:]
        if prompt:
            return g, scale[:, None, :], shift[:, None, :], gate[:, None, :]
        return g, scale, shift, gate

    s5w = _s5_weights(s5_w_in, s5_lam_re, s5_lam_im, s5_log_dt, s5_b_re, s5_b_im, s5_c_re, s5_c_im, s5_d,
                      s5_w_glu, s5_b_glu, s5_w_out)
    tm_p = MM_TM

    xp, s5_re_p, s5_im_p = _s5_layer(xp, mods(0, True), None, s5w, prompt=True, batch=bp, seq=seq, tm=tm_p)
    xs, s5_re_s, s5_im_s = _s5_layer(xs, mods(0, False), (state_s5_re, state_s5_im), s5w, prompt=False,
                                     batch=bs, seq=1, tm=SUBLANES)
    groups, nstate = state_s5_re.shape[1:]
    s5_re_p, s5_im_p = s5_re_p.reshape(bp, groups, nstate), s5_im_p.reshape(bp, groups, nstate)
    s5_re_s, s5_im_s = s5_re_s.reshape(bs, groups, nstate), s5_im_s.reshape(bs, groups, nstate)

    m2w = _m2_weights(m2_w_in, m2_conv_w, m2_conv_b, m2_dt_bias, m2_a_log, m2_d, m2_norm, m2_w_out)
    xp, m2_conv_p, m2_ssm_p = _m2_layer(xp, mods(1, True), None, m2w, prompt=True, batch=bp, seq=seq, tm=tm_p)
    xs, m2_conv_s, m2_ssm_s = _m2_layer(xs, mods(1, False), (state_m2_conv, state_m2_ssm), m2w, prompt=False,
                                        batch=bs, seq=1, tm=SUBLANES)
    m2_ssm_p = m2_ssm_p.reshape((bp,) + state_m2_ssm.shape[1:])
    m2_ssm_s = m2_ssm_s.reshape(state_m2_ssm.shape)

    gdw = _gd_weights(gd_w_in, gd_conv_w, gd_a_log, gd_dt_bias, gd_norm, gd_w_out)
    xp, gd_conv_p, gd_ssm_p = _gd_layer(xp, mods(2, True), None, gdw, prompt=True, batch=bp, seq=seq, tm=tm_p)
    xs, gd_conv_s, gd_ssm_s = _gd_layer(xs, mods(2, False), (state_gd_conv, state_gd_ssm), gdw, prompt=False,
                                        batch=bs, seq=1, tm=SUBLANES)

    atw = _at_weights(at_w_in, rel_bias, at_w_out)
    xp, k_rows_p, v_rows_p, kidx_rows_p = _at_layer_prompt(xp, mods(3, True), atw, batch=bp, seq=seq, tm=tm_p)
    xs, k_rows_s, v_rows_s, kidx_rows_s = _at_layer_sample(xs, mods(3, False), cache_k, cache_v, cache_kidx,
                                                           page_table, atw)

    y_prompt = _final_norm(xp, final_g).reshape(x_prompt.shape).astype(x_prompt.dtype)
    y_sample = _final_norm(xs, final_g).reshape(x_sample.shape).astype(x_sample.dtype)
    return (y_prompt, y_sample, s5_re_p, s5_im_p, s5_re_s, s5_im_s, m2_conv_p, m2_ssm_p, m2_conv_s, m2_ssm_s,
            gd_conv_p, gd_ssm_p, gd_conv_s, gd_ssm_s,
            k_rows_p, v_rows_p, kidx_rows_p, k_rows_s, v_rows_s, kidx_rows_s)
```

```python
import functools
import math

import numpy as np
import jax
import jax.numpy as jnp
from jax import lax
from jax.experimental import pallas as pl
from jax.experimental.pallas import tpu as pltpu

F32 = jnp.float32
BF16 = jnp.bfloat16

EPS = 1e-6
NEG = -1e30
CONV_W = 4
V7X_VMEM_LIMIT_BYTES = 56 * 1024 * 1024
LANES = 128
SUBLANES = 8
MM_TM = 1024
MM_TM_WIDE_ROWS = 512
MM_TN = 1024

S5_GROUP = 16
S5_STATE = 64
S5_CHUNK = 512
S5_SEG = S5_CHUNK // SUBLANES
S5_BLK_CH = 256
S5_BLK_ST = 1024


def _params(sem):
    return pltpu.CompilerParams(dimension_semantics=sem, vmem_limit_bytes=V7X_VMEM_LIMIT_BYTES)


def _sigmoid(x):
    return 1.0 / (1.0 + jnp.exp(-x))


def _silu(x):
    return x * _sigmoid(x)


def _gelu(x):
    return 0.5 * x * (1.0 + jnp.tanh(math.sqrt(2.0 / math.pi) * (x + 0.044715 * (x * x * x))))


def _softplus(x):
    return jnp.maximum(x, 0.0) + jnp.log1p(jnp.exp(-jnp.abs(x)))


def _dot(a, b):
    return jnp.dot(a, b, preferred_element_type=F32)


def _dot_nt(a, b):
    return lax.dot_general(a, b, (((1,), (1,)), ((), ())), preferred_element_type=F32)


def _split3(x):
    hi = x.astype(BF16)
    r1 = x - hi.astype(F32)
    mid = r1.astype(BF16)
    lo = (r1 - mid.astype(F32)).astype(BF16)
    return hi, mid, lo


def _dot_exact_lhs(sel, x):
    hi, mid, lo = _split3(x)
    return _dot(sel, hi) + (_dot(sel, mid) + _dot(sel, lo))


def _dot_f32(a, b):
    ah, am, al = _split3(a)
    bh, bm, bl = _split3(b)
    small = _dot(am, bm) + _dot(ah, bl) + _dot(al, bh)
    return _dot(ah, bh) + (_dot(ah, bm) + _dot(am, bh) + small)


def _mm_kernel(*refs, n_a, n_e, prologue, epilogue, bf16_copy):
    a_refs = refs[:n_a]
    w_ref = refs[n_a]
    e_refs = refs[n_a + 1:n_a + 1 + n_e]
    o_ref = refs[n_a + 1 + n_e]
    n_out = 2 if bf16_copy else 1
    if prologue is None:
        a = a_refs[0][...]
    else:
        a_scr = refs[n_a + 1 + n_e + n_out]

        @pl.when(pl.program_id(1) == 0)
        def _():
            a_scr[...] = prologue(*[r[...] for r in a_refs]).astype(BF16)

        a = a_scr[...]
    acc = _dot(a, w_ref[...])
    out = epilogue(acc, *[r[...] for r in e_refs])
    o_ref[...] = out.astype(o_ref.dtype)
    if bf16_copy:
        refs[n_a + 2 + n_e][...] = out.astype(BF16)


def _fused_matmul(a_ins, w, e_ins, *, prologue, epilogue, out_dtype, tm, tn, rows_per_batch=None, name,
                  bf16_copy=False):
    m = next(item[1].shape[0] for item in a_ins if item[0] == 'row')
    k, n = w.shape
    tm = min(tm, m)
    tn = next(t for t in (2048, 1024, 768, 512, 384, 256, 128) if t <= tn and n % t == 0)
    assert m % tm == 0
    rpb = rows_per_batch

    def bidx(i):
        return (i * tm) // rpb

    in_specs, args = [], []
    for item in a_ins:
        kind, arr = item[0], item[1]
        wd = item[2] if len(item) > 2 else arr.shape[-1]
        coff = item[3] if len(item) > 3 else 0
        if kind == 'row':
            in_specs.append(pl.BlockSpec((tm, wd), lambda i, j, coff=coff: (i, coff)))
        elif kind == 'vec':
            in_specs.append(pl.BlockSpec((1, wd), lambda i, j: (0, 0)))
        else:
            in_specs.append(pl.BlockSpec((None, 1, wd), lambda i, j: (bidx(i), 0, 0)))
        args.append(arr)
    in_specs.append(pl.BlockSpec((k, tn), lambda i, j: (0, j)))
    args.append(w)
    for item in e_ins:
        kind, arr = item[0], item[1]
        off = (item[2] if len(item) > 2 else 0) // tn
        if kind == 'tile':
            assert len(item) < 3 or item[2] % tn == 0
            in_specs.append(pl.BlockSpec((tm, tn), lambda i, j, off=off: (i, j + off)))
        elif kind == 'col':
            in_specs.append(pl.BlockSpec((1, tn), lambda i, j: (0, j)))
        else:
            in_specs.append(pl.BlockSpec((None, 1, tn), lambda i, j: (bidx(i), 0, j)))
        args.append(arr)
    scratch = [] if prologue is None else [pltpu.VMEM((tm, k), BF16)]
    kern = functools.partial(_mm_kernel, n_a=len(a_ins), n_e=len(e_ins), prologue=prologue, epilogue=epilogue,
                             bf16_copy=bf16_copy)
    out_shape = jax.ShapeDtypeStruct((m, n), out_dtype)
    out_spec = pl.BlockSpec((tm, tn), lambda i, j: (i, j))
    if bf16_copy:
        out_shape, out_spec = (out_shape, jax.ShapeDtypeStruct((m, n), BF16)), (out_spec, out_spec)
    return pl.pallas_call(
        kern, out_shape=out_shape, grid=(m // tm, n // tn), in_specs=in_specs, out_specs=out_spec,
        scratch_shapes=scratch, compiler_params=_params(("parallel", "arbitrary")), name=name)(*args)


def _weight_cast_kernel(*refs, n_valid, scaled):
    w_ref, o_ref = refs[0], refs[-1]
    tn = o_ref.shape[1]
    col = pl.program_id(0) * tn + lax.broadcasted_iota(jnp.int32, (1, tn), 1)
    w = w_ref[...].astype(F32)
    if scaled:
        w = w * refs[1][...]
    o_ref[...] = jnp.where(col < n_valid, w, 0.0).astype(o_ref.dtype)


def _weight_bf16(w, n_cols=None, pad_to=512, col_scale=None, tn=512):
    k, n_in = w.shape
    n_cols = n_in if n_cols is None else n_cols
    n_out = -(-n_cols // pad_to) * pad_to
    tn = math.gcd(n_out, tn)
    args, in_specs = [w], [pl.BlockSpec((k, tn), lambda j: (0, j))]
    if col_scale is not None:
        args.append(jnp.pad(col_scale.astype(F32), (0, n_out - col_scale.shape[0])).reshape(1, n_out))
        in_specs.append(pl.BlockSpec((1, tn), lambda j: (0, j)))
    kern = functools.partial(_weight_cast_kernel, n_valid=n_cols, scaled=col_scale is not None)
    return pl.pallas_call(
        kern, out_shape=jax.ShapeDtypeStruct((k, n_out), BF16), grid=(n_out // tn,), in_specs=in_specs,
        out_specs=pl.BlockSpec((k, tn), lambda j: (0, j)),
        compiler_params=_params(("parallel",)), name="weight_cast")(*args)


def _pad_cols(w, mult):
    n = w.shape[-1]
    npad = (-n) % mult
    if npad:
        w = jnp.pad(w, ((0, 0), (0, npad)))
    return w


def _modnorm_prologue(x, g, scale, shift):
    r = x * lax.rsqrt(jnp.mean(x * x, axis=-1, keepdims=True) + EPS) * g
    return r * (1.0 + scale) + shift


def _identity_epilogue(acc):
    return acc


def _residual_epilogue(acc, x, gate):
    return x + gate * acc


def _residual_norm_epilogue(acc, x, gate, g):
    xn = x + gate * acc
    return xn * lax.rsqrt(jnp.mean(xn * xn, axis=-1, keepdims=True) + EPS) * g


def _in_proj(x, g, scale, shift, w, *, batch_kind, rows_per_batch, tm, tn, name, bf16_copy=False):
    kind = 'batch' if batch_kind else 'row'
    return _fused_matmul([('row', x), ('vec', g), (kind, scale), (kind, shift)], w, [],
                         prologue=_modnorm_prologue, epilogue=_identity_epilogue, out_dtype=F32,
                         tm=tm, tn=tn, rows_per_batch=rows_per_batch, name=name, bf16_copy=bf16_copy)


def _gate_proj_kernel(x_ref, g_ref, scale_ref, shift_ref, w_ref, o_ref):
    h = _modnorm_prologue(x_ref[...], g_ref[...], scale_ref[...], shift_ref[...])
    o_ref[...] = _dot_f32(h, w_ref[...])


def _gate_proj(x, g, scale, shift, w, *, batch_kind, rows_per_batch, tm, name):
    m, d = x.shape
    n = w.shape[1]
    tm = min(tm, m, MM_TM_WIDE_ROWS)
    if batch_kind:
        mod_spec = pl.BlockSpec((None, 1, d), lambda i: ((i * tm) // rows_per_batch, 0, 0))
    else:
        mod_spec = pl.BlockSpec((tm, d), lambda i: (i, 0))
    return pl.pallas_call(
        _gate_proj_kernel, out_shape=jax.ShapeDtypeStruct((m, n), F32), grid=(m // tm,),
        in_specs=[pl.BlockSpec((tm, d), lambda i: (i, 0)), pl.BlockSpec((1, d), lambda i: (0, 0)),
                  mod_spec, mod_spec, pl.BlockSpec((d, n), lambda i: (0, 0))],
        out_specs=pl.BlockSpec((tm, n), lambda i: (i, 0)),
        compiler_params=_params(("parallel",)), name=name)(x, g, scale, shift, w)


def _out_proj(a_ins, w, x, gate, *, batch_kind, rows_per_batch, tm, tn, name, prologue=None):
    kind = 'batchcol' if batch_kind else 'tile'
    return _fused_matmul(a_ins, w, [('tile', x), (kind, gate)], prologue=prologue, epilogue=_residual_epilogue,
                         out_dtype=F32, tm=tm, tn=tn, rows_per_batch=rows_per_batch, name=name)


def _mod_kernel(c_ref, w_ref, b_ref, o_ref):
    o_ref[...] = _dot(c_ref[...].astype(BF16), w_ref[...].astype(BF16)) + b_ref[...]


def _modulation(c_all, w_mod, b_mod, tn=512):
    depth, d, n = w_mod.shape
    rows = c_all.shape[0]
    return pl.pallas_call(
        _mod_kernel, out_shape=jax.ShapeDtypeStruct((depth, rows, n), F32), grid=(depth, n // tn),
        in_specs=[pl.BlockSpec((rows, d), lambda l, j: (0, 0)),
                  pl.BlockSpec((None, d, tn), lambda l, j: (l, 0, j)),
                  pl.BlockSpec((None, 1, tn), lambda l, j: (l, 0, j))],
        out_specs=pl.BlockSpec((None, rows, tn), lambda l, j: (l, 0, j)),
        compiler_params=_params(("parallel", "parallel")), name="adaln_modulation")(
            c_all, w_mod, b_mod.reshape(depth, 1, n))


def _rmsnorm_kernel(x_ref, g_ref, o_ref):
    x = x_ref[...]
    o_ref[...] = x * lax.rsqrt(jnp.mean(x * x, axis=-1, keepdims=True) + EPS) * g_ref[...]


def _final_norm(x, g, tm=512):
    m, d = x.shape
    tm = min(tm, m)
    return pl.pallas_call(
        _rmsnorm_kernel, out_shape=jax.ShapeDtypeStruct((m, d), F32), grid=(m // tm,),
        in_specs=[pl.BlockSpec((tm, d), lambda i: (i, 0)), pl.BlockSpec((1, d), lambda i: (0, 0))],
        out_specs=pl.BlockSpec((tm, d), lambda i: (i, 0)),
        compiler_params=_params(("parallel",)), name="final_rmsnorm")(x, g.reshape(1, d))


def _s5_tables(lam_re, lam_im, log_dt, b_re, b_im, c_re, c_im, d_skip):
    f = F32
    groups, p = lam_re.shape
    nblk = groups * S5_GROUP // S5_BLK_CH
    gpb = groups // nblk
    lr, li = lam_re.astype(f), lam_im.astype(f)
    dt = jnp.exp(log_dt.astype(f))[:, None]
    ldr, ldi = lr * dt, li * dt
    kk = jnp.arange(1, S5_SEG + 1, dtype=f)[:, None, None]
    pmag = jnp.exp(kk * ldr)
    pw_re, pw_im = pmag * jnp.cos(kk * ldi), pmag * jnp.sin(kk * ldi)
    ab_re, ab_im = jnp.exp(ldr) * jnp.cos(ldi), jnp.exp(ldr) * jnp.sin(ldi)
    den = lr * lr + li * li
    nr, ni = ab_re - 1.0, ab_im
    fr, fi = (nr * lr + ni * li) / den, (ni * lr - nr * li) / den
    bre, bim = b_re.astype(f), b_im.astype(f)
    bb_re = fr[..., None] * bre - fi[..., None] * bim
    bb_im = fr[..., None] * bim + fi[..., None] * bre
    eye = jnp.eye(gpb, dtype=f)

    def bd_in(bb):
        t = bb.reshape(nblk, gpb, p, S5_GROUP).transpose(0, 1, 3, 2)
        return jnp.einsum('bgkp,gh->bgkhp', t, eye).reshape(nblk, gpb * S5_GROUP, gpb * p).astype(BF16)

    def bd_out(c):
        t = c.astype(f).reshape(nblk, gpb, S5_GROUP, p).transpose(0, 1, 3, 2)
        return jnp.einsum('bgpk,gh->bgphk', t, eye).reshape(nblk, gpb * p, gpb * S5_GROUP).astype(BF16)

    def lanes(t):
        lead = t.shape[:-2]
        t = t.reshape(lead + (nblk, gpb * p))
        return jnp.moveaxis(t, -2, 0)

    return dict(
        bb_re=bd_in(bb_re), bb_im=bd_in(bb_im), c_re=bd_out(c_re), c_im=bd_out(c_im),
        ab_re=lanes(ab_re[None]), ab_im=lanes(ab_im[None]),
        pw_re=lanes(pw_re), pw_im=lanes(pw_im),
        d=d_skip.astype(f).reshape(1, -1), nblk=nblk)


def _s5_perm():
    pm = np.zeros((S5_CHUNK, S5_CHUNK), np.float32)
    r = np.arange(S5_CHUNK)
    pm[r, (r % SUBLANES) * S5_SEG + r // SUBLANES] = 1.0
    return jnp.asarray(pm, BF16), jnp.asarray(pm.T, BF16)


def _s5_scan_kernel(u_ref, pm_ref, pmt_ref, bbre_ref, bbim_ref, cre_ref, cim_ref, abre_ref, abim_ref,
                    pwre_ref, pwim_ref, d_ref, y_ref, sre_out, sim_out,
                    xre, xim, car_re, car_im, cin_re, cin_im, lend_re, lend_im):
    n = pl.program_id(2)
    nst = xre.shape[1]

    @pl.when(n == 0)
    def _():
        car_re[...] = jnp.zeros_like(car_re)
        car_im[...] = jnp.zeros_like(car_im)

    u = u_ref[...]
    up = _dot(pm_ref[...], u.astype(BF16)).astype(BF16)
    xre[...] = _dot(up, bbre_ref[...])
    xim[...] = _dot(up, bbim_ref[...])
    are = jnp.broadcast_to(abre_ref[...], (SUBLANES, nst))
    aim = jnp.broadcast_to(abim_ref[...], (SUBLANES, nst))
    sre = jnp.zeros((SUBLANES, nst), F32)
    sim = jnp.zeros((SUBLANES, nst), F32)
    for i in range(S5_SEG):
        r = slice(SUBLANES * i, SUBLANES * (i + 1))
        nre = are * sre - aim * sim + xre[r, :]
        nim = are * sim + aim * sre + xim[r, :]
        xre[r, :] = nre
        xim[r, :] = nim
        sre, sim = nre, nim
    lend_re[...] = sre
    lend_im[...] = sim
    a_re = pwre_ref[S5_SEG - 1:S5_SEG, :]
    a_im = pwim_ref[S5_SEG - 1:S5_SEG, :]
    cr, ci = car_re[...], car_im[...]
    for s in range(SUBLANES):
        cin_re[s:s + 1, :] = cr
        cin_im[s:s + 1, :] = ci
        lr, li = lend_re[s:s + 1, :], lend_im[s:s + 1, :]
        cr, ci = a_re * cr - a_im * ci + lr, a_re * ci + a_im * cr + li
    car_re[...] = cr
    car_im[...] = ci
    cinr, cini = cin_re[...], cin_im[...]
    for i in range(S5_SEG):
        r = slice(SUBLANES * i, SUBLANES * (i + 1))
        pr, pi_ = pwre_ref[i:i + 1, :], pwim_ref[i:i + 1, :]
        xre[r, :] = xre[r, :] + (pr * cinr - pi_ * cini)
        xim[r, :] = xim[r, :] + (pr * cini + pi_ * cinr)
    yp = _dot(xre[...].astype(BF16), cre_ref[...]) - _dot(xim[...].astype(BF16), cim_ref[...])
    hi = yp.astype(BF16)
    lo = (yp - hi.astype(F32)).astype(BF16)
    y = _dot(pmt_ref[...], hi) + _dot(pmt_ref[...], lo) + d_ref[...] * u
    y_ref[...] = _gelu(y)

    @pl.when(n == pl.num_programs(2) - 1)
    def _():
        sre_out[...] = cr
        sim_out[...] = ci


def _s5_scan(proj, tabs, batch, seq):
    nblk = tabs['nblk']
    nch = seq // S5_CHUNK
    pm, pmt = _s5_perm()
    nstate = nblk * S5_BLK_ST
    const3 = lambda shape: pl.BlockSpec((None,) + shape, lambda k, b, n: (k, 0, 0))
    y, sre, sim = pl.pallas_call(
        _s5_scan_kernel,
        out_shape=(jax.ShapeDtypeStruct((batch * seq, nblk * S5_BLK_CH), F32),
                   jax.ShapeDtypeStruct((batch, 1, nstate), F32),
                   jax.ShapeDtypeStruct((batch, 1, nstate), F32)),
        grid=(nblk, batch, nch),
        in_specs=[pl.BlockSpec((S5_CHUNK, S5_BLK_CH), lambda k, b, n: (b * nch + n, k)),
                  pl.BlockSpec((S5_CHUNK, S5_CHUNK), lambda k, b, n: (0, 0)),
                  pl.BlockSpec((S5_CHUNK, S5_CHUNK), lambda k, b, n: (0, 0)),
                  const3((S5_BLK_CH, S5_BLK_ST)), const3((S5_BLK_CH, S5_BLK_ST)),
                  const3((S5_BLK_ST, S5_BLK_CH)), const3((S5_BLK_ST, S5_BLK_CH)),
                  const3((1, S5_BLK_ST)), const3((1, S5_BLK_ST)),
                  const3((S5_SEG, S5_BLK_ST)), const3((S5_SEG, S5_BLK_ST)),
                  pl.BlockSpec((1, S5_BLK_CH), lambda k, b, n: (0, k))],
        out_specs=(pl.BlockSpec((S5_CHUNK, S5_BLK_CH), lambda k, b, n: (b * nch + n, k)),
                   pl.BlockSpec((None, 1, S5_BLK_ST), lambda k, b, n: (b, 0, k)),
                   pl.BlockSpec((None, 1, S5_BLK_ST), lambda k, b, n: (b, 0, k))),
        scratch_shapes=[pltpu.VMEM((S5_CHUNK, S5_BLK_ST), F32), pltpu.VMEM((S5_CHUNK, S5_BLK_ST), F32),
                        pltpu.VMEM((1, S5_BLK_ST), F32), pltpu.VMEM((1, S5_BLK_ST), F32),
                        pltpu.VMEM((SUBLANES, S5_BLK_ST), F32), pltpu.VMEM((SUBLANES, S5_BLK_ST), F32),
                        pltpu.VMEM((SUBLANES, S5_BLK_ST), F32), pltpu.VMEM((SUBLANES, S5_BLK_ST), F32)],
        compiler_params=_params(("parallel", "parallel", "arbitrary")), name="s5_scan")(
            proj, pm, pmt, tabs['bb_re'], tabs['bb_im'], tabs['c_re'], tabs['c_im'],
            tabs['ab_re'], tabs['ab_im'], tabs['pw_re'], tabs['pw_im'], tabs['d'])
    return y, sre, sim


def _s5_step_kernel(u_ref, hre_ref, him_ref, bbre_ref, bbim_ref, cre_ref, cim_ref, abre_ref, abim_ref, d_ref,
                    y_ref, sre_out, sim_out):
    u = u_ref[...]
    ub = u.astype(BF16)
    are, aim = abre_ref[...], abim_ref[...]
    hre, him = hre_ref[...], him_ref[...]
    sre = are * hre - aim * him + _dot(ub, bbre_ref[...])
    sim = are * him + aim * hre + _dot(ub, bbim_ref[...])
    sre_out[...] = sre
    sim_out[...] = sim
    y = _dot(sre.astype(BF16), cre_ref[...]) - _dot(sim.astype(BF16), cim_ref[...]) + d_ref[...] * u
    y_ref[...] = _gelu(y)


def _s5_step(proj, h_re, h_im, tabs):
    nblk = tabs['nblk']
    rows = proj.shape[0]
    nstate = nblk * S5_BLK_ST
    const3 = lambda shape: pl.BlockSpec((None,) + shape, lambda k: (k, 0, 0))
    lane_blk = lambda w: pl.BlockSpec((rows, w), lambda k: (0, k))
    return pl.pallas_call(
        _s5_step_kernel,
        out_shape=(jax.ShapeDtypeStruct((rows, nblk * S5_BLK_CH), F32),
                   jax.ShapeDtypeStruct((rows, nstate), F32), jax.ShapeDtypeStruct((rows, nstate), F32)),
        grid=(nblk,),
        in_specs=[lane_blk(S5_BLK_CH), lane_blk(S5_BLK_ST), lane_blk(S5_BLK_ST),
                  const3((S5_BLK_CH, S5_BLK_ST)), const3((S5_BLK_CH, S5_BLK_ST)),
                  const3((S5_BLK_ST, S5_BLK_CH)), const3((S5_BLK_ST, S5_BLK_CH)),
                  const3((1, S5_BLK_ST)), const3((1, S5_BLK_ST)),
                  pl.BlockSpec((1, S5_BLK_CH), lambda k: (0, k))],
        out_specs=(lane_blk(S5_BLK_CH), lane_blk(S5_BLK_ST), lane_blk(S5_BLK_ST)),
        compiler_params=_params(("parallel",)), name="s5_step")(
            proj, h_re.reshape(rows, nstate), h_im.reshape(rows, nstate),
            tabs['bb_re'], tabs['bb_im'], tabs['c_re'], tabs['c_im'], tabs['ab_re'], tabs['ab_im'], tabs['d'])


def _s5_weights(w_in, lam_re, lam_im, log_dt, b_re, b_im, c_re, c_im, d_skip, w_glu, b_glu, w_out):
    tabs = _s5_tables(lam_re, lam_im, log_dt, b_re, b_im, c_re, c_im, d_skip)
    return (_weight_bf16(w_in), _weight_bf16(w_glu), b_glu.astype(F32).reshape(1, -1), _weight_bf16(w_out), tabs)


def _glu_epilogue(acc, gy, z, b):
    return gy * _sigmoid(acc + b) * _silu(z)


def _cast_prologue(a):
    return a


def _s5_layer(x, mods, h_state, w, *, prompt, batch, seq, tm):
    g, scale, shift, gate = mods
    w_in, w_glu, b_glu, w_out, tabs = w
    width = w_glu.shape[0]
    proj = _in_proj(x, g, scale, shift, w_in, batch_kind=prompt, rows_per_batch=seq, tm=tm, tn=MM_TN, name="s5_in_proj")
    if prompt:
        gy, sre, sim = _s5_scan(proj, tabs, batch, seq)
    else:
        gy, sre, sim = _s5_step(proj, h_state[0], h_state[1], tabs)
    a = _fused_matmul([('row', gy)], w_glu, [('tile', gy), ('tile', proj, width), ('col', b_glu)],
                      prologue=_cast_prologue, epilogue=_glu_epilogue, out_dtype=BF16, tm=min(tm, MM_TM_WIDE_ROWS),
                      tn=MM_TN, name="s5_glu")
    x_new = _out_proj([('row', a)], w_out, x, gate, batch_kind=prompt, rows_per_batch=seq, tm=tm, tn=MM_TN, name="s5_out_proj")
    return x_new, sre, sim


def _conv_silu_chunk(x_ref, w_ref, b_ref, pad_ref, first):
    c = x_ref.shape[0]

    @pl.when(first)
    def _():
        pad_ref[0:SUBLANES, :] = jnp.zeros((SUBLANES, pad_ref.shape[1]), F32)

    pad_ref[SUBLANES:SUBLANES + c, :] = x_ref[...]
    acc = w_ref[3:4, :] * pad_ref[SUBLANES:SUBLANES + c, :]
    for j in range(CONV_W - 1):
        off = SUBLANES - (CONV_W - 1) + j
        acc = acc + w_ref[j:j + 1, :] * pad_ref[off:off + c, :]
    if b_ref is not None:
        acc = acc + b_ref[...]
    pad_ref[0:SUBLANES, :] = pad_ref[c:c + SUBLANES, :]
    return _silu(acc)


def _conv_silu_step(x, taps_ref, w_ref, b_ref):
    acc = w_ref[3:4, :] * x
    for j in range(CONV_W - 1):
        acc = acc + w_ref[j:j + 1, :] * taps_ref[j]
    if b_ref is not None:
        acc = acc + b_ref[...]
    return _silu(acc)


def _tri_masks(c):
    t = lax.broadcasted_iota(jnp.int32, (c, c), 0)
    s = lax.broadcasted_iota(jnp.int32, (c, c), 1)
    return s <= t, s < t


def _pad_to_square_t(x, n):
    rows = x.shape[0]
    return jnp.concatenate([x, jnp.zeros((n - rows, n), x.dtype)], axis=0).T


M2_HEADDIM = 64
M2_STATE = 128
M2_GROUPS = 8
M2_CHUNK = 128


def _m2_ssd_kernel(x_ref, b_ref, c_ref, dt_ref, z_ref, wx_ref, wb_ref, wc_ref, bx_ref, bb_ref, bc_ref,
                   dtb_ref, nega_ref, dsk_ref, nw_ref, o_ref, sout_ref, xpad, bpad, cpad, s_ref, y_ref):
    n = pl.program_id(1)
    first = n == 0
    c = x_ref.shape[0]
    npairs = s_ref.shape[0]
    pairs_per_group = npairs // M2_GROUPS

    @pl.when(first)
    def _():
        s_ref[...] = jnp.zeros_like(s_ref)

    xs = _conv_silu_chunk(x_ref, wx_ref, bx_ref, xpad, first)
    bm = _conv_silu_chunk(b_ref, wb_ref, bb_ref, bpad, first).astype(BF16)
    cm = _conv_silu_chunk(c_ref, wc_ref, bc_ref, cpad, first).astype(BF16)
    dtv = _softplus(dt_ref[...] + dtb_ref[...])
    la = nega_ref[...] * dtv
    incl, _ = _tri_masks(c)
    tri = jnp.where(incl, 1.0, 0.0).astype(BF16)
    cum = _dot_exact_lhs(tri, la)
    cum_t = cum.T
    ecum_all = jnp.exp(cum)
    wend_all = jnp.exp(cum[c - 1:c, :] - cum)
    elast_t = jnp.exp(cum_t[:, c - 1:c])
    lane_first = lax.broadcasted_iota(jnp.int32, (c, LANES), 1) < M2_HEADDIM
    row_first = lax.broadcasted_iota(jnp.int32, (LANES, LANES), 0) < M2_HEADDIM

    for g in range(M2_GROUPS):
        bg = bm[:, g * M2_STATE:(g + 1) * M2_STATE]
        cg = cm[:, g * M2_STATE:(g + 1) * M2_STATE]
        gm = _dot_nt(cg, bg)
        for j in range(pairs_per_group):
            p = g * pairs_per_group + j
            ha, hb = 2 * p, 2 * p + 1
            xp = xs[:, p * LANES:(p + 1) * LANES]

            def decay_weights(h):
                seg = cum[:, h:h + 1] - cum_t[h:h + 1, :]
                dec = jnp.where(incl, jnp.exp(jnp.where(incl, seg, 0.0)), 0.0)
                return (gm * dec).astype(BF16)

            xdt = xp * jnp.where(lane_first, dtv[:, ha:ha + 1], dtv[:, hb:hb + 1])
            xdt_a = jnp.where(lane_first, xdt, 0.0)
            xdt_b = xdt - xdt_a
            y = _dot(decay_weights(ha), xdt_a.astype(BF16)) + _dot(decay_weights(hb), xdt_b.astype(BF16))
            sp = s_ref[p]
            y = y + _dot_nt(cg, sp.astype(BF16)) * jnp.where(lane_first, ecum_all[:, ha:ha + 1], ecum_all[:, hb:hb + 1])
            y_ref[:, p * LANES:(p + 1) * LANES] = y + dsk_ref[:, p * LANES:(p + 1) * LANES] * xp
            xw = xdt * jnp.where(lane_first, wend_all[:, ha:ha + 1], wend_all[:, hb:hb + 1])
            dmat = jnp.where(row_first, elast_t[ha:ha + 1, :], elast_t[hb:hb + 1, :])
            s_ref[p] = sp * dmat + _dot(xw.T.astype(BF16), bg)

    o_ref[...] = _gated_rmsnorm_prologue(y_ref[...], z_ref[...], nw_ref[...]).astype(o_ref.dtype)

    @pl.when(n == pl.num_programs(1) - 1)
    def _():
        sout_ref[...] = s_ref[...]


def _m2_ssd(proj, gates, mw, batch, seq):
    c = M2_CHUNK
    nch = seq // c
    width = mw['width']
    gs = M2_GROUPS * M2_STATE
    npairs = width // LANES
    xo, bo, co = width // width, (2 * width) // gs, (2 * width + gs) // gs
    row = lambda w, off: pl.BlockSpec((c, w), lambda b, n: (b * nch + n, off))
    par = lambda r, w, off: pl.BlockSpec((r, w), lambda b, n: (0, off))
    return pl.pallas_call(
        _m2_ssd_kernel,
        out_shape=(jax.ShapeDtypeStruct((batch * seq, width), BF16),
                   jax.ShapeDtypeStruct((batch, npairs, LANES, M2_STATE), F32)),
        grid=(batch, nch),
        in_specs=[row(width, xo), row(gs, bo), row(gs, co), row(LANES, 0), row(width, 0),
                  par(CONV_W, width, 0), par(CONV_W, gs, width // gs), par(CONV_W, gs, width // gs + 1),
                  par(1, width, 0), par(1, gs, width // gs), par(1, gs, width // gs + 1),
                  par(1, LANES, 0), par(1, LANES, 0), par(1, width, 0), par(1, width, 0)],
        out_specs=(pl.BlockSpec((c, width), lambda b, n: (b * nch + n, 0)),
                   pl.BlockSpec((None, npairs, LANES, M2_STATE), lambda b, n: (b, 0, 0, 0))),
        scratch_shapes=[pltpu.VMEM((c + SUBLANES, width), F32), pltpu.VMEM((c + SUBLANES, gs), F32),
                        pltpu.VMEM((c + SUBLANES, gs), F32), pltpu.VMEM((npairs, LANES, M2_STATE), F32),
                        pltpu.VMEM((c, width), F32)],
        compiler_params=_params(("parallel", "arbitrary")), name="m2_ssd")(
            proj, proj, proj, gates, proj, mw['conv_w'], mw['conv_w'], mw['conv_w'], mw['conv_b'], mw['conv_b'],
            mw['conv_b'], mw['dt_bias'], mw['neg_a'], mw['d_ch'], mw['norm'])


def _m2_pre_step_kernel(xbc_ref, dt_ref, taps_ref, w_ref, b_ref, dtb_ref, nega_ref, act_ref, dtv_ref, dec_ref):
    act_ref[...] = _conv_silu_step(xbc_ref[...], taps_ref, w_ref, b_ref)
    dtv = _softplus(dt_ref[...] + dtb_ref[...])
    dtv_ref[...] = dtv
    dec_ref[...] = jnp.exp(nega_ref[...] * dtv)


def _m2_pre_step(proj, dt_raw, taps, mw):
    rows = proj.shape[0]
    width = mw['width']
    cdim = mw['conv_w'].shape[1]
    xbc = proj[:, width:width + cdim]
    return pl.pallas_call(
        _m2_pre_step_kernel,
        out_shape=(jax.ShapeDtypeStruct((rows, cdim), F32), jax.ShapeDtypeStruct((rows, LANES), F32),
                   jax.ShapeDtypeStruct((rows, LANES), F32)),
        name="m2_pre_step", compiler_params=pltpu.CompilerParams(vmem_limit_bytes=V7X_VMEM_LIMIT_BYTES))(
            xbc, dt_raw, taps, mw['conv_w'], mw['conv_b'], mw['dt_bias'], mw['neg_a'])


def _m2_state_step_kernel(x_ref, dtc_ref, decc_ref, b_ref, c_ref, dsk_ref, s_ref, y_ref, sout_ref):
    rows = x_ref.shape[0]
    pairs = x_ref.shape[1] // LANES
    bv = b_ref[...]
    cb = c_ref[...].astype(BF16)
    for j in range(pairs):
        sl = slice(j * LANES, (j + 1) * LANES)
        xp = x_ref[:, sl]
        xdt_t = _pad_to_square_t(xp * dtc_ref[:, sl], LANES)
        dec_t = _pad_to_square_t(decc_ref[:, sl], LANES)
        for b in range(rows):
            sp = s_ref[b, j]
            s_new = sp * dec_t[:, b:b + 1] + xdt_t[:, b:b + 1] * bv[b:b + 1, :]
            sout_ref[b, j] = s_new
            y_ref[b:b + 1, sl] = _dot_nt(cb[b:b + 1, :], s_new.astype(BF16)) + dsk_ref[:, sl] * xp[b:b + 1, :]


def _m2_state_step(act, dt_ch, dec_ch, ssm, mw):
    rows = act.shape[0]
    width = mw['width']
    gw = width // M2_GROUPS
    ppg = gw // LANES
    npairs = width // LANES
    gs = M2_GROUPS * M2_STATE
    s4 = ssm.reshape(rows, npairs, LANES, M2_STATE)
    blk = lambda w, base: pl.BlockSpec((rows, w), lambda g: (0, base + g))
    return pl.pallas_call(
        _m2_state_step_kernel,
        out_shape=(jax.ShapeDtypeStruct((rows, width), F32), jax.ShapeDtypeStruct(s4.shape, F32)),
        grid=(M2_GROUPS,),
        in_specs=[blk(gw, 0), blk(gw, 0), blk(gw, 0), blk(M2_STATE, width // M2_STATE),
                  blk(M2_STATE, (width + gs) // M2_STATE), pl.BlockSpec((1, gw), lambda g: (0, g)),
                  pl.BlockSpec((rows, ppg, LANES, M2_STATE), lambda g: (0, g, 0, 0))],
        out_specs=(blk(gw, 0), pl.BlockSpec((rows, ppg, LANES, M2_STATE), lambda g: (0, g, 0, 0))),
        compiler_params=_params(("parallel",)), name="m2_state_step")(
            act, dt_ch, dec_ch, act, act, mw['d_ch'], s4)


def _m2_weights(w_in, conv_w, conv_b, dt_bias, a_log, d_skip, norm_w, w_out):
    f = F32
    heads = dt_bias.shape[0]
    width = norm_w.shape[0]
    pad = LANES - heads
    cdim = conv_w.shape[1]
    return dict(
        w_in=_weight_bf16(w_in, width + cdim),
        w_gate=_pad_cols(w_in[:, width + cdim:].astype(f), LANES),
        conv_w=conv_w.astype(f), conv_b=conv_b.astype(f).reshape(1, -1),
        dt_bias=jnp.pad(dt_bias.astype(f), (0, pad)).reshape(1, LANES),
        neg_a=jnp.pad(-jnp.exp(a_log.astype(f)), (0, pad)).reshape(1, LANES),
        d_ch=jnp.repeat(d_skip.astype(f), width // heads).reshape(1, width),
        norm=norm_w.astype(f).reshape(1, width), w_out=_weight_bf16(w_out), width=width, heads=heads)


def _gated_rmsnorm_prologue(y, z, w):
    v = y * _silu(z)
    return v * lax.rsqrt(jnp.mean(v * v, axis=-1, keepdims=True) + EPS) * w


def _m2_layer(x, mods, state, mw, *, prompt, batch, seq, tm):
    g, scale, shift, gate = mods
    width, heads = mw['width'], mw['heads']
    cdim = mw['conv_w'].shape[1]
    proj = _in_proj(x, g, scale, shift, mw['w_in'], batch_kind=prompt, rows_per_batch=seq, tm=tm, tn=MM_TN,
                    name="m2_in_proj")
    gates = _gate_proj(x, g, scale, shift, mw['w_gate'], batch_kind=prompt, rows_per_batch=seq, tm=tm,
                       name="m2_gate_proj")
    if prompt:
        a, ssm = _m2_ssd(proj, gates, mw, batch, seq)
        conv_new = proj.reshape(batch, seq, -1)[:, seq - (CONV_W - 1):, width:width + cdim]
        x_new = _out_proj([('row', a)], mw['w_out'], x, gate, batch_kind=True, rows_per_batch=seq, tm=tm, tn=MM_TN,
                          name="m2_out_proj")
        return x_new, conv_new, ssm
    else:
        conv_buf, ssm_in = state
        taps = jnp.swapaxes(conv_buf.astype(F32), 0, 1)
        act, dtv, dec = _m2_pre_step(proj, gates, taps, mw)
        rep = width // heads
        dt_ch = jnp.repeat(dtv[:, :heads], rep, axis=1)
        dec_ch = jnp.repeat(dec[:, :heads], rep, axis=1)
        y, ssm = _m2_state_step(act, dt_ch, dec_ch, ssm_in.astype(F32), mw)
        conv_new = jnp.concatenate([conv_buf.astype(F32)[:, 1:], proj[:, None, width:width + cdim]], axis=1)
    x_new = _out_proj([('row', y), ('row', proj, width, 0), ('vec', mw['norm'])], mw['w_out'], x, gate,
                      batch_kind=prompt, rows_per_batch=seq, tm=min(tm, 256), tn=MM_TN, name="m2_out_proj",
                      prologue=_gated_rmsnorm_prologue)
    return x_new, conv_new, ssm


GD_DK = 128
GD_DV = 128
GD_CHUNK = 64


def _dot_3pass(a, b):
    ah = a.astype(BF16)
    al = (a - ah.astype(F32)).astype(BF16)
    bh = b.astype(BF16)
    bl = (b - bh.astype(F32)).astype(BF16)
    return _dot(ah, bh) + (_dot(ah, bl) + _dot(al, bh))


def _l2norm_rows(x):
    return x * lax.rsqrt(jnp.sum(x * x, axis=-1, keepdims=True) + EPS)


def _rmsnorm_rows(x, w):
    return x * lax.rsqrt(jnp.mean(x * x, axis=-1, keepdims=True) + EPS) * w


GD_INV_BASE = 16


def _bdot(a, b):
    return jnp.einsum('hmk,hkn->hmn', a, b, preferred_element_type=F32)


def _bdot_nt(a, b):
    return jnp.einsum('hmk,hnk->hmn', a, b, preferred_element_type=F32)


def _split2(x):
    hi = x.astype(BF16)
    return hi, (x - hi.astype(F32)).astype(BF16)


def _bdot_3pass(a, b):
    (ah, al), (bh, bl) = a, b
    return _bdot(ah, bh) + (_bdot(ah, bl) + _bdot(al, bh))


def _unit_lower_inverse(a_strict):
    c = a_strict.shape[-1]
    row = lax.broadcasted_iota(jnp.int32, (c, c), 0)
    col = lax.broadcasted_iota(jnp.int32, (c, c), 1)
    eye = jnp.where(row == col, 1.0, 0.0)
    blk = GD_INV_BASE
    shift = int(math.log2(blk))
    p = jnp.where((row >> shift) == (col >> shift), -a_strict, 0.0)
    t = eye + p
    ps = _split2(p)
    for _ in range(shift - 1):
        p = _bdot_3pass(ps, ps)
        ps = _split2(p)
        t = t + _bdot_3pass(_split2(t), ps)
    while blk < c:
        below = jnp.logical_and((row >> (shift + 1)) == (col >> (shift + 1)), (row >> shift) != (col >> shift))
        ts = _split2(t)
        tb = _bdot_3pass(ts, _split2(jnp.where(below, a_strict, 0.0)))
        t = t - _bdot_3pass(_split2(tb), ts)
        blk *= 2
        shift += 1
    return t


def _gd_chunk_kernel(qkv_ref, z_ref, braw_ref, araw_ref, cw_ref, nega_ref, dtb_ref, nw_ref,
                     o_ref, sout_ref, pad, s_ref):
    n = pl.program_id(1)
    first = n == 0
    c = qkv_ref.shape[0]
    hv = s_ref.shape[0]
    hk = hv // 2
    rep = hv // hk

    @pl.when(first)
    def _():
        s_ref[...] = jnp.zeros_like(s_ref)

    qkv = _conv_silu_chunk(qkv_ref, cw_ref, None, pad, first)
    beta = _sigmoid(braw_ref[...])
    gl = nega_ref[...] * _softplus(araw_ref[...] + dtb_ref[...])
    incl, strict = _tri_masks(c)
    tri = jnp.where(incl, 1.0, 0.0).astype(BF16)
    gcum = _dot_exact_lhs(tri, gl)
    gcum_t = jnp.concatenate([gcum, jnp.zeros((LANES - c, LANES), F32)], axis=0).T

    heads = range(hv)
    per_value_head = lambda t: jnp.stack([t[h // rep] for h in heads])
    q3 = jnp.stack([qkv[:, i * GD_DK:(i + 1) * GD_DK] for i in range(hk)])
    k3 = jnp.stack([qkv[:, (hk + i) * GD_DK:(hk + i + 1) * GD_DK] for i in range(hk)])
    v3 = jnp.stack([qkv[:, (2 * hk + h) * GD_DV:(2 * hk + h + 1) * GD_DV] for h in heads])
    q3 = _l2norm_rows(q3) * (GD_DK ** -0.5)
    k3 = _l2norm_rows(k3)
    k3b = k3.astype(BF16)
    kk = per_value_head(_bdot_nt(k3b, k3b))
    qk = per_value_head(_bdot_nt(q3.astype(BF16), k3b))
    q_v, k_v = per_value_head(q3), per_value_head(k3)
    colv = jnp.stack([gcum[:, h:h + 1] for h in heads])
    rowv = jnp.stack([gcum_t[h:h + 1, :c] for h in heads])
    bcol = jnp.stack([beta[:, h:h + 1] for h in heads])
    glast = colv[:, c - 1:c, :]
    ecol = jnp.exp(colv)
    dec = jnp.where(incl, jnp.exp(jnp.where(incl, colv - rowv, 0.0)), 0.0)
    a = jnp.where(strict, (bcol * kk) * dec, 0.0)
    tinv = _unit_lower_inverse(a)
    rhs = jnp.concatenate([v3 * bcol, (k_v * bcol) * ecol], axis=-1)
    th, tl = _split2(tinv)
    rb = rhs.astype(BF16)
    sol = _bdot(th, rb) + _bdot(tl, rb)
    u, w = sol[:, :, :GD_DV], sol[:, :, GD_DV:]
    s = s_ref[...]
    sb = s.astype(BF16)
    v_new = u - _bdot(w.astype(BF16), sb)
    o = _bdot((q_v * ecol).astype(BF16), sb) + _bdot((qk * dec).astype(BF16), v_new.astype(BF16))
    zpad = jnp.zeros((hv, LANES - c, GD_DV), F32)
    kd_t = jnp.swapaxes(jnp.concatenate([k_v * jnp.exp(glast - colv), zpad], axis=1), 1, 2)
    vn_pad = jnp.concatenate([v_new, zpad], axis=1)
    s_ref[...] = s * jnp.exp(glast) + _bdot(kd_t.astype(BF16), vn_pad.astype(BF16))
    on = _rmsnorm_rows(o, nw_ref[...])
    for h in heads:
        sl = slice(h * GD_DV, (h + 1) * GD_DV)
        o_ref[:, sl] = (on[h] * _silu(z_ref[:, sl])).astype(o_ref.dtype)

    @pl.when(n == pl.num_programs(1) - 1)
    def _():
        sout_ref[...] = s_ref[...]


def _gd_chunked(proj, gates, gw, batch, seq):
    c = GD_CHUNK
    nch = seq // c
    cdim, width, hv = gw['cdim'], gw['width'], gw['hv']
    row = lambda w, off: pl.BlockSpec((c, w), lambda b, n: (b * nch + n, off))
    par = lambda r, w: pl.BlockSpec((r, w), lambda b, n: (0, 0))
    return pl.pallas_call(
        _gd_chunk_kernel,
        out_shape=(jax.ShapeDtypeStruct((batch * seq, width), BF16),
                   jax.ShapeDtypeStruct((batch, hv, GD_DK, GD_DV), F32)),
        grid=(batch, nch),
        in_specs=[row(cdim, 0), row(width, cdim // width), row(LANES, 0), row(LANES, 1),
                  par(CONV_W, cdim), par(1, LANES), par(1, LANES), par(1, GD_DV)],
        out_specs=(pl.BlockSpec((c, width), lambda b, n: (b * nch + n, 0)),
                   pl.BlockSpec((None, hv, GD_DK, GD_DV), lambda b, n: (b, 0, 0, 0))),
        scratch_shapes=[pltpu.VMEM((c + SUBLANES, cdim), F32), pltpu.VMEM((hv, GD_DK, GD_DV), F32)],
        compiler_params=_params(("parallel", "arbitrary")), name="gd_chunked")(
            proj, proj, gates, gates, gw['conv_w'], gw['neg_a'], gw['dt_bias'], gw['norm'])


def _gd_pre_step_kernel(qkv_ref, braw_ref, araw_ref, taps_ref, cw_ref, nega_ref, dtb_ref,
                        q_ref, k_ref, v_ref, beta_ref, eg_ref):
    hk = q_ref.shape[1] // GD_DK
    act = _conv_silu_step(qkv_ref[...], taps_ref, cw_ref, None)
    for kh in range(hk):
        sl = slice(kh * GD_DK, (kh + 1) * GD_DK)
        q_ref[:, sl] = _l2norm_rows(act[:, kh * GD_DK:(kh + 1) * GD_DK]) * (GD_DK ** -0.5)
        k_ref[:, sl] = _l2norm_rows(act[:, (hk + kh) * GD_DK:(hk + kh + 1) * GD_DK])
    v_ref[...] = act[:, 2 * hk * GD_DK:]
    beta_ref[...] = _sigmoid(braw_ref[...])
    eg_ref[...] = jnp.exp(nega_ref[...] * _softplus(araw_ref[...] + dtb_ref[...]))


def _gd_pre_step(proj, gates, taps, gw):
    rows = proj.shape[0]
    cdim, width, hv = gw['cdim'], gw['width'], gw['hv']
    qk_w = (cdim - width) // 2
    sd = lambda w: jax.ShapeDtypeStruct((rows, w), F32)
    return pl.pallas_call(
        _gd_pre_step_kernel, out_shape=(sd(qk_w), sd(qk_w), sd(width), sd(LANES), sd(LANES)),
        name="gd_pre_step", compiler_params=pltpu.CompilerParams(vmem_limit_bytes=V7X_VMEM_LIMIT_BYTES))(
            proj[:, :cdim], gates[:, :LANES], gates[:, LANES:], taps,
            gw['conv_w'], gw['neg_a'], gw['dt_bias'])


def _gd_state_step_kernel(q_ref, k_ref, v_ref, beta_ref, eg_ref, z_ref, nw_ref, s_ref, o_ref, sout_ref):
    rows = q_ref.shape[0]
    nk = q_ref.shape[1] // GD_DK
    rep = (v_ref.shape[1] // GD_DV) // nk
    nw = nw_ref[...]
    zrows = jnp.zeros((SUBLANES - 2, GD_DK), F32)
    for kh in range(nk):
        ksl = slice(kh * GD_DK, (kh + 1) * GD_DK)
        q8, k8 = q_ref[:, ksl], k_ref[:, ksl]
        k_t = _pad_to_square_t(k8, GD_DK)
        for b in range(rows):
            qb, kb = q8[b:b + 1, :], k8[b:b + 1, :]
            kq = jnp.concatenate([kb, qb, zrows], axis=0).astype(BF16)
            qk = jnp.sum(qb * kb, axis=-1, keepdims=True)
            for r in range(rep):
                h = kh * rep + r
                vsl = slice(h * GD_DV, (h + 1) * GD_DV)
                s = s_ref[b, h]
                ks_qs = _dot(kq, s.astype(BF16))
                eg = eg_ref[b:b + 1, vsl]
                beta = beta_ref[b:b + 1, vsl]
                v_new = beta * (v_ref[b:b + 1, vsl] - eg * ks_qs[0:1, :])
                o = eg * ks_qs[1:2, :] + qk * v_new
                sout_ref[b, h] = s * eg[:, 0:1] + k_t[:, b:b + 1] * v_new
                o_ref[b:b + 1, vsl] = _rmsnorm_rows(o, nw) * _silu(z_ref[b:b + 1, vsl])


def _gd_state_step(proj, qn, kn, v, beta_ch, eg_ch, state, gw, heads_per_step=4):
    rows = qn.shape[0]
    cdim, width, hv = gw['cdim'], gw['width'], gw['hv']
    steps = hv // heads_per_step
    kw = qn.shape[1] // steps
    vw = width // steps
    blk = lambda w, base=0: pl.BlockSpec((rows, w), lambda g, base=base: (0, base + g))
    sspec = pl.BlockSpec((rows, heads_per_step, GD_DK, GD_DV), lambda g: (0, g, 0, 0))
    return pl.pallas_call(
        _gd_state_step_kernel,
        out_shape=(jax.ShapeDtypeStruct((rows, width), F32), jax.ShapeDtypeStruct(state.shape, F32)),
        grid=(steps,),
        in_specs=[blk(kw), blk(kw), blk(vw), blk(vw), blk(vw), blk(vw, cdim // vw),
                  pl.BlockSpec((1, GD_DV), lambda g: (0, 0)), sspec],
        out_specs=(blk(vw), sspec),
        compiler_params=_params(("parallel",)), name="gd_state_step")(
            qn, kn, v, beta_ch, eg_ch, proj, gw['norm'], state)


def _gd_weights(w_in, conv_w, a_log, dt_bias, norm_w, w_out):
    f = F32
    hv = a_log.shape[0]
    cdim = conv_w.shape[1]
    width = w_out.shape[0]
    pad = LANES - hv
    base = cdim + width
    zeros = jnp.zeros((w_in.shape[0], pad), w_in.dtype)
    w_gate = jnp.concatenate([w_in[:, base:base + hv], zeros, w_in[:, base + hv:], zeros], axis=1)
    return dict(
        w_in=_weight_bf16(w_in, base), w_gate=w_gate.astype(f), conv_w=conv_w.astype(f),
        neg_a=jnp.pad(-jnp.exp(a_log.astype(f)), (0, pad)).reshape(1, LANES),
        dt_bias=jnp.pad(dt_bias.astype(f), (0, pad)).reshape(1, LANES),
        norm=norm_w.astype(f).reshape(1, -1), w_out=_weight_bf16(w_out), cdim=cdim, width=width, hv=hv)


def _gd_layer(x, mods, state, gw, *, prompt, batch, seq, tm):
    g, scale, shift, gate = mods
    cdim, width, hv = gw['cdim'], gw['width'], gw['hv']
    proj = _in_proj(x, g, scale, shift, gw['w_in'], batch_kind=prompt, rows_per_batch=seq, tm=tm, tn=MM_TN,
                    name="gd_in_proj")
    gates = _gate_proj(x, g, scale, shift, gw['w_gate'], batch_kind=prompt, rows_per_batch=seq, tm=tm,
                       name="gd_gate_proj")
    if prompt:
        a, ssm = _gd_chunked(proj, gates, gw, batch, seq)
        conv_new = proj.reshape(batch, seq, -1)[:, seq - (CONV_W - 1):, :cdim]
        x_new = _out_proj([('row', a)], gw['w_out'], x, gate, batch_kind=True, rows_per_batch=seq, tm=tm, tn=MM_TN,
                          name="gd_out_proj")
    else:
        conv_buf, ssm_in = state
        taps = jnp.swapaxes(conv_buf.astype(F32), 0, 1)
        qn, kn, v, beta, eg = _gd_pre_step(proj, gates, taps, gw)
        beta_ch = jnp.repeat(beta[:, :hv], GD_DV, axis=1)
        eg_ch = jnp.repeat(eg[:, :hv], GD_DV, axis=1)
        a, ssm = _gd_state_step(proj, qn, kn, v, beta_ch, eg_ch, ssm_in.astype(F32), gw)
        conv_new = jnp.concatenate([conv_buf.astype(F32)[:, 1:], proj[:, None, :cdim]], axis=1)
        x_new = _out_proj([('row', a)], gw['w_out'], x, gate, batch_kind=False, rows_per_batch=seq, tm=tm, tn=MM_TN,
                          name="gd_out_proj", prologue=_cast_prologue)
    return x_new, conv_new, ssm


AT_DIM = 128
IDX_DIM = 128
TOPK_MAX = 256
REL_BUCKETS = 32
REL_MAX_DIST = 128
AT_TILE = 256
INT32_MIN = -2 ** 31
_NEG_BITS = int(np.float32(NEG).view(np.int32))
NEG_SORT_KEY = _NEG_BITS ^ 0x7FFFFFFF if _NEG_BITS < 0 else _NEG_BITS


def _bucket_starts():
    d = np.arange(0, REL_MAX_DIST + 1)
    exact = REL_BUCKETS // 2
    far = exact + (np.log(np.maximum(d, 1).astype(np.float32) / exact) / math.log(REL_MAX_DIST / exact)
                   * (REL_BUCKETS - exact)).astype(np.int32)
    bucket = np.where(d < exact, d, np.minimum(far, REL_BUCKETS - 1))
    assert np.all(np.diff(bucket) >= 0) and bucket[-1] == REL_BUCKETS - 1
    return [int(np.argmax(bucket >= b)) for b in range(REL_BUCKETS)]


def _bias_from_dist(dist, value_of_bucket):
    starts = _bucket_starts()
    val = value_of_bucket(REL_BUCKETS - 1)
    for b in range(REL_BUCKETS - 2, -1, -1):
        val = jnp.where(dist < starts[b + 1], value_of_bucket(b), val)
    return val


def _sort_key(x):
    x = jnp.where(x == 0.0, 0.0, x)
    b = pltpu.bitcast(x, jnp.int32)
    return jnp.where(b < 0, b ^ jnp.int32(0x7FFFFFFF), b)


def _kth_largest_key(count_ge, shape, k):
    def body(it, ans):
        cand = ans | jnp.left_shift(jnp.int32(1), 31 - it)
        cnt = count_ge(cand ^ jnp.int32(INT32_MIN))
        return jnp.where(cnt >= k, cand, ans)

    ans = lax.fori_loop(0, 32, body, jnp.zeros(shape, jnp.int32))
    return ans ^ jnp.int32(INT32_MIN)


def _relbias_tiles_kernel(rb_ref, o_ref):
    delta = pl.program_id(0) * AT_TILE
    h = pl.program_id(1)
    i = lax.broadcasted_iota(jnp.int32, (AT_TILE, AT_TILE), 0)
    j = lax.broadcasted_iota(jnp.int32, (AT_TILE, AT_TILE), 1)
    o_ref[...] = _bias_from_dist(delta + i - j, lambda b: rb_ref[b, h])


def _relbias_tiles(rel_bias):
    heads = rel_bias.shape[1]
    ntile = 3
    assert (ntile - 1) * AT_TILE - (AT_TILE - 1) >= REL_MAX_DIST
    return pl.pallas_call(
        _relbias_tiles_kernel, out_shape=jax.ShapeDtypeStruct((ntile, heads, AT_TILE, AT_TILE), F32),
        grid=(ntile, heads),
        in_specs=[pl.BlockSpec(memory_space=pltpu.SMEM)],
        out_specs=pl.BlockSpec((None, None, AT_TILE, AT_TILE), lambda d, h: (d, h, 0, 0)),
        compiler_params=_params(("parallel", "parallel")), name="at_relbias_tiles")(rel_bias.astype(F32))


def _at_index_kernel(qi_ref, wi_ref, ki_ref, o_ref, keys, cnt, *, k_sel, score_scale):
    qb = pl.program_id(1)
    tq = qi_ref.shape[0]
    nkb = keys.shape[0]
    tk = keys.shape[2]
    nh = qi_ref.shape[1] // IDX_DIM
    wsc = wi_ref[...] * score_scale
    qpos = qb * tq + lax.broadcasted_iota(jnp.int32, (tq, tk), 0)
    kloc = lax.broadcasted_iota(jnp.int32, (tq, tk), 1)
    neg_key = _sort_key(jnp.full((tq, tk), NEG, F32))

    for kb in range(nkb):
        @pl.when(kb <= qb)
        def _():
            kblk = ki_ref[kb * tk:(kb + 1) * tk, :].astype(BF16)
            sc = jnp.zeros((tq, tk), F32)
            for h in range(nh):
                d = _dot_nt(qi_ref[:, h * IDX_DIM:(h + 1) * IDX_DIM].astype(BF16), kblk)
                sc = sc + wsc[:, h:h + 1] * jnp.maximum(d, 0.0)
            adm = kb * tk + kloc <= qpos
            keys[kb] = _sort_key(jnp.where(adm, sc, NEG))

        @pl.when(kb > qb)
        def _():
            keys[kb] = neg_key

    def count_ge(t):
        cnt[...] = jnp.where(keys[0] >= t, 1, 0)
        for kb in range(1, nkb):
            @pl.when(kb <= qb)
            def _():
                cnt[...] += jnp.where(keys[kb] >= t, 1, 0)
        beyond = (nkb - 1 - qb) * tk
        return jnp.sum(cnt[...], axis=1, keepdims=True) + jnp.where(t <= NEG_SORT_KEY, beyond, 0)

    thr = _kth_largest_key(count_ge, (tq, 1), k_sel)
    n_ge = count_ge(thr)
    has_ties = jnp.max(n_ge) > k_sel

    @pl.when(jnp.logical_not(has_ties))
    def _():
        for kb in range(nkb):
            adm = kb * tk + kloc <= qpos
            sel = jnp.logical_and(keys[kb] >= thr, adm)
            o_ref[kb] = jnp.where(sel, 0.0, MASKED).T.astype(o_ref.dtype)

    @pl.when(has_ties)
    def _():
        acc = jnp.zeros((tq, tk), jnp.int32)
        for kb in range(nkb):
            acc = acc + jnp.where(keys[kb] > thr, 1, 0)
        room = (k_sel - jnp.sum(acc, axis=1, keepdims=True)).astype(F32)
        upper = jnp.where(lax.broadcasted_iota(jnp.int32, (tk, tk), 0) <= lax.broadcasted_iota(jnp.int32, (tk, tk), 1),
                          1.0, 0.0).astype(BF16)
        seen = jnp.zeros((tq, 1), F32)
        for kb in range(nkb):
            key = keys[kb]
            eq = key == thr
            eqf = jnp.where(eq, 1.0, 0.0)
            rank = seen + _dot(eqf.astype(BF16), upper)
            seen = seen + jnp.sum(eqf, axis=1, keepdims=True)
            adm = kb * tk + kloc <= qpos
            sel = jnp.logical_and(jnp.logical_or(key > thr, jnp.logical_and(eq, rank <= room)), adm)
            o_ref[kb] = jnp.where(sel, 0.0, MASKED).T.astype(o_ref.dtype)


def _at_index(proj, aw, batch, seq, k_sel):
    tq = tk = AT_TILE
    nq = seq // tq
    width = aw['width']
    nh = aw['idx_heads']
    qio = (4 * width) // (nh * IDX_DIM)
    kio = (4 * width + nh * IDX_DIM) // IDX_DIM
    kern = functools.partial(_at_index_kernel, k_sel=k_sel, score_scale=IDX_DIM ** -0.5 * nh ** -0.5)
    return pl.pallas_call(
        kern, out_shape=jax.ShapeDtypeStruct((batch * nq, seq // tk, tq, tk), BF16), grid=(batch, nq),
        in_specs=[pl.BlockSpec((tq, nh * IDX_DIM), lambda b, q: (b * nq + q, qio)),
                  pl.BlockSpec((tq, LANES), lambda b, q: (b * nq + q, kio + 1)),
                  pl.BlockSpec((seq, IDX_DIM), lambda b, q: (b, kio))],
        out_specs=pl.BlockSpec((None, seq // tk, tq, tk), lambda b, q: (b * nq + q, 0, 0, 0)),
        scratch_shapes=[pltpu.VMEM((seq // tk, tq, tk), jnp.int32), pltpu.VMEM((tq, tk), jnp.int32)],
        compiler_params=_params(("parallel", "parallel")), name="at_index")(proj, proj, proj)


AT_HEAD_GROUP = 4
MASKED = 2.0 * NEG


def _at_attend_kernel(q_ref, k_ref, vt_ref, z_ref, mask_ref, bias_ref, o_ref, acc, m_scr, l_scr):
    qb = pl.program_id(2)
    t = q_ref.shape[0]
    hg = q_ref.shape[1] // AT_DIM
    acc[...] = jnp.zeros_like(acc)
    m_scr[...] = jnp.full(m_scr.shape, NEG, F32)
    l_scr[...] = jnp.zeros_like(l_scr)

    heads = [slice(h * AT_DIM, (h + 1) * AT_DIM) for h in range(hg)]
    q3t = jnp.stack([q_ref[:, sl].T for sl in heads]).astype(BF16)

    def key_tile(kb, carry):
        rows = pl.ds(pl.multiple_of(kb * t, t), t)
        k3 = jnp.stack([k_ref[rows, sl] for sl in heads])
        s_t = _bdot(k3, q3t) + bias_ref[jnp.minimum(qb - kb, 2)] + mask_ref[kb].astype(F32)
        m_old = m_scr[...]
        m_new = jnp.maximum(m_old, jnp.max(s_t, axis=1, keepdims=True))
        alpha = jnp.exp(m_old - m_new)
        p_t = jnp.exp(s_t - m_new)
        l_scr[...] = alpha * l_scr[...] + jnp.sum(p_t, axis=1, keepdims=True)
        acc[...] = alpha * acc[...] + _bdot(vt_ref[:, kb], p_t.astype(BF16))
        m_scr[...] = m_new
        return carry

    lax.fori_loop(0, qb + 1, key_tile, 0)
    o_t = acc[...] / l_scr[...]
    for h, sl in enumerate(heads):
        o_ref[:, sl] = (o_t[h].T * _silu(z_ref[:, sl])).astype(o_ref.dtype)


def _at_attend(proj, proj_bf, maskadd_t, tiles_t, aw, batch, seq):
    t = AT_TILE
    nq = seq // t
    width = aw['width']
    heads = width // AT_DIM
    hg = AT_HEAD_GROUP
    gw = hg * AT_DIM
    ng = width // gw
    v_t = proj_bf[:, 2 * width:3 * width].reshape(batch, nq, t, heads, AT_DIM).transpose(0, 3, 1, 4, 2)
    return pl.pallas_call(
        _at_attend_kernel, out_shape=jax.ShapeDtypeStruct((batch * seq, width), BF16), grid=(batch, ng, nq),
        in_specs=[pl.BlockSpec((t, gw), lambda b, g, q: (b * nq + q, g)),
                  pl.BlockSpec((seq, gw), lambda b, g, q: (b, ng + g)),
                  pl.BlockSpec((None, hg, nq, AT_DIM, t), lambda b, g, q: (b, g, 0, 0, 0)),
                  pl.BlockSpec((t, gw), lambda b, g, q: (b * nq + q, 3 * ng + g)),
                  pl.BlockSpec((None, nq, t, t), lambda b, g, q: (b * nq + q, 0, 0, 0)),
                  pl.BlockSpec((3, hg, t, t), lambda b, g, q: (0, g, 0, 0))],
        out_specs=pl.BlockSpec((t, gw), lambda b, g, q: (b * nq + q, g)),
        scratch_shapes=[pltpu.VMEM((hg, AT_DIM, t), F32), pltpu.VMEM((hg, 1, t), F32), pltpu.VMEM((hg, 1, t), F32)],
        compiler_params=_params(("parallel", "parallel", "arbitrary")), name="at_attend")(
            proj, proj_bf, v_t, proj, maskadd_t, tiles_t)


def _at_weights(w_in, rel_bias, w_out):
    width = w_out.shape[0]
    heads = rel_bias.shape[1]
    idx_heads = (w_in.shape[1] - 4 * width - IDX_DIM) // (IDX_DIM + 1)
    col_scale = jnp.where(jnp.arange(w_in.shape[1]) < width, AT_DIM ** -0.5, 1.0).astype(F32)
    return dict(w_in=_weight_bf16(w_in, col_scale=col_scale), rel_bias=rel_bias.astype(F32),
                w_out=_weight_bf16(w_out), width=width, heads=heads, idx_heads=idx_heads)


def _at_rows_kernel(k_ref, v_ref, ki_ref, ko_ref, vo_ref, kio_ref):
    nh = ko_ref.shape[1]
    heads = lambda x: jnp.stack([x[:, h * AT_DIM:(h + 1) * AT_DIM] for h in range(nh)], axis=1)
    ko_ref[...] = heads(k_ref[...])
    vo_ref[...] = heads(v_ref[...])
    kio_ref[...] = ki_ref[...]


def _at_rows(proj, aw, lead, tm=256):
    width, heads = aw['width'], aw['heads']
    m = proj.shape[0]
    tm = min(tm, m)
    kio = (4 * width + aw['idx_heads'] * IDX_DIM) // IDX_DIM
    row4 = pl.BlockSpec((tm, heads, AT_DIM), lambda i: (i, 0, 0))
    k, v, ki = pl.pallas_call(
        _at_rows_kernel,
        out_shape=(jax.ShapeDtypeStruct((m, heads, AT_DIM), F32), jax.ShapeDtypeStruct((m, heads, AT_DIM), F32),
                   jax.ShapeDtypeStruct((m, IDX_DIM), F32)),
        grid=(m // tm,),
        in_specs=[pl.BlockSpec((tm, width), lambda i: (i, 1)), pl.BlockSpec((tm, width), lambda i: (i, 2)),
                  pl.BlockSpec((tm, IDX_DIM), lambda i: (i, kio))],
        out_specs=(row4, row4, pl.BlockSpec((tm, IDX_DIM), lambda i: (i, 0))),
        compiler_params=_params(("parallel",)), name="at_rows")(proj, proj, proj)
    return (k.reshape(lead + (heads, AT_DIM)), v.reshape(lead + (heads, AT_DIM)), ki.reshape(lead + (IDX_DIM,)))


def _at_layer_prompt(x, mods, aw, final_g, *, batch, seq, tm):
    g, scale, shift, gate = mods
    proj, proj_bf = _in_proj(x, g, scale, shift, aw['w_in'], batch_kind=True, rows_per_batch=seq, tm=tm, tn=MM_TN,
                             name="at_in_proj", bf16_copy=True)
    k_sel = min(TOPK_MAX, seq // 4)
    maskadd = _at_index(proj, aw, batch, seq, k_sel)
    tiles = _relbias_tiles(aw['rel_bias'])
    a = _at_attend(proj, proj_bf, maskadd, jnp.swapaxes(tiles, 2, 3), aw, batch, seq)
    d = x.shape[1]
    y = _fused_matmul([('row', a)], aw['w_out'], [('tile', x), ('batchcol', gate), ('col', final_g.reshape(1, d))],
                      prologue=None, epilogue=_residual_norm_epilogue, out_dtype=F32, tm=min(tm, MM_TM_WIDE_ROWS),
                      tn=d, rows_per_batch=seq, name="at_out_proj_norm")
    return (y,) + _at_rows(proj, aw, (batch, seq))


AT_PAGES_PER_STEP = 16


def _at_page_scores_kernel(pt_ref, qi_ref, w_ref, kidx_ref, o_ref, kbuf, sem):
    b, j = pl.program_id(0), pl.program_id(1)
    nj = pl.num_programs(1)
    npg = kbuf.shape[1]
    step = b * nj + j
    last_step = pl.num_programs(0) * nj - 1

    def page_copy(s, i, slot):
        sb, sj = s // nj, s % nj
        return pltpu.make_async_copy(kidx_ref.at[pt_ref[sb, sj * npg + i]], kbuf.at[slot, i], sem.at[slot])

    def start_all(s, slot):
        for i in range(npg):
            page_copy(s, i, slot).start()

    slot = step % 2

    @pl.when(step == 0)
    def _():
        start_all(step, slot)

    @pl.when(step < last_step)
    def _():
        start_all(step + 1, 1 - slot)

    for i in range(npg):
        page_copy(step, i, slot).wait()
    qi = qi_ref[...].astype(BF16)
    w = w_ref[...]
    for i in range(npg):
        d = _dot_nt(qi, kbuf[slot, i].astype(BF16))
        o_ref[i:i + 1, :] = jnp.sum(w * jnp.maximum(d, 0.0), axis=0, keepdims=True)


def _at_page_scores(qi3, w3, cache_kidx, page_table):
    rows, nh, _ = qi3.shape
    npages = page_table.shape[1]
    page = cache_kidx.shape[1]
    npg = math.gcd(npages, AT_PAGES_PER_STEP)
    grid_spec = pltpu.PrefetchScalarGridSpec(
        num_scalar_prefetch=1, grid=(rows, npages // npg),
        in_specs=[pl.BlockSpec((None, nh, IDX_DIM), lambda b, j, pt: (b, 0, 0)),
                  pl.BlockSpec((None, nh, IDX_DIM), lambda b, j, pt: (b, 0, 0)),
                  pl.BlockSpec(memory_space=pl.ANY)],
        out_specs=pl.BlockSpec((None, npg, page), lambda b, j, pt: (b, j, 0)),
        scratch_shapes=[pltpu.VMEM((2, npg, page, IDX_DIM), F32), pltpu.SemaphoreType.DMA((2,))])
    return pl.pallas_call(
        _at_page_scores_kernel, out_shape=jax.ShapeDtypeStruct((rows, npages, page), F32), grid_spec=grid_spec,
        compiler_params=_params(("arbitrary", "arbitrary")), name="at_page_scores")(page_table, qi3, w3, cache_kidx)


def _at_sample_select_kernel(sc_ref, qi_ref, w_ref, kin_ref, gidx_ref, newadd_ref, rank_scr, *, k_sel):
    npages, page = sc_ref.shape
    upper = jnp.where(lax.broadcasted_iota(jnp.int32, (page, page), 0) <= lax.broadcasted_iota(jnp.int32, (page, page), 1),
                      1.0, 0.0).astype(BF16)
    lower = jnp.where(lax.broadcasted_iota(jnp.int32, (npages, npages), 1) < lax.broadcasted_iota(jnp.int32, (npages, npages), 0),
                      1.0, 0.0).astype(BF16)

    def total(x):
        return jnp.sum(jnp.sum(x, axis=1, keepdims=True), axis=0, keepdims=True)

    def position_rank(flags):
        row_cnt = jnp.broadcast_to(jnp.sum(flags, axis=1, keepdims=True), (npages, page))
        return _dot(lower, row_cnt.astype(BF16)) + _dot(flags.astype(BF16), upper)

    keys = _sort_key(sc_ref[...])
    dots = jnp.sum(qi_ref[...] * kin_ref[...], axis=1, keepdims=True)
    s_new = jnp.sum(w_ref[:, 0:1] * jnp.maximum(dots, 0.0), axis=0, keepdims=True)
    key_new = _sort_key(s_new)

    def count_ge(t):
        return total(jnp.where(keys >= t, 1, 0)) + jnp.where(key_new >= t, 1, 0)

    thr = _kth_largest_key(count_ge, (1, 1), k_sel)
    n_gt = total(jnp.where(keys > thr, 1.0, 0.0)) + jnp.where(key_new > thr, 1.0, 0.0)
    room = k_sel - n_gt
    eq = keys == thr
    eqf = jnp.where(eq, 1.0, 0.0)
    sel = jnp.logical_or(keys > thr, jnp.logical_and(eq, position_rank(eqf) <= room))
    sel_new = jnp.logical_or(key_new > thr, jnp.logical_and(key_new == thr, total(eqf) + 1.0 <= room))
    newadd_ref[...] = jnp.broadcast_to(jnp.where(sel_new, 0.0, NEG), (1, page))

    self_f = jnp.where(sel, 1.0, 0.0)
    rank_scr[...] = jnp.where(sel, position_rank(self_f) - 1.0, -1.0)
    jidx = lax.broadcasted_iota(jnp.int32, (k_sel, page), 0).astype(F32)
    lane = lax.broadcasted_iota(jnp.int32, (page, LANES), 1)
    pick = jnp.where(lane == 0, lax.broadcasted_iota(jnp.int32, (page, LANES), 0).astype(F32),
                     jnp.where(lane <= 2, 1.0, 0.0)).astype(BF16)
    out_lane = lax.broadcasted_iota(jnp.int32, (k_sel, LANES), 1)

    def add_pages(g, acc):
        ranks = rank_scr[pl.ds(pl.multiple_of(g * SUBLANES, SUBLANES), SUBLANES), :]
        for i in range(SUBLANES):
            onehot = jnp.where(ranks[i:i + 1, :] == jidx, 1.0, 0.0).astype(BF16)
            slot = lax.convert_element_type(g * SUBLANES + i, F32)
            acc = acc + _dot(onehot, pick) * jnp.where(out_lane == 1, slot, 1.0)
        return acc

    assert npages % SUBLANES == 0
    gidx_ref[...] = lax.fori_loop(0, npages // SUBLANES, add_pages, jnp.zeros((k_sel, LANES), F32))


def _at_sample_select(scores, qi3, w3, ki_new, k_sel):
    rows, npages, page = scores.shape
    nh = qi3.shape[1]
    assert page == LANES and page >= REL_MAX_DIST
    kern = functools.partial(_at_sample_select_kernel, k_sel=k_sel)
    return pl.pallas_call(
        kern, out_shape=(jax.ShapeDtypeStruct((rows, k_sel, LANES), F32), jax.ShapeDtypeStruct((rows, 1, page), F32)),
        grid=(rows,),
        in_specs=[pl.BlockSpec((None, npages, page), lambda b: (b, 0, 0)),
                  pl.BlockSpec((None, nh, IDX_DIM), lambda b: (b, 0, 0)),
                  pl.BlockSpec((None, nh, IDX_DIM), lambda b: (b, 0, 0)),
                  pl.BlockSpec((None, 1, IDX_DIM), lambda b: (b, 0, 0))],
        out_specs=(pl.BlockSpec((None, k_sel, LANES), lambda b: (b, 0, 0)),
                   pl.BlockSpec((None, 1, page), lambda b: (b, 0, 0))),
        scratch_shapes=[pltpu.VMEM((npages, page), F32)],
        compiler_params=_params(("parallel",)), name="at_sample_select")(
            scores, qi3, w3, ki_new.reshape(rows, 1, IDX_DIM))


def _at_gather_attend_kernel(pt_ref, slot_ref, off_ref, q_ref, ck_ref, cv_ref, pos_ref, newadd_ref, rbt_ref,
                             kn_ref, vn_ref, z_ref, o_ref, kg, vg, sem, *, past):
    b = pl.program_id(0)
    nsel, nh, d = kg.shape
    ncol = nsel * nh

    def row_copies(j):
        page = pt_ref[b, slot_ref[b, j]]
        off = off_ref[b, j]
        return (pltpu.make_async_copy(ck_ref.at[page, off], kg.at[j], sem.at[0]),
                pltpu.make_async_copy(cv_ref.at[page, off], vg.at[j], sem.at[1]))

    def start(j, carry):
        for c in row_copies(j):
            c.start()
        return carry

    def wait(j, carry):
        for c in row_copies(j):
            c.wait()
        return carry

    lax.fori_loop(0, nsel, start, 0)
    qs = q_ref[...].astype(BF16)
    pos = pos_ref[...]
    own = (lax.broadcasted_iota(jnp.int32, (nh, ncol), 1) & (nh - 1)) == lax.broadcasted_iota(jnp.int32, (nh, ncol), 0)
    keep = jnp.logical_and(own, pos >= 0)
    bias = _bias_from_dist(jnp.maximum(past - pos, 0), lambda bk: rbt_ref[:, bk:bk + 1])
    nadd = newadd_ref[:, 0:1]
    s_n = jnp.sum(qs.astype(F32) * kn_ref[...], axis=1, keepdims=True) + rbt_ref[:, 0:1] + nadd
    lax.fori_loop(0, nsel, wait, 0)
    s = jnp.where(keep, _dot_nt(qs, kg[...].reshape(ncol, d).astype(BF16)) + bias, NEG)
    m = jnp.maximum(jnp.max(s, axis=1, keepdims=True), s_n)
    pr = jnp.where(keep, jnp.exp(s - m), 0.0)
    p_n = jnp.where(nadd < 0.0, 0.0, jnp.exp(s_n - m))
    l = jnp.sum(pr, axis=1, keepdims=True) + p_n
    o = _dot(pr.astype(BF16), vg[...].reshape(ncol, d).astype(BF16)) + p_n * vn_ref[...]
    o_ref[...] = o / l * _silu(z_ref[...])


def _at_gather_attend(q3, cache_k, cache_v, page_table, slot, off, pos_cols, newadd, rbt, kn3, vn3, z3):
    rows, nh, d = q3.shape
    assert nh & (nh - 1) == 0
    nsel = slot.shape[1]
    ncol = nsel * nh
    past = page_table.shape[1] * cache_k.shape[1]
    row3 = pl.BlockSpec((None, nh, d), lambda b, *_: (b, 0, 0))
    grid_spec = pltpu.PrefetchScalarGridSpec(
        num_scalar_prefetch=3, grid=(rows,),
        in_specs=[row3, pl.BlockSpec(memory_space=pl.ANY), pl.BlockSpec(memory_space=pl.ANY),
                  pl.BlockSpec((None, 1, ncol), lambda b, *_: (b, 0, 0)),
                  pl.BlockSpec((None, 1, newadd.shape[-1]), lambda b, *_: (b, 0, 0)),
                  pl.BlockSpec(rbt.shape, lambda b, *_: (0, 0)),
                  row3, row3, row3],
        out_specs=row3,
        scratch_shapes=[pltpu.VMEM((nsel, nh, d), F32), pltpu.VMEM((nsel, nh, d), F32), pltpu.SemaphoreType.DMA((2,))])
    kern = functools.partial(_at_gather_attend_kernel, past=past)
    return pl.pallas_call(
        kern, out_shape=jax.ShapeDtypeStruct((rows, nh, d), F32), grid_spec=grid_spec,
        compiler_params=_params(("arbitrary",)), name="at_gather_attend")(
            page_table, slot, off, q3, cache_k, cache_v, pos_cols, newadd, rbt, kn3, vn3, z3)


def _at_layer_sample(x, mods, cache_k, cache_v, cache_kidx, page_table, aw):
    g, scale, shift, gate = mods
    rows = x.shape[0]
    width, heads, nh = aw['width'], aw['heads'], aw['idx_heads']
    proj = _in_proj(x, g, scale, shift, aw['w_in'], batch_kind=False, rows_per_batch=1, tm=SUBLANES, tn=MM_TN,
                    name="at_in_proj")
    npool, page = cache_k.shape[:2]
    npages = page_table.shape[1]
    past = npages * page
    k_sel = min(TOPK_MAX, (past + 1) // 4)
    o = 4 * width
    qi3 = proj[:, o:o + nh * IDX_DIM].reshape(rows, nh, IDX_DIM)
    ki_new = proj[:, o + nh * IDX_DIM:o + nh * IDX_DIM + IDX_DIM]
    wi = proj[:, o + nh * IDX_DIM + IDX_DIM:o + nh * IDX_DIM + IDX_DIM + nh] * (IDX_DIM ** -0.5 * nh ** -0.5)
    w3 = jnp.broadcast_to(wi[:, :, None], (rows, nh, IDX_DIM))
    scores = _at_page_scores(qi3, w3, cache_kidx.astype(F32), page_table)
    rbt = aw['rel_bias'].T
    gidx, newadd = _at_sample_select(scores, qi3, w3, ki_new, k_sel)
    off = gidx[:, :, 0].astype(jnp.int32)
    slot = gidx[:, :, 1].astype(jnp.int32)
    pos = jnp.where(gidx[:, :, 2] > 0.5, slot * page + off, -1)
    pos_cols = jnp.repeat(pos, heads, axis=-1).reshape(rows, 1, k_sel * heads)
    r3 = lambda t: t.reshape(rows, heads, AT_DIM)
    a = _at_gather_attend(r3(proj[:, :width]), cache_k.astype(F32), cache_v.astype(F32), page_table, slot, off,
                          pos_cols, newadd, rbt, r3(proj[:, width:2 * width]), r3(proj[:, 2 * width:3 * width]),
                          r3(proj[:, 3 * width:4 * width]))
    x_new = _out_proj([('row', a.reshape(rows, width))], aw['w_out'], x, gate, batch_kind=False, rows_per_batch=1,
                      tm=SUBLANES, tn=MM_TN, name="at_out_proj", prologue=_cast_prologue)
    return (x_new,) + _at_rows(proj, aw, (rows, 1))


def kernel(x_prompt, x_sample, state_s5_re, state_s5_im, state_m2_conv, state_m2_ssm, state_gd_conv, state_gd_ssm, cache_k, cache_v, cache_kidx, page_table, c_prompt, c_sample, norm_g, w_mod, b_mod, final_g, s5_w_in, s5_lam_re, s5_lam_im, s5_log_dt, s5_b_re, s5_b_im, s5_c_re, s5_c_im, s5_d, s5_w_glu, s5_b_glu, s5_w_out, m2_w_in, m2_conv_w, m2_conv_b, m2_dt_bias, m2_a_log, m2_d, m2_norm, m2_w_out, gd_w_in, gd_conv_w, gd_a_log, gd_dt_bias, gd_norm, gd_w_out, at_w_in, rel_bias, at_w_out):
    f = F32
    bp, seq, d = x_prompt.shape
    bs = x_sample.shape[0]
    depth = norm_g.shape[0]
    xp = x_prompt.astype(f).reshape(bp * seq, d)
    xs = x_sample.astype(f).reshape(bs, d)

    pad_rows = (-(bs + bp)) % SUBLANES
    c_all = jnp.concatenate([c_sample.astype(f), c_prompt.astype(f), jnp.zeros((pad_rows, d), f)], axis=0)
    mod = _modulation(c_all, w_mod, b_mod)

    def mods(i, prompt):
        g = norm_g[i].astype(f).reshape(1, d)
        rows = mod[i, bs:bs + bp] if prompt else mod[i, :bs]
        shift, scale, gate = rows[:, :d], rows[:, d:2 * d], rows[:, 2 * d:]
        if prompt:
            return g, scale[:, None, :], shift[:, None, :], gate[:, None, :]
        return g, scale, shift, gate

    s5w = _s5_weights(s5_w_in, s5_lam_re, s5_lam_im, s5_log_dt, s5_b_re, s5_b_im, s5_c_re, s5_c_im, s5_d,
                      s5_w_glu, s5_b_glu, s5_w_out)
    tm_p = MM_TM

    xp, s5_re_p, s5_im_p = _s5_layer(xp, mods(0, True), None, s5w, prompt=True, batch=bp, seq=seq, tm=tm_p)
    xs, s5_re_s, s5_im_s = _s5_layer(xs, mods(0, False), (state_s5_re, state_s5_im), s5w, prompt=False,
                                     batch=bs, seq=1, tm=SUBLANES)
    groups, nstate = state_s5_re.shape[1:]
    s5_re_p, s5_im_p = s5_re_p.reshape(bp, groups, nstate), s5_im_p.reshape(bp, groups, nstate)
    s5_re_s, s5_im_s = s5_re_s.reshape(bs, groups, nstate), s5_im_s.reshape(bs, groups, nstate)

    m2w = _m2_weights(m2_w_in, m2_conv_w, m2_conv_b, m2_dt_bias, m2_a_log, m2_d, m2_norm, m2_w_out)
    xp, m2_conv_p, m2_ssm_p = _m2_layer(xp, mods(1, True), None, m2w, prompt=True, batch=bp, seq=seq, tm=tm_p)
    xs, m2_conv_s, m2_ssm_s = _m2_layer(xs, mods(1, False), (state_m2_conv, state_m2_ssm), m2w, prompt=False,
                                        batch=bs, seq=1, tm=SUBLANES)
    m2_ssm_p = m2_ssm_p.reshape((bp,) + state_m2_ssm.shape[1:])
    m2_ssm_s = m2_ssm_s.reshape(state_m2_ssm.shape)

    gdw = _gd_weights(gd_w_in, gd_conv_w, gd_a_log, gd_dt_bias, gd_norm, gd_w_out)
    xp, gd_conv_p, gd_ssm_p = _gd_layer(xp, mods(2, True), None, gdw, prompt=True, batch=bp, seq=seq, tm=tm_p)
    xs, gd_conv_s, gd_ssm_s = _gd_layer(xs, mods(2, False), (state_gd_conv, state_gd_ssm), gdw, prompt=False,
                                        batch=bs, seq=1, tm=SUBLANES)

    atw = _at_weights(at_w_in, rel_bias, at_w_out)
    yp, k_rows_p, v_rows_p, kidx_rows_p = _at_layer_prompt(xp, mods(3, True), atw, final_g.astype(f), batch=bp,
                                                           seq=seq, tm=tm_p)
    xs, k_rows_s, v_rows_s, kidx_rows_s = _at_layer_sample(xs, mods(3, False), cache_k, cache_v, cache_kidx,
                                                           page_table, atw)

    y_prompt = yp.reshape(x_prompt.shape).astype(x_prompt.dtype)
    y_sample = _final_norm(xs, final_g).reshape(x_sample.shape).astype(x_sample.dtype)
    return (y_prompt, y_sample, s5_re_p, s5_im_p, s5_re_s, s5_im_s, m2_conv_p, m2_ssm_p, m2_conv_s, m2_ssm_s,
            gd_conv_p, gd_ssm_p, gd_conv_s, gd_ssm_s,
            k_rows_p, v_rows_p, kidx_rows_p, k_rows_s, v_rows_s, kidx_rows_s)
```

```python
import functools
import math

import numpy as np
import jax
import jax.numpy as jnp
from jax import lax
from jax.experimental import pallas as pl
from jax.experimental.pallas import tpu as pltpu

F32 = jnp.float32
BF16 = jnp.bfloat16

EPS = 1e-6
NEG = -1e30
CONV_W = 4
V7X_VMEM_LIMIT_BYTES = 56 * 1024 * 1024
LANES = 128
SUBLANES = 8
MM_TM = 1024
MM_TM_WIDE_ROWS = 512
MM_TN = 1024

S5_GROUP = 16
S5_STATE = 64
S5_CHUNK = 512
S5_SEG = S5_CHUNK // SUBLANES
S5_BLK_CH = 256
S5_BLK_ST = 1024


def _params(sem):
    return pltpu.CompilerParams(dimension_semantics=sem, vmem_limit_bytes=V7X_VMEM_LIMIT_BYTES)


def _sigmoid(x):
    return 1.0 / (1.0 + jnp.exp(-x))


def _silu(x):
    return x * _sigmoid(x)


def _gelu(x):
    return 0.5 * x * (1.0 + jnp.tanh(math.sqrt(2.0 / math.pi) * (x + 0.044715 * (x * x * x))))


def _softplus(x):
    return jnp.maximum(x, 0.0) + jnp.log1p(jnp.exp(-jnp.abs(x)))


def _dot(a, b):
    return jnp.dot(a, b, preferred_element_type=F32)


def _dot_nt(a, b):
    return lax.dot_general(a, b, (((1,), (1,)), ((), ())), preferred_element_type=F32)


def _split3(x):
    hi = x.astype(BF16)
    r1 = x - hi.astype(F32)
    mid = r1.astype(BF16)
    lo = (r1 - mid.astype(F32)).astype(BF16)
    return hi, mid, lo


def _dot_exact_lhs(sel, x):
    hi, mid, lo = _split3(x)
    return _dot(sel, hi) + (_dot(sel, mid) + _dot(sel, lo))


def _dot_f32(a, b):
    ah, am, al = _split3(a)
    bh, bm, bl = _split3(b)
    small = _dot(am, bm) + _dot(ah, bl) + _dot(al, bh)
    return _dot(ah, bh) + (_dot(ah, bm) + _dot(am, bh) + small)


def _mm_kernel(*refs, n_a, n_e, prologue, epilogue, bf16_copy):
    a_refs = refs[:n_a]
    w_ref = refs[n_a]
    e_refs = refs[n_a + 1:n_a + 1 + n_e]
    o_ref = refs[n_a + 1 + n_e]
    n_out = 2 if bf16_copy else 1
    if prologue is None:
        a = a_refs[0][...]
    else:
        a_scr = refs[n_a + 1 + n_e + n_out]

        @pl.when(pl.program_id(1) == 0)
        def _():
            a_scr[...] = prologue(*[r[...] for r in a_refs]).astype(BF16)

        a = a_scr[...]
    acc = _dot(a, w_ref[...])
    out = epilogue(acc, *[r[...] for r in e_refs])
    o_ref[...] = out.astype(o_ref.dtype)
    if bf16_copy:
        refs[n_a + 2 + n_e][...] = out.astype(BF16)


def _fused_matmul(a_ins, w, e_ins, *, prologue, epilogue, out_dtype, tm, tn, rows_per_batch=None, name,
                  bf16_copy=False):
    m = next(item[1].shape[0] for item in a_ins if item[0] == 'row')
    k, n = w.shape
    tm = min(tm, m)
    tn = next(t for t in (2048, 1024, 768, 512, 384, 256, 128) if t <= tn and n % t == 0)
    assert m % tm == 0
    rpb = rows_per_batch

    def bidx(i):
        return (i * tm) // rpb

    in_specs, args = [], []
    for item in a_ins:
        kind, arr = item[0], item[1]
        wd = item[2] if len(item) > 2 else arr.shape[-1]
        coff = item[3] if len(item) > 3 else 0
        if kind == 'row':
            in_specs.append(pl.BlockSpec((tm, wd), lambda i, j, coff=coff: (i, coff)))
        elif kind == 'vec':
            in_specs.append(pl.BlockSpec((1, wd), lambda i, j: (0, 0)))
        else:
            in_specs.append(pl.BlockSpec((None, 1, wd), lambda i, j: (bidx(i), 0, 0)))
        args.append(arr)
    in_specs.append(pl.BlockSpec((k, tn), lambda i, j: (0, j)))
    args.append(w)
    for item in e_ins:
        kind, arr = item[0], item[1]
        off = (item[2] if len(item) > 2 else 0) // tn
        if kind == 'tile':
            assert len(item) < 3 or item[2] % tn == 0
            in_specs.append(pl.BlockSpec((tm, tn), lambda i, j, off=off: (i, j + off)))
        elif kind == 'col':
            in_specs.append(pl.BlockSpec((1, tn), lambda i, j: (0, j)))
        else:
            in_specs.append(pl.BlockSpec((None, 1, tn), lambda i, j: (bidx(i), 0, j)))
        args.append(arr)
    scratch = [] if prologue is None else [pltpu.VMEM((tm, k), BF16)]
    kern = functools.partial(_mm_kernel, n_a=len(a_ins), n_e=len(e_ins), prologue=prologue, epilogue=epilogue,
                             bf16_copy=bf16_copy)
    out_shape = jax.ShapeDtypeStruct((m, n), out_dtype)
    out_spec = pl.BlockSpec((tm, tn), lambda i, j: (i, j))
    if bf16_copy:
        out_shape, out_spec = (out_shape, jax.ShapeDtypeStruct((m, n), BF16)), (out_spec, out_spec)
    return pl.pallas_call(
        kern, out_shape=out_shape, grid=(m // tm, n // tn), in_specs=in_specs, out_specs=out_spec,
        scratch_shapes=scratch, compiler_params=_params(("parallel", "arbitrary")), name=name)(*args)


def _weight_cast_kernel(*refs, n_valid, scaled, transposed):
    w_ref, o_ref = refs[0], refs[-1]
    tn = o_ref.shape[1]
    col = pl.program_id(0) * tn + lax.broadcasted_iota(jnp.int32, (1, tn), 1)
    w = w_ref[...].astype(F32)
    if transposed:
        w = w.T
    if scaled:
        w = w * refs[1][...]
    o_ref[...] = jnp.where(col < n_valid, w, 0.0).astype(o_ref.dtype)


def _weight_bf16(w, n_cols=None, pad_to=512, col_scale=None, tn=512):
    k, n_in = w.shape
    n_cols = n_in if n_cols is None else n_cols
    n_out = -(-n_cols // pad_to) * pad_to
    tn = math.gcd(n_out, tn)
    transposed = n_in % LANES != 0
    if transposed:
        args, in_specs = [w.T], [pl.BlockSpec((tn, k), lambda j: (j, 0))]
    else:
        args, in_specs = [w], [pl.BlockSpec((k, tn), lambda j: (0, j))]
    if col_scale is not None:
        args.append(jnp.pad(col_scale.astype(F32), (0, n_out - col_scale.shape[0])).reshape(1, n_out))
        in_specs.append(pl.BlockSpec((1, tn), lambda j: (0, j)))
    kern = functools.partial(_weight_cast_kernel, n_valid=n_cols, scaled=col_scale is not None, transposed=transposed)
    return pl.pallas_call(
        kern, out_shape=jax.ShapeDtypeStruct((k, n_out), BF16), grid=(n_out // tn,), in_specs=in_specs,
        out_specs=pl.BlockSpec((k, tn), lambda j: (0, j)),
        compiler_params=_params(("parallel",)), name="weight_cast")(*args)


def _pad_cols(w, mult):
    n = w.shape[-1]
    npad = (-n) % mult
    if npad:
        w = jnp.pad(w, ((0, 0), (0, npad)))
    return w


def _modnorm_prologue(x, g, scale, shift):
    r = x * lax.rsqrt(jnp.mean(x * x, axis=-1, keepdims=True) + EPS) * g
    return r * (1.0 + scale) + shift


def _identity_epilogue(acc):
    return acc


def _residual_epilogue(acc, x, gate):
    return x + gate * acc


def _residual_norm_epilogue(acc, x, gate, g):
    xn = x + gate * acc
    return xn * lax.rsqrt(jnp.mean(xn * xn, axis=-1, keepdims=True) + EPS) * g


def _in_proj(x, g, scale, shift, w, *, batch_kind, rows_per_batch, tm, tn, name, bf16_copy=False):
    kind = 'batch' if batch_kind else 'row'
    return _fused_matmul([('row', x), ('vec', g), (kind, scale), (kind, shift)], w, [],
                         prologue=_modnorm_prologue, epilogue=_identity_epilogue, out_dtype=F32,
                         tm=tm, tn=tn, rows_per_batch=rows_per_batch, name=name, bf16_copy=bf16_copy)


def _gate_proj_kernel(x_ref, g_ref, scale_ref, shift_ref, w_ref, o_ref):
    h = _modnorm_prologue(x_ref[...], g_ref[...], scale_ref[...], shift_ref[...])
    o_ref[...] = _dot_f32(h, w_ref[...])


def _gate_proj(x, g, scale, shift, w, *, batch_kind, rows_per_batch, tm, name):
    m, d = x.shape
    n = w.shape[1]
    tm = min(tm, m, MM_TM_WIDE_ROWS)
    if batch_kind:
        mod_spec = pl.BlockSpec((None, 1, d), lambda i: ((i * tm) // rows_per_batch, 0, 0))
    else:
        mod_spec = pl.BlockSpec((tm, d), lambda i: (i, 0))
    return pl.pallas_call(
        _gate_proj_kernel, out_shape=jax.ShapeDtypeStruct((m, n), F32), grid=(m // tm,),
        in_specs=[pl.BlockSpec((tm, d), lambda i: (i, 0)), pl.BlockSpec((1, d), lambda i: (0, 0)),
                  mod_spec, mod_spec, pl.BlockSpec((d, n), lambda i: (0, 0))],
        out_specs=pl.BlockSpec((tm, n), lambda i: (i, 0)),
        compiler_params=_params(("parallel",)), name=name)(x, g, scale, shift, w)


def _out_proj(a_ins, w, x, gate, *, batch_kind, rows_per_batch, tm, tn, name, prologue=None):
    kind = 'batchcol' if batch_kind else 'tile'
    return _fused_matmul(a_ins, w, [('tile', x), (kind, gate)], prologue=prologue, epilogue=_residual_epilogue,
                         out_dtype=F32, tm=tm, tn=tn, rows_per_batch=rows_per_batch, name=name)


def _mod_kernel(c_ref, w_ref, b_ref, o_ref):
    o_ref[...] = _dot(c_ref[...].astype(BF16), w_ref[...].astype(BF16)) + b_ref[...]


def _modulation(c_all, w_mod, b_mod, tn=512):
    depth, d, n = w_mod.shape
    rows = c_all.shape[0]
    return pl.pallas_call(
        _mod_kernel, out_shape=jax.ShapeDtypeStruct((depth, rows, n), F32), grid=(depth, n // tn),
        in_specs=[pl.BlockSpec((rows, d), lambda l, j: (0, 0)),
                  pl.BlockSpec((None, d, tn), lambda l, j: (l, 0, j)),
                  pl.BlockSpec((None, 1, tn), lambda l, j: (l, 0, j))],
        out_specs=pl.BlockSpec((None, rows, tn), lambda l, j: (l, 0, j)),
        compiler_params=_params(("parallel", "parallel")), name="adaln_modulation")(
            c_all, w_mod, b_mod.reshape(depth, 1, n))


def _rmsnorm_kernel(x_ref, g_ref, o_ref):
    x = x_ref[...]
    o_ref[...] = x * lax.rsqrt(jnp.mean(x * x, axis=-1, keepdims=True) + EPS) * g_ref[...]


def _final_norm(x, g, tm=512):
    m, d = x.shape
    tm = min(tm, m)
    return pl.pallas_call(
        _rmsnorm_kernel, out_shape=jax.ShapeDtypeStruct((m, d), F32), grid=(m // tm,),
        in_specs=[pl.BlockSpec((tm, d), lambda i: (i, 0)), pl.BlockSpec((1, d), lambda i: (0, 0))],
        out_specs=pl.BlockSpec((tm, d), lambda i: (i, 0)),
        compiler_params=_params(("parallel",)), name="final_rmsnorm")(x, g.reshape(1, d))


def _s5_tables(lam_re, lam_im, log_dt, b_re, b_im, c_re, c_im, d_skip):
    f = F32
    groups, p = lam_re.shape
    nblk = groups * S5_GROUP // S5_BLK_CH
    gpb = groups // nblk
    lr, li = lam_re.astype(f), lam_im.astype(f)
    dt = jnp.exp(log_dt.astype(f))[:, None]
    ldr, ldi = lr * dt, li * dt
    kk = jnp.arange(1, S5_SEG + 1, dtype=f)[:, None, None]
    pmag = jnp.exp(kk * ldr)
    pw_re, pw_im = pmag * jnp.cos(kk * ldi), pmag * jnp.sin(kk * ldi)
    ab_re, ab_im = jnp.exp(ldr) * jnp.cos(ldi), jnp.exp(ldr) * jnp.sin(ldi)
    den = lr * lr + li * li
    nr, ni = ab_re - 1.0, ab_im
    fr, fi = (nr * lr + ni * li) / den, (ni * lr - nr * li) / den
    bre, bim = b_re.astype(f), b_im.astype(f)
    bb_re = fr[..., None] * bre - fi[..., None] * bim
    bb_im = fr[..., None] * bim + fi[..., None] * bre
    eye = jnp.eye(gpb, dtype=f)

    def bd_in(bb):
        t = bb.reshape(nblk, gpb, p, S5_GROUP).transpose(0, 1, 3, 2)
        return jnp.einsum('bgkp,gh->bgkhp', t, eye).reshape(nblk, gpb * S5_GROUP, gpb * p).astype(BF16)

    def bd_out(c):
        t = c.astype(f).reshape(nblk, gpb, S5_GROUP, p).transpose(0, 1, 3, 2)
        return jnp.einsum('bgpk,gh->bgphk', t, eye).reshape(nblk, gpb * p, gpb * S5_GROUP).astype(BF16)

    def lanes(t):
        lead = t.shape[:-2]
        t = t.reshape(lead + (nblk, gpb * p))
        return jnp.moveaxis(t, -2, 0)

    return dict(
        bb_re=bd_in(bb_re), bb_im=bd_in(bb_im), c_re=bd_out(c_re), c_im=bd_out(c_im),
        ab_re=lanes(ab_re[None]), ab_im=lanes(ab_im[None]),
        pw_re=lanes(pw_re), pw_im=lanes(pw_im),
        d=d_skip.astype(f).reshape(1, -1), nblk=nblk)


def _s5_perm():
    pm = np.zeros((S5_CHUNK, S5_CHUNK), np.float32)
    r = np.arange(S5_CHUNK)
    pm[r, (r % SUBLANES) * S5_SEG + r // SUBLANES] = 1.0
    return jnp.asarray(pm, BF16), jnp.asarray(pm.T, BF16)


def _s5_scan_kernel(u_ref, pm_ref, pmt_ref, bbre_ref, bbim_ref, cre_ref, cim_ref, abre_ref, abim_ref,
                    pwre_ref, pwim_ref, d_ref, y_ref, sre_out, sim_out,
                    xre, xim, car_re, car_im, cin_re, cin_im, lend_re, lend_im):
    n = pl.program_id(2)
    nst = xre.shape[1]

    @pl.when(n == 0)
    def _():
        car_re[...] = jnp.zeros_like(car_re)
        car_im[...] = jnp.zeros_like(car_im)

    u = u_ref[...]
    up = _dot(pm_ref[...], u.astype(BF16)).astype(BF16)
    xre[...] = _dot(up, bbre_ref[...])
    xim[...] = _dot(up, bbim_ref[...])
    are = jnp.broadcast_to(abre_ref[...], (SUBLANES, nst))
    aim = jnp.broadcast_to(abim_ref[...], (SUBLANES, nst))
    sre = jnp.zeros((SUBLANES, nst), F32)
    sim = jnp.zeros((SUBLANES, nst), F32)
    for i in range(S5_SEG):
        r = slice(SUBLANES * i, SUBLANES * (i + 1))
        nre = are * sre - aim * sim + xre[r, :]
        nim = are * sim + aim * sre + xim[r, :]
        xre[r, :] = nre
        xim[r, :] = nim
        sre, sim = nre, nim
    lend_re[...] = sre
    lend_im[...] = sim
    a_re = pwre_ref[S5_SEG - 1:S5_SEG, :]
    a_im = pwim_ref[S5_SEG - 1:S5_SEG, :]
    cr, ci = car_re[...], car_im[...]
    for s in range(SUBLANES):
        cin_re[s:s + 1, :] = cr
        cin_im[s:s + 1, :] = ci
        lr, li = lend_re[s:s + 1, :], lend_im[s:s + 1, :]
        cr, ci = a_re * cr - a_im * ci + lr, a_re * ci + a_im * cr + li
    car_re[...] = cr
    car_im[...] = ci
    cinr, cini = cin_re[...], cin_im[...]
    for i in range(S5_SEG):
        r = slice(SUBLANES * i, SUBLANES * (i + 1))
        pr, pi_ = pwre_ref[i:i + 1, :], pwim_ref[i:i + 1, :]
        xre[r, :] = xre[r, :] + (pr * cinr - pi_ * cini)
        xim[r, :] = xim[r, :] + (pr * cini + pi_ * cinr)
    yp = _dot(xre[...].astype(BF16), cre_ref[...]) - _dot(xim[...].astype(BF16), cim_ref[...])
    hi = yp.astype(BF16)
    lo = (yp - hi.astype(F32)).astype(BF16)
    y = _dot(pmt_ref[...], hi) + _dot(pmt_ref[...], lo) + d_ref[...] * u
    y_ref[...] = _gelu(y)

    @pl.when(n == pl.num_programs(2) - 1)
    def _():
        sre_out[...] = cr
        sim_out[...] = ci


def _s5_scan(proj, tabs, batch, seq):
    nblk = tabs['nblk']
    nch = seq // S5_CHUNK
    pm, pmt = _s5_perm()
    nstate = nblk * S5_BLK_ST
    const3 = lambda shape: pl.BlockSpec((None,) + shape, lambda k, b, n: (k, 0, 0))
    y, sre, sim = pl.pallas_call(
        _s5_scan_kernel,
        out_shape=(jax.ShapeDtypeStruct((batch * seq, nblk * S5_BLK_CH), F32),
                   jax.ShapeDtypeStruct((batch, 1, nstate), F32),
                   jax.ShapeDtypeStruct((batch, 1, nstate), F32)),
        grid=(nblk, batch, nch),
        in_specs=[pl.BlockSpec((S5_CHUNK, S5_BLK_CH), lambda k, b, n: (b * nch + n, k)),
                  pl.BlockSpec((S5_CHUNK, S5_CHUNK), lambda k, b, n: (0, 0)),
                  pl.BlockSpec((S5_CHUNK, S5_CHUNK), lambda k, b, n: (0, 0)),
                  const3((S5_BLK_CH, S5_BLK_ST)), const3((S5_BLK_CH, S5_BLK_ST)),
                  const3((S5_BLK_ST, S5_BLK_CH)), const3((S5_BLK_ST, S5_BLK_CH)),
                  const3((1, S5_BLK_ST)), const3((1, S5_BLK_ST)),
                  const3((S5_SEG, S5_BLK_ST)), const3((S5_SEG, S5_BLK_ST)),
                  pl.BlockSpec((1, S5_BLK_CH), lambda k, b, n: (0, k))],
        out_specs=(pl.BlockSpec((S5_CHUNK, S5_BLK_CH), lambda k, b, n: (b * nch + n, k)),
                   pl.BlockSpec((None, 1, S5_BLK_ST), lambda k, b, n: (b, 0, k)),
                   pl.BlockSpec((None, 1, S5_BLK_ST), lambda k, b, n: (b, 0, k))),
        scratch_shapes=[pltpu.VMEM((S5_CHUNK, S5_BLK_ST), F32), pltpu.VMEM((S5_CHUNK, S5_BLK_ST), F32),
                        pltpu.VMEM((1, S5_BLK_ST), F32), pltpu.VMEM((1, S5_BLK_ST), F32),
                        pltpu.VMEM((SUBLANES, S5_BLK_ST), F32), pltpu.VMEM((SUBLANES, S5_BLK_ST), F32),
                        pltpu.VMEM((SUBLANES, S5_BLK_ST), F32), pltpu.VMEM((SUBLANES, S5_BLK_ST), F32)],
        compiler_params=_params(("parallel", "parallel", "arbitrary")), name="s5_scan")(
            proj, pm, pmt, tabs['bb_re'], tabs['bb_im'], tabs['c_re'], tabs['c_im'],
            tabs['ab_re'], tabs['ab_im'], tabs['pw_re'], tabs['pw_im'], tabs['d'])
    return y, sre, sim


def _s5_step_kernel(u_ref, hre_ref, him_ref, bbre_ref, bbim_ref, cre_ref, cim_ref, abre_ref, abim_ref, d_ref,
                    y_ref, sre_out, sim_out):
    u = u_ref[...]
    ub = u.astype(BF16)
    are, aim = abre_ref[...], abim_ref[...]
    hre, him = hre_ref[...], him_ref[...]
    sre = are * hre - aim * him + _dot(ub, bbre_ref[...])
    sim = are * him + aim * hre + _dot(ub, bbim_ref[...])
    sre_out[...] = sre
    sim_out[...] = sim
    y = _dot(sre.astype(BF16), cre_ref[...]) - _dot(sim.astype(BF16), cim_ref[...]) + d_ref[...] * u
    y_ref[...] = _gelu(y)


def _s5_step(proj, h_re, h_im, tabs):
    nblk = tabs['nblk']
    rows = proj.shape[0]
    nstate = nblk * S5_BLK_ST
    const3 = lambda shape: pl.BlockSpec((None,) + shape, lambda k: (k, 0, 0))
    lane_blk = lambda w: pl.BlockSpec((rows, w), lambda k: (0, k))
    return pl.pallas_call(
        _s5_step_kernel,
        out_shape=(jax.ShapeDtypeStruct((rows, nblk * S5_BLK_CH), F32),
                   jax.ShapeDtypeStruct((rows, nstate), F32), jax.ShapeDtypeStruct((rows, nstate), F32)),
        grid=(nblk,),
        in_specs=[lane_blk(S5_BLK_CH), lane_blk(S5_BLK_ST), lane_blk(S5_BLK_ST),
                  const3((S5_BLK_CH, S5_BLK_ST)), const3((S5_BLK_CH, S5_BLK_ST)),
                  const3((S5_BLK_ST, S5_BLK_CH)), const3((S5_BLK_ST, S5_BLK_CH)),
                  const3((1, S5_BLK_ST)), const3((1, S5_BLK_ST)),
                  pl.BlockSpec((1, S5_BLK_CH), lambda k: (0, k))],
        out_specs=(lane_blk(S5_BLK_CH), lane_blk(S5_BLK_ST), lane_blk(S5_BLK_ST)),
        compiler_params=_params(("parallel",)), name="s5_step")(
            proj, h_re.reshape(rows, nstate), h_im.reshape(rows, nstate),
            tabs['bb_re'], tabs['bb_im'], tabs['c_re'], tabs['c_im'], tabs['ab_re'], tabs['ab_im'], tabs['d'])


def _s5_weights(w_in, lam_re, lam_im, log_dt, b_re, b_im, c_re, c_im, d_skip, w_glu, b_glu, w_out):
    tabs = _s5_tables(lam_re, lam_im, log_dt, b_re, b_im, c_re, c_im, d_skip)
    return (_weight_bf16(w_in), _weight_bf16(w_glu), b_glu.astype(F32).reshape(1, -1), _weight_bf16(w_out), tabs)


def _glu_epilogue(acc, gy, z, b):
    return gy * _sigmoid(acc + b) * _silu(z)


def _cast_prologue(a):
    return a


def _s5_layer(x, mods, h_state, w, *, prompt, batch, seq, tm):
    g, scale, shift, gate = mods
    w_in, w_glu, b_glu, w_out, tabs = w
    width = w_glu.shape[0]
    proj = _in_proj(x, g, scale, shift, w_in, batch_kind=prompt, rows_per_batch=seq, tm=tm, tn=MM_TN, name="s5_in_proj")
    if prompt:
        gy, sre, sim = _s5_scan(proj, tabs, batch, seq)
    else:
        gy, sre, sim = _s5_step(proj, h_state[0], h_state[1], tabs)
    a = _fused_matmul([('row', gy)], w_glu, [('tile', gy), ('tile', proj, width), ('col', b_glu)],
                      prologue=_cast_prologue, epilogue=_glu_epilogue, out_dtype=BF16, tm=min(tm, MM_TM_WIDE_ROWS),
                      tn=MM_TN, name="s5_glu")
    x_new = _out_proj([('row', a)], w_out, x, gate, batch_kind=prompt, rows_per_batch=seq, tm=tm, tn=MM_TN, name="s5_out_proj")
    return x_new, sre, sim


def _conv_silu_chunk(x_ref, w_ref, b_ref, pad_ref, first):
    c = x_ref.shape[0]

    @pl.when(first)
    def _():
        pad_ref[0:SUBLANES, :] = jnp.zeros((SUBLANES, pad_ref.shape[1]), F32)

    pad_ref[SUBLANES:SUBLANES + c, :] = x_ref[...]
    acc = w_ref[3:4, :] * pad_ref[SUBLANES:SUBLANES + c, :]
    for j in range(CONV_W - 1):
        off = SUBLANES - (CONV_W - 1) + j
        acc = acc + w_ref[j:j + 1, :] * pad_ref[off:off + c, :]
    if b_ref is not None:
        acc = acc + b_ref[...]
    pad_ref[0:SUBLANES, :] = pad_ref[c:c + SUBLANES, :]
    return _silu(acc)


def _conv_silu_step(x, taps_ref, w_ref, b_ref):
    acc = w_ref[3:4, :] * x
    for j in range(CONV_W - 1):
        acc = acc + w_ref[j:j + 1, :] * taps_ref[j]
    if b_ref is not None:
        acc = acc + b_ref[...]
    return _silu(acc)


def _tri_masks(c):
    t = lax.broadcasted_iota(jnp.int32, (c, c), 0)
    s = lax.broadcasted_iota(jnp.int32, (c, c), 1)
    return s <= t, s < t


def _pad_to_square_t(x, n):
    rows = x.shape[0]
    return jnp.concatenate([x, jnp.zeros((n - rows, n), x.dtype)], axis=0).T


M2_HEADDIM = 64
M2_STATE = 128
M2_GROUPS = 8
M2_CHUNK = 128


def _m2_ssd_kernel(x_ref, b_ref, c_ref, dt_ref, z_ref, wx_ref, wb_ref, wc_ref, bx_ref, bb_ref, bc_ref,
                   dtb_ref, nega_ref, dsk_ref, nw_ref, o_ref, sout_ref, xpad, bpad, cpad, s_ref, y_ref):
    n = pl.program_id(1)
    first = n == 0
    c = x_ref.shape[0]
    npairs = s_ref.shape[0]
    pairs_per_group = npairs // M2_GROUPS

    @pl.when(first)
    def _():
        s_ref[...] = jnp.zeros_like(s_ref)

    xs = _conv_silu_chunk(x_ref, wx_ref, bx_ref, xpad, first)
    bm = _conv_silu_chunk(b_ref, wb_ref, bb_ref, bpad, first).astype(BF16)
    cm = _conv_silu_chunk(c_ref, wc_ref, bc_ref, cpad, first).astype(BF16)
    dtv = _softplus(dt_ref[...] + dtb_ref[...])
    la = nega_ref[...] * dtv
    incl, _ = _tri_masks(c)
    tri = jnp.where(incl, 1.0, 0.0).astype(BF16)
    cum = _dot_exact_lhs(tri, la)
    cum_t = cum.T
    ecum_all = jnp.exp(cum)
    wend_all = jnp.exp(cum[c - 1:c, :] - cum)
    elast_t = jnp.exp(cum_t[:, c - 1:c])
    lane_first = lax.broadcasted_iota(jnp.int32, (c, LANES), 1) < M2_HEADDIM
    row_first = lax.broadcasted_iota(jnp.int32, (LANES, LANES), 0) < M2_HEADDIM

    for g in range(M2_GROUPS):
        bg = bm[:, g * M2_STATE:(g + 1) * M2_STATE]
        cg = cm[:, g * M2_STATE:(g + 1) * M2_STATE]
        gm = _dot_nt(cg, bg)
        for j in range(pairs_per_group):
            p = g * pairs_per_group + j
            ha, hb = 2 * p, 2 * p + 1
            xp = xs[:, p * LANES:(p + 1) * LANES]

            def decay_weights(h):
                seg = cum[:, h:h + 1] - cum_t[h:h + 1, :]
                dec = jnp.where(incl, jnp.exp(jnp.where(incl, seg, 0.0)), 0.0)
                return (gm * dec).astype(BF16)

            xdt = xp * jnp.where(lane_first, dtv[:, ha:ha + 1], dtv[:, hb:hb + 1])
            xdt_a = jnp.where(lane_first, xdt, 0.0)
            xdt_b = xdt - xdt_a
            y = _dot(decay_weights(ha), xdt_a.astype(BF16)) + _dot(decay_weights(hb), xdt_b.astype(BF16))
            sp = s_ref[p]
            y = y + _dot_nt(cg, sp.astype(BF16)) * jnp.where(lane_first, ecum_all[:, ha:ha + 1], ecum_all[:, hb:hb + 1])
            y_ref[:, p * LANES:(p + 1) * LANES] = y + dsk_ref[:, p * LANES:(p + 1) * LANES] * xp
            xw = xdt * jnp.where(lane_first, wend_all[:, ha:ha + 1], wend_all[:, hb:hb + 1])
            dmat = jnp.where(row_first, elast_t[ha:ha + 1, :], elast_t[hb:hb + 1, :])
            s_ref[p] = sp * dmat + _dot(xw.T.astype(BF16), bg)

    o_ref[...] = _gated_rmsnorm_prologue(y_ref[...], z_ref[...], nw_ref[...]).astype(o_ref.dtype)

    @pl.when(n == pl.num_programs(1) - 1)
    def _():
        sout_ref[...] = s_ref[...]


def _m2_ssd(proj, gates, mw, batch, seq):
    c = M2_CHUNK
    nch = seq // c
    width = mw['width']
    gs = M2_GROUPS * M2_STATE
    npairs = width // LANES
    xo, bo, co = width // width, (2 * width) // gs, (2 * width + gs) // gs
    row = lambda w, off: pl.BlockSpec((c, w), lambda b, n: (b * nch + n, off))
    par = lambda r, w, off: pl.BlockSpec((r, w), lambda b, n: (0, off))
    return pl.pallas_call(
        _m2_ssd_kernel,
        out_shape=(jax.ShapeDtypeStruct((batch * seq, width), BF16),
                   jax.ShapeDtypeStruct((batch, npairs, LANES, M2_STATE), F32)),
        grid=(batch, nch),
        in_specs=[row(width, xo), row(gs, bo), row(gs, co), row(LANES, 0), row(width, 0),
                  par(CONV_W, width, 0), par(CONV_W, gs, width // gs), par(CONV_W, gs, width // gs + 1),
                  par(1, width, 0), par(1, gs, width // gs), par(1, gs, width // gs + 1),
                  par(1, LANES, 0), par(1, LANES, 0), par(1, width, 0), par(1, width, 0)],
        out_specs=(pl.BlockSpec((c, width), lambda b, n: (b * nch + n, 0)),
                   pl.BlockSpec((None, npairs, LANES, M2_STATE), lambda b, n: (b, 0, 0, 0))),
        scratch_shapes=[pltpu.VMEM((c + SUBLANES, width), F32), pltpu.VMEM((c + SUBLANES, gs), F32),
                        pltpu.VMEM((c + SUBLANES, gs), F32), pltpu.VMEM((npairs, LANES, M2_STATE), F32),
                        pltpu.VMEM((c, width), F32)],
        compiler_params=_params(("parallel", "arbitrary")), name="m2_ssd")(
            proj, proj, proj, gates, proj, mw['conv_w'], mw['conv_w'], mw['conv_w'], mw['conv_b'], mw['conv_b'],
            mw['conv_b'], mw['dt_bias'], mw['neg_a'], mw['d_ch'], mw['norm'])


def _m2_pre_step_kernel(xbc_ref, dt_ref, taps_ref, w_ref, b_ref, dtb_ref, nega_ref, act_ref, dtv_ref, dec_ref):
    act_ref[...] = _conv_silu_step(xbc_ref[...], taps_ref, w_ref, b_ref)
    dtv = _softplus(dt_ref[...] + dtb_ref[...])
    dtv_ref[...] = dtv
    dec_ref[...] = jnp.exp(nega_ref[...] * dtv)


def _m2_pre_step(proj, dt_raw, taps, mw):
    rows = proj.shape[0]
    width = mw['width']
    cdim = mw['conv_w'].shape[1]
    xbc = proj[:, width:width + cdim]
    return pl.pallas_call(
        _m2_pre_step_kernel,
        out_shape=(jax.ShapeDtypeStruct((rows, cdim), F32), jax.ShapeDtypeStruct((rows, LANES), F32),
                   jax.ShapeDtypeStruct((rows, LANES), F32)),
        name="m2_pre_step", compiler_params=pltpu.CompilerParams(vmem_limit_bytes=V7X_VMEM_LIMIT_BYTES))(
            xbc, dt_raw, taps, mw['conv_w'], mw['conv_b'], mw['dt_bias'], mw['neg_a'])


def _m2_state_step_kernel(x_ref, dtc_ref, decc_ref, b_ref, c_ref, dsk_ref, s_ref, y_ref, sout_ref):
    rows = x_ref.shape[0]
    pairs = x_ref.shape[1] // LANES
    bv = b_ref[...]
    cb = c_ref[...].astype(BF16)
    for j in range(pairs):
        sl = slice(j * LANES, (j + 1) * LANES)
        xp = x_ref[:, sl]
        xdt_t = _pad_to_square_t(xp * dtc_ref[:, sl], LANES)
        dec_t = _pad_to_square_t(decc_ref[:, sl], LANES)
        for b in range(rows):
            sp = s_ref[b, j]
            s_new = sp * dec_t[:, b:b + 1] + xdt_t[:, b:b + 1] * bv[b:b + 1, :]
            sout_ref[b, j] = s_new
            y_ref[b:b + 1, sl] = _dot_nt(cb[b:b + 1, :], s_new.astype(BF16)) + dsk_ref[:, sl] * xp[b:b + 1, :]


def _m2_state_step(act, dt_ch, dec_ch, ssm, mw):
    rows = act.shape[0]
    width = mw['width']
    gw = width // M2_GROUPS
    ppg = gw // LANES
    npairs = width // LANES
    gs = M2_GROUPS * M2_STATE
    s4 = ssm.reshape(rows, npairs, LANES, M2_STATE)
    blk = lambda w, base: pl.BlockSpec((rows, w), lambda g: (0, base + g))
    return pl.pallas_call(
        _m2_state_step_kernel,
        out_shape=(jax.ShapeDtypeStruct((rows, width), F32), jax.ShapeDtypeStruct(s4.shape, F32)),
        grid=(M2_GROUPS,),
        in_specs=[blk(gw, 0), blk(gw, 0), blk(gw, 0), blk(M2_STATE, width // M2_STATE),
                  blk(M2_STATE, (width + gs) // M2_STATE), pl.BlockSpec((1, gw), lambda g: (0, g)),
                  pl.BlockSpec((rows, ppg, LANES, M2_STATE), lambda g: (0, g, 0, 0))],
        out_specs=(blk(gw, 0), pl.BlockSpec((rows, ppg, LANES, M2_STATE), lambda g: (0, g, 0, 0))),
        compiler_params=_params(("parallel",)), name="m2_state_step")(
            act, dt_ch, dec_ch, act, act, mw['d_ch'], s4)


def _m2_weights(w_in, conv_w, conv_b, dt_bias, a_log, d_skip, norm_w, w_out):
    f = F32
    heads = dt_bias.shape[0]
    width = norm_w.shape[0]
    pad = LANES - heads
    cdim = conv_w.shape[1]
    return dict(
        w_in=_weight_bf16(w_in, width + cdim),
        w_gate=_pad_cols(w_in[:, width + cdim:].astype(f), LANES),
        conv_w=conv_w.astype(f), conv_b=conv_b.astype(f).reshape(1, -1),
        dt_bias=jnp.pad(dt_bias.astype(f), (0, pad)).reshape(1, LANES),
        neg_a=jnp.pad(-jnp.exp(a_log.astype(f)), (0, pad)).reshape(1, LANES),
        d_ch=jnp.repeat(d_skip.astype(f), width // heads).reshape(1, width),
        norm=norm_w.astype(f).reshape(1, width), w_out=_weight_bf16(w_out), width=width, heads=heads)


def _gated_rmsnorm_prologue(y, z, w):
    v = y * _silu(z)
    return v * lax.rsqrt(jnp.mean(v * v, axis=-1, keepdims=True) + EPS) * w


def _m2_layer(x, mods, state, mw, *, prompt, batch, seq, tm):
    g, scale, shift, gate = mods
    width, heads = mw['width'], mw['heads']
    cdim = mw['conv_w'].shape[1]
    proj = _in_proj(x, g, scale, shift, mw['w_in'], batch_kind=prompt, rows_per_batch=seq, tm=tm, tn=MM_TN,
                    name="m2_in_proj")
    gates = _gate_proj(x, g, scale, shift, mw['w_gate'], batch_kind=prompt, rows_per_batch=seq, tm=tm,
                       name="m2_gate_proj")
    if prompt:
        a, ssm = _m2_ssd(proj, gates, mw, batch, seq)
        conv_new = proj.reshape(batch, seq, -1)[:, seq - (CONV_W - 1):, width:width + cdim]
        x_new = _out_proj([('row', a)], mw['w_out'], x, gate, batch_kind=True, rows_per_batch=seq, tm=tm, tn=MM_TN,
                          name="m2_out_proj")
        return x_new, conv_new, ssm
    else:
        conv_buf, ssm_in = state
        taps = jnp.swapaxes(conv_buf.astype(F32), 0, 1)
        act, dtv, dec = _m2_pre_step(proj, gates, taps, mw)
        rep = width // heads
        dt_ch = jnp.repeat(dtv[:, :heads], rep, axis=1)
        dec_ch = jnp.repeat(dec[:, :heads], rep, axis=1)
        y, ssm = _m2_state_step(act, dt_ch, dec_ch, ssm_in.astype(F32), mw)
        conv_new = jnp.concatenate([conv_buf.astype(F32)[:, 1:], proj[:, None, width:width + cdim]], axis=1)
    x_new = _out_proj([('row', y), ('row', proj, width, 0), ('vec', mw['norm'])], mw['w_out'], x, gate,
                      batch_kind=prompt, rows_per_batch=seq, tm=min(tm, 256), tn=MM_TN, name="m2_out_proj",
                      prologue=_gated_rmsnorm_prologue)
    return x_new, conv_new, ssm


GD_DK = 128
GD_DV = 128
GD_CHUNK = 64


def _dot_3pass(a, b):
    ah = a.astype(BF16)
    al = (a - ah.astype(F32)).astype(BF16)
    bh = b.astype(BF16)
    bl = (b - bh.astype(F32)).astype(BF16)
    return _dot(ah, bh) + (_dot(ah, bl) + _dot(al, bh))


def _l2norm_rows(x):
    return x * lax.rsqrt(jnp.sum(x * x, axis=-1, keepdims=True) + EPS)


def _rmsnorm_rows(x, w):
    return x * lax.rsqrt(jnp.mean(x * x, axis=-1, keepdims=True) + EPS) * w


GD_INV_BASE = 16


def _bdot(a, b):
    return jnp.einsum('hmk,hkn->hmn', a, b, preferred_element_type=F32)


def _bdot_nt(a, b):
    return jnp.einsum('hmk,hnk->hmn', a, b, preferred_element_type=F32)


def _split2(x):
    hi = x.astype(BF16)
    return hi, (x - hi.astype(F32)).astype(BF16)


def _bdot_3pass(a, b):
    (ah, al), (bh, bl) = a, b
    return _bdot(ah, bh) + (_bdot(ah, bl) + _bdot(al, bh))


def _unit_lower_inverse(a_strict):
    c = a_strict.shape[-1]
    row = lax.broadcasted_iota(jnp.int32, (c, c), 0)
    col = lax.broadcasted_iota(jnp.int32, (c, c), 1)
    eye = jnp.where(row == col, 1.0, 0.0)
    blk = GD_INV_BASE
    shift = int(math.log2(blk))
    p = jnp.where((row >> shift) == (col >> shift), -a_strict, 0.0)
    t = eye + p
    ps = _split2(p)
    for _ in range(shift - 1):
        p = _bdot_3pass(ps, ps)
        ps = _split2(p)
        t = t + _bdot_3pass(_split2(t), ps)
    while blk < c:
        below = jnp.logical_and((row >> (shift + 1)) == (col >> (shift + 1)), (row >> shift) != (col >> shift))
        ts = _split2(t)
        tb = _bdot_3pass(ts, _split2(jnp.where(below, a_strict, 0.0)))
        t = t - _bdot_3pass(_split2(tb), ts)
        blk *= 2
        shift += 1
    return t


def _gd_chunk_kernel(qkv_ref, z_ref, braw_ref, araw_ref, cw_ref, nega_ref, dtb_ref, nw_ref,
                     o_ref, sout_ref, pad, s_ref):
    n = pl.program_id(1)
    first = n == 0
    c = qkv_ref.shape[0]
    hv = s_ref.shape[0]
    hk = hv // 2
    rep = hv // hk

    @pl.when(first)
    def _():
        s_ref[...] = jnp.zeros_like(s_ref)

    qkv = _conv_silu_chunk(qkv_ref, cw_ref, None, pad, first)
    beta = _sigmoid(braw_ref[...])
    gl = nega_ref[...] * _softplus(araw_ref[...] + dtb_ref[...])
    incl, strict = _tri_masks(c)
    tri = jnp.where(incl, 1.0, 0.0).astype(BF16)
    gcum = _dot_exact_lhs(tri, gl)
    gcum_t = jnp.concatenate([gcum, jnp.zeros((LANES - c, LANES), F32)], axis=0).T

    heads = range(hv)
    per_value_head = lambda t: jnp.stack([t[h // rep] for h in heads])
    q3 = jnp.stack([qkv[:, i * GD_DK:(i + 1) * GD_DK] for i in range(hk)])
    k3 = jnp.stack([qkv[:, (hk + i) * GD_DK:(hk + i + 1) * GD_DK] for i in range(hk)])
    v3 = jnp.stack([qkv[:, (2 * hk + h) * GD_DV:(2 * hk + h + 1) * GD_DV] for h in heads])
    q3 = _l2norm_rows(q3) * (GD_DK ** -0.5)
    k3 = _l2norm_rows(k3)
    k3b = k3.astype(BF16)
    kk = per_value_head(_bdot_nt(k3b, k3b))
    qk = per_value_head(_bdot_nt(q3.astype(BF16), k3b))
    q_v, k_v = per_value_head(q3), per_value_head(k3)
    colv = jnp.stack([gcum[:, h:h + 1] for h in heads])
    rowv = jnp.stack([gcum_t[h:h + 1, :c] for h in heads])
    bcol = jnp.stack([beta[:, h:h + 1] for h in heads])
    glast = colv[:, c - 1:c, :]
    ecol = jnp.exp(colv)
    dec = jnp.where(incl, jnp.exp(jnp.where(incl, colv - rowv, 0.0)), 0.0)
    a = jnp.where(strict, (bcol * kk) * dec, 0.0)
    tinv = _unit_lower_inverse(a)
    rhs = jnp.concatenate([v3 * bcol, (k_v * bcol) * ecol], axis=-1)
    th, tl = _split2(tinv)
    rb = rhs.astype(BF16)
    sol = _bdot(th, rb) + _bdot(tl, rb)
    u, w = sol[:, :, :GD_DV], sol[:, :, GD_DV:]
    s = s_ref[...]
    sb = s.astype(BF16)
    v_new = u - _bdot(w.astype(BF16), sb)
    o = _bdot((q_v * ecol).astype(BF16), sb) + _bdot((qk * dec).astype(BF16), v_new.astype(BF16))
    zpad = jnp.zeros((hv, LANES - c, GD_DV), F32)
    kd_t = jnp.swapaxes(jnp.concatenate([k_v * jnp.exp(glast - colv), zpad], axis=1), 1, 2)
    vn_pad = jnp.concatenate([v_new, zpad], axis=1)
    s_ref[...] = s * jnp.exp(glast) + _bdot(kd_t.astype(BF16), vn_pad.astype(BF16))
    on = _rmsnorm_rows(o, nw_ref[...])
    for h in heads:
        sl = slice(h * GD_DV, (h + 1) * GD_DV)
        o_ref[:, sl] = (on[h] * _silu(z_ref[:, sl])).astype(o_ref.dtype)

    @pl.when(n == pl.num_programs(1) - 1)
    def _():
        sout_ref[...] = s_ref[...]


def _gd_chunked(proj, gates, gw, batch, seq):
    c = GD_CHUNK
    nch = seq // c
    cdim, width, hv = gw['cdim'], gw['width'], gw['hv']
    row = lambda w, off: pl.BlockSpec((c, w), lambda b, n: (b * nch + n, off))
    par = lambda r, w: pl.BlockSpec((r, w), lambda b, n: (0, 0))
    return pl.pallas_call(
        _gd_chunk_kernel,
        out_shape=(jax.ShapeDtypeStruct((batch * seq, width), BF16),
                   jax.ShapeDtypeStruct((batch, hv, GD_DK, GD_DV), F32)),
        grid=(batch, nch),
        in_specs=[row(cdim, 0), row(width, cdim // width), row(LANES, 0), row(LANES, 1),
                  par(CONV_W, cdim), par(1, LANES), par(1, LANES), par(1, GD_DV)],
        out_specs=(pl.BlockSpec((c, width), lambda b, n: (b * nch + n, 0)),
                   pl.BlockSpec((None, hv, GD_DK, GD_DV), lambda b, n: (b, 0, 0, 0))),
        scratch_shapes=[pltpu.VMEM((c + SUBLANES, cdim), F32), pltpu.VMEM((hv, GD_DK, GD_DV), F32)],
        compiler_params=_params(("parallel", "arbitrary")), name="gd_chunked")(
            proj, proj, gates, gates, gw['conv_w'], gw['neg_a'], gw['dt_bias'], gw['norm'])


def _gd_pre_step_kernel(qkv_ref, braw_ref, araw_ref, taps_ref, cw_ref, nega_ref, dtb_ref,
                        q_ref, k_ref, v_ref, beta_ref, eg_ref):
    hk = q_ref.shape[1] // GD_DK
    act = _conv_silu_step(qkv_ref[...], taps_ref, cw_ref, None)
    for kh in range(hk):
        sl = slice(kh * GD_DK, (kh + 1) * GD_DK)
        q_ref[:, sl] = _l2norm_rows(act[:, kh * GD_DK:(kh + 1) * GD_DK]) * (GD_DK ** -0.5)
        k_ref[:, sl] = _l2norm_rows(act[:, (hk + kh) * GD_DK:(hk + kh + 1) * GD_DK])
    v_ref[...] = act[:, 2 * hk * GD_DK:]
    beta_ref[...] = _sigmoid(braw_ref[...])
    eg_ref[...] = jnp.exp(nega_ref[...] * _softplus(araw_ref[...] + dtb_ref[...]))


def _gd_pre_step(proj, gates, taps, gw):
    rows = proj.shape[0]
    cdim, width, hv = gw['cdim'], gw['width'], gw['hv']
    qk_w = (cdim - width) // 2
    sd = lambda w: jax.ShapeDtypeStruct((rows, w), F32)
    return pl.pallas_call(
        _gd_pre_step_kernel, out_shape=(sd(qk_w), sd(qk_w), sd(width), sd(LANES), sd(LANES)),
        name="gd_pre_step", compiler_params=pltpu.CompilerParams(vmem_limit_bytes=V7X_VMEM_LIMIT_BYTES))(
            proj[:, :cdim], gates[:, :LANES], gates[:, LANES:], taps,
            gw['conv_w'], gw['neg_a'], gw['dt_bias'])


def _gd_state_step_kernel(q_ref, k_ref, v_ref, beta_ref, eg_ref, z_ref, nw_ref, s_ref, o_ref, sout_ref):
    rows = q_ref.shape[0]
    nk = q_ref.shape[1] // GD_DK
    rep = (v_ref.shape[1] // GD_DV) // nk
    nw = nw_ref[...]
    zrows = jnp.zeros((SUBLANES - 2, GD_DK), F32)
    for kh in range(nk):
        ksl = slice(kh * GD_DK, (kh + 1) * GD_DK)
        q8, k8 = q_ref[:, ksl], k_ref[:, ksl]
        k_t = _pad_to_square_t(k8, GD_DK)
        for b in range(rows):
            qb, kb = q8[b:b + 1, :], k8[b:b + 1, :]
            kq = jnp.concatenate([kb, qb, zrows], axis=0).astype(BF16)
            qk = jnp.sum(qb * kb, axis=-1, keepdims=True)
            for r in range(rep):
                h = kh * rep + r
                vsl = slice(h * GD_DV, (h + 1) * GD_DV)
                s = s_ref[b, h]
                ks_qs = _dot(kq, s.astype(BF16))
                eg = eg_ref[b:b + 1, vsl]
                beta = beta_ref[b:b + 1, vsl]
                v_new = beta * (v_ref[b:b + 1, vsl] - eg * ks_qs[0:1, :])
                o = eg * ks_qs[1:2, :] + qk * v_new
                sout_ref[b, h] = s * eg[:, 0:1] + k_t[:, b:b + 1] * v_new
                o_ref[b:b + 1, vsl] = _rmsnorm_rows(o, nw) * _silu(z_ref[b:b + 1, vsl])


def _gd_state_step(proj, qn, kn, v, beta_ch, eg_ch, state, gw, heads_per_step=4):
    rows = qn.shape[0]
    cdim, width, hv = gw['cdim'], gw['width'], gw['hv']
    steps = hv // heads_per_step
    kw = qn.shape[1] // steps
    vw = width // steps
    blk = lambda w, base=0: pl.BlockSpec((rows, w), lambda g, base=base: (0, base + g))
    sspec = pl.BlockSpec((rows, heads_per_step, GD_DK, GD_DV), lambda g: (0, g, 0, 0))
    return pl.pallas_call(
        _gd_state_step_kernel,
        out_shape=(jax.ShapeDtypeStruct((rows, width), F32), jax.ShapeDtypeStruct(state.shape, F32)),
        grid=(steps,),
        in_specs=[blk(kw), blk(kw), blk(vw), blk(vw), blk(vw), blk(vw, cdim // vw),
                  pl.BlockSpec((1, GD_DV), lambda g: (0, 0)), sspec],
        out_specs=(blk(vw), sspec),
        compiler_params=_params(("parallel",)), name="gd_state_step")(
            qn, kn, v, beta_ch, eg_ch, proj, gw['norm'], state)


def _gd_weights(w_in, conv_w, a_log, dt_bias, norm_w, w_out):
    f = F32
    hv = a_log.shape[0]
    cdim = conv_w.shape[1]
    width = w_out.shape[0]
    pad = LANES - hv
    base = cdim + width
    zeros = jnp.zeros((w_in.shape[0], pad), w_in.dtype)
    w_gate = jnp.concatenate([w_in[:, base:base + hv], zeros, w_in[:, base + hv:], zeros], axis=1)
    return dict(
        w_in=_weight_bf16(w_in, base), w_gate=w_gate.astype(f), conv_w=conv_w.astype(f),
        neg_a=jnp.pad(-jnp.exp(a_log.astype(f)), (0, pad)).reshape(1, LANES),
        dt_bias=jnp.pad(dt_bias.astype(f), (0, pad)).reshape(1, LANES),
        norm=norm_w.astype(f).reshape(1, -1), w_out=_weight_bf16(w_out), cdim=cdim, width=width, hv=hv)


def _gd_layer(x, mods, state, gw, *, prompt, batch, seq, tm):
    g, scale, shift, gate = mods
    cdim, width, hv = gw['cdim'], gw['width'], gw['hv']
    proj = _in_proj(x, g, scale, shift, gw['w_in'], batch_kind=prompt, rows_per_batch=seq, tm=tm, tn=MM_TN,
                    name="gd_in_proj")
    gates = _gate_proj(x, g, scale, shift, gw['w_gate'], batch_kind=prompt, rows_per_batch=seq, tm=tm,
                       name="gd_gate_proj")
    if prompt:
        a, ssm = _gd_chunked(proj, gates, gw, batch, seq)
        conv_new = proj.reshape(batch, seq, -1)[:, seq - (CONV_W - 1):, :cdim]
        x_new = _out_proj([('row', a)], gw['w_out'], x, gate, batch_kind=True, rows_per_batch=seq, tm=tm, tn=MM_TN,
                          name="gd_out_proj")
    else:
        conv_buf, ssm_in = state
        taps = jnp.swapaxes(conv_buf.astype(F32), 0, 1)
        qn, kn, v, beta, eg = _gd_pre_step(proj, gates, taps, gw)
        beta_ch = jnp.repeat(beta[:, :hv], GD_DV, axis=1)
        eg_ch = jnp.repeat(eg[:, :hv], GD_DV, axis=1)
        a, ssm = _gd_state_step(proj, qn, kn, v, beta_ch, eg_ch, ssm_in.astype(F32), gw)
        conv_new = jnp.concatenate([conv_buf.astype(F32)[:, 1:], proj[:, None, :cdim]], axis=1)
        x_new = _out_proj([('row', a)], gw['w_out'], x, gate, batch_kind=False, rows_per_batch=seq, tm=tm, tn=MM_TN,
                          name="gd_out_proj", prologue=_cast_prologue)
    return x_new, conv_new, ssm


AT_DIM = 128
IDX_DIM = 128
TOPK_MAX = 256
REL_BUCKETS = 32
REL_MAX_DIST = 128
AT_TILE = 256
INT32_MIN = -2 ** 31
_NEG_BITS = int(np.float32(NEG).view(np.int32))
NEG_SORT_KEY = _NEG_BITS ^ 0x7FFFFFFF if _NEG_BITS < 0 else _NEG_BITS


def _bucket_starts():
    d = np.arange(0, REL_MAX_DIST + 1)
    exact = REL_BUCKETS // 2
    far = exact + (np.log(np.maximum(d, 1).astype(np.float32) / exact) / math.log(REL_MAX_DIST / exact)
                   * (REL_BUCKETS - exact)).astype(np.int32)
    bucket = np.where(d < exact, d, np.minimum(far, REL_BUCKETS - 1))
    assert np.all(np.diff(bucket) >= 0) and bucket[-1] == REL_BUCKETS - 1
    return [int(np.argmax(bucket >= b)) for b in range(REL_BUCKETS)]


def _bias_from_dist(dist, value_of_bucket):
    starts = _bucket_starts()
    val = value_of_bucket(REL_BUCKETS - 1)
    for b in range(REL_BUCKETS - 2, -1, -1):
        val = jnp.where(dist < starts[b + 1], value_of_bucket(b), val)
    return val


def _sort_key(x):
    x = jnp.where(x == 0.0, 0.0, x)
    b = pltpu.bitcast(x, jnp.int32)
    return jnp.where(b < 0, b ^ jnp.int32(0x7FFFFFFF), b)


def _kth_largest_key(count_ge, shape, k):
    def body(it, ans):
        cand = ans | jnp.left_shift(jnp.int32(1), 31 - it)
        cnt = count_ge(cand ^ jnp.int32(INT32_MIN))
        return jnp.where(cnt >= k, cand, ans)

    ans = lax.fori_loop(0, 32, body, jnp.zeros(shape, jnp.int32))
    return ans ^ jnp.int32(INT32_MIN)


def _relbias_tiles_kernel(rb_ref, o_ref):
    delta = pl.program_id(0) * AT_TILE
    h = pl.program_id(1)
    i = lax.broadcasted_iota(jnp.int32, (AT_TILE, AT_TILE), 0)
    j = lax.broadcasted_iota(jnp.int32, (AT_TILE, AT_TILE), 1)
    o_ref[...] = _bias_from_dist(delta + i - j, lambda b: rb_ref[b, h]) - rb_ref[REL_BUCKETS - 1, h]


def _relbias_tiles(rel_bias):
    heads = rel_bias.shape[1]
    ntile = 2
    assert ntile * AT_TILE - (AT_TILE - 1) >= REL_MAX_DIST
    return pl.pallas_call(
        _relbias_tiles_kernel, out_shape=jax.ShapeDtypeStruct((ntile, heads, AT_TILE, AT_TILE), F32),
        grid=(ntile, heads),
        in_specs=[pl.BlockSpec(memory_space=pltpu.SMEM)],
        out_specs=pl.BlockSpec((None, None, AT_TILE, AT_TILE), lambda d, h: (d, h, 0, 0)),
        compiler_params=_params(("parallel", "parallel")), name="at_relbias_tiles")(rel_bias.astype(F32))


def _at_index_kernel(qi_ref, wi_ref, ki_ref, o_ref, keys, cnt, *, k_sel, score_scale):
    qb = pl.program_id(1)
    tq = qi_ref.shape[0]
    nkb = keys.shape[0]
    tk = keys.shape[2]
    nh = qi_ref.shape[1] // IDX_DIM
    wsc = wi_ref[...] * score_scale
    qpos = qb * tq + lax.broadcasted_iota(jnp.int32, (tq, tk), 0)
    kloc = lax.broadcasted_iota(jnp.int32, (tq, tk), 1)
    neg_key = _sort_key(jnp.full((tq, tk), NEG, F32))

    for kb in range(nkb):
        @pl.when(kb <= qb)
        def _():
            kblk = ki_ref[kb * tk:(kb + 1) * tk, :].astype(BF16)
            sc = jnp.zeros((tq, tk), F32)
            for h in range(nh):
                d = _dot_nt(qi_ref[:, h * IDX_DIM:(h + 1) * IDX_DIM].astype(BF16), kblk)
                sc = sc + wsc[:, h:h + 1] * jnp.maximum(d, 0.0)
            adm = kb * tk + kloc <= qpos
            keys[kb] = _sort_key(jnp.where(adm, sc, NEG))

        @pl.when(kb > qb)
        def _():
            keys[kb] = neg_key

    def count_ge(t):
        cnt[...] = jnp.where(keys[0] >= t, 1, 0)
        for kb in range(1, nkb):
            @pl.when(kb <= qb)
            def _():
                cnt[...] += jnp.where(keys[kb] >= t, 1, 0)
        beyond = (nkb - 1 - qb) * tk
        return jnp.sum(cnt[...], axis=1, keepdims=True) + jnp.where(t <= NEG_SORT_KEY, beyond, 0)

    thr = _kth_largest_key(count_ge, (tq, 1), k_sel)
    n_ge = count_ge(thr)
    has_ties = jnp.max(n_ge) > k_sel

    @pl.when(jnp.logical_not(has_ties))
    def _():
        for kb in range(nkb):
            adm = kb * tk + kloc <= qpos
            sel = jnp.logical_and(keys[kb] >= thr, adm)
            o_ref[kb] = jnp.where(sel, 0.0, MASKED).T.astype(o_ref.dtype)

    @pl.when(has_ties)
    def _():
        acc = jnp.zeros((tq, tk), jnp.int32)
        for kb in range(nkb):
            acc = acc + jnp.where(keys[kb] > thr, 1, 0)
        room = (k_sel - jnp.sum(acc, axis=1, keepdims=True)).astype(F32)
        upper = jnp.where(lax.broadcasted_iota(jnp.int32, (tk, tk), 0) <= lax.broadcasted_iota(jnp.int32, (tk, tk), 1),
                          1.0, 0.0).astype(BF16)
        seen = jnp.zeros((tq, 1), F32)
        for kb in range(nkb):
            key = keys[kb]
            eq = key == thr
            eqf = jnp.where(eq, 1.0, 0.0)
            rank = seen + _dot(eqf.astype(BF16), upper)
            seen = seen + jnp.sum(eqf, axis=1, keepdims=True)
            adm = kb * tk + kloc <= qpos
            sel = jnp.logical_and(jnp.logical_or(key > thr, jnp.logical_and(eq, rank <= room)), adm)
            o_ref[kb] = jnp.where(sel, 0.0, MASKED).T.astype(o_ref.dtype)


def _at_index(proj, aw, batch, seq, k_sel):
    tq = tk = AT_TILE
    nq = seq // tq
    width = aw['width']
    nh = aw['idx_heads']
    qio = (4 * width) // (nh * IDX_DIM)
    kio = (4 * width + nh * IDX_DIM) // IDX_DIM
    kern = functools.partial(_at_index_kernel, k_sel=k_sel, score_scale=IDX_DIM ** -0.5 * nh ** -0.5)
    return pl.pallas_call(
        kern, out_shape=jax.ShapeDtypeStruct((batch * nq, seq // tk, tq, tk), BF16), grid=(batch, nq),
        in_specs=[pl.BlockSpec((tq, nh * IDX_DIM), lambda b, q: (b * nq + q, qio)),
                  pl.BlockSpec((tq, LANES), lambda b, q: (b * nq + q, kio + 1)),
                  pl.BlockSpec((seq, IDX_DIM), lambda b, q: (b, kio))],
        out_specs=pl.BlockSpec((None, seq // tk, tq, tk), lambda b, q: (b * nq + q, 0, 0, 0)),
        scratch_shapes=[pltpu.VMEM((seq // tk, tq, tk), jnp.int32), pltpu.VMEM((tq, tk), jnp.int32)],
        compiler_params=_params(("parallel", "parallel")), name="at_index")(proj, proj, proj)


AT_HEAD_GROUP = 4
MASKED = 2.0 * NEG


def _at_attend_kernel(q_ref, k_ref, vt_ref, z_ref, mask_ref, bias_ref, o_ref, acc, m_scr, l_scr):
    qb = pl.program_id(2)
    t = q_ref.shape[0]
    hg = q_ref.shape[1] // AT_DIM
    acc[...] = jnp.zeros_like(acc)
    m_scr[...] = jnp.full(m_scr.shape, NEG, F32)
    l_scr[...] = jnp.zeros_like(l_scr)

    heads = [slice(h * AT_DIM, (h + 1) * AT_DIM) for h in range(hg)]
    q3t = jnp.stack([q_ref[:, sl].T for sl in heads]).astype(BF16)

    def key_tile(kb, bias):
        rows = pl.ds(pl.multiple_of(kb * t, t), t)
        k3 = jnp.stack([k_ref[rows, sl] for sl in heads])
        s_t = _bdot(k3, q3t) + mask_ref[kb].astype(F32)
        if bias is not None:
            s_t = s_t + bias
        m_old = m_scr[...]
        m_new = jnp.maximum(m_old, jnp.max(s_t, axis=1, keepdims=True))
        alpha = jnp.exp(m_old - m_new)
        p_t = jnp.exp(s_t - m_new)
        l_scr[...] = alpha * l_scr[...] + jnp.sum(p_t, axis=1, keepdims=True)
        acc[...] = alpha * acc[...] + _bdot(vt_ref[:, kb], p_t.astype(BF16))
        m_scr[...] = m_new

    def far_tile(kb, carry):
        key_tile(kb, None)
        return carry

    lax.fori_loop(0, jnp.maximum(qb - 1, 0), far_tile, 0)

    @pl.when(qb >= 1)
    def _():
        key_tile(qb - 1, bias_ref[1])

    key_tile(qb, bias_ref[0])
    o_t = acc[...] / l_scr[...]
    for h, sl in enumerate(heads):
        o_ref[:, sl] = (o_t[h].T * _silu(z_ref[:, sl])).astype(o_ref.dtype)


def _at_attend(proj, proj_bf, maskadd_t, tiles_t, aw, batch, seq):
    t = AT_TILE
    nq = seq // t
    width = aw['width']
    heads = width // AT_DIM
    hg = AT_HEAD_GROUP
    gw = hg * AT_DIM
    ng = width // gw
    v_t = proj_bf[:, 2 * width:3 * width].reshape(batch, nq, t, heads, AT_DIM).transpose(0, 3, 1, 4, 2)
    return pl.pallas_call(
        _at_attend_kernel, out_shape=jax.ShapeDtypeStruct((batch * seq, width), BF16), grid=(batch, ng, nq),
        in_specs=[pl.BlockSpec((t, gw), lambda b, g, q: (b * nq + q, g)),
                  pl.BlockSpec((seq, gw), lambda b, g, q: (b, ng + g)),
                  pl.BlockSpec((None, hg, nq, AT_DIM, t), lambda b, g, q: (b, g, 0, 0, 0)),
                  pl.BlockSpec((t, gw), lambda b, g, q: (b * nq + q, 3 * ng + g)),
                  pl.BlockSpec((None, nq, t, t), lambda b, g, q: (b * nq + q, 0, 0, 0)),
                  pl.BlockSpec((2, hg, t, t), lambda b, g, q: (0, g, 0, 0))],
        out_specs=pl.BlockSpec((t, gw), lambda b, g, q: (b * nq + q, g)),
        scratch_shapes=[pltpu.VMEM((hg, AT_DIM, t), F32), pltpu.VMEM((hg, 1, t), F32), pltpu.VMEM((hg, 1, t), F32)],
        compiler_params=_params(("parallel", "parallel", "arbitrary")), name="at_attend")(
            proj, proj_bf, v_t, proj, maskadd_t, tiles_t)


def _at_weights(w_in, rel_bias, w_out):
    width = w_out.shape[0]
    heads = rel_bias.shape[1]
    idx_heads = (w_in.shape[1] - 4 * width - IDX_DIM) // (IDX_DIM + 1)
    col_scale = jnp.where(jnp.arange(w_in.shape[1]) < width, AT_DIM ** -0.5, 1.0).astype(F32)
    return dict(w_in=_weight_bf16(w_in, col_scale=col_scale), rel_bias=rel_bias.astype(F32),
                w_out=_weight_bf16(w_out), width=width, heads=heads, idx_heads=idx_heads)


def _at_rows_kernel(k_ref, v_ref, ki_ref, ko_ref, vo_ref, kio_ref):
    nh = ko_ref.shape[1]
    heads = lambda x: jnp.stack([x[:, h * AT_DIM:(h + 1) * AT_DIM] for h in range(nh)], axis=1)
    ko_ref[...] = heads(k_ref[...])
    vo_ref[...] = heads(v_ref[...])
    kio_ref[...] = ki_ref[...]


def _at_rows(proj, aw, lead, tm=256):
    width, heads = aw['width'], aw['heads']
    m = proj.shape[0]
    tm = min(tm, m)
    kio = (4 * width + aw['idx_heads'] * IDX_DIM) // IDX_DIM
    row4 = pl.BlockSpec((tm, heads, AT_DIM), lambda i: (i, 0, 0))
    k, v, ki = pl.pallas_call(
        _at_rows_kernel,
        out_shape=(jax.ShapeDtypeStruct((m, heads, AT_DIM), F32), jax.ShapeDtypeStruct((m, heads, AT_DIM), F32),
                   jax.ShapeDtypeStruct((m, IDX_DIM), F32)),
        grid=(m // tm,),
        in_specs=[pl.BlockSpec((tm, width), lambda i: (i, 1)), pl.BlockSpec((tm, width), lambda i: (i, 2)),
                  pl.BlockSpec((tm, IDX_DIM), lambda i: (i, kio))],
        out_specs=(row4, row4, pl.BlockSpec((tm, IDX_DIM), lambda i: (i, 0))),
        compiler_params=_params(("parallel",)), name="at_rows")(proj, proj, proj)
    return (k.reshape(lead + (heads, AT_DIM)), v.reshape(lead + (heads, AT_DIM)), ki.reshape(lead + (IDX_DIM,)))


def _at_layer_prompt(x, mods, aw, final_g, *, batch, seq, tm):
    g, scale, shift, gate = mods
    proj, proj_bf = _in_proj(x, g, scale, shift, aw['w_in'], batch_kind=True, rows_per_batch=seq, tm=tm, tn=MM_TN,
                             name="at_in_proj", bf16_copy=True)
    k_sel = min(TOPK_MAX, seq // 4)
    maskadd = _at_index(proj, aw, batch, seq, k_sel)
    tiles = _relbias_tiles(aw['rel_bias'])
    a = _at_attend(proj, proj_bf, maskadd, jnp.swapaxes(tiles, 2, 3), aw, batch, seq)
    d = x.shape[1]
    y = _fused_matmul([('row', a)], aw['w_out'], [('tile', x), ('batchcol', gate), ('col', final_g.reshape(1, d))],
                      prologue=None, epilogue=_residual_norm_epilogue, out_dtype=F32, tm=min(tm, MM_TM_WIDE_ROWS),
                      tn=d, rows_per_batch=seq, name="at_out_proj_norm")
    return (y,) + _at_rows(proj, aw, (batch, seq))


AT_PAGES_PER_STEP = 16


def _at_page_scores_kernel(pt_ref, qi_ref, w_ref, kidx_ref, o_ref, kbuf, sem):
    b, j = pl.program_id(0), pl.program_id(1)
    nj = pl.num_programs(1)
    npg = kbuf.shape[1]
    step = b * nj + j
    last_step = pl.num_programs(0) * nj - 1

    def page_copy(s, i, slot):
        sb, sj = s // nj, s % nj
        return pltpu.make_async_copy(kidx_ref.at[pt_ref[sb, sj * npg + i]], kbuf.at[slot, i], sem.at[slot])

    def start_all(s, slot):
        for i in range(npg):
            page_copy(s, i, slot).start()

    slot = step % 2

    @pl.when(step == 0)
    def _():
        start_all(step, slot)

    @pl.when(step < last_step)
    def _():
        start_all(step + 1, 1 - slot)

    for i in range(npg):
        page_copy(step, i, slot).wait()
    qi = qi_ref[...].astype(BF16)
    w = w_ref[...]
    for i in range(npg):
        d = _dot_nt(qi, kbuf[slot, i].astype(BF16))
        o_ref[i:i + 1, :] = jnp.sum(w * jnp.maximum(d, 0.0), axis=0, keepdims=True)


def _at_page_scores(qi3, w3, cache_kidx, page_table):
    rows, nh, _ = qi3.shape
    npages = page_table.shape[1]
    page = cache_kidx.shape[1]
    npg = math.gcd(npages, AT_PAGES_PER_STEP)
    grid_spec = pltpu.PrefetchScalarGridSpec(
        num_scalar_prefetch=1, grid=(rows, npages // npg),
        in_specs=[pl.BlockSpec((None, nh, IDX_DIM), lambda b, j, pt: (b, 0, 0)),
                  pl.BlockSpec((None, nh, IDX_DIM), lambda b, j, pt: (b, 0, 0)),
                  pl.BlockSpec(memory_space=pl.ANY)],
        out_specs=pl.BlockSpec((None, npg, page), lambda b, j, pt: (b, j, 0)),
        scratch_shapes=[pltpu.VMEM((2, npg, page, IDX_DIM), F32), pltpu.SemaphoreType.DMA((2,))])
    return pl.pallas_call(
        _at_page_scores_kernel, out_shape=jax.ShapeDtypeStruct((rows, npages, page), F32), grid_spec=grid_spec,
        compiler_params=_params(("arbitrary", "arbitrary")), name="at_page_scores")(page_table, qi3, w3, cache_kidx)


def _at_sample_select_kernel(sc_ref, qi_ref, w_ref, kin_ref, gidx_ref, newadd_ref, rank_scr, *, k_sel):
    npages, page = sc_ref.shape
    upper = jnp.where(lax.broadcasted_iota(jnp.int32, (page, page), 0) <= lax.broadcasted_iota(jnp.int32, (page, page), 1),
                      1.0, 0.0).astype(BF16)
    lower = jnp.where(lax.broadcasted_iota(jnp.int32, (npages, npages), 1) < lax.broadcasted_iota(jnp.int32, (npages, npages), 0),
                      1.0, 0.0).astype(BF16)

    def total(x):
        return jnp.sum(jnp.sum(x, axis=1, keepdims=True), axis=0, keepdims=True)

    def position_rank(flags):
        row_cnt = jnp.broadcast_to(jnp.sum(flags, axis=1, keepdims=True), (npages, page))
        return _dot(lower, row_cnt.astype(BF16)) + _dot(flags.astype(BF16), upper)

    keys = _sort_key(sc_ref[...])
    dots = jnp.sum(qi_ref[...] * kin_ref[...], axis=1, keepdims=True)
    s_new = jnp.sum(w_ref[:, 0:1] * jnp.maximum(dots, 0.0), axis=0, keepdims=True)
    key_new = _sort_key(s_new)

    def count_ge(t):
        return total(jnp.where(keys >= t, 1, 0)) + jnp.where(key_new >= t, 1, 0)

    thr = _kth_largest_key(count_ge, (1, 1), k_sel)
    n_gt = total(jnp.where(keys > thr, 1.0, 0.0)) + jnp.where(key_new > thr, 1.0, 0.0)
    room = k_sel - n_gt
    eq = keys == thr
    eqf = jnp.where(eq, 1.0, 0.0)
    sel = jnp.logical_or(keys > thr, jnp.logical_and(eq, position_rank(eqf) <= room))
    sel_new = jnp.logical_or(key_new > thr, jnp.logical_and(key_new == thr, total(eqf) + 1.0 <= room))
    newadd_ref[...] = jnp.broadcast_to(jnp.where(sel_new, 0.0, NEG), (1, page))

    self_f = jnp.where(sel, 1.0, 0.0)
    rank_scr[...] = jnp.where(sel, position_rank(self_f) - 1.0, -1.0)
    jidx = lax.broadcasted_iota(jnp.int32, (k_sel, page), 0).astype(F32)
    lane = lax.broadcasted_iota(jnp.int32, (page, LANES), 1)
    pick = jnp.where(lane == 0, lax.broadcasted_iota(jnp.int32, (page, LANES), 0).astype(F32),
                     jnp.where(lane <= 2, 1.0, 0.0)).astype(BF16)
    out_lane = lax.broadcasted_iota(jnp.int32, (k_sel, LANES), 1)

    def add_pages(g, acc):
        ranks = rank_scr[pl.ds(pl.multiple_of(g * SUBLANES, SUBLANES), SUBLANES), :]
        for i in range(SUBLANES):
            onehot = jnp.where(ranks[i:i + 1, :] == jidx, 1.0, 0.0).astype(BF16)
            slot = lax.convert_element_type(g * SUBLANES + i, F32)
            acc = acc + _dot(onehot, pick) * jnp.where(out_lane == 1, slot, 1.0)
        return acc

    assert npages % SUBLANES == 0
    gidx_ref[...] = lax.fori_loop(0, npages // SUBLANES, add_pages, jnp.zeros((k_sel, LANES), F32))


def _at_sample_select(scores, qi3, w3, ki_new, k_sel):
    rows, npages, page = scores.shape
    nh = qi3.shape[1]
    assert page == LANES and page >= REL_MAX_DIST
    kern = functools.partial(_at_sample_select_kernel, k_sel=k_sel)
    return pl.pallas_call(
        kern, out_shape=(jax.ShapeDtypeStruct((rows, k_sel, LANES), F32), jax.ShapeDtypeStruct((rows, 1, page), F32)),
        grid=(rows,),
        in_specs=[pl.BlockSpec((None, npages, page), lambda b: (b, 0, 0)),
                  pl.BlockSpec((None, nh, IDX_DIM), lambda b: (b, 0, 0)),
                  pl.BlockSpec((None, nh, IDX_DIM), lambda b: (b, 0, 0)),
                  pl.BlockSpec((None, 1, IDX_DIM), lambda b: (b, 0, 0))],
        out_specs=(pl.BlockSpec((None, k_sel, LANES), lambda b: (b, 0, 0)),
                   pl.BlockSpec((None, 1, page), lambda b: (b, 0, 0))),
        scratch_shapes=[pltpu.VMEM((npages, page), F32)],
        compiler_params=_params(("parallel",)), name="at_sample_select")(
            scores, qi3, w3, ki_new.reshape(rows, 1, IDX_DIM))


def _at_gather_attend_kernel(pt_ref, slot_ref, off_ref, q_ref, ck_ref, cv_ref, pos_ref, newadd_ref, rbt_ref,
                             kn_ref, vn_ref, z_ref, o_ref, kg, vg, sem, *, past):
    b = pl.program_id(0)
    nsel, nh, d = kg.shape
    ncol = nsel * nh

    def row_copies(j):
        page = pt_ref[b, slot_ref[b, j]]
        off = off_ref[b, j]
        return (pltpu.make_async_copy(ck_ref.at[page, off], kg.at[j], sem.at[0]),
                pltpu.make_async_copy(cv_ref.at[page, off], vg.at[j], sem.at[1]))

    def start(j, carry):
        for c in row_copies(j):
            c.start()
        return carry

    def wait(j, carry):
        for c in row_copies(j):
            c.wait()
        return carry

    lax.fori_loop(0, nsel, start, 0)
    qs = q_ref[...].astype(BF16)
    pos = pos_ref[...]
    own = (lax.broadcasted_iota(jnp.int32, (nh, ncol), 1) & (nh - 1)) == lax.broadcasted_iota(jnp.int32, (nh, ncol), 0)
    keep = jnp.logical_and(own, pos >= 0)
    bias = _bias_from_dist(jnp.maximum(past - pos, 0), lambda bk: rbt_ref[:, bk:bk + 1])
    nadd = newadd_ref[:, 0:1]
    s_n = jnp.sum(qs.astype(F32) * kn_ref[...], axis=1, keepdims=True) + rbt_ref[:, 0:1] + nadd
    lax.fori_loop(0, nsel, wait, 0)
    s = jnp.where(keep, _dot_nt(qs, kg[...].reshape(ncol, d).astype(BF16)) + bias, NEG)
    m = jnp.maximum(jnp.max(s, axis=1, keepdims=True), s_n)
    pr = jnp.where(keep, jnp.exp(s - m), 0.0)
    p_n = jnp.where(nadd < 0.0, 0.0, jnp.exp(s_n - m))
    l = jnp.sum(pr, axis=1, keepdims=True) + p_n
    o = _dot(pr.astype(BF16), vg[...].reshape(ncol, d).astype(BF16)) + p_n * vn_ref[...]
    o_ref[...] = o / l * _silu(z_ref[...])


def _at_gather_attend(q3, cache_k, cache_v, page_table, slot, off, pos_cols, newadd, rbt, kn3, vn3, z3):
    rows, nh, d = q3.shape
    assert nh & (nh - 1) == 0
    nsel = slot.shape[1]
    ncol = nsel * nh
    past = page_table.shape[1] * cache_k.shape[1]
    row3 = pl.BlockSpec((None, nh, d), lambda b, *_: (b, 0, 0))
    grid_spec = pltpu.PrefetchScalarGridSpec(
        num_scalar_prefetch=3, grid=(rows,),
        in_specs=[row3, pl.BlockSpec(memory_space=pl.ANY), pl.BlockSpec(memory_space=pl.ANY),
                  pl.BlockSpec((None, 1, ncol), lambda b, *_: (b, 0, 0)),
                  pl.BlockSpec((None, 1, newadd.shape[-1]), lambda b, *_: (b, 0, 0)),
                  pl.BlockSpec(rbt.shape, lambda b, *_: (0, 0)),
                  row3, row3, row3],
        out_specs=row3,
        scratch_shapes=[pltpu.VMEM((nsel, nh, d), F32), pltpu.VMEM((nsel, nh, d), F32), pltpu.SemaphoreType.DMA((2,))])
    kern = functools.partial(_at_gather_attend_kernel, past=past)
    return pl.pallas_call(
        kern, out_shape=jax.ShapeDtypeStruct((rows, nh, d), F32), grid_spec=grid_spec,
        compiler_params=_params(("arbitrary",)), name="at_gather_attend")(
            page_table, slot, off, q3, cache_k, cache_v, pos_cols, newadd, rbt, kn3, vn3, z3)


def _at_layer_sample(x, mods, cache_k, cache_v, cache_kidx, page_table, aw):
    g, scale, shift, gate = mods
    rows = x.shape[0]
    width, heads, nh = aw['width'], aw['heads'], aw['idx_heads']
    proj = _in_proj(x, g, scale, shift, aw['w_in'], batch_kind=False, rows_per_batch=1, tm=SUBLANES, tn=MM_TN,
                    name="at_in_proj")
    npool, page = cache_k.shape[:2]
    npages = page_table.shape[1]
    past = npages * page
    k_sel = min(TOPK_MAX, (past + 1) // 4)
    o = 4 * width
    qi3 = proj[:, o:o + nh * IDX_DIM].reshape(rows, nh, IDX_DIM)
    ki_new = proj[:, o + nh * IDX_DIM:o + nh * IDX_DIM + IDX_DIM]
    wi = proj[:, o + nh * IDX_DIM + IDX_DIM:o + nh * IDX_DIM + IDX_DIM + nh] * (IDX_DIM ** -0.5 * nh ** -0.5)
    w3 = jnp.broadcast_to(wi[:, :, None], (rows, nh, IDX_DIM))
    scores = _at_page_scores(qi3, w3, cache_kidx.astype(F32), page_table)
    rbt = aw['rel_bias'].T
    gidx, newadd = _at_sample_select(scores, qi3, w3, ki_new, k_sel)
    off = gidx[:, :, 0].astype(jnp.int32)
    slot = gidx[:, :, 1].astype(jnp.int32)
    pos = jnp.where(gidx[:, :, 2] > 0.5, slot * page + off, -1)
    pos_cols = jnp.repeat(pos, heads, axis=-1).reshape(rows, 1, k_sel * heads)
    r3 = lambda t: t.reshape(rows, heads, AT_DIM)
    a = _at_gather_attend(r3(proj[:, :width]), cache_k.astype(F32), cache_v.astype(F32), page_table, slot, off,
                          pos_cols, newadd, rbt, r3(proj[:, width:2 * width]), r3(proj[:, 2 * width:3 * width]),
                          r3(proj[:, 3 * width:4 * width]))
    x_new = _out_proj([('row', a.reshape(rows, width))], aw['w_out'], x, gate, batch_kind=False, rows_per_batch=1,
                      tm=SUBLANES, tn=MM_TN, name="at_out_proj", prologue=_cast_prologue)
    return (x_new,) + _at_rows(proj, aw, (rows, 1))


def kernel(x_prompt, x_sample, state_s5_re, state_s5_im, state_m2_conv, state_m2_ssm, state_gd_conv, state_gd_ssm, cache_k, cache_v, cache_kidx, page_table, c_prompt, c_sample, norm_g, w_mod, b_mod, final_g, s5_w_in, s5_lam_re, s5_lam_im, s5_log_dt, s5_b_re, s5_b_im, s5_c_re, s5_c_im, s5_d, s5_w_glu, s5_b_glu, s5_w_out, m2_w_in, m2_conv_w, m2_conv_b, m2_dt_bias, m2_a_log, m2_d, m2_norm, m2_w_out, gd_w_in, gd_conv_w, gd_a_log, gd_dt_bias, gd_norm, gd_w_out, at_w_in, rel_bias, at_w_out):
    f = F32
    bp, seq, d = x_prompt.shape
    bs = x_sample.shape[0]
    depth = norm_g.shape[0]
    xp = x_prompt.astype(f).reshape(bp * seq, d)
    xs = x_sample.astype(f).reshape(bs, d)

    pad_rows = (-(bs + bp)) % SUBLANES
    c_all = jnp.concatenate([c_sample.astype(f), c_prompt.astype(f), jnp.zeros((pad_rows, d), f)], axis=0)
    mod = _modulation(c_all, w_mod, b_mod)

    def mods(i, prompt):
        g = norm_g[i].astype(f).reshape(1, d)
        rows = mod[i, bs:bs + bp] if prompt else mod[i, :bs]
        shift, scale, gate = rows[:, :d], rows[:, d:2 * d], rows[:, 2 * d:]
        if prompt:
            return g, scale[:, None, :], shift[:, None, :], gate[:, None, :]
        return g, scale, shift, gate

    s5w = _s5_weights(s5_w_in, s5_lam_re, s5_lam_im, s5_log_dt, s5_b_re, s5_b_im, s5_c_re, s5_c_im, s5_d,
                      s5_w_glu, s5_b_glu, s5_w_out)
    tm_p = MM_TM

    xp, s5_re_p, s5_im_p = _s5_layer(xp, mods(0, True), None, s5w, prompt=True, batch=bp, seq=seq, tm=tm_p)
    xs, s5_re_s, s5_im_s = _s5_layer(xs, mods(0, False), (state_s5_re, state_s5_im), s5w, prompt=False,
                                     batch=bs, seq=1, tm=SUBLANES)
    groups, nstate = state_s5_re.shape[1:]
    s5_re_p, s5_im_p = s5_re_p.reshape(bp, groups, nstate), s5_im_p.reshape(bp, groups, nstate)
    s5_re_s, s5_im_s = s5_re_s.reshape(bs, groups, nstate), s5_im_s.reshape(bs, groups, nstate)

    m2w = _m2_weights(m2_w_in, m2_conv_w, m2_conv_b, m2_dt_bias, m2_a_log, m2_d, m2_norm, m2_w_out)
    xp, m2_conv_p, m2_ssm_p = _m2_layer(xp, mods(1, True), None, m2w, prompt=True, batch=bp, seq=seq, tm=tm_p)
    xs, m2_conv_s, m2_ssm_s = _m2_layer(xs, mods(1, False), (state_m2_conv, state_m2_ssm), m2w, prompt=False,
                                        batch=bs, seq=1, tm=SUBLANES)
    m2_ssm_p = m2_ssm_p.reshape((bp,) + state_m2_ssm.shape[1:])
    m2_ssm_s = m2_ssm_s.reshape(state_m2_ssm.shape)

    gdw = _gd_weights(gd_w_in, gd_conv_w, gd_a_log, gd_dt_bias, gd_norm, gd_w_out)
    xp, gd_conv_p, gd_ssm_p = _gd_layer(xp, mods(2, True), None, gdw, prompt=True, batch=bp, seq=seq, tm=tm_p)
    xs, gd_conv_s, gd_ssm_s = _gd_layer(xs, mods(2, False), (state_gd_conv, state_gd_ssm), gdw, prompt=False,
                                        batch=bs, seq=1, tm=SUBLANES)

    atw = _at_weights(at_w_in, rel_bias, at_w_out)
    yp, k_rows_p, v_rows_p, kidx_rows_p = _at_layer_prompt(xp, mods(3, True), atw, final_g.astype(f), batch=bp,
                                                           seq=seq, tm=tm_p)
    xs, k_rows_s, v_rows_s, kidx_rows_s = _at_layer_sample(xs, mods(3, False), cache_k, cache_v, cache_kidx,
                                                           page_table, atw)

    y_prompt = yp.reshape(x_prompt.shape).astype(x_prompt.dtype)
    y_sample = _final_norm(xs, final_g).reshape(x_sample.shape).astype(x_sample.dtype)
    return (y_prompt, y_sample, s5_re_p, s5_im_p, s5_re_s, s5_im_s, m2_conv_p, m2_ssm_p, m2_conv_s, m2_ssm_s,
            gd_conv_p, gd_ssm_p, gd_conv_s, gd_ssm_s,
            k_rows_p, v_rows_p, kidx_rows_p, k_rows_s, v_rows_s, kidx_rows_s)
```

```python
import functools
import math

import numpy as np
import jax
import jax.numpy as jnp
from jax import lax
from jax.experimental import pallas as pl
from jax.experimental.pallas import tpu as pltpu

F32 = jnp.float32
BF16 = jnp.bfloat16

EPS = 1e-6
NEG = -1e30
CONV_W = 4
V7X_VMEM_LIMIT_BYTES = 56 * 1024 * 1024
LANES = 128
SUBLANES = 8
MM_TM = 1024
MM_TM_WIDE_ROWS = 512
MM_TN = 1024

S5_GROUP = 16
S5_STATE = 64
S5_CHUNK = 512
S5_SEG = S5_CHUNK // SUBLANES
S5_BLK_CH = 256
S5_BLK_ST = 1024


def _params(sem):
    return pltpu.CompilerParams(dimension_semantics=sem, vmem_limit_bytes=V7X_VMEM_LIMIT_BYTES)


def _sigmoid(x):
    return 1.0 / (1.0 + jnp.exp(-x))


def _silu(x):
    return x * _sigmoid(x)


def _gelu(x):
    return 0.5 * x * (1.0 + jnp.tanh(math.sqrt(2.0 / math.pi) * (x + 0.044715 * (x * x * x))))


def _softplus(x):
    return jnp.maximum(x, 0.0) + jnp.log1p(jnp.exp(-jnp.abs(x)))


def _dot(a, b):
    return jnp.dot(a, b, preferred_element_type=F32)


def _dot_nt(a, b):
    return lax.dot_general(a, b, (((1,), (1,)), ((), ())), preferred_element_type=F32)


def _split3(x):
    hi = x.astype(BF16)
    r1 = x - hi.astype(F32)
    mid = r1.astype(BF16)
    lo = (r1 - mid.astype(F32)).astype(BF16)
    return hi, mid, lo


def _dot_exact_lhs(sel, x):
    hi, mid, lo = _split3(x)
    return _dot(sel, hi) + (_dot(sel, mid) + _dot(sel, lo))


def _mm_kernel(*refs, n_a, n_e, prologue, epilogue, bf16_copy):
    a_refs = refs[:n_a]
    w_ref = refs[n_a]
    e_refs = refs[n_a + 1:n_a + 1 + n_e]
    o_ref = refs[n_a + 1 + n_e]
    n_out = 2 if bf16_copy else 1
    if prologue is None:
        a = a_refs[0][...]
    else:
        a_scr = refs[n_a + 1 + n_e + n_out]

        @pl.when(pl.program_id(1) == 0)
        def _():
            a_scr[...] = prologue(*[r[...] for r in a_refs]).astype(BF16)

        a = a_scr[...]
    acc = _dot(a, w_ref[...])
    out = epilogue(acc, *[r[...] for r in e_refs])
    o_ref[...] = out.astype(o_ref.dtype)
    if bf16_copy:
        refs[n_a + 2 + n_e][...] = out.astype(BF16)


def _fused_matmul(a_ins, w, e_ins, *, prologue, epilogue, out_dtype, tm, tn, rows_per_batch=None, name,
                  bf16_copy=False):
    m = next(item[1].shape[0] for item in a_ins if item[0] == 'row')
    k, n = w.shape
    tm = min(tm, m)
    tn = next(t for t in (2048, 1024, 768, 512, 384, 256, 128) if t <= tn and n % t == 0)
    assert m % tm == 0
    rpb = rows_per_batch

    def bidx(i):
        return (i * tm) // rpb

    in_specs, args = [], []
    for item in a_ins:
        kind, arr = item[0], item[1]
        wd = item[2] if len(item) > 2 else arr.shape[-1]
        coff = item[3] if len(item) > 3 else 0
        if kind == 'row':
            in_specs.append(pl.BlockSpec((tm, wd), lambda i, j, coff=coff: (i, coff)))
        elif kind == 'vec':
            in_specs.append(pl.BlockSpec((1, wd), lambda i, j: (0, 0)))
        else:
            in_specs.append(pl.BlockSpec((None, 1, wd), lambda i, j: (bidx(i), 0, 0)))
        args.append(arr)
    in_specs.append(pl.BlockSpec((k, tn), lambda i, j: (0, j)))
    args.append(w)
    for item in e_ins:
        kind, arr = item[0], item[1]
        off = (item[2] if len(item) > 2 else 0) // tn
        if kind == 'tile':
            assert len(item) < 3 or item[2] % tn == 0
            in_specs.append(pl.BlockSpec((tm, tn), lambda i, j, off=off: (i, j + off)))
        elif kind == 'col':
            in_specs.append(pl.BlockSpec((1, tn), lambda i, j: (0, j)))
        else:
            in_specs.append(pl.BlockSpec((None, 1, tn), lambda i, j: (bidx(i), 0, j)))
        args.append(arr)
    scratch = [] if prologue is None else [pltpu.VMEM((tm, k), BF16)]
    kern = functools.partial(_mm_kernel, n_a=len(a_ins), n_e=len(e_ins), prologue=prologue, epilogue=epilogue,
                             bf16_copy=bf16_copy)
    out_shape = jax.ShapeDtypeStruct((m, n), out_dtype)
    out_spec = pl.BlockSpec((tm, tn), lambda i, j: (i, j))
    if bf16_copy:
        out_shape, out_spec = (out_shape, jax.ShapeDtypeStruct((m, n), BF16)), (out_spec, out_spec)
    return pl.pallas_call(
        kern, out_shape=out_shape, grid=(m // tm, n // tn), in_specs=in_specs, out_specs=out_spec,
        scratch_shapes=scratch, compiler_params=_params(("parallel", "arbitrary")), name=name)(*args)


def _weight_cast_kernel(*refs, n_valid, scaled, transposed):
    w_ref, o_ref = refs[0], refs[-1]
    tn = o_ref.shape[1]
    col = pl.program_id(0) * tn + lax.broadcasted_iota(jnp.int32, (1, tn), 1)
    w = w_ref[...].astype(F32)
    if transposed:
        w = w.T
    if scaled:
        w = w * refs[1][...]
    o_ref[...] = jnp.where(col < n_valid, w, 0.0).astype(o_ref.dtype)


def _weight_bf16(w, n_cols=None, pad_to=512, col_scale=None, tn=512):
    k, n_in = w.shape
    n_cols = n_in if n_cols is None else n_cols
    n_out = -(-n_cols // pad_to) * pad_to
    tn = math.gcd(n_out, tn)
    transposed = n_in % LANES != 0
    if transposed:
        args, in_specs = [w.T], [pl.BlockSpec((tn, k), lambda j: (j, 0))]
    else:
        args, in_specs = [w], [pl.BlockSpec((k, tn), lambda j: (0, j))]
    if col_scale is not None:
        args.append(jnp.pad(col_scale.astype(F32), (0, n_out - col_scale.shape[0])).reshape(1, n_out))
        in_specs.append(pl.BlockSpec((1, tn), lambda j: (0, j)))
    kern = functools.partial(_weight_cast_kernel, n_valid=n_cols, scaled=col_scale is not None, transposed=transposed)
    return pl.pallas_call(
        kern, out_shape=jax.ShapeDtypeStruct((k, n_out), BF16), grid=(n_out // tn,), in_specs=in_specs,
        out_specs=pl.BlockSpec((k, tn), lambda j: (0, j)),
        compiler_params=_params(("parallel",)), name="weight_cast")(*args)


def _pad_cols(w, mult):
    n = w.shape[-1]
    npad = (-n) % mult
    if npad:
        w = jnp.pad(w, ((0, 0), (0, npad)))
    return w


def _modnorm_prologue(x, g, scale, shift):
    r = x * lax.rsqrt(jnp.mean(x * x, axis=-1, keepdims=True) + EPS) * g
    return r * (1.0 + scale) + shift


def _identity_epilogue(acc):
    return acc


def _residual_epilogue(acc, x, gate):
    return x + gate * acc


def _residual_norm_epilogue(acc, x, gate, g):
    xn = x + gate * acc
    return xn * lax.rsqrt(jnp.mean(xn * xn, axis=-1, keepdims=True) + EPS) * g


def _in_proj(x, g, scale, shift, w, *, batch_kind, rows_per_batch, tm, tn, name, bf16_copy=False):
    kind = 'batch' if batch_kind else 'row'
    return _fused_matmul([('row', x), ('vec', g), (kind, scale), (kind, shift)], w, [],
                         prologue=_modnorm_prologue, epilogue=_identity_epilogue, out_dtype=F32,
                         tm=tm, tn=tn, rows_per_batch=rows_per_batch, name=name, bf16_copy=bf16_copy)


def _gate_proj_kernel(x_ref, g_ref, scale_ref, shift_ref, w_ref, o_ref):
    h = _modnorm_prologue(x_ref[...], g_ref[...], scale_ref[...], shift_ref[...])
    o_ref[...] = _dot_3pass(h, w_ref[...])


def _gate_proj(x, g, scale, shift, w, *, batch_kind, rows_per_batch, tm, name):
    m, d = x.shape
    n = w.shape[1]
    tm = min(tm, m, MM_TM_WIDE_ROWS)
    if batch_kind:
        mod_spec = pl.BlockSpec((None, 1, d), lambda i: ((i * tm) // rows_per_batch, 0, 0))
    else:
        mod_spec = pl.BlockSpec((tm, d), lambda i: (i, 0))
    return pl.pallas_call(
        _gate_proj_kernel, out_shape=jax.ShapeDtypeStruct((m, n), F32), grid=(m // tm,),
        in_specs=[pl.BlockSpec((tm, d), lambda i: (i, 0)), pl.BlockSpec((1, d), lambda i: (0, 0)),
                  mod_spec, mod_spec, pl.BlockSpec((d, n), lambda i: (0, 0))],
        out_specs=pl.BlockSpec((tm, n), lambda i: (i, 0)),
        compiler_params=_params(("parallel",)), name=name)(x, g, scale, shift, w)


def _out_proj(a_ins, w, x, gate, *, batch_kind, rows_per_batch, tm, tn, name, prologue=None):
    kind = 'batchcol' if batch_kind else 'tile'
    return _fused_matmul(a_ins, w, [('tile', x), (kind, gate)], prologue=prologue, epilogue=_residual_epilogue,
                         out_dtype=F32, tm=tm, tn=tn, rows_per_batch=rows_per_batch, name=name)


def _mod_kernel(c_ref, w_ref, b_ref, o_ref):
    o_ref[...] = _dot(c_ref[...].astype(BF16), w_ref[...].astype(BF16)) + b_ref[...]


def _modulation(c_all, w_mod, b_mod, tn=512):
    depth, d, n = w_mod.shape
    rows = c_all.shape[0]
    return pl.pallas_call(
        _mod_kernel, out_shape=jax.ShapeDtypeStruct((depth, rows, n), F32), grid=(depth, n // tn),
        in_specs=[pl.BlockSpec((rows, d), lambda l, j: (0, 0)),
                  pl.BlockSpec((None, d, tn), lambda l, j: (l, 0, j)),
                  pl.BlockSpec((None, 1, tn), lambda l, j: (l, 0, j))],
        out_specs=pl.BlockSpec((None, rows, tn), lambda l, j: (l, 0, j)),
        compiler_params=_params(("parallel", "parallel")), name="adaln_modulation")(
            c_all, w_mod, b_mod.reshape(depth, 1, n))


def _rmsnorm_kernel(x_ref, g_ref, o_ref):
    x = x_ref[...]
    o_ref[...] = x * lax.rsqrt(jnp.mean(x * x, axis=-1, keepdims=True) + EPS) * g_ref[...]


def _final_norm(x, g, tm=512):
    m, d = x.shape
    tm = min(tm, m)
    return pl.pallas_call(
        _rmsnorm_kernel, out_shape=jax.ShapeDtypeStruct((m, d), F32), grid=(m // tm,),
        in_specs=[pl.BlockSpec((tm, d), lambda i: (i, 0)), pl.BlockSpec((1, d), lambda i: (0, 0))],
        out_specs=pl.BlockSpec((tm, d), lambda i: (i, 0)),
        compiler_params=_params(("parallel",)), name="final_rmsnorm")(x, g.reshape(1, d))


def _s5_tables(lam_re, lam_im, log_dt, b_re, b_im, c_re, c_im, d_skip):
    f = F32
    groups, p = lam_re.shape
    nblk = groups * S5_GROUP // S5_BLK_CH
    gpb = groups // nblk
    lr, li = lam_re.astype(f), lam_im.astype(f)
    dt = jnp.exp(log_dt.astype(f))[:, None]
    ldr, ldi = lr * dt, li * dt
    kk = jnp.arange(1, S5_SEG + 1, dtype=f)[:, None, None]
    pmag = jnp.exp(kk * ldr)
    pw_re, pw_im = pmag * jnp.cos(kk * ldi), pmag * jnp.sin(kk * ldi)
    ab_re, ab_im = jnp.exp(ldr) * jnp.cos(ldi), jnp.exp(ldr) * jnp.sin(ldi)
    den = lr * lr + li * li
    nr, ni = ab_re - 1.0, ab_im
    fr, fi = (nr * lr + ni * li) / den, (ni * lr - nr * li) / den
    bre, bim = b_re.astype(f), b_im.astype(f)
    bb_re = fr[..., None] * bre - fi[..., None] * bim
    bb_im = fr[..., None] * bim + fi[..., None] * bre
    eye = jnp.eye(gpb, dtype=f)

    def bd_in(bb):
        t = bb.reshape(nblk, gpb, p, S5_GROUP).transpose(0, 1, 3, 2)
        return jnp.einsum('bgkp,gh->bgkhp', t, eye).reshape(nblk, gpb * S5_GROUP, gpb * p).astype(BF16)

    def bd_out(c):
        t = c.astype(f).reshape(nblk, gpb, S5_GROUP, p).transpose(0, 1, 3, 2)
        return jnp.einsum('bgpk,gh->bgphk', t, eye).reshape(nblk, gpb * p, gpb * S5_GROUP).astype(BF16)

    def lanes(t):
        lead = t.shape[:-2]
        t = t.reshape(lead + (nblk, gpb * p))
        return jnp.moveaxis(t, -2, 0)

    return dict(
        bb_re=bd_in(bb_re), bb_im=bd_in(bb_im), c_re=bd_out(c_re), c_im=bd_out(c_im),
        ab_re=lanes(ab_re[None]), ab_im=lanes(ab_im[None]),
        pw_re=lanes(pw_re), pw_im=lanes(pw_im),
        d=d_skip.astype(f).reshape(1, -1), nblk=nblk)


def _s5_perm():
    pm = np.zeros((S5_CHUNK, S5_CHUNK), np.float32)
    r = np.arange(S5_CHUNK)
    pm[r, (r % SUBLANES) * S5_SEG + r // SUBLANES] = 1.0
    return jnp.asarray(pm, BF16), jnp.asarray(pm.T, BF16)


def _s5_scan_kernel(u_ref, pm_ref, pmt_ref, bbre_ref, bbim_ref, cre_ref, cim_ref, abre_ref, abim_ref,
                    pwre_ref, pwim_ref, d_ref, y_ref, sre_out, sim_out,
                    xre, xim, car_re, car_im, cin_re, cin_im, lend_re, lend_im):
    n = pl.program_id(2)
    nst = xre.shape[1]

    @pl.when(n == 0)
    def _():
        car_re[...] = jnp.zeros_like(car_re)
        car_im[...] = jnp.zeros_like(car_im)

    u = u_ref[...]
    up = _dot(pm_ref[...], u.astype(BF16)).astype(BF16)
    xre[...] = _dot(up, bbre_ref[...])
    xim[...] = _dot(up, bbim_ref[...])
    are = jnp.broadcast_to(abre_ref[...], (SUBLANES, nst))
    aim = jnp.broadcast_to(abim_ref[...], (SUBLANES, nst))
    sre = jnp.zeros((SUBLANES, nst), F32)
    sim = jnp.zeros((SUBLANES, nst), F32)
    for i in range(S5_SEG):
        r = slice(SUBLANES * i, SUBLANES * (i + 1))
        nre = are * sre - aim * sim + xre[r, :]
        nim = are * sim + aim * sre + xim[r, :]
        xre[r, :] = nre
        xim[r, :] = nim
        sre, sim = nre, nim
    lend_re[...] = sre
    lend_im[...] = sim
    a_re = pwre_ref[S5_SEG - 1:S5_SEG, :]
    a_im = pwim_ref[S5_SEG - 1:S5_SEG, :]
    cr, ci = car_re[...], car_im[...]
    for s in range(SUBLANES):
        cin_re[s:s + 1, :] = cr
        cin_im[s:s + 1, :] = ci
        lr, li = lend_re[s:s + 1, :], lend_im[s:s + 1, :]
        cr, ci = a_re * cr - a_im * ci + lr, a_re * ci + a_im * cr + li
    car_re[...] = cr
    car_im[...] = ci
    cinr, cini = cin_re[...], cin_im[...]
    for i in range(S5_SEG):
        r = slice(SUBLANES * i, SUBLANES * (i + 1))
        pr, pi_ = pwre_ref[i:i + 1, :], pwim_ref[i:i + 1, :]
        xre[r, :] = xre[r, :] + (pr * cinr - pi_ * cini)
        xim[r, :] = xim[r, :] + (pr * cini + pi_ * cinr)
    yp = _dot(xre[...].astype(BF16), cre_ref[...]) - _dot(xim[...].astype(BF16), cim_ref[...])
    hi = yp.astype(BF16)
    lo = (yp - hi.astype(F32)).astype(BF16)
    y = _dot(pmt_ref[...], hi) + _dot(pmt_ref[...], lo) + d_ref[...] * u
    y_ref[...] = _gelu(y)

    @pl.when(n == pl.num_programs(2) - 1)
    def _():
        sre_out[...] = cr
        sim_out[...] = ci


def _s5_scan(proj, tabs, batch, seq):
    nblk = tabs['nblk']
    nch = seq // S5_CHUNK
    pm, pmt = _s5_perm()
    nstate = nblk * S5_BLK_ST
    const3 = lambda shape: pl.BlockSpec((None,) + shape, lambda k, b, n: (k, 0, 0))
    y, sre, sim = pl.pallas_call(
        _s5_scan_kernel,
        out_shape=(jax.ShapeDtypeStruct((batch * seq, nblk * S5_BLK_CH), F32),
                   jax.ShapeDtypeStruct((batch, 1, nstate), F32),
                   jax.ShapeDtypeStruct((batch, 1, nstate), F32)),
        grid=(nblk, batch, nch),
        in_specs=[pl.BlockSpec((S5_CHUNK, S5_BLK_CH), lambda k, b, n: (b * nch + n, k)),
                  pl.BlockSpec((S5_CHUNK, S5_CHUNK), lambda k, b, n: (0, 0)),
                  pl.BlockSpec((S5_CHUNK, S5_CHUNK), lambda k, b, n: (0, 0)),
                  const3((S5_BLK_CH, S5_BLK_ST)), const3((S5_BLK_CH, S5_BLK_ST)),
                  const3((S5_BLK_ST, S5_BLK_CH)), const3((S5_BLK_ST, S5_BLK_CH)),
                  const3((1, S5_BLK_ST)), const3((1, S5_BLK_ST)),
                  const3((S5_SEG, S5_BLK_ST)), const3((S5_SEG, S5_BLK_ST)),
                  pl.BlockSpec((1, S5_BLK_CH), lambda k, b, n: (0, k))],
        out_specs=(pl.BlockSpec((S5_CHUNK, S5_BLK_CH), lambda k, b, n: (b * nch + n, k)),
                   pl.BlockSpec((None, 1, S5_BLK_ST), lambda k, b, n: (b, 0, k)),
                   pl.BlockSpec((None, 1, S5_BLK_ST), lambda k, b, n: (b, 0, k))),
        scratch_shapes=[pltpu.VMEM((S5_CHUNK, S5_BLK_ST), F32), pltpu.VMEM((S5_CHUNK, S5_BLK_ST), F32),
                        pltpu.VMEM((1, S5_BLK_ST), F32), pltpu.VMEM((1, S5_BLK_ST), F32),
                        pltpu.VMEM((SUBLANES, S5_BLK_ST), F32), pltpu.VMEM((SUBLANES, S5_BLK_ST), F32),
                        pltpu.VMEM((SUBLANES, S5_BLK_ST), F32), pltpu.VMEM((SUBLANES, S5_BLK_ST), F32)],
        compiler_params=_params(("parallel", "parallel", "arbitrary")), name="s5_scan")(
            proj, pm, pmt, tabs['bb_re'], tabs['bb_im'], tabs['c_re'], tabs['c_im'],
            tabs['ab_re'], tabs['ab_im'], tabs['pw_re'], tabs['pw_im'], tabs['d'])
    return y, sre, sim


def _s5_step_kernel(u_ref, hre_ref, him_ref, bbre_ref, bbim_ref, cre_ref, cim_ref, abre_ref, abim_ref, d_ref,
                    y_ref, sre_out, sim_out):
    u = u_ref[...]
    ub = u.astype(BF16)
    are, aim = abre_ref[...], abim_ref[...]
    hre, him = hre_ref[...], him_ref[...]
    sre = are * hre - aim * him + _dot(ub, bbre_ref[...])
    sim = are * him + aim * hre + _dot(ub, bbim_ref[...])
    sre_out[...] = sre
    sim_out[...] = sim
    y = _dot(sre.astype(BF16), cre_ref[...]) - _dot(sim.astype(BF16), cim_ref[...]) + d_ref[...] * u
    y_ref[...] = _gelu(y)


def _s5_step(proj, h_re, h_im, tabs):
    nblk = tabs['nblk']
    rows = proj.shape[0]
    nstate = nblk * S5_BLK_ST
    const3 = lambda shape: pl.BlockSpec((None,) + shape, lambda k: (k, 0, 0))
    lane_blk = lambda w: pl.BlockSpec((rows, w), lambda k: (0, k))
    return pl.pallas_call(
        _s5_step_kernel,
        out_shape=(jax.ShapeDtypeStruct((rows, nblk * S5_BLK_CH), F32),
                   jax.ShapeDtypeStruct((rows, nstate), F32), jax.ShapeDtypeStruct((rows, nstate), F32)),
        grid=(nblk,),
        in_specs=[lane_blk(S5_BLK_CH), lane_blk(S5_BLK_ST), lane_blk(S5_BLK_ST),
                  const3((S5_BLK_CH, S5_BLK_ST)), const3((S5_BLK_CH, S5_BLK_ST)),
                  const3((S5_BLK_ST, S5_BLK_CH)), const3((S5_BLK_ST, S5_BLK_CH)),
                  const3((1, S5_BLK_ST)), const3((1, S5_BLK_ST)),
                  pl.BlockSpec((1, S5_BLK_CH), lambda k: (0, k))],
        out_specs=(lane_blk(S5_BLK_CH), lane_blk(S5_BLK_ST), lane_blk(S5_BLK_ST)),
        compiler_params=_params(("parallel",)), name="s5_step")(
            proj, h_re.reshape(rows, nstate), h_im.reshape(rows, nstate),
            tabs['bb_re'], tabs['bb_im'], tabs['c_re'], tabs['c_im'], tabs['ab_re'], tabs['ab_im'], tabs['d'])


def _s5_weights(w_in, lam_re, lam_im, log_dt, b_re, b_im, c_re, c_im, d_skip, w_glu, b_glu, w_out):
    tabs = _s5_tables(lam_re, lam_im, log_dt, b_re, b_im, c_re, c_im, d_skip)
    return (_weight_bf16(w_in), _weight_bf16(w_glu), b_glu.astype(F32).reshape(1, -1), _weight_bf16(w_out), tabs)


def _glu_epilogue(acc, gy, z, b):
    return gy * _sigmoid(acc + b) * _silu(z)


def _cast_prologue(a):
    return a


def _s5_layer(x, mods, h_state, w, *, prompt, batch, seq, tm):
    g, scale, shift, gate = mods
    w_in, w_glu, b_glu, w_out, tabs = w
    width = w_glu.shape[0]
    proj = _in_proj(x, g, scale, shift, w_in, batch_kind=prompt, rows_per_batch=seq, tm=tm, tn=MM_TN, name="s5_in_proj")
    if prompt:
        gy, sre, sim = _s5_scan(proj, tabs, batch, seq)
    else:
        gy, sre, sim = _s5_step(proj, h_state[0], h_state[1], tabs)
    a = _fused_matmul([('row', gy)], w_glu, [('tile', gy), ('tile', proj, width), ('col', b_glu)],
                      prologue=_cast_prologue, epilogue=_glu_epilogue, out_dtype=BF16, tm=min(tm, MM_TM_WIDE_ROWS),
                      tn=MM_TN, name="s5_glu")
    x_new = _out_proj([('row', a)], w_out, x, gate, batch_kind=prompt, rows_per_batch=seq, tm=tm, tn=MM_TN, name="s5_out_proj")
    return x_new, sre, sim


def _conv_silu_chunk(x_ref, w_ref, b_ref, pad_ref, first):
    c = x_ref.shape[0]

    @pl.when(first)
    def _():
        pad_ref[0:SUBLANES, :] = jnp.zeros((SUBLANES, pad_ref.shape[1]), F32)

    pad_ref[SUBLANES:SUBLANES + c, :] = x_ref[...]
    acc = w_ref[3:4, :] * pad_ref[SUBLANES:SUBLANES + c, :]
    for j in range(CONV_W - 1):
        off = SUBLANES - (CONV_W - 1) + j
        acc = acc + w_ref[j:j + 1, :] * pad_ref[off:off + c, :]
    if b_ref is not None:
        acc = acc + b_ref[...]
    pad_ref[0:SUBLANES, :] = pad_ref[c:c + SUBLANES, :]
    return _silu(acc)


def _conv_silu_step(x, taps_ref, w_ref, b_ref):
    acc = w_ref[3:4, :] * x
    for j in range(CONV_W - 1):
        acc = acc + w_ref[j:j + 1, :] * taps_ref[j]
    if b_ref is not None:
        acc = acc + b_ref[...]
    return _silu(acc)


def _tri_masks(c):
    t = lax.broadcasted_iota(jnp.int32, (c, c), 0)
    s = lax.broadcasted_iota(jnp.int32, (c, c), 1)
    return s <= t, s < t


def _pad_to_square_t(x, n):
    rows = x.shape[0]
    return jnp.concatenate([x, jnp.zeros((n - rows, n), x.dtype)], axis=0).T


M2_HEADDIM = 64
M2_STATE = 128
M2_GROUPS = 8
M2_CHUNK = 128


def _m2_ssd_kernel(x_ref, b_ref, c_ref, dt_ref, z_ref, wx_ref, wb_ref, wc_ref, bx_ref, bb_ref, bc_ref,
                   dtb_ref, nega_ref, dsk_ref, nw_ref, o_ref, sout_ref, xpad, bpad, cpad, s_ref, y_ref):
    n = pl.program_id(1)
    first = n == 0
    c = x_ref.shape[0]
    npairs = s_ref.shape[0]
    pairs_per_group = npairs // M2_GROUPS

    @pl.when(first)
    def _():
        s_ref[...] = jnp.zeros_like(s_ref)

    xs = _conv_silu_chunk(x_ref, wx_ref, bx_ref, xpad, first)
    bm = _conv_silu_chunk(b_ref, wb_ref, bb_ref, bpad, first).astype(BF16)
    cm = _conv_silu_chunk(c_ref, wc_ref, bc_ref, cpad, first).astype(BF16)
    dtv = _softplus(dt_ref[...] + dtb_ref[...])
    la = nega_ref[...] * dtv
    incl, _ = _tri_masks(c)
    tri = jnp.where(incl, 1.0, 0.0).astype(BF16)
    cum = _dot_exact_lhs(tri, la)
    cum_t = cum.T
    ecum_all = jnp.exp(cum)
    wend_all = jnp.exp(cum[c - 1:c, :] - cum)
    elast_t = jnp.exp(cum_t[:, c - 1:c])
    lane_first = lax.broadcasted_iota(jnp.int32, (c, LANES), 1) < M2_HEADDIM
    row_first = lax.broadcasted_iota(jnp.int32, (LANES, LANES), 0) < M2_HEADDIM

    for g in range(M2_GROUPS):
        bg = bm[:, g * M2_STATE:(g + 1) * M2_STATE]
        cg = cm[:, g * M2_STATE:(g + 1) * M2_STATE]
        gm = _dot_nt(cg, bg)
        for j in range(pairs_per_group):
            p = g * pairs_per_group + j
            ha, hb = 2 * p, 2 * p + 1
            xp = xs[:, p * LANES:(p + 1) * LANES]

            def decay_weights(h):
                seg = cum[:, h:h + 1] - cum_t[h:h + 1, :]
                dec = jnp.where(incl, jnp.exp(jnp.where(incl, seg, 0.0)), 0.0)
                return (gm * dec).astype(BF16)

            xdt = xp * jnp.where(lane_first, dtv[:, ha:ha + 1], dtv[:, hb:hb + 1])
            xdt_a = jnp.where(lane_first, xdt, 0.0)
            xdt_b = xdt - xdt_a
            y = _dot(decay_weights(ha), xdt_a.astype(BF16)) + _dot(decay_weights(hb), xdt_b.astype(BF16))
            sp = s_ref[p]
            y = y + _dot_nt(cg, sp.astype(BF16)) * jnp.where(lane_first, ecum_all[:, ha:ha + 1], ecum_all[:, hb:hb + 1])
            y_ref[:, p * LANES:(p + 1) * LANES] = y + dsk_ref[:, p * LANES:(p + 1) * LANES] * xp
            xw = xdt * jnp.where(lane_first, wend_all[:, ha:ha + 1], wend_all[:, hb:hb + 1])
            dmat = jnp.where(row_first, elast_t[ha:ha + 1, :], elast_t[hb:hb + 1, :])
            s_ref[p] = sp * dmat + _dot(xw.T.astype(BF16), bg)

    o_ref[...] = _gated_rmsnorm_prologue(y_ref[...], z_ref[...], nw_ref[...]).astype(o_ref.dtype)

    @pl.when(n == pl.num_programs(1) - 1)
    def _():
        sout_ref[...] = s_ref[...]


def _m2_ssd(proj, gates, mw, batch, seq):
    c = M2_CHUNK
    nch = seq // c
    width = mw['width']
    gs = M2_GROUPS * M2_STATE
    npairs = width // LANES
    xo, bo, co = width // width, (2 * width) // gs, (2 * width + gs) // gs
    row = lambda w, off: pl.BlockSpec((c, w), lambda b, n: (b * nch + n, off))
    par = lambda r, w, off: pl.BlockSpec((r, w), lambda b, n: (0, off))
    return pl.pallas_call(
        _m2_ssd_kernel,
        out_shape=(jax.ShapeDtypeStruct((batch * seq, width), BF16),
                   jax.ShapeDtypeStruct((batch, npairs, LANES, M2_STATE), F32)),
        grid=(batch, nch),
        in_specs=[row(width, xo), row(gs, bo), row(gs, co), row(LANES, 0), row(width, 0),
                  par(CONV_W, width, 0), par(CONV_W, gs, width // gs), par(CONV_W, gs, width // gs + 1),
                  par(1, width, 0), par(1, gs, width // gs), par(1, gs, width // gs + 1),
                  par(1, LANES, 0), par(1, LANES, 0), par(1, width, 0), par(1, width, 0)],
        out_specs=(pl.BlockSpec((c, width), lambda b, n: (b * nch + n, 0)),
                   pl.BlockSpec((None, npairs, LANES, M2_STATE), lambda b, n: (b, 0, 0, 0))),
        scratch_shapes=[pltpu.VMEM((c + SUBLANES, width), F32), pltpu.VMEM((c + SUBLANES, gs), F32),
                        pltpu.VMEM((c + SUBLANES, gs), F32), pltpu.VMEM((npairs, LANES, M2_STATE), F32),
                        pltpu.VMEM((c, width), F32)],
        compiler_params=_params(("parallel", "arbitrary")), name="m2_ssd")(
            proj, proj, proj, gates, proj, mw['conv_w'], mw['conv_w'], mw['conv_w'], mw['conv_b'], mw['conv_b'],
            mw['conv_b'], mw['dt_bias'], mw['neg_a'], mw['d_ch'], mw['norm'])


def _m2_pre_step_kernel(xbc_ref, dt_ref, taps_ref, w_ref, b_ref, dtb_ref, nega_ref, act_ref, dtv_ref, dec_ref):
    act_ref[...] = _conv_silu_step(xbc_ref[...], taps_ref, w_ref, b_ref)
    dtv = _softplus(dt_ref[...] + dtb_ref[...])
    dtv_ref[...] = dtv
    dec_ref[...] = jnp.exp(nega_ref[...] * dtv)


def _m2_pre_step(proj, dt_raw, taps, mw):
    rows = proj.shape[0]
    width = mw['width']
    cdim = mw['conv_w'].shape[1]
    xbc = proj[:, width:width + cdim]
    return pl.pallas_call(
        _m2_pre_step_kernel,
        out_shape=(jax.ShapeDtypeStruct((rows, cdim), F32), jax.ShapeDtypeStruct((rows, LANES), F32),
                   jax.ShapeDtypeStruct((rows, LANES), F32)),
        name="m2_pre_step", compiler_params=pltpu.CompilerParams(vmem_limit_bytes=V7X_VMEM_LIMIT_BYTES))(
            xbc, dt_raw, taps, mw['conv_w'], mw['conv_b'], mw['dt_bias'], mw['neg_a'])


def _m2_state_step_kernel(x_ref, dtc_ref, decc_ref, b_ref, c_ref, dsk_ref, s_ref, y_ref, sout_ref):
    rows = x_ref.shape[0]
    pairs = x_ref.shape[1] // LANES
    bv = b_ref[...]
    cb = c_ref[...].astype(BF16)
    for j in range(pairs):
        sl = slice(j * LANES, (j + 1) * LANES)
        xp = x_ref[:, sl]
        xdt_t = _pad_to_square_t(xp * dtc_ref[:, sl], LANES)
        dec_t = _pad_to_square_t(decc_ref[:, sl], LANES)
        for b in range(rows):
            sp = s_ref[b, j]
            s_new = sp * dec_t[:, b:b + 1] + xdt_t[:, b:b + 1] * bv[b:b + 1, :]
            sout_ref[b, j] = s_new
            y_ref[b:b + 1, sl] = _dot_nt(cb[b:b + 1, :], s_new.astype(BF16)) + dsk_ref[:, sl] * xp[b:b + 1, :]


def _m2_state_step(act, dt_ch, dec_ch, ssm, mw):
    rows = act.shape[0]
    width = mw['width']
    gw = width // M2_GROUPS
    ppg = gw // LANES
    npairs = width // LANES
    gs = M2_GROUPS * M2_STATE
    s4 = ssm.reshape(rows, npairs, LANES, M2_STATE)
    blk = lambda w, base: pl.BlockSpec((rows, w), lambda g: (0, base + g))
    return pl.pallas_call(
        _m2_state_step_kernel,
        out_shape=(jax.ShapeDtypeStruct((rows, width), F32), jax.ShapeDtypeStruct(s4.shape, F32)),
        grid=(M2_GROUPS,),
        in_specs=[blk(gw, 0), blk(gw, 0), blk(gw, 0), blk(M2_STATE, width // M2_STATE),
                  blk(M2_STATE, (width + gs) // M2_STATE), pl.BlockSpec((1, gw), lambda g: (0, g)),
                  pl.BlockSpec((rows, ppg, LANES, M2_STATE), lambda g: (0, g, 0, 0))],
        out_specs=(blk(gw, 0), pl.BlockSpec((rows, ppg, LANES, M2_STATE), lambda g: (0, g, 0, 0))),
        compiler_params=_params(("parallel",)), name="m2_state_step")(
            act, dt_ch, dec_ch, act, act, mw['d_ch'], s4)


def _m2_weights(w_in, conv_w, conv_b, dt_bias, a_log, d_skip, norm_w, w_out):
    f = F32
    heads = dt_bias.shape[0]
    width = norm_w.shape[0]
    pad = LANES - heads
    cdim = conv_w.shape[1]
    return dict(
        w_in=_weight_bf16(w_in, width + cdim),
        w_gate=_pad_cols(w_in[:, width + cdim:].astype(f), LANES),
        conv_w=conv_w.astype(f), conv_b=conv_b.astype(f).reshape(1, -1),
        dt_bias=jnp.pad(dt_bias.astype(f), (0, pad)).reshape(1, LANES),
        neg_a=jnp.pad(-jnp.exp(a_log.astype(f)), (0, pad)).reshape(1, LANES),
        d_ch=jnp.repeat(d_skip.astype(f), width // heads).reshape(1, width),
        norm=norm_w.astype(f).reshape(1, width), w_out=_weight_bf16(w_out), width=width, heads=heads)


def _gated_rmsnorm_prologue(y, z, w):
    v = y * _silu(z)
    return v * lax.rsqrt(jnp.mean(v * v, axis=-1, keepdims=True) + EPS) * w


def _m2_layer(x, mods, state, mw, *, prompt, batch, seq, tm):
    g, scale, shift, gate = mods
    width, heads = mw['width'], mw['heads']
    cdim = mw['conv_w'].shape[1]
    proj = _in_proj(x, g, scale, shift, mw['w_in'], batch_kind=prompt, rows_per_batch=seq, tm=tm, tn=MM_TN,
                    name="m2_in_proj")
    gates = _gate_proj(x, g, scale, shift, mw['w_gate'], batch_kind=prompt, rows_per_batch=seq, tm=tm,
                       name="m2_gate_proj")
    if prompt:
        a, ssm = _m2_ssd(proj, gates, mw, batch, seq)
        conv_new = proj.reshape(batch, seq, -1)[:, seq - (CONV_W - 1):, width:width + cdim]
        x_new = _out_proj([('row', a)], mw['w_out'], x, gate, batch_kind=True, rows_per_batch=seq, tm=tm, tn=MM_TN,
                          name="m2_out_proj")
        return x_new, conv_new, ssm
    else:
        conv_buf, ssm_in = state
        taps = jnp.swapaxes(conv_buf.astype(F32), 0, 1)
        act, dtv, dec = _m2_pre_step(proj, gates, taps, mw)
        rep = width // heads
        dt_ch = jnp.repeat(dtv[:, :heads], rep, axis=1)
        dec_ch = jnp.repeat(dec[:, :heads], rep, axis=1)
        y, ssm = _m2_state_step(act, dt_ch, dec_ch, ssm_in.astype(F32), mw)
        conv_new = jnp.concatenate([conv_buf.astype(F32)[:, 1:], proj[:, None, width:width + cdim]], axis=1)
    x_new = _out_proj([('row', y), ('row', proj, width, 0), ('vec', mw['norm'])], mw['w_out'], x, gate,
                      batch_kind=prompt, rows_per_batch=seq, tm=min(tm, 256), tn=MM_TN, name="m2_out_proj",
                      prologue=_gated_rmsnorm_prologue)
    return x_new, conv_new, ssm


GD_DK = 128
GD_DV = 128
GD_CHUNK = 64


def _dot_3pass(a, b):
    ah = a.astype(BF16)
    al = (a - ah.astype(F32)).astype(BF16)
    bh = b.astype(BF16)
    bl = (b - bh.astype(F32)).astype(BF16)
    return _dot(ah, bh) + (_dot(ah, bl) + _dot(al, bh))


def _l2norm_rows(x):
    return x * lax.rsqrt(jnp.sum(x * x, axis=-1, keepdims=True) + EPS)


def _rmsnorm_rows(x, w):
    return x * lax.rsqrt(jnp.mean(x * x, axis=-1, keepdims=True) + EPS) * w


GD_INV_BASE = 16


def _bdot(a, b):
    return jnp.einsum('hmk,hkn->hmn', a, b, preferred_element_type=F32)


def _bdot_nt(a, b):
    return jnp.einsum('hmk,hnk->hmn', a, b, preferred_element_type=F32)


def _split2(x):
    hi = x.astype(BF16)
    return hi, (x - hi.astype(F32)).astype(BF16)


def _bdot_3pass(a, b):
    (ah, al), (bh, bl) = a, b
    return _bdot(ah, bh) + (_bdot(ah, bl) + _bdot(al, bh))


def _unit_lower_inverse(a_strict):
    c = a_strict.shape[-1]
    row = lax.broadcasted_iota(jnp.int32, (c, c), 0)
    col = lax.broadcasted_iota(jnp.int32, (c, c), 1)
    eye = jnp.where(row == col, 1.0, 0.0)
    blk = GD_INV_BASE
    shift = int(math.log2(blk))
    p = jnp.where((row >> shift) == (col >> shift), -a_strict, 0.0)
    t = eye + p
    ps = _split2(p)
    for _ in range(shift - 1):
        p = _bdot_3pass(ps, ps)
        ps = _split2(p)
        t = t + _bdot_3pass(_split2(t), ps)
    while blk < c:
        below = jnp.logical_and((row >> (shift + 1)) == (col >> (shift + 1)), (row >> shift) != (col >> shift))
        ts = _split2(t)
        tb = _bdot_3pass(ts, _split2(jnp.where(below, a_strict, 0.0)))
        t = t - _bdot_3pass(_split2(tb), ts)
        blk *= 2
        shift += 1
    return t


def _gd_chunk_kernel(qkv_ref, z_ref, braw_ref, araw_ref, cw_ref, nega_ref, dtb_ref, nw_ref,
                     o_ref, sout_ref, pad, s_ref):
    n = pl.program_id(1)
    first = n == 0
    c = qkv_ref.shape[0]
    hv = s_ref.shape[0]
    hk = hv // 2
    rep = hv // hk

    @pl.when(first)
    def _():
        s_ref[...] = jnp.zeros_like(s_ref)

    qkv = _conv_silu_chunk(qkv_ref, cw_ref, None, pad, first)
    beta = _sigmoid(braw_ref[...])
    gl = nega_ref[...] * _softplus(araw_ref[...] + dtb_ref[...])
    incl, strict = _tri_masks(c)
    tri = jnp.where(incl, 1.0, 0.0).astype(BF16)
    gcum = _dot_exact_lhs(tri, gl)
    gcum_t = jnp.concatenate([gcum, jnp.zeros((LANES - c, LANES), F32)], axis=0).T

    heads = range(hv)
    per_value_head = lambda t: jnp.stack([t[h // rep] for h in heads])
    q3 = jnp.stack([qkv[:, i * GD_DK:(i + 1) * GD_DK] for i in range(hk)])
    k3 = jnp.stack([qkv[:, (hk + i) * GD_DK:(hk + i + 1) * GD_DK] for i in range(hk)])
    v3 = jnp.stack([qkv[:, (2 * hk + h) * GD_DV:(2 * hk + h + 1) * GD_DV] for h in heads])
    q3 = _l2norm_rows(q3) * (GD_DK ** -0.5)
    k3 = _l2norm_rows(k3)
    k3b = k3.astype(BF16)
    kk = per_value_head(_bdot_nt(k3b, k3b))
    qk = per_value_head(_bdot_nt(q3.astype(BF16), k3b))
    q_v, k_v = per_value_head(q3), per_value_head(k3)
    colv = jnp.stack([gcum[:, h:h + 1] for h in heads])
    rowv = jnp.stack([gcum_t[h:h + 1, :c] for h in heads])
    bcol = jnp.stack([beta[:, h:h + 1] for h in heads])
    glast = colv[:, c - 1:c, :]
    ecol = jnp.exp(colv)
    dec = jnp.where(incl, jnp.exp(jnp.where(incl, colv - rowv, 0.0)), 0.0)
    a = jnp.where(strict, (bcol * kk) * dec, 0.0)
    tinv = _unit_lower_inverse(a)
    rhs = jnp.concatenate([v3 * bcol, (k_v * bcol) * ecol], axis=-1)
    th, tl = _split2(tinv)
    rb = rhs.astype(BF16)
    sol = _bdot(th, rb) + _bdot(tl, rb)
    u, w = sol[:, :, :GD_DV], sol[:, :, GD_DV:]
    s = s_ref[...]
    sb = s.astype(BF16)
    v_new = u - _bdot(w.astype(BF16), sb)
    o = _bdot((q_v * ecol).astype(BF16), sb) + _bdot((qk * dec).astype(BF16), v_new.astype(BF16))
    zpad = jnp.zeros((hv, LANES - c, GD_DV), F32)
    kd_t = jnp.swapaxes(jnp.concatenate([k_v * jnp.exp(glast - colv), zpad], axis=1), 1, 2)
    vn_pad = jnp.concatenate([v_new, zpad], axis=1)
    s_ref[...] = s * jnp.exp(glast) + _bdot(kd_t.astype(BF16), vn_pad.astype(BF16))
    on = _rmsnorm_rows(o, nw_ref[...])
    for h in heads:
        sl = slice(h * GD_DV, (h + 1) * GD_DV)
        o_ref[:, sl] = (on[h] * _silu(z_ref[:, sl])).astype(o_ref.dtype)

    @pl.when(n == pl.num_programs(1) - 1)
    def _():
        sout_ref[...] = s_ref[...]


def _gd_chunked(proj, gates, gw, batch, seq):
    c = GD_CHUNK
    nch = seq // c
    cdim, width, hv = gw['cdim'], gw['width'], gw['hv']
    row = lambda w, off: pl.BlockSpec((c, w), lambda b, n: (b * nch + n, off))
    par = lambda r, w: pl.BlockSpec((r, w), lambda b, n: (0, 0))
    return pl.pallas_call(
        _gd_chunk_kernel,
        out_shape=(jax.ShapeDtypeStruct((batch * seq, width), BF16),
                   jax.ShapeDtypeStruct((batch, hv, GD_DK, GD_DV), F32)),
        grid=(batch, nch),
        in_specs=[row(cdim, 0), row(width, cdim // width), row(LANES, 0), row(LANES, 1),
                  par(CONV_W, cdim), par(1, LANES), par(1, LANES), par(1, GD_DV)],
        out_specs=(pl.BlockSpec((c, width), lambda b, n: (b * nch + n, 0)),
                   pl.BlockSpec((None, hv, GD_DK, GD_DV), lambda b, n: (b, 0, 0, 0))),
        scratch_shapes=[pltpu.VMEM((c + SUBLANES, cdim), F32), pltpu.VMEM((hv, GD_DK, GD_DV), F32)],
        compiler_params=_params(("parallel", "arbitrary")), name="gd_chunked")(
            proj, proj, gates, gates, gw['conv_w'], gw['neg_a'], gw['dt_bias'], gw['norm'])


def _gd_pre_step_kernel(qkv_ref, braw_ref, araw_ref, taps_ref, cw_ref, nega_ref, dtb_ref,
                        q_ref, k_ref, v_ref, beta_ref, eg_ref):
    hk = q_ref.shape[1] // GD_DK
    act = _conv_silu_step(qkv_ref[...], taps_ref, cw_ref, None)
    for kh in range(hk):
        sl = slice(kh * GD_DK, (kh + 1) * GD_DK)
        q_ref[:, sl] = _l2norm_rows(act[:, kh * GD_DK:(kh + 1) * GD_DK]) * (GD_DK ** -0.5)
        k_ref[:, sl] = _l2norm_rows(act[:, (hk + kh) * GD_DK:(hk + kh + 1) * GD_DK])
    v_ref[...] = act[:, 2 * hk * GD_DK:]
    beta_ref[...] = _sigmoid(braw_ref[...])
    eg_ref[...] = jnp.exp(nega_ref[...] * _softplus(araw_ref[...] + dtb_ref[...]))


def _gd_pre_step(proj, gates, taps, gw):
    rows = proj.shape[0]
    cdim, width, hv = gw['cdim'], gw['width'], gw['hv']
    qk_w = (cdim - width) // 2
    sd = lambda w: jax.ShapeDtypeStruct((rows, w), F32)
    return pl.pallas_call(
        _gd_pre_step_kernel, out_shape=(sd(qk_w), sd(qk_w), sd(width), sd(LANES), sd(LANES)),
        name="gd_pre_step", compiler_params=pltpu.CompilerParams(vmem_limit_bytes=V7X_VMEM_LIMIT_BYTES))(
            proj[:, :cdim], gates[:, :LANES], gates[:, LANES:], taps,
            gw['conv_w'], gw['neg_a'], gw['dt_bias'])


def _gd_state_step_kernel(q_ref, k_ref, v_ref, beta_ref, eg_ref, z_ref, nw_ref, s_ref, o_ref, sout_ref):
    rows = q_ref.shape[0]
    nk = q_ref.shape[1] // GD_DK
    rep = (v_ref.shape[1] // GD_DV) // nk
    nw = nw_ref[...]
    zrows = jnp.zeros((SUBLANES - 2, GD_DK), F32)
    for kh in range(nk):
        ksl = slice(kh * GD_DK, (kh + 1) * GD_DK)
        q8, k8 = q_ref[:, ksl], k_ref[:, ksl]
        k_t = _pad_to_square_t(k8, GD_DK)
        for b in range(rows):
            qb, kb = q8[b:b + 1, :], k8[b:b + 1, :]
            kq = jnp.concatenate([kb, qb, zrows], axis=0).astype(BF16)
            qk = jnp.sum(qb * kb, axis=-1, keepdims=True)
            for r in range(rep):
                h = kh * rep + r
                vsl = slice(h * GD_DV, (h + 1) * GD_DV)
                s = s_ref[b, h]
                ks_qs = _dot(kq, s.astype(BF16))
                eg = eg_ref[b:b + 1, vsl]
                beta = beta_ref[b:b + 1, vsl]
                v_new = beta * (v_ref[b:b + 1, vsl] - eg * ks_qs[0:1, :])
                o = eg * ks_qs[1:2, :] + qk * v_new
                sout_ref[b, h] = s * eg[:, 0:1] + k_t[:, b:b + 1] * v_new
                o_ref[b:b + 1, vsl] = _rmsnorm_rows(o, nw) * _silu(z_ref[b:b + 1, vsl])


def _gd_state_step(proj, qn, kn, v, beta_ch, eg_ch, state, gw, heads_per_step=4):
    rows = qn.shape[0]
    cdim, width, hv = gw['cdim'], gw['width'], gw['hv']
    steps = hv // heads_per_step
    kw = qn.shape[1] // steps
    vw = width // steps
    blk = lambda w, base=0: pl.BlockSpec((rows, w), lambda g, base=base: (0, base + g))
    sspec = pl.BlockSpec((rows, heads_per_step, GD_DK, GD_DV), lambda g: (0, g, 0, 0))
    return pl.pallas_call(
        _gd_state_step_kernel,
        out_shape=(jax.ShapeDtypeStruct((rows, width), F32), jax.ShapeDtypeStruct(state.shape, F32)),
        grid=(steps,),
        in_specs=[blk(kw), blk(kw), blk(vw), blk(vw), blk(vw), blk(vw, cdim // vw),
                  pl.BlockSpec((1, GD_DV), lambda g: (0, 0)), sspec],
        out_specs=(blk(vw), sspec),
        compiler_params=_params(("parallel",)), name="gd_state_step")(
            qn, kn, v, beta_ch, eg_ch, proj, gw['norm'], state)


def _gd_weights(w_in, conv_w, a_log, dt_bias, norm_w, w_out):
    f = F32
    hv = a_log.shape[0]
    cdim = conv_w.shape[1]
    width = w_out.shape[0]
    pad = LANES - hv
    base = cdim + width
    zeros = jnp.zeros((w_in.shape[0], pad), w_in.dtype)
    w_gate = jnp.concatenate([w_in[:, base:base + hv], zeros, w_in[:, base + hv:], zeros], axis=1)
    return dict(
        w_in=_weight_bf16(w_in, base), w_gate=w_gate.astype(f), conv_w=conv_w.astype(f),
        neg_a=jnp.pad(-jnp.exp(a_log.astype(f)), (0, pad)).reshape(1, LANES),
        dt_bias=jnp.pad(dt_bias.astype(f), (0, pad)).reshape(1, LANES),
        norm=norm_w.astype(f).reshape(1, -1), w_out=_weight_bf16(w_out), cdim=cdim, width=width, hv=hv)


def _gd_layer(x, mods, state, gw, *, prompt, batch, seq, tm):
    g, scale, shift, gate = mods
    cdim, width, hv = gw['cdim'], gw['width'], gw['hv']
    proj = _in_proj(x, g, scale, shift, gw['w_in'], batch_kind=prompt, rows_per_batch=seq, tm=tm, tn=MM_TN,
                    name="gd_in_proj")
    gates = _gate_proj(x, g, scale, shift, gw['w_gate'], batch_kind=prompt, rows_per_batch=seq, tm=tm,
                       name="gd_gate_proj")
    if prompt:
        a, ssm = _gd_chunked(proj, gates, gw, batch, seq)
        conv_new = proj.reshape(batch, seq, -1)[:, seq - (CONV_W - 1):, :cdim]
        x_new = _out_proj([('row', a)], gw['w_out'], x, gate, batch_kind=True, rows_per_batch=seq, tm=tm, tn=MM_TN,
                          name="gd_out_proj")
    else:
        conv_buf, ssm_in = state
        taps = jnp.swapaxes(conv_buf.astype(F32), 0, 1)
        qn, kn, v, beta, eg = _gd_pre_step(proj, gates, taps, gw)
        beta_ch = jnp.repeat(beta[:, :hv], GD_DV, axis=1)
        eg_ch = jnp.repeat(eg[:, :hv], GD_DV, axis=1)
        a, ssm = _gd_state_step(proj, qn, kn, v, beta_ch, eg_ch, ssm_in.astype(F32), gw)
        conv_new = jnp.concatenate([conv_buf.astype(F32)[:, 1:], proj[:, None, :cdim]], axis=1)
        x_new = _out_proj([('row', a)], gw['w_out'], x, gate, batch_kind=False, rows_per_batch=seq, tm=tm, tn=MM_TN,
                          name="gd_out_proj", prologue=_cast_prologue)
    return x_new, conv_new, ssm


AT_DIM = 128
IDX_DIM = 128
TOPK_MAX = 256
REL_BUCKETS = 32
REL_MAX_DIST = 128
AT_TILE = 256
INT32_MIN = -2 ** 31
_NEG_BITS = int(np.float32(NEG).view(np.int32))
NEG_SORT_KEY = _NEG_BITS ^ 0x7FFFFFFF if _NEG_BITS < 0 else _NEG_BITS


def _bucket_starts():
    d = np.arange(0, REL_MAX_DIST + 1)
    exact = REL_BUCKETS // 2
    far = exact + (np.log(np.maximum(d, 1).astype(np.float32) / exact) / math.log(REL_MAX_DIST / exact)
                   * (REL_BUCKETS - exact)).astype(np.int32)
    bucket = np.where(d < exact, d, np.minimum(far, REL_BUCKETS - 1))
    assert np.all(np.diff(bucket) >= 0) and bucket[-1] == REL_BUCKETS - 1
    return [int(np.argmax(bucket >= b)) for b in range(REL_BUCKETS)]


def _bias_from_dist(dist, value_of_bucket):
    starts = _bucket_starts()
    val = value_of_bucket(REL_BUCKETS - 1)
    for b in range(REL_BUCKETS - 2, -1, -1):
        val = jnp.where(dist < starts[b + 1], value_of_bucket(b), val)
    return val


def _sort_key(x):
    x = jnp.where(x == 0.0, 0.0, x)
    b = pltpu.bitcast(x, jnp.int32)
    return jnp.where(b < 0, b ^ jnp.int32(0x7FFFFFFF), b)


def _kth_largest_key(count_ge, shape, k):
    def body(it, ans):
        cand = ans | jnp.left_shift(jnp.int32(1), 31 - it)
        cnt = count_ge(cand ^ jnp.int32(INT32_MIN))
        return jnp.where(cnt >= k, cand, ans)

    ans = lax.fori_loop(0, 32, body, jnp.zeros(shape, jnp.int32))
    return ans ^ jnp.int32(INT32_MIN)


def _relbias_tiles_kernel(rb_ref, o_ref):
    delta = pl.program_id(0) * AT_TILE
    h = pl.program_id(1)
    i = lax.broadcasted_iota(jnp.int32, (AT_TILE, AT_TILE), 0)
    j = lax.broadcasted_iota(jnp.int32, (AT_TILE, AT_TILE), 1)
    o_ref[...] = _bias_from_dist(delta + i - j, lambda b: rb_ref[b, h]) - rb_ref[REL_BUCKETS - 1, h]


def _relbias_tiles(rel_bias):
    heads = rel_bias.shape[1]
    ntile = 2
    assert ntile * AT_TILE - (AT_TILE - 1) >= REL_MAX_DIST
    return pl.pallas_call(
        _relbias_tiles_kernel, out_shape=jax.ShapeDtypeStruct((ntile, heads, AT_TILE, AT_TILE), F32),
        grid=(ntile, heads),
        in_specs=[pl.BlockSpec(memory_space=pltpu.SMEM)],
        out_specs=pl.BlockSpec((None, None, AT_TILE, AT_TILE), lambda d, h: (d, h, 0, 0)),
        compiler_params=_params(("parallel", "parallel")), name="at_relbias_tiles")(rel_bias.astype(F32))


def _at_index_kernel(qi_ref, wi_ref, ki_ref, o_ref, keys, cnt, *, k_sel, score_scale):
    qb = pl.program_id(1)
    tq = qi_ref.shape[0]
    nkb = keys.shape[0]
    tk = keys.shape[2]
    nh = qi_ref.shape[1] // IDX_DIM
    wsc = wi_ref[...] * score_scale
    qpos = qb * tq + lax.broadcasted_iota(jnp.int32, (tq, tk), 0)
    kloc = lax.broadcasted_iota(jnp.int32, (tq, tk), 1)
    neg_key = _sort_key(jnp.full((tq, tk), NEG, F32))

    for kb in range(nkb):
        @pl.when(kb <= qb)
        def _():
            kblk = ki_ref[kb * tk:(kb + 1) * tk, :].astype(BF16)
            sc = jnp.zeros((tq, tk), F32)
            for h in range(nh):
                d = _dot_nt(qi_ref[:, h * IDX_DIM:(h + 1) * IDX_DIM].astype(BF16), kblk)
                sc = sc + wsc[:, h:h + 1] * jnp.maximum(d, 0.0)
            adm = kb * tk + kloc <= qpos
            keys[kb] = _sort_key(jnp.where(adm, sc, NEG))

        @pl.when(kb > qb)
        def _():
            keys[kb] = neg_key

    def count_ge(t):
        cnt[...] = jnp.where(keys[0] >= t, 1, 0)
        for kb in range(1, nkb):
            @pl.when(kb <= qb)
            def _():
                cnt[...] += jnp.where(keys[kb] >= t, 1, 0)
        beyond = (nkb - 1 - qb) * tk
        return jnp.sum(cnt[...], axis=1, keepdims=True) + jnp.where(t <= NEG_SORT_KEY, beyond, 0)

    thr = _kth_largest_key(count_ge, (tq, 1), k_sel)
    n_ge = count_ge(thr)
    has_ties = jnp.max(n_ge) > k_sel

    @pl.when(jnp.logical_not(has_ties))
    def _():
        for kb in range(nkb):
            adm = kb * tk + kloc <= qpos
            sel = jnp.logical_and(keys[kb] >= thr, adm)
            o_ref[kb] = jnp.where(sel, 0.0, MASKED).T.astype(o_ref.dtype)

    @pl.when(has_ties)
    def _():
        acc = jnp.zeros((tq, tk), jnp.int32)
        for kb in range(nkb):
            acc = acc + jnp.where(keys[kb] > thr, 1, 0)
        room = (k_sel - jnp.sum(acc, axis=1, keepdims=True)).astype(F32)
        upper = jnp.where(lax.broadcasted_iota(jnp.int32, (tk, tk), 0) <= lax.broadcasted_iota(jnp.int32, (tk, tk), 1),
                          1.0, 0.0).astype(BF16)
        seen = jnp.zeros((tq, 1), F32)
        for kb in range(nkb):
            key = keys[kb]
            eq = key == thr
            eqf = jnp.where(eq, 1.0, 0.0)
            rank = seen + _dot(eqf.astype(BF16), upper)
            seen = seen + jnp.sum(eqf, axis=1, keepdims=True)
            adm = kb * tk + kloc <= qpos
            sel = jnp.logical_and(jnp.logical_or(key > thr, jnp.logical_and(eq, rank <= room)), adm)
            o_ref[kb] = jnp.where(sel, 0.0, MASKED).T.astype(o_ref.dtype)


def _at_index(proj, aw, batch, seq, k_sel):
    tq = tk = AT_TILE
    nq = seq // tq
    width = aw['width']
    nh = aw['idx_heads']
    qio = (4 * width) // (nh * IDX_DIM)
    kio = (4 * width + nh * IDX_DIM) // IDX_DIM
    kern = functools.partial(_at_index_kernel, k_sel=k_sel, score_scale=IDX_DIM ** -0.5 * nh ** -0.5)
    return pl.pallas_call(
        kern, out_shape=jax.ShapeDtypeStruct((batch * nq, seq // tk, tq, tk), BF16), grid=(batch, nq),
        in_specs=[pl.BlockSpec((tq, nh * IDX_DIM), lambda b, q: (b * nq + q, qio)),
                  pl.BlockSpec((tq, LANES), lambda b, q: (b * nq + q, kio + 1)),
                  pl.BlockSpec((seq, IDX_DIM), lambda b, q: (b, kio))],
        out_specs=pl.BlockSpec((None, seq // tk, tq, tk), lambda b, q: (b * nq + q, 0, 0, 0)),
        scratch_shapes=[pltpu.VMEM((seq // tk, tq, tk), jnp.int32), pltpu.VMEM((tq, tk), jnp.int32)],
        compiler_params=_params(("parallel", "parallel")), name="at_index")(proj, proj, proj)


AT_HEAD_GROUP = 8
MASKED = 2.0 * NEG


def _at_attend_kernel(q_ref, k_ref, vt_ref, z_ref, mask_ref, bias_ref, o_ref, acc, m_scr, l_scr):
    qb = pl.program_id(2)
    t = q_ref.shape[0]
    hg = q_ref.shape[1] // AT_DIM
    acc[...] = jnp.zeros_like(acc)
    m_scr[...] = jnp.full(m_scr.shape, NEG, F32)
    l_scr[...] = jnp.zeros_like(l_scr)

    heads = [slice(h * AT_DIM, (h + 1) * AT_DIM) for h in range(hg)]
    q3t = jnp.stack([q_ref[:, sl].T for sl in heads]).astype(BF16)

    def key_tile(kb, bias):
        rows = pl.ds(pl.multiple_of(kb * t, t), t)
        k3 = jnp.stack([k_ref[rows, sl] for sl in heads])
        s_t = _bdot(k3, q3t) + mask_ref[kb].astype(F32)
        if bias is not None:
            s_t = s_t + bias
        m_old = m_scr[...]
        m_new = jnp.maximum(m_old, jnp.max(s_t, axis=1, keepdims=True))
        alpha = jnp.exp(m_old - m_new)
        p_t = jnp.exp(s_t - m_new)
        l_scr[...] = alpha * l_scr[...] + jnp.sum(p_t, axis=1, keepdims=True)
        acc[...] = alpha * acc[...] + _bdot(vt_ref[:, kb], p_t.astype(BF16))
        m_scr[...] = m_new

    def far_tile(kb, carry):
        key_tile(kb, None)
        return carry

    lax.fori_loop(0, jnp.maximum(qb - 1, 0), far_tile, 0)

    @pl.when(qb >= 1)
    def _():
        key_tile(qb - 1, bias_ref[1])

    key_tile(qb, bias_ref[0])
    o_t = acc[...] / l_scr[...]
    for h, sl in enumerate(heads):
        o_ref[:, sl] = (o_t[h].T * _silu(z_ref[:, sl])).astype(o_ref.dtype)


def _at_attend(proj, proj_bf, maskadd_t, tiles_t, aw, batch, seq):
    t = AT_TILE
    nq = seq // t
    width = aw['width']
    heads = width // AT_DIM
    hg = AT_HEAD_GROUP
    gw = hg * AT_DIM
    ng = width // gw
    v_t = proj_bf[:, 2 * width:3 * width].reshape(batch, nq, t, heads, AT_DIM).transpose(0, 3, 1, 4, 2)
    return pl.pallas_call(
        _at_attend_kernel, out_shape=jax.ShapeDtypeStruct((batch * seq, width), BF16), grid=(batch, ng, nq),
        in_specs=[pl.BlockSpec((t, gw), lambda b, g, q: (b * nq + q, g)),
                  pl.BlockSpec((seq, gw), lambda b, g, q: (b, ng + g)),
                  pl.BlockSpec((None, hg, nq, AT_DIM, t), lambda b, g, q: (b, g, 0, 0, 0)),
                  pl.BlockSpec((t, gw), lambda b, g, q: (b * nq + q, 3 * ng + g)),
                  pl.BlockSpec((None, nq, t, t), lambda b, g, q: (b * nq + q, 0, 0, 0)),
                  pl.BlockSpec((2, hg, t, t), lambda b, g, q: (0, g, 0, 0))],
        out_specs=pl.BlockSpec((t, gw), lambda b, g, q: (b * nq + q, g)),
        scratch_shapes=[pltpu.VMEM((hg, AT_DIM, t), F32), pltpu.VMEM((hg, 1, t), F32), pltpu.VMEM((hg, 1, t), F32)],
        compiler_params=_params(("parallel", "parallel", "arbitrary")), name="at_attend")(
            proj, proj_bf, v_t, proj, maskadd_t, tiles_t)


def _at_weights(w_in, rel_bias, w_out):
    width = w_out.shape[0]
    heads = rel_bias.shape[1]
    idx_heads = (w_in.shape[1] - 4 * width - IDX_DIM) // (IDX_DIM + 1)
    col_scale = jnp.where(jnp.arange(w_in.shape[1]) < width, AT_DIM ** -0.5, 1.0).astype(F32)
    return dict(w_in=_weight_bf16(w_in, col_scale=col_scale), rel_bias=rel_bias.astype(F32),
                w_out=_weight_bf16(w_out), width=width, heads=heads, idx_heads=idx_heads)


def _at_rows_kernel(k_ref, v_ref, ki_ref, ko_ref, vo_ref, kio_ref):
    nh = ko_ref.shape[1]
    heads = lambda x: jnp.stack([x[:, h * AT_DIM:(h + 1) * AT_DIM] for h in range(nh)], axis=1)
    ko_ref[...] = heads(k_ref[...])
    vo_ref[...] = heads(v_ref[...])
    kio_ref[...] = ki_ref[...]


def _at_rows(proj, aw, lead, tm=256):
    width, heads = aw['width'], aw['heads']
    m = proj.shape[0]
    tm = min(tm, m)
    kio = (4 * width + aw['idx_heads'] * IDX_DIM) // IDX_DIM
    row4 = pl.BlockSpec((tm, heads, AT_DIM), lambda i: (i, 0, 0))
    k, v, ki = pl.pallas_call(
        _at_rows_kernel,
        out_shape=(jax.ShapeDtypeStruct((m, heads, AT_DIM), F32), jax.ShapeDtypeStruct((m, heads, AT_DIM), F32),
                   jax.ShapeDtypeStruct((m, IDX_DIM), F32)),
        grid=(m // tm,),
        in_specs=[pl.BlockSpec((tm, width), lambda i: (i, 1)), pl.BlockSpec((tm, width), lambda i: (i, 2)),
                  pl.BlockSpec((tm, IDX_DIM), lambda i: (i, kio))],
        out_specs=(row4, row4, pl.BlockSpec((tm, IDX_DIM), lambda i: (i, 0))),
        compiler_params=_params(("parallel",)), name="at_rows")(proj, proj, proj)
    return (k.reshape(lead + (heads, AT_DIM)), v.reshape(lead + (heads, AT_DIM)), ki.reshape(lead + (IDX_DIM,)))


def _at_layer_prompt(x, mods, aw, final_g, *, batch, seq, tm):
    g, scale, shift, gate = mods
    proj, proj_bf = _in_proj(x, g, scale, shift, aw['w_in'], batch_kind=True, rows_per_batch=seq, tm=tm, tn=MM_TN,
                             name="at_in_proj", bf16_copy=True)
    k_sel = min(TOPK_MAX, seq // 4)
    maskadd = _at_index(proj, aw, batch, seq, k_sel)
    tiles = _relbias_tiles(aw['rel_bias'])
    a = _at_attend(proj, proj_bf, maskadd, jnp.swapaxes(tiles, 2, 3), aw, batch, seq)
    d = x.shape[1]
    y = _fused_matmul([('row', a)], aw['w_out'], [('tile', x), ('batchcol', gate), ('col', final_g.reshape(1, d))],
                      prologue=None, epilogue=_residual_norm_epilogue, out_dtype=F32, tm=min(tm, MM_TM_WIDE_ROWS),
                      tn=d, rows_per_batch=seq, name="at_out_proj_norm")
    return (y,) + _at_rows(proj, aw, (batch, seq))


AT_PAGES_PER_STEP = 16


def _at_page_scores_kernel(pt_ref, qi_ref, w_ref, kidx_ref, o_ref, kbuf, sem):
    b, j = pl.program_id(0), pl.program_id(1)
    nj = pl.num_programs(1)
    npg = kbuf.shape[1]
    step = b * nj + j
    last_step = pl.num_programs(0) * nj - 1

    def page_copy(s, i, slot):
        sb, sj = s // nj, s % nj
        return pltpu.make_async_copy(kidx_ref.at[pt_ref[sb, sj * npg + i]], kbuf.at[slot, i], sem.at[slot])

    def start_all(s, slot):
        for i in range(npg):
            page_copy(s, i, slot).start()

    slot = step % 2

    @pl.when(step == 0)
    def _():
        start_all(step, slot)

    @pl.when(step < last_step)
    def _():
        start_all(step + 1, 1 - slot)

    for i in range(npg):
        page_copy(step, i, slot).wait()
    qi = qi_ref[...].astype(BF16)
    w = w_ref[...]
    for i in range(npg):
        d = _dot_nt(qi, kbuf[slot, i].astype(BF16))
        o_ref[i:i + 1, :] = jnp.sum(w * jnp.maximum(d, 0.0), axis=0, keepdims=True)


def _at_page_scores(qi3, w3, cache_kidx, page_table):
    rows, nh, _ = qi3.shape
    npages = page_table.shape[1]
    page = cache_kidx.shape[1]
    npg = math.gcd(npages, AT_PAGES_PER_STEP)
    grid_spec = pltpu.PrefetchScalarGridSpec(
        num_scalar_prefetch=1, grid=(rows, npages // npg),
        in_specs=[pl.BlockSpec((None, nh, IDX_DIM), lambda b, j, pt: (b, 0, 0)),
                  pl.BlockSpec((None, nh, IDX_DIM), lambda b, j, pt: (b, 0, 0)),
                  pl.BlockSpec(memory_space=pl.ANY)],
        out_specs=pl.BlockSpec((None, npg, page), lambda b, j, pt: (b, j, 0)),
        scratch_shapes=[pltpu.VMEM((2, npg, page, IDX_DIM), F32), pltpu.SemaphoreType.DMA((2,))])
    return pl.pallas_call(
        _at_page_scores_kernel, out_shape=jax.ShapeDtypeStruct((rows, npages, page), F32), grid_spec=grid_spec,
        compiler_params=_params(("arbitrary", "arbitrary")), name="at_page_scores")(page_table, qi3, w3, cache_kidx)


def _at_sample_select_kernel(sc_ref, qi_ref, w_ref, kin_ref, gidx_ref, newadd_ref, rank_scr, *, k_sel):
    npages, page = sc_ref.shape
    upper = jnp.where(lax.broadcasted_iota(jnp.int32, (page, page), 0) <= lax.broadcasted_iota(jnp.int32, (page, page), 1),
                      1.0, 0.0).astype(BF16)
    lower = jnp.where(lax.broadcasted_iota(jnp.int32, (npages, npages), 1) < lax.broadcasted_iota(jnp.int32, (npages, npages), 0),
                      1.0, 0.0).astype(BF16)

    def total(x):
        return jnp.sum(jnp.sum(x, axis=1, keepdims=True), axis=0, keepdims=True)

    def position_rank(flags):
        row_cnt = jnp.broadcast_to(jnp.sum(flags, axis=1, keepdims=True), (npages, page))
        return _dot(lower, row_cnt.astype(BF16)) + _dot(flags.astype(BF16), upper)

    keys = _sort_key(sc_ref[...])
    dots = jnp.sum(qi_ref[...] * kin_ref[...], axis=1, keepdims=True)
    s_new = jnp.sum(w_ref[:, 0:1] * jnp.maximum(dots, 0.0), axis=0, keepdims=True)
    key_new = _sort_key(s_new)

    def count_ge(t):
        return total(jnp.where(keys >= t, 1, 0)) + jnp.where(key_new >= t, 1, 0)

    thr = _kth_largest_key(count_ge, (1, 1), k_sel)
    n_gt = total(jnp.where(keys > thr, 1.0, 0.0)) + jnp.where(key_new > thr, 1.0, 0.0)
    room = k_sel - n_gt
    eq = keys == thr
    eqf = jnp.where(eq, 1.0, 0.0)
    sel = jnp.logical_or(keys > thr, jnp.logical_and(eq, position_rank(eqf) <= room))
    sel_new = jnp.logical_or(key_new > thr, jnp.logical_and(key_new == thr, total(eqf) + 1.0 <= room))
    newadd_ref[...] = jnp.broadcast_to(jnp.where(sel_new, 0.0, NEG), (1, page))

    self_f = jnp.where(sel, 1.0, 0.0)
    rank_scr[...] = jnp.where(sel, position_rank(self_f) - 1.0, -1.0)
    jidx = lax.broadcasted_iota(jnp.int32, (k_sel, page), 0).astype(F32)
    lane = lax.broadcasted_iota(jnp.int32, (page, LANES), 1)
    pick = jnp.where(lane == 0, lax.broadcasted_iota(jnp.int32, (page, LANES), 0).astype(F32),
                     jnp.where(lane <= 2, 1.0, 0.0)).astype(BF16)
    out_lane = lax.broadcasted_iota(jnp.int32, (k_sel, LANES), 1)

    def add_pages(g, acc):
        ranks = rank_scr[pl.ds(pl.multiple_of(g * SUBLANES, SUBLANES), SUBLANES), :]
        for i in range(SUBLANES):
            onehot = jnp.where(ranks[i:i + 1, :] == jidx, 1.0, 0.0).astype(BF16)
            slot = lax.convert_element_type(g * SUBLANES + i, F32)
            acc = acc + _dot(onehot, pick) * jnp.where(out_lane == 1, slot, 1.0)
        return acc

    assert npages % SUBLANES == 0
    gidx_ref[...] = lax.fori_loop(0, npages // SUBLANES, add_pages, jnp.zeros((k_sel, LANES), F32))


def _at_sample_select(scores, qi3, w3, ki_new, k_sel):
    rows, npages, page = scores.shape
    nh = qi3.shape[1]
    assert page == LANES and page >= REL_MAX_DIST
    kern = functools.partial(_at_sample_select_kernel, k_sel=k_sel)
    return pl.pallas_call(
        kern, out_shape=(jax.ShapeDtypeStruct((rows, k_sel, LANES), F32), jax.ShapeDtypeStruct((rows, 1, page), F32)),
        grid=(rows,),
        in_specs=[pl.BlockSpec((None, npages, page), lambda b: (b, 0, 0)),
                  pl.BlockSpec((None, nh, IDX_DIM), lambda b: (b, 0, 0)),
                  pl.BlockSpec((None, nh, IDX_DIM), lambda b: (b, 0, 0)),
                  pl.BlockSpec((None, 1, IDX_DIM), lambda b: (b, 0, 0))],
        out_specs=(pl.BlockSpec((None, k_sel, LANES), lambda b: (b, 0, 0)),
                   pl.BlockSpec((None, 1, page), lambda b: (b, 0, 0))),
        scratch_shapes=[pltpu.VMEM((npages, page), F32)],
        compiler_params=_params(("parallel",)), name="at_sample_select")(
            scores, qi3, w3, ki_new.reshape(rows, 1, IDX_DIM))


def _at_gather_attend_kernel(pt_ref, slot_ref, off_ref, q_ref, ck_ref, cv_ref, pos_ref, newadd_ref, rbt_ref,
                             kn_ref, vn_ref, z_ref, o_ref, kg, vg, sem, *, past):
    b = pl.program_id(0)
    nsel, nh, d = kg.shape
    ncol = nsel * nh

    def row_copies(j):
        page = pt_ref[b, slot_ref[b, j]]
        off = off_ref[b, j]
        return (pltpu.make_async_copy(ck_ref.at[page, off], kg.at[j], sem.at[0]),
                pltpu.make_async_copy(cv_ref.at[page, off], vg.at[j], sem.at[1]))

    def start(j, carry):
        for c in row_copies(j):
            c.start()
        return carry

    def wait(j, carry):
        for c in row_copies(j):
            c.wait()
        return carry

    lax.fori_loop(0, nsel, start, 0)
    qs = q_ref[...].astype(BF16)
    pos = pos_ref[...]
    own = (lax.broadcasted_iota(jnp.int32, (nh, ncol), 1) & (nh - 1)) == lax.broadcasted_iota(jnp.int32, (nh, ncol), 0)
    keep = jnp.logical_and(own, pos >= 0)
    bias = _bias_from_dist(jnp.maximum(past - pos, 0), lambda bk: rbt_ref[:, bk:bk + 1])
    nadd = newadd_ref[:, 0:1]
    s_n = jnp.sum(qs.astype(F32) * kn_ref[...], axis=1, keepdims=True) + rbt_ref[:, 0:1] + nadd
    lax.fori_loop(0, nsel, wait, 0)
    s = jnp.where(keep, _dot_nt(qs, kg[...].reshape(ncol, d).astype(BF16)) + bias, NEG)
    m = jnp.maximum(jnp.max(s, axis=1, keepdims=True), s_n)
    pr = jnp.where(keep, jnp.exp(s - m), 0.0)
    p_n = jnp.where(nadd < 0.0, 0.0, jnp.exp(s_n - m))
    l = jnp.sum(pr, axis=1, keepdims=True) + p_n
    o = _dot(pr.astype(BF16), vg[...].reshape(ncol, d).astype(BF16)) + p_n * vn_ref[...]
    o_ref[...] = o / l * _silu(z_ref[...])


def _at_gather_attend(q3, cache_k, cache_v, page_table, slot, off, pos_cols, newadd, rbt, kn3, vn3, z3):
    rows, nh, d = q3.shape
    assert nh & (nh - 1) == 0
    nsel = slot.shape[1]
    ncol = nsel * nh
    past = page_table.shape[1] * cache_k.shape[1]
    row3 = pl.BlockSpec((None, nh, d), lambda b, *_: (b, 0, 0))
    grid_spec = pltpu.PrefetchScalarGridSpec(
        num_scalar_prefetch=3, grid=(rows,),
        in_specs=[row3, pl.BlockSpec(memory_space=pl.ANY), pl.BlockSpec(memory_space=pl.ANY),
                  pl.BlockSpec((None, 1, ncol), lambda b, *_: (b, 0, 0)),
                  pl.BlockSpec((None, 1, newadd.shape[-1]), lambda b, *_: (b, 0, 0)),
                  pl.BlockSpec(rbt.shape, lambda b, *_: (0, 0)),
                  row3, row3, row3],
        out_specs=row3,
        scratch_shapes=[pltpu.VMEM((nsel, nh, d), F32), pltpu.VMEM((nsel, nh, d), F32), pltpu.SemaphoreType.DMA((2,))])
    kern = functools.partial(_at_gather_attend_kernel, past=past)
    return pl.pallas_call(
        kern, out_shape=jax.ShapeDtypeStruct((rows, nh, d), F32), grid_spec=grid_spec,
        compiler_params=_params(("arbitrary",)), name="at_gather_attend")(
            page_table, slot, off, q3, cache_k, cache_v, pos_cols, newadd, rbt, kn3, vn3, z3)


def _at_layer_sample(x, mods, cache_k, cache_v, cache_kidx, page_table, aw):
    g, scale, shift, gate = mods
    rows = x.shape[0]
    width, heads, nh = aw['width'], aw['heads'], aw['idx_heads']
    proj = _in_proj(x, g, scale, shift, aw['w_in'], batch_kind=False, rows_per_batch=1, tm=SUBLANES, tn=MM_TN,
                    name="at_in_proj")
    npool, page = cache_k.shape[:2]
    npages = page_table.shape[1]
    past = npages * page
    k_sel = min(TOPK_MAX, (past + 1) // 4)
    o = 4 * width
    qi3 = proj[:, o:o + nh * IDX_DIM].reshape(rows, nh, IDX_DIM)
    ki_new = proj[:, o + nh * IDX_DIM:o + nh * IDX_DIM + IDX_DIM]
    wi = proj[:, o + nh * IDX_DIM + IDX_DIM:o + nh * IDX_DIM + IDX_DIM + nh] * (IDX_DIM ** -0.5 * nh ** -0.5)
    w3 = jnp.broadcast_to(wi[:, :, None], (rows, nh, IDX_DIM))
    scores = _at_page_scores(qi3, w3, cache_kidx.astype(F32), page_table)
    rbt = aw['rel_bias'].T
    gidx, newadd = _at_sample_select(scores, qi3, w3, ki_new, k_sel)
    off = gidx[:, :, 0].astype(jnp.int32)
    slot = gidx[:, :, 1].astype(jnp.int32)
    pos = jnp.where(gidx[:, :, 2] > 0.5, slot * page + off, -1)
    pos_cols = jnp.repeat(pos, heads, axis=-1).reshape(rows, 1, k_sel * heads)
    r3 = lambda t: t.reshape(rows, heads, AT_DIM)
    a = _at_gather_attend(r3(proj[:, :width]), cache_k.astype(F32), cache_v.astype(F32), page_table, slot, off,
                          pos_cols, newadd, rbt, r3(proj[:, width:2 * width]), r3(proj[:, 2 * width:3 * width]),
                          r3(proj[:, 3 * width:4 * width]))
    x_new = _out_proj([('row', a.reshape(rows, width))], aw['w_out'], x, gate, batch_kind=False, rows_per_batch=1,
                      tm=SUBLANES, tn=MM_TN, name="at_out_proj", prologue=_cast_prologue)
    return (x_new,) + _at_rows(proj, aw, (rows, 1))


def kernel(x_prompt, x_sample, state_s5_re, state_s5_im, state_m2_conv, state_m2_ssm, state_gd_conv, state_gd_ssm, cache_k, cache_v, cache_kidx, page_table, c_prompt, c_sample, norm_g, w_mod, b_mod, final_g, s5_w_in, s5_lam_re, s5_lam_im, s5_log_dt, s5_b_re, s5_b_im, s5_c_re, s5_c_im, s5_d, s5_w_glu, s5_b_glu, s5_w_out, m2_w_in, m2_conv_w, m2_conv_b, m2_dt_bias, m2_a_log, m2_d, m2_norm, m2_w_out, gd_w_in, gd_conv_w, gd_a_log, gd_dt_bias, gd_norm, gd_w_out, at_w_in, rel_bias, at_w_out):
    f = F32
    bp, seq, d = x_prompt.shape
    bs = x_sample.shape[0]
    depth = norm_g.shape[0]
    xp = x_prompt.astype(f).reshape(bp * seq, d)
    xs = x_sample.astype(f).reshape(bs, d)

    pad_rows = (-(bs + bp)) % SUBLANES
    c_all = jnp.concatenate([c_sample.astype(f), c_prompt.astype(f), jnp.zeros((pad_rows, d), f)], axis=0)
    mod = _modulation(c_all, w_mod, b_mod)

    def mods(i, prompt):
        g = norm_g[i].astype(f).reshape(1, d)
        rows = mod[i, bs:bs + bp] if prompt else mod[i, :bs]
        shift, scale, gate = rows[:, :d], rows[:, d:2 * d], rows[:, 2 * d:]
        if prompt:
            return g, scale[:, None, :], shift[:, None, :], gate[:, None, :]
        return g, scale, shift, gate

    s5w = _s5_weights(s5_w_in, s5_lam_re, s5_lam_im, s5_log_dt, s5_b_re, s5_b_im, s5_c_re, s5_c_im, s5_d,
                      s5_w_glu, s5_b_glu, s5_w_out)
    tm_p = MM_TM

    xp, s5_re_p, s5_im_p = _s5_layer(xp, mods(0, True), None, s5w, prompt=True, batch=bp, seq=seq, tm=tm_p)
    xs, s5_re_s, s5_im_s = _s5_layer(xs, mods(0, False), (state_s5_re, state_s5_im), s5w, prompt=False,
                                     batch=bs, seq=1, tm=SUBLANES)
    groups, nstate = state_s5_re.shape[1:]
    s5_re_p, s5_im_p = s5_re_p.reshape(bp, groups, nstate), s5_im_p.reshape(bp, groups, nstate)
    s5_re_s, s5_im_s = s5_re_s.reshape(bs, groups, nstate), s5_im_s.reshape(bs, groups, nstate)

    m2w = _m2_weights(m2_w_in, m2_conv_w, m2_conv_b, m2_dt_bias, m2_a_log, m2_d, m2_norm, m2_w_out)
    xp, m2_conv_p, m2_ssm_p = _m2_layer(xp, mods(1, True), None, m2w, prompt=True, batch=bp, seq=seq, tm=tm_p)
    xs, m2_conv_s, m2_ssm_s = _m2_layer(xs, mods(1, False), (state_m2_conv, state_m2_ssm), m2w, prompt=False,
                                        batch=bs, seq=1, tm=SUBLANES)
    m2_ssm_p = m2_ssm_p.reshape((bp,) + state_m2_ssm.shape[1:])
    m2_ssm_s = m2_ssm_s.reshape(state_m2_ssm.shape)

    gdw = _gd_weights(gd_w_in, gd_conv_w, gd_a_log, gd_dt_bias, gd_norm, gd_w_out)
    xp, gd_conv_p, gd_ssm_p = _gd_layer(xp, mods(2, True), None, gdw, prompt=True, batch=bp, seq=seq, tm=tm_p)
    xs, gd_conv_s, gd_ssm_s = _gd_layer(xs, mods(2, False), (state_gd_conv, state_gd_ssm), gdw, prompt=False,
                                        batch=bs, seq=1, tm=SUBLANES)

    atw = _at_weights(at_w_in, rel_bias, at_w_out)
    yp, k_rows_p, v_rows_p, kidx_rows_p = _at_layer_prompt(xp, mods(3, True), atw, final_g.astype(f), batch=bp,
                                                           seq=seq, tm=tm_p)
    xs, k_rows_s, v_rows_s, kidx_rows_s = _at_layer_sample(xs, mods(3, False), cache_k, cache_v, cache_kidx,
                                                           page_table, atw)

    y_prompt = yp.reshape(x_prompt.shape).astype(x_prompt.dtype)
    y_sample = _final_norm(xs, final_g).reshape(x_sample.shape).astype(x_sample.dtype)
    return (y_prompt, y_sample, s5_re_p, s5_im_p, s5_re_s, s5_im_s, m2_conv_p, m2_ssm_p, m2_conv_s, m2_ssm_s,
            gd_conv_p, gd_ssm_p, gd_conv_s, gd_ssm_s,
            k_rows_p, v_rows_p, kidx_rows_p, k_rows_s, v_rows_s, kidx_rows_s)
```

```python
import functools
import math

import numpy as np
import jax
import jax.numpy as jnp
from jax import lax
from jax.experimental import pallas as pl
from jax.experimental.pallas import tpu as pltpu

F32 = jnp.float32
BF16 = jnp.bfloat16

EPS = 1e-6
NEG = -1e30
CONV_W = 4
V7X_VMEM_LIMIT_BYTES = 56 * 1024 * 1024
LANES = 128
SUBLANES = 8
MM_TM = 1024
MM_TM_WIDE_ROWS = 512
MM_TN = 1024

S5_GROUP = 16
S5_STATE = 64
S5_CHUNK = 512
S5_SEG = S5_CHUNK // SUBLANES
S5_BLK_CH = 256
S5_BLK_ST = 1024


def _params(sem):
    return pltpu.CompilerParams(dimension_semantics=sem, vmem_limit_bytes=V7X_VMEM_LIMIT_BYTES)


def _sigmoid(x):
    return 1.0 / (1.0 + jnp.exp(-x))


def _silu(x):
    return x * _sigmoid(x)


def _gelu(x):
    return 0.5 * x * (1.0 + jnp.tanh(math.sqrt(2.0 / math.pi) * (x + 0.044715 * (x * x * x))))


def _softplus(x):
    return jnp.maximum(x, 0.0) + jnp.log1p(jnp.exp(-jnp.abs(x)))


def _dot(a, b):
    return jnp.dot(a, b, preferred_element_type=F32)


def _dot_nt(a, b):
    return lax.dot_general(a, b, (((1,), (1,)), ((), ())), preferred_element_type=F32)


def _split3(x):
    hi = x.astype(BF16)
    r1 = x - hi.astype(F32)
    mid = r1.astype(BF16)
    lo = (r1 - mid.astype(F32)).astype(BF16)
    return hi, mid, lo


def _dot_exact_lhs(sel, x):
    hi, mid, lo = _split3(x)
    return _dot(sel, hi) + (_dot(sel, mid) + _dot(sel, lo))


def _mm_kernel(*refs, n_a, n_e, prologue, epilogue, bf16_copy):
    a_refs = refs[:n_a]
    w_ref = refs[n_a]
    e_refs = refs[n_a + 1:n_a + 1 + n_e]
    o_ref = refs[n_a + 1 + n_e]
    n_out = 2 if bf16_copy else 1
    if prologue is None:
        a = a_refs[0][...]
    else:
        a_scr = refs[n_a + 1 + n_e + n_out]

        @pl.when(pl.program_id(1) == 0)
        def _():
            a_scr[...] = prologue(*[r[...] for r in a_refs]).astype(BF16)

        a = a_scr[...]
    acc = _dot(a, w_ref[...])
    out = epilogue(acc, *[r[...] for r in e_refs])
    o_ref[...] = out.astype(o_ref.dtype)
    if bf16_copy:
        refs[n_a + 2 + n_e][...] = out.astype(BF16)


def _fused_matmul(a_ins, w, e_ins, *, prologue, epilogue, out_dtype, tm, tn, rows_per_batch=None, name,
                  bf16_copy=False):
    m = next(item[1].shape[0] for item in a_ins if item[0] == 'row')
    k, n = w.shape
    tm = min(tm, m)
    tn = next(t for t in (2048, 1024, 768, 512, 384, 256, 128) if t <= tn and n % t == 0)
    assert m % tm == 0
    rpb = rows_per_batch

    def bidx(i):
        return (i * tm) // rpb

    in_specs, args = [], []
    for item in a_ins:
        kind, arr = item[0], item[1]
        wd = item[2] if len(item) > 2 else arr.shape[-1]
        coff = item[3] if len(item) > 3 else 0
        if kind == 'row':
            in_specs.append(pl.BlockSpec((tm, wd), lambda i, j, coff=coff: (i, coff)))
        elif kind == 'vec':
            in_specs.append(pl.BlockSpec((1, wd), lambda i, j: (0, 0)))
        else:
            in_specs.append(pl.BlockSpec((None, 1, wd), lambda i, j: (bidx(i), 0, 0)))
        args.append(arr)
    in_specs.append(pl.BlockSpec((k, tn), lambda i, j: (0, j)))
    args.append(w)
    for item in e_ins:
        kind, arr = item[0], item[1]
        off = (item[2] if len(item) > 2 else 0) // tn
        if kind == 'tile':
            assert len(item) < 3 or item[2] % tn == 0
            in_specs.append(pl.BlockSpec((tm, tn), lambda i, j, off=off: (i, j + off)))
        elif kind == 'col':
            in_specs.append(pl.BlockSpec((1, tn), lambda i, j: (0, j)))
        else:
            in_specs.append(pl.BlockSpec((None, 1, tn), lambda i, j: (bidx(i), 0, j)))
        args.append(arr)
    scratch = [] if prologue is None else [pltpu.VMEM((tm, k), BF16)]
    kern = functools.partial(_mm_kernel, n_a=len(a_ins), n_e=len(e_ins), prologue=prologue, epilogue=epilogue,
                             bf16_copy=bf16_copy)
    out_shape = jax.ShapeDtypeStruct((m, n), out_dtype)
    out_spec = pl.BlockSpec((tm, tn), lambda i, j: (i, j))
    if bf16_copy:
        out_shape, out_spec = (out_shape, jax.ShapeDtypeStruct((m, n), BF16)), (out_spec, out_spec)
    return pl.pallas_call(
        kern, out_shape=out_shape, grid=(m // tm, n // tn), in_specs=in_specs, out_specs=out_spec,
        scratch_shapes=scratch, compiler_params=_params(("parallel", "arbitrary")), name=name)(*args)


def _weight_cast_kernel(*refs, n_valid, scaled, transposed):
    w_ref, o_ref = refs[0], refs[-1]
    tn = o_ref.shape[1]
    col = pl.program_id(0) * tn + lax.broadcasted_iota(jnp.int32, (1, tn), 1)
    w = w_ref[...].astype(F32)
    if transposed:
        w = w.T
    if scaled:
        w = w * refs[1][...]
    o_ref[...] = jnp.where(col < n_valid, w, 0.0).astype(o_ref.dtype)


def _weight_bf16(w, n_cols=None, pad_to=512, col_scale=None, tn=512):
    k, n_in = w.shape
    n_cols = n_in if n_cols is None else n_cols
    n_out = -(-n_cols // pad_to) * pad_to
    tn = math.gcd(n_out, tn)
    transposed = n_in % LANES != 0
    if transposed:
        args, in_specs = [w.T], [pl.BlockSpec((tn, k), lambda j: (j, 0))]
    else:
        args, in_specs = [w], [pl.BlockSpec((k, tn), lambda j: (0, j))]
    if col_scale is not None:
        args.append(jnp.pad(col_scale.astype(F32), (0, n_out - col_scale.shape[0])).reshape(1, n_out))
        in_specs.append(pl.BlockSpec((1, tn), lambda j: (0, j)))
    kern = functools.partial(_weight_cast_kernel, n_valid=n_cols, scaled=col_scale is not None, transposed=transposed)
    return pl.pallas_call(
        kern, out_shape=jax.ShapeDtypeStruct((k, n_out), BF16), grid=(n_out // tn,), in_specs=in_specs,
        out_specs=pl.BlockSpec((k, tn), lambda j: (0, j)),
        compiler_params=_params(("parallel",)), name="weight_cast")(*args)


def _pad_cols(w, mult):
    n = w.shape[-1]
    npad = (-n) % mult
    if npad:
        w = jnp.pad(w, ((0, 0), (0, npad)))
    return w


def _modnorm_prologue(x, g, scale, shift):
    r = x * lax.rsqrt(jnp.mean(x * x, axis=-1, keepdims=True) + EPS) * g
    return r * (1.0 + scale) + shift


def _identity_epilogue(acc):
    return acc


def _residual_epilogue(acc, x, gate):
    return x + gate * acc


def _residual_norm_epilogue(acc, x, gate, g):
    xn = x + gate * acc
    return xn * lax.rsqrt(jnp.mean(xn * xn, axis=-1, keepdims=True) + EPS) * g


def _in_proj(x, g, scale, shift, w, *, batch_kind, rows_per_batch, tm, tn, name, bf16_copy=False):
    kind = 'batch' if batch_kind else 'row'
    return _fused_matmul([('row', x), ('vec', g), (kind, scale), (kind, shift)], w, [],
                         prologue=_modnorm_prologue, epilogue=_identity_epilogue, out_dtype=F32,
                         tm=tm, tn=tn, rows_per_batch=rows_per_batch, name=name, bf16_copy=bf16_copy)


def _gate_proj_kernel(x_ref, g_ref, scale_ref, shift_ref, w_ref, o_ref):
    h = _modnorm_prologue(x_ref[...], g_ref[...], scale_ref[...], shift_ref[...])
    o_ref[...] = _dot_3pass(h, w_ref[...])


def _gate_proj(x, g, scale, shift, w, *, batch_kind, rows_per_batch, tm, name):
    m, d = x.shape
    n = w.shape[1]
    tm = min(tm, m, MM_TM_WIDE_ROWS)
    if batch_kind:
        mod_spec = pl.BlockSpec((None, 1, d), lambda i: ((i * tm) // rows_per_batch, 0, 0))
    else:
        mod_spec = pl.BlockSpec((tm, d), lambda i: (i, 0))
    return pl.pallas_call(
        _gate_proj_kernel, out_shape=jax.ShapeDtypeStruct((m, n), F32), grid=(m // tm,),
        in_specs=[pl.BlockSpec((tm, d), lambda i: (i, 0)), pl.BlockSpec((1, d), lambda i: (0, 0)),
                  mod_spec, mod_spec, pl.BlockSpec((d, n), lambda i: (0, 0))],
        out_specs=pl.BlockSpec((tm, n), lambda i: (i, 0)),
        compiler_params=_params(("parallel",)), name=name)(x, g, scale, shift, w)


def _out_proj(a_ins, w, x, gate, *, batch_kind, rows_per_batch, tm, tn, name, prologue=None):
    kind = 'batchcol' if batch_kind else 'tile'
    return _fused_matmul(a_ins, w, [('tile', x), (kind, gate)], prologue=prologue, epilogue=_residual_epilogue,
                         out_dtype=F32, tm=tm, tn=tn, rows_per_batch=rows_per_batch, name=name)


def _mod_kernel(c_ref, w_ref, b_ref, o_ref):
    o_ref[...] = _dot(c_ref[...].astype(BF16), w_ref[...].astype(BF16)) + b_ref[...]


def _modulation(c_all, w_mod, b_mod, tn=512):
    depth, d, n = w_mod.shape
    rows = c_all.shape[0]
    return pl.pallas_call(
        _mod_kernel, out_shape=jax.ShapeDtypeStruct((depth, rows, n), F32), grid=(depth, n // tn),
        in_specs=[pl.BlockSpec((rows, d), lambda l, j: (0, 0)),
                  pl.BlockSpec((None, d, tn), lambda l, j: (l, 0, j)),
                  pl.BlockSpec((None, 1, tn), lambda l, j: (l, 0, j))],
        out_specs=pl.BlockSpec((None, rows, tn), lambda l, j: (l, 0, j)),
        compiler_params=_params(("parallel", "parallel")), name="adaln_modulation")(
            c_all, w_mod, b_mod.reshape(depth, 1, n))


def _rmsnorm_kernel(x_ref, g_ref, o_ref):
    x = x_ref[...]
    o_ref[...] = x * lax.rsqrt(jnp.mean(x * x, axis=-1, keepdims=True) + EPS) * g_ref[...]


def _final_norm(x, g, tm=512):
    m, d = x.shape
    tm = min(tm, m)
    return pl.pallas_call(
        _rmsnorm_kernel, out_shape=jax.ShapeDtypeStruct((m, d), F32), grid=(m // tm,),
        in_specs=[pl.BlockSpec((tm, d), lambda i: (i, 0)), pl.BlockSpec((1, d), lambda i: (0, 0))],
        out_specs=pl.BlockSpec((tm, d), lambda i: (i, 0)),
        compiler_params=_params(("parallel",)), name="final_rmsnorm")(x, g.reshape(1, d))


def _s5_tables(lam_re, lam_im, log_dt, b_re, b_im, c_re, c_im, d_skip):
    f = F32
    groups, p = lam_re.shape
    nblk = groups * S5_GROUP // S5_BLK_CH
    gpb = groups // nblk
    lr, li = lam_re.astype(f), lam_im.astype(f)
    dt = jnp.exp(log_dt.astype(f))[:, None]
    ldr, ldi = lr * dt, li * dt
    kk = jnp.arange(1, S5_SEG + 1, dtype=f)[:, None, None]
    pmag = jnp.exp(kk * ldr)
    pw_re, pw_im = pmag * jnp.cos(kk * ldi), pmag * jnp.sin(kk * ldi)
    ab_re, ab_im = jnp.exp(ldr) * jnp.cos(ldi), jnp.exp(ldr) * jnp.sin(ldi)
    den = lr * lr + li * li
    nr, ni = ab_re - 1.0, ab_im
    fr, fi = (nr * lr + ni * li) / den, (ni * lr - nr * li) / den
    bre, bim = b_re.astype(f), b_im.astype(f)
    bb_re = fr[..., None] * bre - fi[..., None] * bim
    bb_im = fr[..., None] * bim + fi[..., None] * bre
    eye = jnp.eye(gpb, dtype=f)

    def bd_in(bb):
        t = bb.reshape(nblk, gpb, p, S5_GROUP).transpose(0, 1, 3, 2)
        return jnp.einsum('bgkp,gh->bgkhp', t, eye).reshape(nblk, gpb * S5_GROUP, gpb * p).astype(BF16)

    def bd_out(c):
        t = c.astype(f).reshape(nblk, gpb, S5_GROUP, p).transpose(0, 1, 3, 2)
        return jnp.einsum('bgpk,gh->bgphk', t, eye).reshape(nblk, gpb * p, gpb * S5_GROUP).astype(BF16)

    def lanes(t):
        lead = t.shape[:-2]
        t = t.reshape(lead + (nblk, gpb * p))
        return jnp.moveaxis(t, -2, 0)

    return dict(
        bb_re=bd_in(bb_re), bb_im=bd_in(bb_im), c_re=bd_out(c_re), c_im=bd_out(c_im),
        ab_re=lanes(ab_re[None]), ab_im=lanes(ab_im[None]),
        pw_re=lanes(pw_re), pw_im=lanes(pw_im),
        d=d_skip.astype(f).reshape(1, -1), nblk=nblk)


def _s5_perm():
    pm = np.zeros((S5_CHUNK, S5_CHUNK), np.float32)
    r = np.arange(S5_CHUNK)
    pm[r, (r % SUBLANES) * S5_SEG + r // SUBLANES] = 1.0
    return jnp.asarray(pm, BF16), jnp.asarray(pm.T, BF16)


def _s5_scan_kernel(u_ref, pm_ref, pmt_ref, bbre_ref, bbim_ref, cre_ref, cim_ref, abre_ref, abim_ref,
                    pwre_ref, pwim_ref, d_ref, y_ref, y16_ref, sre_out, sim_out,
                    xre, xim, car_re, car_im, cin_re, cin_im, lend_re, lend_im):
    n = pl.program_id(2)
    nst = xre.shape[1]

    @pl.when(n == 0)
    def _():
        car_re[...] = jnp.zeros_like(car_re)
        car_im[...] = jnp.zeros_like(car_im)

    u = u_ref[...]
    up = _dot(pm_ref[...], u.astype(BF16)).astype(BF16)
    xre[...] = _dot(up, bbre_ref[...])
    xim[...] = _dot(up, bbim_ref[...])
    are = jnp.broadcast_to(abre_ref[...], (SUBLANES, nst))
    aim = jnp.broadcast_to(abim_ref[...], (SUBLANES, nst))
    sre = jnp.zeros((SUBLANES, nst), F32)
    sim = jnp.zeros((SUBLANES, nst), F32)
    for i in range(S5_SEG):
        r = slice(SUBLANES * i, SUBLANES * (i + 1))
        nre = are * sre - aim * sim + xre[r, :]
        nim = are * sim + aim * sre + xim[r, :]
        xre[r, :] = nre
        xim[r, :] = nim
        sre, sim = nre, nim
    lend_re[...] = sre
    lend_im[...] = sim
    a_re = pwre_ref[S5_SEG - 1:S5_SEG, :]
    a_im = pwim_ref[S5_SEG - 1:S5_SEG, :]
    cr, ci = car_re[...], car_im[...]
    for s in range(SUBLANES):
        cin_re[s:s + 1, :] = cr
        cin_im[s:s + 1, :] = ci
        lr, li = lend_re[s:s + 1, :], lend_im[s:s + 1, :]
        cr, ci = a_re * cr - a_im * ci + lr, a_re * ci + a_im * cr + li
    car_re[...] = cr
    car_im[...] = ci
    cinr, cini = cin_re[...], cin_im[...]
    for i in range(S5_SEG):
        r = slice(SUBLANES * i, SUBLANES * (i + 1))
        pr, pi_ = pwre_ref[i:i + 1, :], pwim_ref[i:i + 1, :]
        xre[r, :] = xre[r, :] + (pr * cinr - pi_ * cini)
        xim[r, :] = xim[r, :] + (pr * cini + pi_ * cinr)
    yp = _dot(xre[...].astype(BF16), cre_ref[...]) - _dot(xim[...].astype(BF16), cim_ref[...])
    hi = yp.astype(BF16)
    lo = (yp - hi.astype(F32)).astype(BF16)
    gy = _gelu(_dot(pmt_ref[...], hi) + _dot(pmt_ref[...], lo) + d_ref[...] * u)
    y_ref[...] = gy
    y16_ref[...] = gy.astype(BF16)

    @pl.when(n == pl.num_programs(2) - 1)
    def _():
        sre_out[...] = cr
        sim_out[...] = ci


def _s5_scan(proj, tabs, batch, seq):
    nblk = tabs['nblk']
    nch = seq // S5_CHUNK
    pm, pmt = _s5_perm()
    nstate = nblk * S5_BLK_ST
    const3 = lambda shape: pl.BlockSpec((None,) + shape, lambda k, b, n: (k, 0, 0))
    y, y16, sre, sim = pl.pallas_call(
        _s5_scan_kernel,
        out_shape=(jax.ShapeDtypeStruct((batch * seq, nblk * S5_BLK_CH), F32),
                   jax.ShapeDtypeStruct((batch * seq, nblk * S5_BLK_CH), BF16),
                   jax.ShapeDtypeStruct((batch, 1, nstate), F32),
                   jax.ShapeDtypeStruct((batch, 1, nstate), F32)),
        grid=(nblk, batch, nch),
        in_specs=[pl.BlockSpec((S5_CHUNK, S5_BLK_CH), lambda k, b, n: (b * nch + n, k)),
                  pl.BlockSpec((S5_CHUNK, S5_CHUNK), lambda k, b, n: (0, 0)),
                  pl.BlockSpec((S5_CHUNK, S5_CHUNK), lambda k, b, n: (0, 0)),
                  const3((S5_BLK_CH, S5_BLK_ST)), const3((S5_BLK_CH, S5_BLK_ST)),
                  const3((S5_BLK_ST, S5_BLK_CH)), const3((S5_BLK_ST, S5_BLK_CH)),
                  const3((1, S5_BLK_ST)), const3((1, S5_BLK_ST)),
                  const3((S5_SEG, S5_BLK_ST)), const3((S5_SEG, S5_BLK_ST)),
                  pl.BlockSpec((1, S5_BLK_CH), lambda k, b, n: (0, k))],
        out_specs=(pl.BlockSpec((S5_CHUNK, S5_BLK_CH), lambda k, b, n: (b * nch + n, k)),
                   pl.BlockSpec((S5_CHUNK, S5_BLK_CH), lambda k, b, n: (b * nch + n, k)),
                   pl.BlockSpec((None, 1, S5_BLK_ST), lambda k, b, n: (b, 0, k)),
                   pl.BlockSpec((None, 1, S5_BLK_ST), lambda k, b, n: (b, 0, k))),
        scratch_shapes=[pltpu.VMEM((S5_CHUNK, S5_BLK_ST), F32), pltpu.VMEM((S5_CHUNK, S5_BLK_ST), F32),
                        pltpu.VMEM((1, S5_BLK_ST), F32), pltpu.VMEM((1, S5_BLK_ST), F32),
                        pltpu.VMEM((SUBLANES, S5_BLK_ST), F32), pltpu.VMEM((SUBLANES, S5_BLK_ST), F32),
                        pltpu.VMEM((SUBLANES, S5_BLK_ST), F32), pltpu.VMEM((SUBLANES, S5_BLK_ST), F32)],
        compiler_params=_params(("parallel", "parallel", "arbitrary")), name="s5_scan")(
            proj, pm, pmt, tabs['bb_re'], tabs['bb_im'], tabs['c_re'], tabs['c_im'],
            tabs['ab_re'], tabs['ab_im'], tabs['pw_re'], tabs['pw_im'], tabs['d'])
    return y, y16, sre, sim


def _s5_step_kernel(u_ref, hre_ref, him_ref, bbre_ref, bbim_ref, cre_ref, cim_ref, abre_ref, abim_ref, d_ref,
                    y_ref, sre_out, sim_out):
    u = u_ref[...]
    ub = u.astype(BF16)
    are, aim = abre_ref[...], abim_ref[...]
    hre, him = hre_ref[...], him_ref[...]
    sre = are * hre - aim * him + _dot(ub, bbre_ref[...])
    sim = are * him + aim * hre + _dot(ub, bbim_ref[...])
    sre_out[...] = sre
    sim_out[...] = sim
    y = _dot(sre.astype(BF16), cre_ref[...]) - _dot(sim.astype(BF16), cim_ref[...]) + d_ref[...] * u
    y_ref[...] = _gelu(y)


def _s5_step(proj, h_re, h_im, tabs):
    nblk = tabs['nblk']
    rows = proj.shape[0]
    nstate = nblk * S5_BLK_ST
    const3 = lambda shape: pl.BlockSpec((None,) + shape, lambda k: (k, 0, 0))
    lane_blk = lambda w: pl.BlockSpec((rows, w), lambda k: (0, k))
    return pl.pallas_call(
        _s5_step_kernel,
        out_shape=(jax.ShapeDtypeStruct((rows, nblk * S5_BLK_CH), F32),
                   jax.ShapeDtypeStruct((rows, nstate), F32), jax.ShapeDtypeStruct((rows, nstate), F32)),
        grid=(nblk,),
        in_specs=[lane_blk(S5_BLK_CH), lane_blk(S5_BLK_ST), lane_blk(S5_BLK_ST),
                  const3((S5_BLK_CH, S5_BLK_ST)), const3((S5_BLK_CH, S5_BLK_ST)),
                  const3((S5_BLK_ST, S5_BLK_CH)), const3((S5_BLK_ST, S5_BLK_CH)),
                  const3((1, S5_BLK_ST)), const3((1, S5_BLK_ST)),
                  pl.BlockSpec((1, S5_BLK_CH), lambda k: (0, k))],
        out_specs=(lane_blk(S5_BLK_CH), lane_blk(S5_BLK_ST), lane_blk(S5_BLK_ST)),
        compiler_params=_params(("parallel",)), name="s5_step")(
            proj, h_re.reshape(rows, nstate), h_im.reshape(rows, nstate),
            tabs['bb_re'], tabs['bb_im'], tabs['c_re'], tabs['c_im'], tabs['ab_re'], tabs['ab_im'], tabs['d'])


def _s5_weights(w_in, lam_re, lam_im, log_dt, b_re, b_im, c_re, c_im, d_skip, w_glu, b_glu, w_out):
    tabs = _s5_tables(lam_re, lam_im, log_dt, b_re, b_im, c_re, c_im, d_skip)
    return (_weight_bf16(w_in), _weight_bf16(w_glu), b_glu.astype(F32).reshape(1, -1), _weight_bf16(w_out), tabs)


def _glu_epilogue(acc, gy, z, b):
    return gy * _sigmoid(acc + b) * _silu(z)


def _cast_prologue(a):
    return a


def _s5_layer(x, mods, h_state, w, *, prompt, batch, seq, tm):
    g, scale, shift, gate = mods
    w_in, w_glu, b_glu, w_out, tabs = w
    width = w_glu.shape[0]
    proj = _in_proj(x, g, scale, shift, w_in, batch_kind=prompt, rows_per_batch=seq, tm=tm, tn=MM_TN, name="s5_in_proj")
    if prompt:
        gy, gy16, sre, sim = _s5_scan(proj, tabs, batch, seq)
        a_ins, prologue = [('row', gy16)], None
    else:
        gy, sre, sim = _s5_step(proj, h_state[0], h_state[1], tabs)
        a_ins, prologue = [('row', gy)], _cast_prologue
    a = _fused_matmul(a_ins, w_glu, [('tile', gy), ('tile', proj, width), ('col', b_glu)], prologue=prologue,
                      epilogue=_glu_epilogue, out_dtype=BF16, tm=tm, tn=MM_TN, name="s5_glu")
    x_new = _out_proj([('row', a)], w_out, x, gate, batch_kind=prompt, rows_per_batch=seq, tm=tm, tn=MM_TN, name="s5_out_proj")
    return x_new, sre, sim


def _conv_silu_chunk(x_ref, w_ref, b_ref, pad_ref, first):
    c = x_ref.shape[0]

    @pl.when(first)
    def _():
        pad_ref[0:SUBLANES, :] = jnp.zeros((SUBLANES, pad_ref.shape[1]), F32)

    pad_ref[SUBLANES:SUBLANES + c, :] = x_ref[...]
    acc = w_ref[3:4, :] * pad_ref[SUBLANES:SUBLANES + c, :]
    for j in range(CONV_W - 1):
        off = SUBLANES - (CONV_W - 1) + j
        acc = acc + w_ref[j:j + 1, :] * pad_ref[off:off + c, :]
    if b_ref is not None:
        acc = acc + b_ref[...]
    pad_ref[0:SUBLANES, :] = pad_ref[c:c + SUBLANES, :]
    return _silu(acc)


def _conv_silu_step(x, taps_ref, w_ref, b_ref):
    acc = w_ref[3:4, :] * x
    for j in range(CONV_W - 1):
        acc = acc + w_ref[j:j + 1, :] * taps_ref[j]
    if b_ref is not None:
        acc = acc + b_ref[...]
    return _silu(acc)


def _tri_masks(c):
    t = lax.broadcasted_iota(jnp.int32, (c, c), 0)
    s = lax.broadcasted_iota(jnp.int32, (c, c), 1)
    return s <= t, s < t


def _pad_to_square_t(x, n):
    rows = x.shape[0]
    return jnp.concatenate([x, jnp.zeros((n - rows, n), x.dtype)], axis=0).T


M2_HEADDIM = 64
M2_STATE = 128
M2_GROUPS = 8
M2_CHUNK = 128


def _m2_ssd_kernel(x_ref, b_ref, c_ref, dt_ref, z_ref, wx_ref, wb_ref, wc_ref, bx_ref, bb_ref, bc_ref,
                   dtb_ref, nega_ref, dsk_ref, nw_ref, o_ref, sout_ref, xpad, bpad, cpad, s_ref, y_ref):
    n = pl.program_id(1)
    first = n == 0
    c = x_ref.shape[0]
    npairs = s_ref.shape[0]
    pairs_per_group = npairs // M2_GROUPS

    @pl.when(first)
    def _():
        s_ref[...] = jnp.zeros_like(s_ref)

    xs = _conv_silu_chunk(x_ref, wx_ref, bx_ref, xpad, first)
    bm = _conv_silu_chunk(b_ref, wb_ref, bb_ref, bpad, first).astype(BF16)
    cm = _conv_silu_chunk(c_ref, wc_ref, bc_ref, cpad, first).astype(BF16)
    dtv = _softplus(dt_ref[...] + dtb_ref[...])
    la = nega_ref[...] * dtv
    incl, _ = _tri_masks(c)
    tri = jnp.where(incl, 1.0, 0.0).astype(BF16)
    cum = _dot_exact_lhs(tri, la)
    cum_t = cum.T
    ecum_all = jnp.exp(cum)
    wend_all = jnp.exp(cum[c - 1:c, :] - cum)
    elast_t = jnp.exp(cum_t[:, c - 1:c])
    lane_first = lax.broadcasted_iota(jnp.int32, (c, LANES), 1) < M2_HEADDIM
    row_first = lax.broadcasted_iota(jnp.int32, (LANES, LANES), 0) < M2_HEADDIM

    for g in range(M2_GROUPS):
        bg = bm[:, g * M2_STATE:(g + 1) * M2_STATE]
        cg = cm[:, g * M2_STATE:(g + 1) * M2_STATE]
        gm = _dot_nt(cg, bg)
        for j in range(pairs_per_group):
            p = g * pairs_per_group + j
            ha, hb = 2 * p, 2 * p + 1
            xp = xs[:, p * LANES:(p + 1) * LANES]

            def decay_weights(h):
                seg = cum[:, h:h + 1] - cum_t[h:h + 1, :]
                dec = jnp.where(incl, jnp.exp(jnp.where(incl, seg, 0.0)), 0.0)
                return (gm * dec).astype(BF16)

            xdt = xp * jnp.where(lane_first, dtv[:, ha:ha + 1], dtv[:, hb:hb + 1])
            xdt_a = jnp.where(lane_first, xdt, 0.0)
            xdt_b = xdt - xdt_a
            y = _dot(decay_weights(ha), xdt_a.astype(BF16)) + _dot(decay_weights(hb), xdt_b.astype(BF16))
            sp = s_ref[p]
            y = y + _dot_nt(cg, sp.astype(BF16)) * jnp.where(lane_first, ecum_all[:, ha:ha + 1], ecum_all[:, hb:hb + 1])
            y_ref[:, p * LANES:(p + 1) * LANES] = y + dsk_ref[:, p * LANES:(p + 1) * LANES] * xp
            xw = xdt * jnp.where(lane_first, wend_all[:, ha:ha + 1], wend_all[:, hb:hb + 1])
            dmat = jnp.where(row_first, elast_t[ha:ha + 1, :], elast_t[hb:hb + 1, :])
            s_ref[p] = sp * dmat + _dot(xw.T.astype(BF16), bg)

    o_ref[...] = _gated_rmsnorm_prologue(y_ref[...], z_ref[...], nw_ref[...]).astype(o_ref.dtype)

    @pl.when(n == pl.num_programs(1) - 1)
    def _():
        sout_ref[...] = s_ref[...]


def _m2_ssd(proj, gates, mw, batch, seq):
    c = M2_CHUNK
    nch = seq // c
    width = mw['width']
    gs = M2_GROUPS * M2_STATE
    npairs = width // LANES
    xo, bo, co = width // width, (2 * width) // gs, (2 * width + gs) // gs
    row = lambda w, off: pl.BlockSpec((c, w), lambda b, n: (b * nch + n, off))
    par = lambda r, w, off: pl.BlockSpec((r, w), lambda b, n: (0, off))
    return pl.pallas_call(
        _m2_ssd_kernel,
        out_shape=(jax.ShapeDtypeStruct((batch * seq, width), BF16),
                   jax.ShapeDtypeStruct((batch, npairs, LANES, M2_STATE), F32)),
        grid=(batch, nch),
        in_specs=[row(width, xo), row(gs, bo), row(gs, co), row(LANES, 0), row(width, 0),
                  par(CONV_W, width, 0), par(CONV_W, gs, width // gs), par(CONV_W, gs, width // gs + 1),
                  par(1, width, 0), par(1, gs, width // gs), par(1, gs, width // gs + 1),
                  par(1, LANES, 0), par(1, LANES, 0), par(1, width, 0), par(1, width, 0)],
        out_specs=(pl.BlockSpec((c, width), lambda b, n: (b * nch + n, 0)),
                   pl.BlockSpec((None, npairs, LANES, M2_STATE), lambda b, n: (b, 0, 0, 0))),
        scratch_shapes=[pltpu.VMEM((c + SUBLANES, width), F32), pltpu.VMEM((c + SUBLANES, gs), F32),
                        pltpu.VMEM((c + SUBLANES, gs), F32), pltpu.VMEM((npairs, LANES, M2_STATE), F32),
                        pltpu.VMEM((c, width), F32)],
        compiler_params=_params(("parallel", "arbitrary")), name="m2_ssd")(
            proj, proj, proj, gates, proj, mw['conv_w'], mw['conv_w'], mw['conv_w'], mw['conv_b'], mw['conv_b'],
            mw['conv_b'], mw['dt_bias'], mw['neg_a'], mw['d_ch'], mw['norm'])


def _m2_pre_step_kernel(xbc_ref, dt_ref, taps_ref, w_ref, b_ref, dtb_ref, nega_ref, act_ref, dtv_ref, dec_ref):
    act_ref[...] = _conv_silu_step(xbc_ref[...], taps_ref, w_ref, b_ref)
    dtv = _softplus(dt_ref[...] + dtb_ref[...])
    dtv_ref[...] = dtv
    dec_ref[...] = jnp.exp(nega_ref[...] * dtv)


def _m2_pre_step(proj, dt_raw, taps, mw):
    rows = proj.shape[0]
    width = mw['width']
    cdim = mw['conv_w'].shape[1]
    xbc = proj[:, width:width + cdim]
    return pl.pallas_call(
        _m2_pre_step_kernel,
        out_shape=(jax.ShapeDtypeStruct((rows, cdim), F32), jax.ShapeDtypeStruct((rows, LANES), F32),
                   jax.ShapeDtypeStruct((rows, LANES), F32)),
        name="m2_pre_step", compiler_params=pltpu.CompilerParams(vmem_limit_bytes=V7X_VMEM_LIMIT_BYTES))(
            xbc, dt_raw, taps, mw['conv_w'], mw['conv_b'], mw['dt_bias'], mw['neg_a'])


def _m2_state_step_kernel(x_ref, dtc_ref, decc_ref, b_ref, c_ref, dsk_ref, s_ref, y_ref, sout_ref):
    rows = x_ref.shape[0]
    pairs = x_ref.shape[1] // LANES
    bv = b_ref[...]
    cb = c_ref[...].astype(BF16)
    for j in range(pairs):
        sl = slice(j * LANES, (j + 1) * LANES)
        xp = x_ref[:, sl]
        xdt_t = _pad_to_square_t(xp * dtc_ref[:, sl], LANES)
        dec_t = _pad_to_square_t(decc_ref[:, sl], LANES)
        for b in range(rows):
            sp = s_ref[b, j]
            s_new = sp * dec_t[:, b:b + 1] + xdt_t[:, b:b + 1] * bv[b:b + 1, :]
            sout_ref[b, j] = s_new
            y_ref[b:b + 1, sl] = _dot_nt(cb[b:b + 1, :], s_new.astype(BF16)) + dsk_ref[:, sl] * xp[b:b + 1, :]


def _m2_state_step(act, dt_ch, dec_ch, ssm, mw):
    rows = act.shape[0]
    width = mw['width']
    gw = width // M2_GROUPS
    ppg = gw // LANES
    npairs = width // LANES
    gs = M2_GROUPS * M2_STATE
    s4 = ssm.reshape(rows, npairs, LANES, M2_STATE)
    blk = lambda w, base: pl.BlockSpec((rows, w), lambda g: (0, base + g))
    return pl.pallas_call(
        _m2_state_step_kernel,
        out_shape=(jax.ShapeDtypeStruct((rows, width), F32), jax.ShapeDtypeStruct(s4.shape, F32)),
        grid=(M2_GROUPS,),
        in_specs=[blk(gw, 0), blk(gw, 0), blk(gw, 0), blk(M2_STATE, width // M2_STATE),
                  blk(M2_STATE, (width + gs) // M2_STATE), pl.BlockSpec((1, gw), lambda g: (0, g)),
                  pl.BlockSpec((rows, ppg, LANES, M2_STATE), lambda g: (0, g, 0, 0))],
        out_specs=(blk(gw, 0), pl.BlockSpec((rows, ppg, LANES, M2_STATE), lambda g: (0, g, 0, 0))),
        compiler_params=_params(("parallel",)), name="m2_state_step")(
            act, dt_ch, dec_ch, act, act, mw['d_ch'], s4)


def _m2_weights(w_in, conv_w, conv_b, dt_bias, a_log, d_skip, norm_w, w_out):
    f = F32
    heads = dt_bias.shape[0]
    width = norm_w.shape[0]
    pad = LANES - heads
    cdim = conv_w.shape[1]
    return dict(
        w_in=_weight_bf16(w_in, width + cdim),
        w_gate=_pad_cols(w_in[:, width + cdim:].astype(f), LANES),
        conv_w=conv_w.astype(f), conv_b=conv_b.astype(f).reshape(1, -1),
        dt_bias=jnp.pad(dt_bias.astype(f), (0, pad)).reshape(1, LANES),
        neg_a=jnp.pad(-jnp.exp(a_log.astype(f)), (0, pad)).reshape(1, LANES),
        d_ch=jnp.repeat(d_skip.astype(f), width // heads).reshape(1, width),
        norm=norm_w.astype(f).reshape(1, width), w_out=_weight_bf16(w_out), width=width, heads=heads)


def _gated_rmsnorm_prologue(y, z, w):
    v = y * _silu(z)
    return v * lax.rsqrt(jnp.mean(v * v, axis=-1, keepdims=True) + EPS) * w


def _m2_layer(x, mods, state, mw, *, prompt, batch, seq, tm):
    g, scale, shift, gate = mods
    width, heads = mw['width'], mw['heads']
    cdim = mw['conv_w'].shape[1]
    proj = _in_proj(x, g, scale, shift, mw['w_in'], batch_kind=prompt, rows_per_batch=seq, tm=tm, tn=MM_TN,
                    name="m2_in_proj")
    gates = _gate_proj(x, g, scale, shift, mw['w_gate'], batch_kind=prompt, rows_per_batch=seq, tm=tm,
                       name="m2_gate_proj")
    if prompt:
        a, ssm = _m2_ssd(proj, gates, mw, batch, seq)
        conv_new = proj.reshape(batch, seq, -1)[:, seq - (CONV_W - 1):, width:width + cdim]
        x_new = _out_proj([('row', a)], mw['w_out'], x, gate, batch_kind=True, rows_per_batch=seq, tm=tm, tn=MM_TN,
                          name="m2_out_proj")
        return x_new, conv_new, ssm
    else:
        conv_buf, ssm_in = state
        taps = jnp.swapaxes(conv_buf.astype(F32), 0, 1)
        act, dtv, dec = _m2_pre_step(proj, gates, taps, mw)
        rep = width // heads
        dt_ch = jnp.repeat(dtv[:, :heads], rep, axis=1)
        dec_ch = jnp.repeat(dec[:, :heads], rep, axis=1)
        y, ssm = _m2_state_step(act, dt_ch, dec_ch, ssm_in.astype(F32), mw)
        conv_new = jnp.concatenate([conv_buf.astype(F32)[:, 1:], proj[:, None, width:width + cdim]], axis=1)
    x_new = _out_proj([('row', y), ('row', proj, width, 0), ('vec', mw['norm'])], mw['w_out'], x, gate,
                      batch_kind=prompt, rows_per_batch=seq, tm=min(tm, 256), tn=MM_TN, name="m2_out_proj",
                      prologue=_gated_rmsnorm_prologue)
    return x_new, conv_new, ssm


GD_DK = 128
GD_DV = 128
GD_CHUNK = 64


def _dot_3pass(a, b):
    ah = a.astype(BF16)
    al = (a - ah.astype(F32)).astype(BF16)
    bh = b.astype(BF16)
    bl = (b - bh.astype(F32)).astype(BF16)
    return _dot(ah, bh) + (_dot(ah, bl) + _dot(al, bh))


def _l2norm_rows(x):
    return x * lax.rsqrt(jnp.sum(x * x, axis=-1, keepdims=True) + EPS)


def _rmsnorm_rows(x, w):
    return x * lax.rsqrt(jnp.mean(x * x, axis=-1, keepdims=True) + EPS) * w


GD_INV_BASE = 16


def _bdot(a, b):
    return jnp.einsum('hmk,hkn->hmn', a, b, preferred_element_type=F32)


def _bdot_nt(a, b):
    return jnp.einsum('hmk,hnk->hmn', a, b, preferred_element_type=F32)


def _split2(x):
    hi = x.astype(BF16)
    return hi, (x - hi.astype(F32)).astype(BF16)


def _bdot_3pass(a, b):
    (ah, al), (bh, bl) = a, b
    return _bdot(ah, bh) + (_bdot(ah, bl) + _bdot(al, bh))


def _unit_lower_inverse(a_strict):
    c = a_strict.shape[-1]
    row = lax.broadcasted_iota(jnp.int32, (c, c), 0)
    col = lax.broadcasted_iota(jnp.int32, (c, c), 1)
    eye = jnp.where(row == col, 1.0, 0.0)
    blk = GD_INV_BASE
    shift = int(math.log2(blk))
    p = jnp.where((row >> shift) == (col >> shift), -a_strict, 0.0)
    t = eye + p
    ps = _split2(p)
    for _ in range(shift - 1):
        p = _bdot_3pass(ps, ps)
        ps = _split2(p)
        t = t + _bdot_3pass(_split2(t), ps)
    while blk < c:
        below = jnp.logical_and((row >> (shift + 1)) == (col >> (shift + 1)), (row >> shift) != (col >> shift))
        ts = _split2(t)
        tb = _bdot_3pass(ts, _split2(jnp.where(below, a_strict, 0.0)))
        t = t - _bdot_3pass(_split2(tb), ts)
        blk *= 2
        shift += 1
    return t


def _gd_chunk_kernel(qkv_ref, z_ref, braw_ref, araw_ref, cw_ref, nega_ref, dtb_ref, nw_ref,
                     o_ref, sout_ref, pad, s_ref):
    n = pl.program_id(1)
    first = n == 0
    c = qkv_ref.shape[0]
    hv = s_ref.shape[0]
    hk = hv // 2
    rep = hv // hk

    @pl.when(first)
    def _():
        s_ref[...] = jnp.zeros_like(s_ref)

    qkv = _conv_silu_chunk(qkv_ref, cw_ref, None, pad, first)
    beta = _sigmoid(braw_ref[...])
    gl = nega_ref[...] * _softplus(araw_ref[...] + dtb_ref[...])
    incl, strict = _tri_masks(c)
    tri = jnp.where(incl, 1.0, 0.0).astype(BF16)
    gcum = _dot_exact_lhs(tri, gl)
    gcum_t = jnp.concatenate([gcum, jnp.zeros((LANES - c, LANES), F32)], axis=0).T

    heads = range(hv)
    per_value_head = lambda t: jnp.stack([t[h // rep] for h in heads])
    q3 = jnp.stack([qkv[:, i * GD_DK:(i + 1) * GD_DK] for i in range(hk)])
    k3 = jnp.stack([qkv[:, (hk + i) * GD_DK:(hk + i + 1) * GD_DK] for i in range(hk)])
    v3 = jnp.stack([qkv[:, (2 * hk + h) * GD_DV:(2 * hk + h + 1) * GD_DV] for h in heads])
    q3 = _l2norm_rows(q3) * (GD_DK ** -0.5)
    k3 = _l2norm_rows(k3)
    k3b = k3.astype(BF16)
    kk = per_value_head(_bdot_nt(k3b, k3b))
    qk = per_value_head(_bdot_nt(q3.astype(BF16), k3b))
    q_v, k_v = per_value_head(q3), per_value_head(k3)
    colv = jnp.stack([gcum[:, h:h + 1] for h in heads])
    rowv = jnp.stack([gcum_t[h:h + 1, :c] for h in heads])
    bcol = jnp.stack([beta[:, h:h + 1] for h in heads])
    glast = colv[:, c - 1:c, :]
    ecol = jnp.exp(colv)
    dec = jnp.where(incl, jnp.exp(jnp.where(incl, colv - rowv, 0.0)), 0.0)
    a = jnp.where(strict, (bcol * kk) * dec, 0.0)
    tinv = _unit_lower_inverse(a)
    rhs = jnp.concatenate([v3 * bcol, (k_v * bcol) * ecol], axis=-1)
    th, tl = _split2(tinv)
    rb = rhs.astype(BF16)
    sol = _bdot(th, rb) + _bdot(tl, rb)
    u, w = sol[:, :, :GD_DV], sol[:, :, GD_DV:]
    s = s_ref[...]
    sb = s.astype(BF16)
    v_new = u - _bdot(w.astype(BF16), sb)
    o = _bdot((q_v * ecol).astype(BF16), sb) + _bdot((qk * dec).astype(BF16), v_new.astype(BF16))
    zpad = jnp.zeros((hv, LANES - c, GD_DV), F32)
    kd_t = jnp.swapaxes(jnp.concatenate([k_v * jnp.exp(glast - colv), zpad], axis=1), 1, 2)
    vn_pad = jnp.concatenate([v_new, zpad], axis=1)
    s_ref[...] = s * jnp.exp(glast) + _bdot(kd_t.astype(BF16), vn_pad.astype(BF16))
    on = _rmsnorm_rows(o, nw_ref[...])
    for h in heads:
        sl = slice(h * GD_DV, (h + 1) * GD_DV)
        o_ref[:, sl] = (on[h] * _silu(z_ref[:, sl])).astype(o_ref.dtype)

    @pl.when(n == pl.num_programs(1) - 1)
    def _():
        sout_ref[...] = s_ref[...]


def _gd_chunked(proj, gates, gw, batch, seq):
    c = GD_CHUNK
    nch = seq // c
    cdim, width, hv = gw['cdim'], gw['width'], gw['hv']
    row = lambda w, off: pl.BlockSpec((c, w), lambda b, n: (b * nch + n, off))
    par = lambda r, w: pl.BlockSpec((r, w), lambda b, n: (0, 0))
    return pl.pallas_call(
        _gd_chunk_kernel,
        out_shape=(jax.ShapeDtypeStruct((batch * seq, width), BF16),
                   jax.ShapeDtypeStruct((batch, hv, GD_DK, GD_DV), F32)),
        grid=(batch, nch),
        in_specs=[row(cdim, 0), row(width, cdim // width), row(LANES, 0), row(LANES, 1),
                  par(CONV_W, cdim), par(1, LANES), par(1, LANES), par(1, GD_DV)],
        out_specs=(pl.BlockSpec((c, width), lambda b, n: (b * nch + n, 0)),
                   pl.BlockSpec((None, hv, GD_DK, GD_DV), lambda b, n: (b, 0, 0, 0))),
        scratch_shapes=[pltpu.VMEM((c + SUBLANES, cdim), F32), pltpu.VMEM((hv, GD_DK, GD_DV), F32)],
        compiler_params=_params(("parallel", "arbitrary")), name="gd_chunked")(
            proj, proj, gates, gates, gw['conv_w'], gw['neg_a'], gw['dt_bias'], gw['norm'])


def _gd_pre_step_kernel(qkv_ref, braw_ref, araw_ref, taps_ref, cw_ref, nega_ref, dtb_ref,
                        q_ref, k_ref, v_ref, beta_ref, eg_ref):
    hk = q_ref.shape[1] // GD_DK
    act = _conv_silu_step(qkv_ref[...], taps_ref, cw_ref, None)
    for kh in range(hk):
        sl = slice(kh * GD_DK, (kh + 1) * GD_DK)
        q_ref[:, sl] = _l2norm_rows(act[:, kh * GD_DK:(kh + 1) * GD_DK]) * (GD_DK ** -0.5)
        k_ref[:, sl] = _l2norm_rows(act[:, (hk + kh) * GD_DK:(hk + kh + 1) * GD_DK])
    v_ref[...] = act[:, 2 * hk * GD_DK:]
    beta_ref[...] = _sigmoid(braw_ref[...])
    eg_ref[...] = jnp.exp(nega_ref[...] * _softplus(araw_ref[...] + dtb_ref[...]))


def _gd_pre_step(proj, gates, taps, gw):
    rows = proj.shape[0]
    cdim, width, hv = gw['cdim'], gw['width'], gw['hv']
    qk_w = (cdim - width) // 2
    sd = lambda w: jax.ShapeDtypeStruct((rows, w), F32)
    return pl.pallas_call(
        _gd_pre_step_kernel, out_shape=(sd(qk_w), sd(qk_w), sd(width), sd(LANES), sd(LANES)),
        name="gd_pre_step", compiler_params=pltpu.CompilerParams(vmem_limit_bytes=V7X_VMEM_LIMIT_BYTES))(
            proj[:, :cdim], gates[:, :LANES], gates[:, LANES:], taps,
            gw['conv_w'], gw['neg_a'], gw['dt_bias'])


def _gd_state_step_kernel(q_ref, k_ref, v_ref, beta_ref, eg_ref, z_ref, nw_ref, s_ref, o_ref, sout_ref):
    rows = q_ref.shape[0]
    nk = q_ref.shape[1] // GD_DK
    rep = (v_ref.shape[1] // GD_DV) // nk
    nw = nw_ref[...]
    zrows = jnp.zeros((SUBLANES - 2, GD_DK), F32)
    for kh in range(nk):
        ksl = slice(kh * GD_DK, (kh + 1) * GD_DK)
        q8, k8 = q_ref[:, ksl], k_ref[:, ksl]
        k_t = _pad_to_square_t(k8, GD_DK)
        for b in range(rows):
            qb, kb = q8[b:b + 1, :], k8[b:b + 1, :]
            kq = jnp.concatenate([kb, qb, zrows], axis=0).astype(BF16)
            qk = jnp.sum(qb * kb, axis=-1, keepdims=True)
            for r in range(rep):
                h = kh * rep + r
                vsl = slice(h * GD_DV, (h + 1) * GD_DV)
                s = s_ref[b, h]
                ks_qs = _dot(kq, s.astype(BF16))
                eg = eg_ref[b:b + 1, vsl]
                beta = beta_ref[b:b + 1, vsl]
                v_new = beta * (v_ref[b:b + 1, vsl] - eg * ks_qs[0:1, :])
                o = eg * ks_qs[1:2, :] + qk * v_new
                sout_ref[b, h] = s * eg[:, 0:1] + k_t[:, b:b + 1] * v_new
                o_ref[b:b + 1, vsl] = _rmsnorm_rows(o, nw) * _silu(z_ref[b:b + 1, vsl])


def _gd_state_step(proj, qn, kn, v, beta_ch, eg_ch, state, gw, heads_per_step=4):
    rows = qn.shape[0]
    cdim, width, hv = gw['cdim'], gw['width'], gw['hv']
    steps = hv // heads_per_step
    kw = qn.shape[1] // steps
    vw = width // steps
    blk = lambda w, base=0: pl.BlockSpec((rows, w), lambda g, base=base: (0, base + g))
    sspec = pl.BlockSpec((rows, heads_per_step, GD_DK, GD_DV), lambda g: (0, g, 0, 0))
    return pl.pallas_call(
        _gd_state_step_kernel,
        out_shape=(jax.ShapeDtypeStruct((rows, width), F32), jax.ShapeDtypeStruct(state.shape, F32)),
        grid=(steps,),
        in_specs=[blk(kw), blk(kw), blk(vw), blk(vw), blk(vw), blk(vw, cdim // vw),
                  pl.BlockSpec((1, GD_DV), lambda g: (0, 0)), sspec],
        out_specs=(blk(vw), sspec),
        compiler_params=_params(("parallel",)), name="gd_state_step")(
            qn, kn, v, beta_ch, eg_ch, proj, gw['norm'], state)


def _gd_weights(w_in, conv_w, a_log, dt_bias, norm_w, w_out):
    f = F32
    hv = a_log.shape[0]
    cdim = conv_w.shape[1]
    width = w_out.shape[0]
    pad = LANES - hv
    base = cdim + width
    zeros = jnp.zeros((w_in.shape[0], pad), w_in.dtype)
    w_gate = jnp.concatenate([w_in[:, base:base + hv], zeros, w_in[:, base + hv:], zeros], axis=1)
    return dict(
        w_in=_weight_bf16(w_in, base), w_gate=w_gate.astype(f), conv_w=conv_w.astype(f),
        neg_a=jnp.pad(-jnp.exp(a_log.astype(f)), (0, pad)).reshape(1, LANES),
        dt_bias=jnp.pad(dt_bias.astype(f), (0, pad)).reshape(1, LANES),
        norm=norm_w.astype(f).reshape(1, -1), w_out=_weight_bf16(w_out), cdim=cdim, width=width, hv=hv)


def _gd_layer(x, mods, state, gw, *, prompt, batch, seq, tm):
    g, scale, shift, gate = mods
    cdim, width, hv = gw['cdim'], gw['width'], gw['hv']
    proj = _in_proj(x, g, scale, shift, gw['w_in'], batch_kind=prompt, rows_per_batch=seq, tm=tm, tn=MM_TN,
                    name="gd_in_proj")
    gates = _gate_proj(x, g, scale, shift, gw['w_gate'], batch_kind=prompt, rows_per_batch=seq, tm=tm,
                       name="gd_gate_proj")
    if prompt:
        a, ssm = _gd_chunked(proj, gates, gw, batch, seq)
        conv_new = proj.reshape(batch, seq, -1)[:, seq - (CONV_W - 1):, :cdim]
        x_new = _out_proj([('row', a)], gw['w_out'], x, gate, batch_kind=True, rows_per_batch=seq, tm=tm, tn=MM_TN,
                          name="gd_out_proj")
    else:
        conv_buf, ssm_in = state
        taps = jnp.swapaxes(conv_buf.astype(F32), 0, 1)
        qn, kn, v, beta, eg = _gd_pre_step(proj, gates, taps, gw)
        beta_ch = jnp.repeat(beta[:, :hv], GD_DV, axis=1)
        eg_ch = jnp.repeat(eg[:, :hv], GD_DV, axis=1)
        a, ssm = _gd_state_step(proj, qn, kn, v, beta_ch, eg_ch, ssm_in.astype(F32), gw)
        conv_new = jnp.concatenate([conv_buf.astype(F32)[:, 1:], proj[:, None, :cdim]], axis=1)
        x_new = _out_proj([('row', a)], gw['w_out'], x, gate, batch_kind=False, rows_per_batch=seq, tm=tm, tn=MM_TN,
                          name="gd_out_proj", prologue=_cast_prologue)
    return x_new, conv_new, ssm


AT_DIM = 128
IDX_DIM = 128
TOPK_MAX = 256
REL_BUCKETS = 32
REL_MAX_DIST = 128
AT_TILE = 256
INT32_MIN = -2 ** 31
_NEG_BITS = int(np.float32(NEG).view(np.int32))
NEG_SORT_KEY = _NEG_BITS ^ 0x7FFFFFFF if _NEG_BITS < 0 else _NEG_BITS


def _bucket_starts():
    d = np.arange(0, REL_MAX_DIST + 1)
    exact = REL_BUCKETS // 2
    far = exact + (np.log(np.maximum(d, 1).astype(np.float32) / exact) / math.log(REL_MAX_DIST / exact)
                   * (REL_BUCKETS - exact)).astype(np.int32)
    bucket = np.where(d < exact, d, np.minimum(far, REL_BUCKETS - 1))
    assert np.all(np.diff(bucket) >= 0) and bucket[-1] == REL_BUCKETS - 1
    return [int(np.argmax(bucket >= b)) for b in range(REL_BUCKETS)]


def _bias_from_dist(dist, value_of_bucket):
    starts = _bucket_starts()
    val = value_of_bucket(REL_BUCKETS - 1)
    for b in range(REL_BUCKETS - 2, -1, -1):
        val = jnp.where(dist < starts[b + 1], value_of_bucket(b), val)
    return val


def _sort_key(x):
    x = jnp.where(x == 0.0, 0.0, x)
    b = pltpu.bitcast(x, jnp.int32)
    return jnp.where(b < 0, b ^ jnp.int32(0x7FFFFFFF), b)


def _kth_largest_key(count_ge, shape, k):
    def body(it, ans):
        cand = ans | jnp.left_shift(jnp.int32(1), 31 - it)
        cnt = count_ge(cand ^ jnp.int32(INT32_MIN))
        return jnp.where(cnt >= k, cand, ans)

    ans = lax.fori_loop(0, 32, body, jnp.zeros(shape, jnp.int32))
    return ans ^ jnp.int32(INT32_MIN)


def _relbias_tiles_kernel(rb_ref, o_ref):
    delta = pl.program_id(0) * AT_TILE
    h = pl.program_id(1)
    i = lax.broadcasted_iota(jnp.int32, (AT_TILE, AT_TILE), 0)
    j = lax.broadcasted_iota(jnp.int32, (AT_TILE, AT_TILE), 1)
    o_ref[...] = _bias_from_dist(delta + i - j, lambda b: rb_ref[b, h]) - rb_ref[REL_BUCKETS - 1, h]


def _relbias_tiles(rel_bias):
    heads = rel_bias.shape[1]
    ntile = 2
    assert ntile * AT_TILE - (AT_TILE - 1) >= REL_MAX_DIST
    return pl.pallas_call(
        _relbias_tiles_kernel, out_shape=jax.ShapeDtypeStruct((ntile, heads, AT_TILE, AT_TILE), F32),
        grid=(ntile, heads),
        in_specs=[pl.BlockSpec(memory_space=pltpu.SMEM)],
        out_specs=pl.BlockSpec((None, None, AT_TILE, AT_TILE), lambda d, h: (d, h, 0, 0)),
        compiler_params=_params(("parallel", "parallel")), name="at_relbias_tiles")(rel_bias.astype(F32))


def _at_index_kernel(qi_ref, wi_ref, ki_ref, o_ref, keys, cnt, *, k_sel, score_scale):
    qb = pl.program_id(1)
    tq = qi_ref.shape[0]
    nkb = keys.shape[0]
    tk = keys.shape[2]
    nh = qi_ref.shape[1] // IDX_DIM
    wsc = wi_ref[...] * score_scale
    qpos = qb * tq + lax.broadcasted_iota(jnp.int32, (tq, tk), 0)
    kloc = lax.broadcasted_iota(jnp.int32, (tq, tk), 1)
    neg_key = _sort_key(jnp.full((tq, tk), NEG, F32))

    for kb in range(nkb):
        @pl.when(kb <= qb)
        def _():
            kblk = ki_ref[kb * tk:(kb + 1) * tk, :].astype(BF16)
            sc = jnp.zeros((tq, tk), F32)
            for h in range(nh):
                d = _dot_nt(qi_ref[:, h * IDX_DIM:(h + 1) * IDX_DIM].astype(BF16), kblk)
                sc = sc + wsc[:, h:h + 1] * jnp.maximum(d, 0.0)
            adm = kb * tk + kloc <= qpos
            keys[kb] = _sort_key(jnp.where(adm, sc, NEG))

        @pl.when(kb > qb)
        def _():
            keys[kb] = neg_key

    def count_ge(t):
        cnt[...] = jnp.where(keys[0] >= t, 1, 0)
        for kb in range(1, nkb):
            @pl.when(kb <= qb)
            def _():
                cnt[...] += jnp.where(keys[kb] >= t, 1, 0)
        beyond = (nkb - 1 - qb) * tk
        return jnp.sum(cnt[...], axis=1, keepdims=True) + jnp.where(t <= NEG_SORT_KEY, beyond, 0)

    thr = _kth_largest_key(count_ge, (tq, 1), k_sel)
    n_ge = count_ge(thr)
    has_ties = jnp.max(n_ge) > k_sel

    @pl.when(jnp.logical_not(has_ties))
    def _():
        for kb in range(nkb):
            adm = kb * tk + kloc <= qpos
            sel = jnp.logical_and(keys[kb] >= thr, adm)
            o_ref[kb] = jnp.where(sel, 0.0, MASKED).T.astype(o_ref.dtype)

    @pl.when(has_ties)
    def _():
        acc = jnp.zeros((tq, tk), jnp.int32)
        for kb in range(nkb):
            acc = acc + jnp.where(keys[kb] > thr, 1, 0)
        room = (k_sel - jnp.sum(acc, axis=1, keepdims=True)).astype(F32)
        upper = jnp.where(lax.broadcasted_iota(jnp.int32, (tk, tk), 0) <= lax.broadcasted_iota(jnp.int32, (tk, tk), 1),
                          1.0, 0.0).astype(BF16)
        seen = jnp.zeros((tq, 1), F32)
        for kb in range(nkb):
            key = keys[kb]
            eq = key == thr
            eqf = jnp.where(eq, 1.0, 0.0)
            rank = seen + _dot(eqf.astype(BF16), upper)
            seen = seen + jnp.sum(eqf, axis=1, keepdims=True)
            adm = kb * tk + kloc <= qpos
            sel = jnp.logical_and(jnp.logical_or(key > thr, jnp.logical_and(eq, rank <= room)), adm)
            o_ref[kb] = jnp.where(sel, 0.0, MASKED).T.astype(o_ref.dtype)


def _at_index(proj, aw, batch, seq, k_sel):
    tq = tk = AT_TILE
    nq = seq // tq
    width = aw['width']
    nh = aw['idx_heads']
    qio = (4 * width) // (nh * IDX_DIM)
    kio = (4 * width + nh * IDX_DIM) // IDX_DIM
    kern = functools.partial(_at_index_kernel, k_sel=k_sel, score_scale=IDX_DIM ** -0.5 * nh ** -0.5)
    return pl.pallas_call(
        kern, out_shape=jax.ShapeDtypeStruct((batch * nq, seq // tk, tq, tk), BF16), grid=(batch, nq),
        in_specs=[pl.BlockSpec((tq, nh * IDX_DIM), lambda b, q: (b * nq + q, qio)),
                  pl.BlockSpec((tq, LANES), lambda b, q: (b * nq + q, kio + 1)),
                  pl.BlockSpec((seq, IDX_DIM), lambda b, q: (b, kio))],
        out_specs=pl.BlockSpec((None, seq // tk, tq, tk), lambda b, q: (b * nq + q, 0, 0, 0)),
        scratch_shapes=[pltpu.VMEM((seq // tk, tq, tk), jnp.int32), pltpu.VMEM((tq, tk), jnp.int32)],
        compiler_params=_params(("parallel", "parallel")), name="at_index")(proj, proj, proj)


AT_HEAD_GROUP = 8
MASKED = 2.0 * NEG


def _at_attend_kernel(q_ref, k_ref, vt_ref, z_ref, mask_ref, bias_ref, o_ref, acc, m_scr, l_scr):
    qb = pl.program_id(2)
    t = q_ref.shape[0]
    hg = q_ref.shape[1] // AT_DIM
    acc[...] = jnp.zeros_like(acc)
    m_scr[...] = jnp.full(m_scr.shape, NEG, F32)
    l_scr[...] = jnp.zeros_like(l_scr)

    heads = [slice(h * AT_DIM, (h + 1) * AT_DIM) for h in range(hg)]
    q3t = jnp.stack([q_ref[:, sl].T for sl in heads]).astype(BF16)

    def key_tile(kb, bias):
        rows = pl.ds(pl.multiple_of(kb * t, t), t)
        k3 = jnp.stack([k_ref[rows, sl] for sl in heads])
        s_t = _bdot(k3, q3t) + mask_ref[kb].astype(F32)
        if bias is not None:
            s_t = s_t + bias
        m_old = m_scr[...]
        m_new = jnp.maximum(m_old, jnp.max(s_t, axis=1, keepdims=True))
        alpha = jnp.exp(m_old - m_new)
        p_t = jnp.exp(s_t - m_new)
        l_scr[...] = alpha * l_scr[...] + jnp.sum(p_t, axis=1, keepdims=True)
        acc[...] = alpha * acc[...] + _bdot(vt_ref[:, kb], p_t.astype(BF16))
        m_scr[...] = m_new

    def far_tile(kb, carry):
        key_tile(kb, None)
        return carry

    lax.fori_loop(0, jnp.maximum(qb - 1, 0), far_tile, 0)

    @pl.when(qb >= 1)
    def _():
        key_tile(qb - 1, bias_ref[1])

    key_tile(qb, bias_ref[0])
    o_t = acc[...] / l_scr[...]
    for h, sl in enumerate(heads):
        o_ref[:, sl] = (o_t[h].T * _silu(z_ref[:, sl])).astype(o_ref.dtype)


def _at_attend(proj, proj_bf, maskadd_t, tiles_t, aw, batch, seq):
    t = AT_TILE
    nq = seq // t
    width = aw['width']
    heads = width // AT_DIM
    hg = AT_HEAD_GROUP
    gw = hg * AT_DIM
    ng = width // gw
    v_t = proj_bf[:, 2 * width:3 * width].reshape(batch, nq, t, heads, AT_DIM).transpose(0, 3, 1, 4, 2)
    return pl.pallas_call(
        _at_attend_kernel, out_shape=jax.ShapeDtypeStruct((batch * seq, width), BF16), grid=(batch, ng, nq),
        in_specs=[pl.BlockSpec((t, gw), lambda b, g, q: (b * nq + q, g)),
                  pl.BlockSpec((seq, gw), lambda b, g, q: (b, ng + g)),
                  pl.BlockSpec((None, hg, nq, AT_DIM, t), lambda b, g, q: (b, g, 0, 0, 0)),
                  pl.BlockSpec((t, gw), lambda b, g, q: (b * nq + q, 3 * ng + g)),
                  pl.BlockSpec((None, nq, t, t), lambda b, g, q: (b * nq + q, 0, 0, 0)),
                  pl.BlockSpec((2, hg, t, t), lambda b, g, q: (0, g, 0, 0))],
        out_specs=pl.BlockSpec((t, gw), lambda b, g, q: (b * nq + q, g)),
        scratch_shapes=[pltpu.VMEM((hg, AT_DIM, t), F32), pltpu.VMEM((hg, 1, t), F32), pltpu.VMEM((hg, 1, t), F32)],
        compiler_params=_params(("parallel", "parallel", "arbitrary")), name="at_attend")(
            proj, proj_bf, v_t, proj, maskadd_t, tiles_t)


def _at_weights(w_in, rel_bias, w_out):
    width = w_out.shape[0]
    heads = rel_bias.shape[1]
    idx_heads = (w_in.shape[1] - 4 * width - IDX_DIM) // (IDX_DIM + 1)
    col_scale = jnp.where(jnp.arange(w_in.shape[1]) < width, AT_DIM ** -0.5, 1.0).astype(F32)
    return dict(w_in=_weight_bf16(w_in, col_scale=col_scale), rel_bias=rel_bias.astype(F32),
                w_out=_weight_bf16(w_out), width=width, heads=heads, idx_heads=idx_heads)


def _at_rows_kernel(k_ref, v_ref, ki_ref, ko_ref, vo_ref, kio_ref):
    nh = ko_ref.shape[1]
    heads = lambda x: jnp.stack([x[:, h * AT_DIM:(h + 1) * AT_DIM] for h in range(nh)], axis=1)
    ko_ref[...] = heads(k_ref[...])
    vo_ref[...] = heads(v_ref[...])
    kio_ref[...] = ki_ref[...]


def _at_rows(proj, aw, lead, tm=256):
    width, heads = aw['width'], aw['heads']
    m = proj.shape[0]
    tm = min(tm, m)
    kio = (4 * width + aw['idx_heads'] * IDX_DIM) // IDX_DIM
    row4 = pl.BlockSpec((tm, heads, AT_DIM), lambda i: (i, 0, 0))
    k, v, ki = pl.pallas_call(
        _at_rows_kernel,
        out_shape=(jax.ShapeDtypeStruct((m, heads, AT_DIM), F32), jax.ShapeDtypeStruct((m, heads, AT_DIM), F32),
                   jax.ShapeDtypeStruct((m, IDX_DIM), F32)),
        grid=(m // tm,),
        in_specs=[pl.BlockSpec((tm, width), lambda i: (i, 1)), pl.BlockSpec((tm, width), lambda i: (i, 2)),
                  pl.BlockSpec((tm, IDX_DIM), lambda i: (i, kio))],
        out_specs=(row4, row4, pl.BlockSpec((tm, IDX_DIM), lambda i: (i, 0))),
        compiler_params=_params(("parallel",)), name="at_rows")(proj, proj, proj)
    return (k.reshape(lead + (heads, AT_DIM)), v.reshape(lead + (heads, AT_DIM)), ki.reshape(lead + (IDX_DIM,)))


def _at_layer_prompt(x, mods, aw, final_g, *, batch, seq, tm):
    g, scale, shift, gate = mods
    proj, proj_bf = _in_proj(x, g, scale, shift, aw['w_in'], batch_kind=True, rows_per_batch=seq, tm=tm, tn=MM_TN,
                             name="at_in_proj", bf16_copy=True)
    k_sel = min(TOPK_MAX, seq // 4)
    maskadd = _at_index(proj, aw, batch, seq, k_sel)
    tiles = _relbias_tiles(aw['rel_bias'])
    a = _at_attend(proj, proj_bf, maskadd, jnp.swapaxes(tiles, 2, 3), aw, batch, seq)
    d = x.shape[1]
    y = _fused_matmul([('row', a)], aw['w_out'], [('tile', x), ('batchcol', gate), ('col', final_g.reshape(1, d))],
                      prologue=None, epilogue=_residual_norm_epilogue, out_dtype=F32, tm=min(tm, MM_TM_WIDE_ROWS),
                      tn=d, rows_per_batch=seq, name="at_out_proj_norm")
    return (y,) + _at_rows(proj, aw, (batch, seq))


AT_PAGES_PER_STEP = 16


def _at_page_scores_kernel(pt_ref, qi_ref, w_ref, kidx_ref, o_ref, kbuf, sem):
    b, j = pl.program_id(0), pl.program_id(1)
    nj = pl.num_programs(1)
    npg = kbuf.shape[1]
    step = b * nj + j
    last_step = pl.num_programs(0) * nj - 1

    def page_copy(s, i, slot):
        sb, sj = s // nj, s % nj
        return pltpu.make_async_copy(kidx_ref.at[pt_ref[sb, sj * npg + i]], kbuf.at[slot, i], sem.at[slot])

    def start_all(s, slot):
        for i in range(npg):
            page_copy(s, i, slot).start()

    slot = step % 2

    @pl.when(step == 0)
    def _():
        start_all(step, slot)

    @pl.when(step < last_step)
    def _():
        start_all(step + 1, 1 - slot)

    for i in range(npg):
        page_copy(step, i, slot).wait()
    qi = qi_ref[...].astype(BF16)
    w = w_ref[...]
    for i in range(npg):
        d = _dot_nt(qi, kbuf[slot, i].astype(BF16))
        o_ref[i:i + 1, :] = jnp.sum(w * jnp.maximum(d, 0.0), axis=0, keepdims=True)


def _at_page_scores(qi3, w3, cache_kidx, page_table):
    rows, nh, _ = qi3.shape
    npages = page_table.shape[1]
    page = cache_kidx.shape[1]
    npg = math.gcd(npages, AT_PAGES_PER_STEP)
    grid_spec = pltpu.PrefetchScalarGridSpec(
        num_scalar_prefetch=1, grid=(rows, npages // npg),
        in_specs=[pl.BlockSpec((None, nh, IDX_DIM), lambda b, j, pt: (b, 0, 0)),
                  pl.BlockSpec((None, nh, IDX_DIM), lambda b, j, pt: (b, 0, 0)),
                  pl.BlockSpec(memory_space=pl.ANY)],
        out_specs=pl.BlockSpec((None, npg, page), lambda b, j, pt: (b, j, 0)),
        scratch_shapes=[pltpu.VMEM((2, npg, page, IDX_DIM), F32), pltpu.SemaphoreType.DMA((2,))])
    return pl.pallas_call(
        _at_page_scores_kernel, out_shape=jax.ShapeDtypeStruct((rows, npages, page), F32), grid_spec=grid_spec,
        compiler_params=_params(("arbitrary", "arbitrary")), name="at_page_scores")(page_table, qi3, w3, cache_kidx)


def _at_sample_select_kernel(sc_ref, qi_ref, w_ref, kin_ref, gidx_ref, newadd_ref, rank_scr, *, k_sel):
    npages, page = sc_ref.shape
    upper = jnp.where(lax.broadcasted_iota(jnp.int32, (page, page), 0) <= lax.broadcasted_iota(jnp.int32, (page, page), 1),
                      1.0, 0.0).astype(BF16)
    lower = jnp.where(lax.broadcasted_iota(jnp.int32, (npages, npages), 1) < lax.broadcasted_iota(jnp.int32, (npages, npages), 0),
                      1.0, 0.0).astype(BF16)

    def total(x):
        return jnp.sum(jnp.sum(x, axis=1, keepdims=True), axis=0, keepdims=True)

    def position_rank(flags):
        row_cnt = jnp.broadcast_to(jnp.sum(flags, axis=1, keepdims=True), (npages, page))
        return _dot(lower, row_cnt.astype(BF16)) + _dot(flags.astype(BF16), upper)

    keys = _sort_key(sc_ref[...])
    dots = jnp.sum(qi_ref[...] * kin_ref[...], axis=1, keepdims=True)
    s_new = jnp.sum(w_ref[:, 0:1] * jnp.maximum(dots, 0.0), axis=0, keepdims=True)
    key_new = _sort_key(s_new)

    def count_ge(t):
        return total(jnp.where(keys >= t, 1, 0)) + jnp.where(key_new >= t, 1, 0)

    thr = _kth_largest_key(count_ge, (1, 1), k_sel)
    n_gt = total(jnp.where(keys > thr, 1.0, 0.0)) + jnp.where(key_new > thr, 1.0, 0.0)
    room = k_sel - n_gt
    eq = keys == thr
    eqf = jnp.where(eq, 1.0, 0.0)
    sel = jnp.logical_or(keys > thr, jnp.logical_and(eq, position_rank(eqf) <= room))
    sel_new = jnp.logical_or(key_new > thr, jnp.logical_and(key_new == thr, total(eqf) + 1.0 <= room))
    newadd_ref[...] = jnp.broadcast_to(jnp.where(sel_new, 0.0, NEG), (1, page))

    self_f = jnp.where(sel, 1.0, 0.0)
    rank_scr[...] = jnp.where(sel, position_rank(self_f) - 1.0, -1.0)
    jidx = lax.broadcasted_iota(jnp.int32, (k_sel, page), 0).astype(F32)
    lane = lax.broadcasted_iota(jnp.int32, (page, LANES), 1)
    pick = jnp.where(lane == 0, lax.broadcasted_iota(jnp.int32, (page, LANES), 0).astype(F32),
                     jnp.where(lane <= 2, 1.0, 0.0)).astype(BF16)
    out_lane = lax.broadcasted_iota(jnp.int32, (k_sel, LANES), 1)

    def add_pages(g, acc):
        ranks = rank_scr[pl.ds(pl.multiple_of(g * SUBLANES, SUBLANES), SUBLANES), :]
        for i in range(SUBLANES):
            onehot = jnp.where(ranks[i:i + 1, :] == jidx, 1.0, 0.0).astype(BF16)
            slot = lax.convert_element_type(g * SUBLANES + i, F32)
            acc = acc + _dot(onehot, pick) * jnp.where(out_lane == 1, slot, 1.0)
        return acc

    assert npages % SUBLANES == 0
    gidx_ref[...] = lax.fori_loop(0, npages // SUBLANES, add_pages, jnp.zeros((k_sel, LANES), F32))


def _at_sample_select(scores, qi3, w3, ki_new, k_sel):
    rows, npages, page = scores.shape
    nh = qi3.shape[1]
    assert page == LANES and page >= REL_MAX_DIST
    kern = functools.partial(_at_sample_select_kernel, k_sel=k_sel)
    return pl.pallas_call(
        kern, out_shape=(jax.ShapeDtypeStruct((rows, k_sel, LANES), F32), jax.ShapeDtypeStruct((rows, 1, page), F32)),
        grid=(rows,),
        in_specs=[pl.BlockSpec((None, npages, page), lambda b: (b, 0, 0)),
                  pl.BlockSpec((None, nh, IDX_DIM), lambda b: (b, 0, 0)),
                  pl.BlockSpec((None, nh, IDX_DIM), lambda b: (b, 0, 0)),
                  pl.BlockSpec((None, 1, IDX_DIM), lambda b: (b, 0, 0))],
        out_specs=(pl.BlockSpec((None, k_sel, LANES), lambda b: (b, 0, 0)),
                   pl.BlockSpec((None, 1, page), lambda b: (b, 0, 0))),
        scratch_shapes=[pltpu.VMEM((npages, page), F32)],
        compiler_params=_params(("parallel",)), name="at_sample_select")(
            scores, qi3, w3, ki_new.reshape(rows, 1, IDX_DIM))


def _at_gather_attend_kernel(pt_ref, slot_ref, off_ref, q_ref, ck_ref, cv_ref, pos_ref, newadd_ref, rbt_ref,
                             kn_ref, vn_ref, z_ref, o_ref, kg, vg, sem, *, past):
    b = pl.program_id(0)
    nsel, nh, d = kg.shape
    ncol = nsel * nh

    def row_copies(j):
        page = pt_ref[b, slot_ref[b, j]]
        off = off_ref[b, j]
        return (pltpu.make_async_copy(ck_ref.at[page, off], kg.at[j], sem.at[0]),
                pltpu.make_async_copy(cv_ref.at[page, off], vg.at[j], sem.at[1]))

    def start(j, carry):
        for c in row_copies(j):
            c.start()
        return carry

    def wait(j, carry):
        for c in row_copies(j):
            c.wait()
        return carry

    lax.fori_loop(0, nsel, start, 0)
    qs = q_ref[...].astype(BF16)
    pos = pos_ref[...]
    own = (lax.broadcasted_iota(jnp.int32, (nh, ncol), 1) & (nh - 1)) == lax.broadcasted_iota(jnp.int32, (nh, ncol), 0)
    keep = jnp.logical_and(own, pos >= 0)
    bias = _bias_from_dist(jnp.maximum(past - pos, 0), lambda bk: rbt_ref[:, bk:bk + 1])
    nadd = newadd_ref[:, 0:1]
    s_n = jnp.sum(qs.astype(F32) * kn_ref[...], axis=1, keepdims=True) + rbt_ref[:, 0:1] + nadd
    lax.fori_loop(0, nsel, wait, 0)
    s = jnp.where(keep, _dot_nt(qs, kg[...].reshape(ncol, d).astype(BF16)) + bias, NEG)
    m = jnp.maximum(jnp.max(s, axis=1, keepdims=True), s_n)
    pr = jnp.where(keep, jnp.exp(s - m), 0.0)
    p_n = jnp.where(nadd < 0.0, 0.0, jnp.exp(s_n - m))
    l = jnp.sum(pr, axis=1, keepdims=True) + p_n
    o = _dot(pr.astype(BF16), vg[...].reshape(ncol, d).astype(BF16)) + p_n * vn_ref[...]
    o_ref[...] = o / l * _silu(z_ref[...])


def _at_gather_attend(q3, cache_k, cache_v, page_table, slot, off, pos_cols, newadd, rbt, kn3, vn3, z3):
    rows, nh, d = q3.shape
    assert nh & (nh - 1) == 0
    nsel = slot.shape[1]
    ncol = nsel * nh
    past = page_table.shape[1] * cache_k.shape[1]
    row3 = pl.BlockSpec((None, nh, d), lambda b, *_: (b, 0, 0))
    grid_spec = pltpu.PrefetchScalarGridSpec(
        num_scalar_prefetch=3, grid=(rows,),
        in_specs=[row3, pl.BlockSpec(memory_space=pl.ANY), pl.BlockSpec(memory_space=pl.ANY),
                  pl.BlockSpec((None, 1, ncol), lambda b, *_: (b, 0, 0)),
                  pl.BlockSpec((None, 1, newadd.shape[-1]), lambda b, *_: (b, 0, 0)),
                  pl.BlockSpec(rbt.shape, lambda b, *_: (0, 0)),
                  row3, row3, row3],
        out_specs=row3,
        scratch_shapes=[pltpu.VMEM((nsel, nh, d), F32), pltpu.VMEM((nsel, nh, d), F32), pltpu.SemaphoreType.DMA((2,))])
    kern = functools.partial(_at_gather_attend_kernel, past=past)
    return pl.pallas_call(
        kern, out_shape=jax.ShapeDtypeStruct((rows, nh, d), F32), grid_spec=grid_spec,
        compiler_params=_params(("arbitrary",)), name="at_gather_attend")(
            page_table, slot, off, q3, cache_k, cache_v, pos_cols, newadd, rbt, kn3, vn3, z3)


def _at_layer_sample(x, mods, cache_k, cache_v, cache_kidx, page_table, aw):
    g, scale, shift, gate = mods
    rows = x.shape[0]
    width, heads, nh = aw['width'], aw['heads'], aw['idx_heads']
    proj = _in_proj(x, g, scale, shift, aw['w_in'], batch_kind=False, rows_per_batch=1, tm=SUBLANES, tn=MM_TN,
                    name="at_in_proj")
    npool, page = cache_k.shape[:2]
    npages = page_table.shape[1]
    past = npages * page
    k_sel = min(TOPK_MAX, (past + 1) // 4)
    o = 4 * width
    qi3 = proj[:, o:o + nh * IDX_DIM].reshape(rows, nh, IDX_DIM)
    ki_new = proj[:, o + nh * IDX_DIM:o + nh * IDX_DIM + IDX_DIM]
    wi = proj[:, o + nh * IDX_DIM + IDX_DIM:o + nh * IDX_DIM + IDX_DIM + nh] * (IDX_DIM ** -0.5 * nh ** -0.5)
    w3 = jnp.broadcast_to(wi[:, :, None], (rows, nh, IDX_DIM))
    scores = _at_page_scores(qi3, w3, cache_kidx.astype(F32), page_table)
    rbt = aw['rel_bias'].T
    gidx, newadd = _at_sample_select(scores, qi3, w3, ki_new, k_sel)
    off = gidx[:, :, 0].astype(jnp.int32)
    slot = gidx[:, :, 1].astype(jnp.int32)
    pos = jnp.where(gidx[:, :, 2] > 0.5, slot * page + off, -1)
    pos_cols = jnp.repeat(pos, heads, axis=-1).reshape(rows, 1, k_sel * heads)
    r3 = lambda t: t.reshape(rows, heads, AT_DIM)
    a = _at_gather_attend(r3(proj[:, :width]), cache_k.astype(F32), cache_v.astype(F32), page_table, slot, off,
                          pos_cols, newadd, rbt, r3(proj[:, width:2 * width]), r3(proj[:, 2 * width:3 * width]),
                          r3(proj[:, 3 * width:4 * width]))
    x_new = _out_proj([('row', a.reshape(rows, width))], aw['w_out'], x, gate, batch_kind=False, rows_per_batch=1,
                      tm=SUBLANES, tn=MM_TN, name="at_out_proj", prologue=_cast_prologue)
    return (x_new,) + _at_rows(proj, aw, (rows, 1))


def kernel(x_prompt, x_sample, state_s5_re, state_s5_im, state_m2_conv, state_m2_ssm, state_gd_conv, state_gd_ssm, cache_k, cache_v, cache_kidx, page_table, c_prompt, c_sample, norm_g, w_mod, b_mod, final_g, s5_w_in, s5_lam_re, s5_lam_im, s5_log_dt, s5_b_re, s5_b_im, s5_c_re, s5_c_im, s5_d, s5_w_glu, s5_b_glu, s5_w_out, m2_w_in, m2_conv_w, m2_conv_b, m2_dt_bias, m2_a_log, m2_d, m2_norm, m2_w_out, gd_w_in, gd_conv_w, gd_a_log, gd_dt_bias, gd_norm, gd_w_out, at_w_in, rel_bias, at_w_out):
    f = F32
    bp, seq, d = x_prompt.shape
    bs = x_sample.shape[0]
    depth = norm_g.shape[0]
    xp = x_prompt.astype(f).reshape(bp * seq, d)
    xs = x_sample.astype(f).reshape(bs, d)

    pad_rows = (-(bs + bp)) % SUBLANES
    c_all = jnp.concatenate([c_sample.astype(f), c_prompt.astype(f), jnp.zeros((pad_rows, d), f)], axis=0)
    mod = _modulation(c_all, w_mod, b_mod)

    def mods(i, prompt):
        g = norm_g[i].astype(f).reshape(1, d)
        rows = mod[i, bs:bs + bp] if prompt else mod[i, :bs]
        shift, scale, gate = rows[:, :d], rows[:, d:2 * d], rows[:, 2 * d:]
        if prompt:
            return g, scale[:, None, :], shift[:, None, :], gate[:, None, :]
        return g, scale, shift, gate

    s5w = _s5_weights(s5_w_in, s5_lam_re, s5_lam_im, s5_log_dt, s5_b_re, s5_b_im, s5_c_re, s5_c_im, s5_d,
                      s5_w_glu, s5_b_glu, s5_w_out)
    tm_p = MM_TM

    xp, s5_re_p, s5_im_p = _s5_layer(xp, mods(0, True), None, s5w, prompt=True, batch=bp, seq=seq, tm=tm_p)
    xs, s5_re_s, s5_im_s = _s5_layer(xs, mods(0, False), (state_s5_re, state_s5_im), s5w, prompt=False,
                                     batch=bs, seq=1, tm=SUBLANES)
    groups, nstate = state_s5_re.shape[1:]
    s5_re_p, s5_im_p = s5_re_p.reshape(bp, groups, nstate), s5_im_p.reshape(bp, groups, nstate)
    s5_re_s, s5_im_s = s5_re_s.reshape(bs, groups, nstate), s5_im_s.reshape(bs, groups, nstate)

    m2w = _m2_weights(m2_w_in, m2_conv_w, m2_conv_b, m2_dt_bias, m2_a_log, m2_d, m2_norm, m2_w_out)
    xp, m2_conv_p, m2_ssm_p = _m2_layer(xp, mods(1, True), None, m2w, prompt=True, batch=bp, seq=seq, tm=tm_p)
    xs, m2_conv_s, m2_ssm_s = _m2_layer(xs, mods(1, False), (state_m2_conv, state_m2_ssm), m2w, prompt=False,
                                        batch=bs, seq=1, tm=SUBLANES)
    m2_ssm_p = m2_ssm_p.reshape((bp,) + state_m2_ssm.shape[1:])
    m2_ssm_s = m2_ssm_s.reshape(state_m2_ssm.shape)

    gdw = _gd_weights(gd_w_in, gd_conv_w, gd_a_log, gd_dt_bias, gd_norm, gd_w_out)
    xp, gd_conv_p, gd_ssm_p = _gd_layer(xp, mods(2, True), None, gdw, prompt=True, batch=bp, seq=seq, tm=tm_p)
    xs, gd_conv_s, gd_ssm_s = _gd_layer(xs, mods(2, False), (state_gd_conv, state_gd_ssm), gdw, prompt=False,
                                        batch=bs, seq=1, tm=SUBLANES)

    atw = _at_weights(at_w_in, rel_bias, at_w_out)
    yp, k_rows_p, v_rows_p, kidx_rows_p = _at_layer_prompt(xp, mods(3, True), atw, final_g.astype(f), batch=bp,
                                                           seq=seq, tm=tm_p)
    xs, k_rows_s, v_rows_s, kidx_rows_s = _at_layer_sample(xs, mods(3, False), cache_k, cache_v, cache_kidx,
                                                           page_table, atw)

    y_prompt = yp.reshape(x_prompt.shape).astype(x_prompt.dtype)
    y_sample = _final_norm(xs, final_g).reshape(x_sample.shape).astype(x_sample.dtype)
    return (y_prompt, y_sample, s5_re_p, s5_im_p, s5_re_s, s5_im_s, m2_conv_p, m2_ssm_p, m2_conv_s, m2_ssm_s,
            gd_conv_p, gd_ssm_p, gd_conv_s, gd_ssm_s,
            k_rows_p, v_rows_p, kidx_rows_p, k_rows_s, v_rows_s, kidx_rows_s)
```

```python
import functools
import math

import numpy as np
import jax
import jax.numpy as jnp
from jax import lax
from jax.experimental import pallas as pl
from jax.experimental.pallas import tpu as pltpu

F32 = jnp.float32
BF16 = jnp.bfloat16

EPS = 1e-6
NEG = -1e30
CONV_W = 4
V7X_VMEM_LIMIT_BYTES = 56 * 1024 * 1024
LANES = 128
SUBLANES = 8
MM_TM = 1024
MM_TM_WIDE_ROWS = 512
MM_TN = 1024

S5_GROUP = 16
S5_STATE = 64
S5_CHUNK = 512
S5_SEG = S5_CHUNK // SUBLANES
S5_BLK_CH = 256
S5_BLK_ST = 1024


def _params(sem):
    return pltpu.CompilerParams(dimension_semantics=sem, vmem_limit_bytes=V7X_VMEM_LIMIT_BYTES)


def _sigmoid(x):
    return 1.0 / (1.0 + jnp.exp(-x))


def _silu(x):
    return x * _sigmoid(x)


def _gelu(x):
    return 0.5 * x * (1.0 + jnp.tanh(math.sqrt(2.0 / math.pi) * (x + 0.044715 * (x * x * x))))


def _softplus(x):
    return jnp.maximum(x, 0.0) + jnp.log1p(jnp.exp(-jnp.abs(x)))


def _dot(a, b):
    return jnp.dot(a, b, preferred_element_type=F32)


def _dot_nt(a, b):
    return lax.dot_general(a, b, (((1,), (1,)), ((), ())), preferred_element_type=F32)


def _split3(x):
    hi = x.astype(BF16)
    r1 = x - hi.astype(F32)
    mid = r1.astype(BF16)
    lo = (r1 - mid.astype(F32)).astype(BF16)
    return hi, mid, lo


def _dot_exact_lhs(sel, x):
    hi, mid, lo = _split3(x)
    return _dot(sel, hi) + (_dot(sel, mid) + _dot(sel, lo))


def _mm_kernel(*refs, n_a, n_e, prologue, epilogue, bf16_copy):
    a_refs = refs[:n_a]
    w_ref = refs[n_a]
    e_refs = refs[n_a + 1:n_a + 1 + n_e]
    o_ref = refs[n_a + 1 + n_e]
    n_out = 2 if bf16_copy else 1
    if prologue is None:
        a = a_refs[0][...]
    else:
        a_scr = refs[n_a + 1 + n_e + n_out]

        @pl.when(pl.program_id(1) == 0)
        def _():
            a_scr[...] = prologue(*[r[...] for r in a_refs]).astype(BF16)

        a = a_scr[...]
    acc = _dot(a, w_ref[...])
    out = epilogue(acc, *[r[...] for r in e_refs])
    o_ref[...] = out.astype(o_ref.dtype)
    if bf16_copy:
        refs[n_a + 2 + n_e][...] = out.astype(BF16)


def _fused_matmul(a_ins, w, e_ins, *, prologue, epilogue, out_dtype, tm, tn, rows_per_batch=None, name,
                  bf16_copy=False):
    m = next(item[1].shape[0] for item in a_ins if item[0] == 'row')
    k, n = w.shape
    tm = min(tm, m)
    tn = next(t for t in (2048, 1024, 768, 512, 384, 256, 128) if t <= tn and n % t == 0)
    assert m % tm == 0
    rpb = rows_per_batch

    def bidx(i):
        return (i * tm) // rpb

    in_specs, args = [], []
    for item in a_ins:
        kind, arr = item[0], item[1]
        wd = item[2] if len(item) > 2 else arr.shape[-1]
        coff = item[3] if len(item) > 3 else 0
        if kind == 'row':
            in_specs.append(pl.BlockSpec((tm, wd), lambda i, j, coff=coff: (i, coff)))
        elif kind == 'vec':
            in_specs.append(pl.BlockSpec((1, wd), lambda i, j: (0, 0)))
        else:
            in_specs.append(pl.BlockSpec((None, 1, wd), lambda i, j: (bidx(i), 0, 0)))
        args.append(arr)
    in_specs.append(pl.BlockSpec((k, tn), lambda i, j: (0, j)))
    args.append(w)
    for item in e_ins:
        kind, arr = item[0], item[1]
        off = (item[2] if len(item) > 2 else 0) // tn
        if kind == 'tile':
            assert len(item) < 3 or item[2] % tn == 0
            in_specs.append(pl.BlockSpec((tm, tn), lambda i, j, off=off: (i, j + off)))
        elif kind == 'col':
            in_specs.append(pl.BlockSpec((1, tn), lambda i, j: (0, j)))
        else:
            in_specs.append(pl.BlockSpec((None, 1, tn), lambda i, j: (bidx(i), 0, j)))
        args.append(arr)
    scratch = [] if prologue is None else [pltpu.VMEM((tm, k), BF16)]
    kern = functools.partial(_mm_kernel, n_a=len(a_ins), n_e=len(e_ins), prologue=prologue, epilogue=epilogue,
                             bf16_copy=bf16_copy)
    out_shape = jax.ShapeDtypeStruct((m, n), out_dtype)
    out_spec = pl.BlockSpec((tm, tn), lambda i, j: (i, j))
    if bf16_copy:
        out_shape, out_spec = (out_shape, jax.ShapeDtypeStruct((m, n), BF16)), (out_spec, out_spec)
    return pl.pallas_call(
        kern, out_shape=out_shape, grid=(m // tm, n // tn), in_specs=in_specs, out_specs=out_spec,
        scratch_shapes=scratch, compiler_params=_params(("parallel", "arbitrary")), name=name)(*args)


def _weight_cast_kernel(*refs, n_valid, scaled, transposed):
    w_ref, o_ref = refs[0], refs[-1]
    tn = o_ref.shape[1]
    col = pl.program_id(0) * tn + lax.broadcasted_iota(jnp.int32, (1, tn), 1)
    w = w_ref[...].astype(F32)
    if transposed:
        w = w.T
    if scaled:
        w = w * refs[1][...]
    o_ref[...] = jnp.where(col < n_valid, w, 0.0).astype(o_ref.dtype)


def _weight_bf16(w, n_cols=None, pad_to=512, col_scale=None, tn=512):
    k, n_in = w.shape
    n_cols = n_in if n_cols is None else n_cols
    n_out = -(-n_cols // pad_to) * pad_to
    tn = math.gcd(n_out, tn)
    transposed = n_in % LANES != 0
    if transposed:
        args, in_specs = [w.T], [pl.BlockSpec((tn, k), lambda j: (j, 0))]
    else:
        args, in_specs = [w], [pl.BlockSpec((k, tn), lambda j: (0, j))]
    if col_scale is not None:
        args.append(jnp.pad(col_scale.astype(F32), (0, n_out - col_scale.shape[0])).reshape(1, n_out))
        in_specs.append(pl.BlockSpec((1, tn), lambda j: (0, j)))
    kern = functools.partial(_weight_cast_kernel, n_valid=n_cols, scaled=col_scale is not None, transposed=transposed)
    return pl.pallas_call(
        kern, out_shape=jax.ShapeDtypeStruct((k, n_out), BF16), grid=(n_out // tn,), in_specs=in_specs,
        out_specs=pl.BlockSpec((k, tn), lambda j: (0, j)),
        compiler_params=_params(("parallel",)), name="weight_cast")(*args)


def _pad_cols(w, mult):
    n = w.shape[-1]
    npad = (-n) % mult
    if npad:
        w = jnp.pad(w, ((0, 0), (0, npad)))
    return w


def _modnorm_prologue(x, g, scale, shift):
    r = x * lax.rsqrt(jnp.mean(x * x, axis=-1, keepdims=True) + EPS) * g
    return r * (1.0 + scale) + shift


def _identity_epilogue(acc):
    return acc


def _residual_epilogue(acc, x, gate):
    return x + gate * acc


def _residual_norm_epilogue(acc, x, gate, g):
    xn = x + gate * acc
    return xn * lax.rsqrt(jnp.mean(xn * xn, axis=-1, keepdims=True) + EPS) * g


def _in_proj(x, g, scale, shift, w, *, batch_kind, rows_per_batch, tm, tn, name, bf16_copy=False):
    kind = 'batch' if batch_kind else 'row'
    return _fused_matmul([('row', x), ('vec', g), (kind, scale), (kind, shift)], w, [],
                         prologue=_modnorm_prologue, epilogue=_identity_epilogue, out_dtype=F32,
                         tm=tm, tn=tn, rows_per_batch=rows_per_batch, name=name, bf16_copy=bf16_copy)


def _gate_proj_kernel(x_ref, g_ref, scale_ref, shift_ref, w_ref, o_ref):
    h = _modnorm_prologue(x_ref[...], g_ref[...], scale_ref[...], shift_ref[...])
    o_ref[...] = _dot_3pass(h, w_ref[...])


def _gate_proj(x, g, scale, shift, w, *, batch_kind, rows_per_batch, tm, name):
    m, d = x.shape
    n = w.shape[1]
    tm = min(tm, m, MM_TM_WIDE_ROWS)
    if batch_kind:
        mod_spec = pl.BlockSpec((None, 1, d), lambda i: ((i * tm) // rows_per_batch, 0, 0))
    else:
        mod_spec = pl.BlockSpec((tm, d), lambda i: (i, 0))
    return pl.pallas_call(
        _gate_proj_kernel, out_shape=jax.ShapeDtypeStruct((m, n), F32), grid=(m // tm,),
        in_specs=[pl.BlockSpec((tm, d), lambda i: (i, 0)), pl.BlockSpec((1, d), lambda i: (0, 0)),
                  mod_spec, mod_spec, pl.BlockSpec((d, n), lambda i: (0, 0))],
        out_specs=pl.BlockSpec((tm, n), lambda i: (i, 0)),
        compiler_params=_params(("parallel",)), name=name)(x, g, scale, shift, w)


def _out_proj(a_ins, w, x, gate, *, batch_kind, rows_per_batch, tm, tn, name, prologue=None):
    kind = 'batchcol' if batch_kind else 'tile'
    return _fused_matmul(a_ins, w, [('tile', x), (kind, gate)], prologue=prologue, epilogue=_residual_epilogue,
                         out_dtype=F32, tm=tm, tn=tn, rows_per_batch=rows_per_batch, name=name)


def _mod_kernel(c_ref, w_ref, b_ref, o_ref):
    o_ref[...] = _dot(c_ref[...].astype(BF16), w_ref[...].astype(BF16)) + b_ref[...]


def _modulation(c_all, w_mod, b_mod, tn=512):
    depth, d, n = w_mod.shape
    rows = c_all.shape[0]
    return pl.pallas_call(
        _mod_kernel, out_shape=jax.ShapeDtypeStruct((depth, rows, n), F32), grid=(depth, n // tn),
        in_specs=[pl.BlockSpec((rows, d), lambda l, j: (0, 0)),
                  pl.BlockSpec((None, d, tn), lambda l, j: (l, 0, j)),
                  pl.BlockSpec((None, 1, tn), lambda l, j: (l, 0, j))],
        out_specs=pl.BlockSpec((None, rows, tn), lambda l, j: (l, 0, j)),
        compiler_params=_params(("parallel", "parallel")), name="adaln_modulation")(
            c_all, w_mod, b_mod.reshape(depth, 1, n))


def _rmsnorm_kernel(x_ref, g_ref, o_ref):
    x = x_ref[...]
    o_ref[...] = x * lax.rsqrt(jnp.mean(x * x, axis=-1, keepdims=True) + EPS) * g_ref[...]


def _final_norm(x, g, tm=512):
    m, d = x.shape
    tm = min(tm, m)
    return pl.pallas_call(
        _rmsnorm_kernel, out_shape=jax.ShapeDtypeStruct((m, d), F32), grid=(m // tm,),
        in_specs=[pl.BlockSpec((tm, d), lambda i: (i, 0)), pl.BlockSpec((1, d), lambda i: (0, 0))],
        out_specs=pl.BlockSpec((tm, d), lambda i: (i, 0)),
        compiler_params=_params(("parallel",)), name="final_rmsnorm")(x, g.reshape(1, d))


def _s5_tables(lam_re, lam_im, log_dt, b_re, b_im, c_re, c_im, d_skip):
    f = F32
    groups, p = lam_re.shape
    nblk = groups * S5_GROUP // S5_BLK_CH
    gpb = groups // nblk
    lr, li = lam_re.astype(f), lam_im.astype(f)
    dt = jnp.exp(log_dt.astype(f))[:, None]
    ldr, ldi = lr * dt, li * dt
    kk = jnp.arange(1, S5_SEG + 1, dtype=f)[:, None, None]
    pmag = jnp.exp(kk * ldr)
    pw_re, pw_im = pmag * jnp.cos(kk * ldi), pmag * jnp.sin(kk * ldi)
    ab_re, ab_im = jnp.exp(ldr) * jnp.cos(ldi), jnp.exp(ldr) * jnp.sin(ldi)
    den = lr * lr + li * li
    nr, ni = ab_re - 1.0, ab_im
    fr, fi = (nr * lr + ni * li) / den, (ni * lr - nr * li) / den
    bre, bim = b_re.astype(f), b_im.astype(f)
    bb_re = fr[..., None] * bre - fi[..., None] * bim
    bb_im = fr[..., None] * bim + fi[..., None] * bre
    eye = jnp.eye(gpb, dtype=f)

    def bd_in(bb):
        t = bb.reshape(nblk, gpb, p, S5_GROUP).transpose(0, 1, 3, 2)
        return jnp.einsum('bgkp,gh->bgkhp', t, eye).reshape(nblk, gpb * S5_GROUP, gpb * p).astype(BF16)

    def bd_out(c):
        t = c.astype(f).reshape(nblk, gpb, S5_GROUP, p).transpose(0, 1, 3, 2)
        return jnp.einsum('bgpk,gh->bgphk', t, eye).reshape(nblk, gpb * p, gpb * S5_GROUP).astype(BF16)

    def lanes(t):
        lead = t.shape[:-2]
        t = t.reshape(lead + (nblk, gpb * p))
        return jnp.moveaxis(t, -2, 0)

    return dict(
        bb_re=bd_in(bb_re), bb_im=bd_in(bb_im), c_re=bd_out(c_re), c_im=bd_out(c_im),
        ab_re=lanes(ab_re[None]), ab_im=lanes(ab_im[None]),
        pw_re=lanes(pw_re), pw_im=lanes(pw_im),
        d=d_skip.astype(f).reshape(1, -1), nblk=nblk)


def _s5_perm():
    pm = np.zeros((S5_CHUNK, S5_CHUNK), np.float32)
    r = np.arange(S5_CHUNK)
    pm[r, (r % SUBLANES) * S5_SEG + r // SUBLANES] = 1.0
    return jnp.asarray(pm, BF16), jnp.asarray(pm.T, BF16)


def _s5_scan_kernel(u_ref, pm_ref, pmt_ref, bbre_ref, bbim_ref, cre_ref, cim_ref, abre_ref, abim_ref,
                    pwre_ref, pwim_ref, d_ref, y_ref, y16_ref, sre_out, sim_out,
                    xre, xim, car_re, car_im, cin_re, cin_im, lend_re, lend_im):
    n = pl.program_id(2)
    nst = xre.shape[1]

    @pl.when(n == 0)
    def _():
        car_re[...] = jnp.zeros_like(car_re)
        car_im[...] = jnp.zeros_like(car_im)

    u = u_ref[...]
    up = _dot(pm_ref[...], u.astype(BF16)).astype(BF16)
    xre[...] = _dot(up, bbre_ref[...])
    xim[...] = _dot(up, bbim_ref[...])
    are = jnp.broadcast_to(abre_ref[...], (SUBLANES, nst))
    aim = jnp.broadcast_to(abim_ref[...], (SUBLANES, nst))
    sre = jnp.zeros((SUBLANES, nst), F32)
    sim = jnp.zeros((SUBLANES, nst), F32)
    for i in range(S5_SEG):
        r = slice(SUBLANES * i, SUBLANES * (i + 1))
        nre = are * sre - aim * sim + xre[r, :]
        nim = are * sim + aim * sre + xim[r, :]
        xre[r, :] = nre
        xim[r, :] = nim
        sre, sim = nre, nim
    lend_re[...] = sre
    lend_im[...] = sim
    a_re = pwre_ref[S5_SEG - 1:S5_SEG, :]
    a_im = pwim_ref[S5_SEG - 1:S5_SEG, :]
    cr, ci = car_re[...], car_im[...]
    for s in range(SUBLANES):
        cin_re[s:s + 1, :] = cr
        cin_im[s:s + 1, :] = ci
        lr, li = lend_re[s:s + 1, :], lend_im[s:s + 1, :]
        cr, ci = a_re * cr - a_im * ci + lr, a_re * ci + a_im * cr + li
    car_re[...] = cr
    car_im[...] = ci
    cinr, cini = cin_re[...], cin_im[...]
    for i in range(S5_SEG):
        r = slice(SUBLANES * i, SUBLANES * (i + 1))
        pr, pi_ = pwre_ref[i:i + 1, :], pwim_ref[i:i + 1, :]
        xre[r, :] = xre[r, :] + (pr * cinr - pi_ * cini)
        xim[r, :] = xim[r, :] + (pr * cini + pi_ * cinr)
    yp = _dot(xre[...].astype(BF16), cre_ref[...]) - _dot(xim[...].astype(BF16), cim_ref[...])
    hi = yp.astype(BF16)
    lo = (yp - hi.astype(F32)).astype(BF16)
    gy = _gelu(_dot(pmt_ref[...], hi) + _dot(pmt_ref[...], lo) + d_ref[...] * u)
    y_ref[...] = gy
    y16_ref[...] = gy.astype(BF16)

    @pl.when(n == pl.num_programs(2) - 1)
    def _():
        sre_out[...] = cr
        sim_out[...] = ci


def _s5_scan(proj, tabs, batch, seq):
    nblk = tabs['nblk']
    nch = seq // S5_CHUNK
    pm, pmt = _s5_perm()
    nstate = nblk * S5_BLK_ST
    const3 = lambda shape: pl.BlockSpec((None,) + shape, lambda k, b, n: (k, 0, 0))
    y, y16, sre, sim = pl.pallas_call(
        _s5_scan_kernel,
        out_shape=(jax.ShapeDtypeStruct((batch * seq, nblk * S5_BLK_CH), F32),
                   jax.ShapeDtypeStruct((batch * seq, nblk * S5_BLK_CH), BF16),
                   jax.ShapeDtypeStruct((batch, 1, nstate), F32),
                   jax.ShapeDtypeStruct((batch, 1, nstate), F32)),
        grid=(nblk, batch, nch),
        in_specs=[pl.BlockSpec((S5_CHUNK, S5_BLK_CH), lambda k, b, n: (b * nch + n, k)),
                  pl.BlockSpec((S5_CHUNK, S5_CHUNK), lambda k, b, n: (0, 0)),
                  pl.BlockSpec((S5_CHUNK, S5_CHUNK), lambda k, b, n: (0, 0)),
                  const3((S5_BLK_CH, S5_BLK_ST)), const3((S5_BLK_CH, S5_BLK_ST)),
                  const3((S5_BLK_ST, S5_BLK_CH)), const3((S5_BLK_ST, S5_BLK_CH)),
                  const3((1, S5_BLK_ST)), const3((1, S5_BLK_ST)),
                  const3((S5_SEG, S5_BLK_ST)), const3((S5_SEG, S5_BLK_ST)),
                  pl.BlockSpec((1, S5_BLK_CH), lambda k, b, n: (0, k))],
        out_specs=(pl.BlockSpec((S5_CHUNK, S5_BLK_CH), lambda k, b, n: (b * nch + n, k)),
                   pl.BlockSpec((S5_CHUNK, S5_BLK_CH), lambda k, b, n: (b * nch + n, k)),
                   pl.BlockSpec((None, 1, S5_BLK_ST), lambda k, b, n: (b, 0, k)),
                   pl.BlockSpec((None, 1, S5_BLK_ST), lambda k, b, n: (b, 0, k))),
        scratch_shapes=[pltpu.VMEM((S5_CHUNK, S5_BLK_ST), F32), pltpu.VMEM((S5_CHUNK, S5_BLK_ST), F32),
                        pltpu.VMEM((1, S5_BLK_ST), F32), pltpu.VMEM((1, S5_BLK_ST), F32),
                        pltpu.VMEM((SUBLANES, S5_BLK_ST), F32), pltpu.VMEM((SUBLANES, S5_BLK_ST), F32),
                        pltpu.VMEM((SUBLANES, S5_BLK_ST), F32), pltpu.VMEM((SUBLANES, S5_BLK_ST), F32)],
        compiler_params=_params(("parallel", "parallel", "arbitrary")), name="s5_scan")(
            proj, pm, pmt, tabs['bb_re'], tabs['bb_im'], tabs['c_re'], tabs['c_im'],
            tabs['ab_re'], tabs['ab_im'], tabs['pw_re'], tabs['pw_im'], tabs['d'])
    return y, y16, sre, sim


def _s5_step_kernel(u_ref, hre_ref, him_ref, bbre_ref, bbim_ref, cre_ref, cim_ref, abre_ref, abim_ref, d_ref,
                    y_ref, sre_out, sim_out):
    u = u_ref[...]
    ub = u.astype(BF16)
    are, aim = abre_ref[...], abim_ref[...]
    hre, him = hre_ref[...], him_ref[...]
    sre = are * hre - aim * him + _dot(ub, bbre_ref[...])
    sim = are * him + aim * hre + _dot(ub, bbim_ref[...])
    sre_out[...] = sre
    sim_out[...] = sim
    y = _dot(sre.astype(BF16), cre_ref[...]) - _dot(sim.astype(BF16), cim_ref[...]) + d_ref[...] * u
    y_ref[...] = _gelu(y)


def _s5_step(proj, h_re, h_im, tabs):
    nblk = tabs['nblk']
    rows = proj.shape[0]
    nstate = nblk * S5_BLK_ST
    const3 = lambda shape: pl.BlockSpec((None,) + shape, lambda k: (k, 0, 0))
    lane_blk = lambda w: pl.BlockSpec((rows, w), lambda k: (0, k))
    return pl.pallas_call(
        _s5_step_kernel,
        out_shape=(jax.ShapeDtypeStruct((rows, nblk * S5_BLK_CH), F32),
                   jax.ShapeDtypeStruct((rows, nstate), F32), jax.ShapeDtypeStruct((rows, nstate), F32)),
        grid=(nblk,),
        in_specs=[lane_blk(S5_BLK_CH), lane_blk(S5_BLK_ST), lane_blk(S5_BLK_ST),
                  const3((S5_BLK_CH, S5_BLK_ST)), const3((S5_BLK_CH, S5_BLK_ST)),
                  const3((S5_BLK_ST, S5_BLK_CH)), const3((S5_BLK_ST, S5_BLK_CH)),
                  const3((1, S5_BLK_ST)), const3((1, S5_BLK_ST)),
                  pl.BlockSpec((1, S5_BLK_CH), lambda k: (0, k))],
        out_specs=(lane_blk(S5_BLK_CH), lane_blk(S5_BLK_ST), lane_blk(S5_BLK_ST)),
        compiler_params=_params(("parallel",)), name="s5_step")(
            proj, h_re.reshape(rows, nstate), h_im.reshape(rows, nstate),
            tabs['bb_re'], tabs['bb_im'], tabs['c_re'], tabs['c_im'], tabs['ab_re'], tabs['ab_im'], tabs['d'])


def _s5_weights(w_in, lam_re, lam_im, log_dt, b_re, b_im, c_re, c_im, d_skip, w_glu, b_glu, w_out):
    tabs = _s5_tables(lam_re, lam_im, log_dt, b_re, b_im, c_re, c_im, d_skip)
    return (_weight_bf16(w_in), _weight_bf16(w_glu), b_glu.astype(F32).reshape(1, -1), _weight_bf16(w_out), tabs)


def _glu_epilogue(acc, gy, z, b):
    return gy * _sigmoid(acc + b) * _silu(z)


def _cast_prologue(a):
    return a


def _s5_layer(x, mods, h_state, w, *, prompt, batch, seq, tm):
    g, scale, shift, gate = mods
    w_in, w_glu, b_glu, w_out, tabs = w
    width = w_glu.shape[0]
    proj = _in_proj(x, g, scale, shift, w_in, batch_kind=prompt, rows_per_batch=seq, tm=tm, tn=MM_TN, name="s5_in_proj")
    if prompt:
        gy, gy16, sre, sim = _s5_scan(proj, tabs, batch, seq)
        a_ins, prologue = [('row', gy16)], None
    else:
        gy, sre, sim = _s5_step(proj, h_state[0], h_state[1], tabs)
        a_ins, prologue = [('row', gy)], _cast_prologue
    a = _fused_matmul(a_ins, w_glu, [('tile', gy), ('tile', proj, width), ('col', b_glu)], prologue=prologue,
                      epilogue=_glu_epilogue, out_dtype=BF16, tm=tm, tn=MM_TN, name="s5_glu")
    x_new = _out_proj([('row', a)], w_out, x, gate, batch_kind=prompt, rows_per_batch=seq, tm=tm, tn=MM_TN, name="s5_out_proj")
    return x_new, sre, sim


def _conv_silu_chunk(x_ref, w_ref, b_ref, pad_ref, first):
    c = x_ref.shape[0]

    @pl.when(first)
    def _():
        pad_ref[0:SUBLANES, :] = jnp.zeros((SUBLANES, pad_ref.shape[1]), F32)

    pad_ref[SUBLANES:SUBLANES + c, :] = x_ref[...]
    acc = w_ref[3:4, :] * pad_ref[SUBLANES:SUBLANES + c, :]
    for j in range(CONV_W - 1):
        off = SUBLANES - (CONV_W - 1) + j
        acc = acc + w_ref[j:j + 1, :] * pad_ref[off:off + c, :]
    if b_ref is not None:
        acc = acc + b_ref[...]
    pad_ref[0:SUBLANES, :] = pad_ref[c:c + SUBLANES, :]
    return _silu(acc)


def _conv_silu_step(x, taps_ref, w_ref, b_ref):
    acc = w_ref[3:4, :] * x
    for j in range(CONV_W - 1):
        acc = acc + w_ref[j:j + 1, :] * taps_ref[j]
    if b_ref is not None:
        acc = acc + b_ref[...]
    return _silu(acc)


def _tri_masks(c):
    t = lax.broadcasted_iota(jnp.int32, (c, c), 0)
    s = lax.broadcasted_iota(jnp.int32, (c, c), 1)
    return s <= t, s < t


def _pad_to_square_t(x, n):
    rows = x.shape[0]
    return jnp.concatenate([x, jnp.zeros((n - rows, n), x.dtype)], axis=0).T


M2_HEADDIM = 64
M2_STATE = 128
M2_GROUPS = 8
M2_CHUNK = 128


def _m2_ssd_kernel(x_ref, b_ref, c_ref, dt_ref, z_ref, wx_ref, wb_ref, wc_ref, bx_ref, bb_ref, bc_ref,
                   dtb_ref, nega_ref, dsk_ref, nw_ref, o_ref, sout_ref, xpad, bpad, cpad, s_ref, y_ref):
    n = pl.program_id(1)
    first = n == 0
    c = x_ref.shape[0]
    npairs = s_ref.shape[0]
    pairs_per_group = npairs // M2_GROUPS

    @pl.when(first)
    def _():
        s_ref[...] = jnp.zeros_like(s_ref)

    xs = _conv_silu_chunk(x_ref, wx_ref, bx_ref, xpad, first)
    bm = _conv_silu_chunk(b_ref, wb_ref, bb_ref, bpad, first).astype(BF16)
    cm = _conv_silu_chunk(c_ref, wc_ref, bc_ref, cpad, first).astype(BF16)
    dtv = _softplus(dt_ref[...] + dtb_ref[...])
    la = nega_ref[...] * dtv
    incl, _ = _tri_masks(c)
    tri = jnp.where(incl, 1.0, 0.0).astype(BF16)
    cum = _dot_exact_lhs(tri, la)
    cum_t = cum.T
    ecum_all = jnp.exp(cum)
    wend_all = jnp.exp(cum[c - 1:c, :] - cum)
    elast_t = jnp.exp(cum_t[:, c - 1:c])
    lane_first = lax.broadcasted_iota(jnp.int32, (c, LANES), 1) < M2_HEADDIM
    row_first = lax.broadcasted_iota(jnp.int32, (LANES, LANES), 0) < M2_HEADDIM

    for g in range(M2_GROUPS):
        bg = bm[:, g * M2_STATE:(g + 1) * M2_STATE]
        cg = cm[:, g * M2_STATE:(g + 1) * M2_STATE]
        gm = _dot_nt(cg, bg)
        for j in range(pairs_per_group):
            p = g * pairs_per_group + j
            ha, hb = 2 * p, 2 * p + 1
            xp = xs[:, p * LANES:(p + 1) * LANES]

            def decay_weights(h):
                seg = cum[:, h:h + 1] - cum_t[h:h + 1, :]
                dec = jnp.where(incl, jnp.exp(jnp.where(incl, seg, 0.0)), 0.0)
                return (gm * dec).astype(BF16)

            xdt = xp * jnp.where(lane_first, dtv[:, ha:ha + 1], dtv[:, hb:hb + 1])
            xdt_a = jnp.where(lane_first, xdt, 0.0)
            xdt_b = xdt - xdt_a
            y = _dot(decay_weights(ha), xdt_a.astype(BF16)) + _dot(decay_weights(hb), xdt_b.astype(BF16))
            sp = s_ref[p]
            y = y + _dot_nt(cg, sp.astype(BF16)) * jnp.where(lane_first, ecum_all[:, ha:ha + 1], ecum_all[:, hb:hb + 1])
            y_ref[:, p * LANES:(p + 1) * LANES] = y + dsk_ref[:, p * LANES:(p + 1) * LANES] * xp
            xw = xdt * jnp.where(lane_first, wend_all[:, ha:ha + 1], wend_all[:, hb:hb + 1])
            dmat = jnp.where(row_first, elast_t[ha:ha + 1, :], elast_t[hb:hb + 1, :])
            s_ref[p] = sp * dmat + _dot(xw.T.astype(BF16), bg)

    o_ref[...] = _gated_rmsnorm_prologue(y_ref[...], z_ref[...], nw_ref[...]).astype(o_ref.dtype)

    @pl.when(n == pl.num_programs(1) - 1)
    def _():
        sout_ref[...] = s_ref[...]


def _m2_ssd(proj, gates, mw, batch, seq):
    c = M2_CHUNK
    nch = seq // c
    width = mw['width']
    gs = M2_GROUPS * M2_STATE
    npairs = width // LANES
    xo, bo, co = width // width, (2 * width) // gs, (2 * width + gs) // gs
    row = lambda w, off: pl.BlockSpec((c, w), lambda b, n: (b * nch + n, off))
    par = lambda r, w, off: pl.BlockSpec((r, w), lambda b, n: (0, off))
    return pl.pallas_call(
        _m2_ssd_kernel,
        out_shape=(jax.ShapeDtypeStruct((batch * seq, width), BF16),
                   jax.ShapeDtypeStruct((batch, npairs, LANES, M2_STATE), F32)),
        grid=(batch, nch),
        in_specs=[row(width, xo), row(gs, bo), row(gs, co), row(LANES, 0), row(width, 0),
                  par(CONV_W, width, 0), par(CONV_W, gs, width // gs), par(CONV_W, gs, width // gs + 1),
                  par(1, width, 0), par(1, gs, width // gs), par(1, gs, width // gs + 1),
                  par(1, LANES, 0), par(1, LANES, 0), par(1, width, 0), par(1, width, 0)],
        out_specs=(pl.BlockSpec((c, width), lambda b, n: (b * nch + n, 0)),
                   pl.BlockSpec((None, npairs, LANES, M2_STATE), lambda b, n: (b, 0, 0, 0))),
        scratch_shapes=[pltpu.VMEM((c + SUBLANES, width), F32), pltpu.VMEM((c + SUBLANES, gs), F32),
                        pltpu.VMEM((c + SUBLANES, gs), F32), pltpu.VMEM((npairs, LANES, M2_STATE), F32),
                        pltpu.VMEM((c, width), F32)],
        compiler_params=_params(("parallel", "arbitrary")), name="m2_ssd")(
            proj, proj, proj, gates, proj, mw['conv_w'], mw['conv_w'], mw['conv_w'], mw['conv_b'], mw['conv_b'],
            mw['conv_b'], mw['dt_bias'], mw['neg_a'], mw['d_ch'], mw['norm'])


def _m2_pre_step_kernel(xbc_ref, dt_ref, taps_ref, w_ref, b_ref, dtb_ref, nega_ref, act_ref, dtv_ref, dec_ref):
    act_ref[...] = _conv_silu_step(xbc_ref[...], taps_ref, w_ref, b_ref)
    dtv = _softplus(dt_ref[...] + dtb_ref[...])
    dtv_ref[...] = dtv
    dec_ref[...] = jnp.exp(nega_ref[...] * dtv)


def _m2_pre_step(proj, dt_raw, taps, mw):
    rows = proj.shape[0]
    width = mw['width']
    cdim = mw['conv_w'].shape[1]
    xbc = proj[:, width:width + cdim]
    return pl.pallas_call(
        _m2_pre_step_kernel,
        out_shape=(jax.ShapeDtypeStruct((rows, cdim), F32), jax.ShapeDtypeStruct((rows, LANES), F32),
                   jax.ShapeDtypeStruct((rows, LANES), F32)),
        name="m2_pre_step", compiler_params=pltpu.CompilerParams(vmem_limit_bytes=V7X_VMEM_LIMIT_BYTES))(
            xbc, dt_raw, taps, mw['conv_w'], mw['conv_b'], mw['dt_bias'], mw['neg_a'])


def _m2_state_step_kernel(x_ref, dtc_ref, decc_ref, b_ref, c_ref, dsk_ref, s_ref, y_ref, sout_ref):
    rows = x_ref.shape[0]
    pairs = x_ref.shape[1] // LANES
    bv = b_ref[...]
    cb = c_ref[...].astype(BF16)
    for j in range(pairs):
        sl = slice(j * LANES, (j + 1) * LANES)
        xp = x_ref[:, sl]
        xdt_t = _pad_to_square_t(xp * dtc_ref[:, sl], LANES)
        dec_t = _pad_to_square_t(decc_ref[:, sl], LANES)
        for b in range(rows):
            sp = s_ref[b, j]
            s_new = sp * dec_t[:, b:b + 1] + xdt_t[:, b:b + 1] * bv[b:b + 1, :]
            sout_ref[b, j] = s_new
            y_ref[b:b + 1, sl] = _dot_nt(cb[b:b + 1, :], s_new.astype(BF16)) + dsk_ref[:, sl] * xp[b:b + 1, :]


def _m2_state_step(act, dt_ch, dec_ch, ssm, mw):
    rows = act.shape[0]
    width = mw['width']
    gw = width // M2_GROUPS
    ppg = gw // LANES
    npairs = width // LANES
    gs = M2_GROUPS * M2_STATE
    s4 = ssm.reshape(rows, npairs, LANES, M2_STATE)
    blk = lambda w, base: pl.BlockSpec((rows, w), lambda g: (0, base + g))
    return pl.pallas_call(
        _m2_state_step_kernel,
        out_shape=(jax.ShapeDtypeStruct((rows, width), F32), jax.ShapeDtypeStruct(s4.shape, F32)),
        grid=(M2_GROUPS,),
        in_specs=[blk(gw, 0), blk(gw, 0), blk(gw, 0), blk(M2_STATE, width // M2_STATE),
                  blk(M2_STATE, (width + gs) // M2_STATE), pl.BlockSpec((1, gw), lambda g: (0, g)),
                  pl.BlockSpec((rows, ppg, LANES, M2_STATE), lambda g: (0, g, 0, 0))],
        out_specs=(blk(gw, 0), pl.BlockSpec((rows, ppg, LANES, M2_STATE), lambda g: (0, g, 0, 0))),
        compiler_params=_params(("parallel",)), name="m2_state_step")(
            act, dt_ch, dec_ch, act, act, mw['d_ch'], s4)


def _m2_weights(w_in, conv_w, conv_b, dt_bias, a_log, d_skip, norm_w, w_out):
    f = F32
    heads = dt_bias.shape[0]
    width = norm_w.shape[0]
    pad = LANES - heads
    cdim = conv_w.shape[1]
    return dict(
        w_in=_weight_bf16(w_in, width + cdim),
        w_gate=_pad_cols(w_in[:, width + cdim:].astype(f), LANES),
        conv_w=conv_w.astype(f), conv_b=conv_b.astype(f).reshape(1, -1),
        dt_bias=jnp.pad(dt_bias.astype(f), (0, pad)).reshape(1, LANES),
        neg_a=jnp.pad(-jnp.exp(a_log.astype(f)), (0, pad)).reshape(1, LANES),
        d_ch=jnp.repeat(d_skip.astype(f), width // heads).reshape(1, width),
        norm=norm_w.astype(f).reshape(1, width), w_out=_weight_bf16(w_out), width=width, heads=heads)


def _gated_rmsnorm_prologue(y, z, w):
    v = y * _silu(z)
    return v * lax.rsqrt(jnp.mean(v * v, axis=-1, keepdims=True) + EPS) * w


def _m2_layer(x, mods, state, mw, *, prompt, batch, seq, tm):
    g, scale, shift, gate = mods
    width, heads = mw['width'], mw['heads']
    cdim = mw['conv_w'].shape[1]
    proj = _in_proj(x, g, scale, shift, mw['w_in'], batch_kind=prompt, rows_per_batch=seq, tm=tm, tn=MM_TN,
                    name="m2_in_proj")
    gates = _gate_proj(x, g, scale, shift, mw['w_gate'], batch_kind=prompt, rows_per_batch=seq, tm=tm,
                       name="m2_gate_proj")
    if prompt:
        a, ssm = _m2_ssd(proj, gates, mw, batch, seq)
        conv_new = proj.reshape(batch, seq, -1)[:, seq - (CONV_W - 1):, width:width + cdim]
        x_new = _out_proj([('row', a)], mw['w_out'], x, gate, batch_kind=True, rows_per_batch=seq, tm=tm, tn=MM_TN,
                          name="m2_out_proj")
        return x_new, conv_new, ssm
    else:
        conv_buf, ssm_in = state
        taps = jnp.swapaxes(conv_buf.astype(F32), 0, 1)
        act, dtv, dec = _m2_pre_step(proj, gates, taps, mw)
        rep = width // heads
        dt_ch = jnp.repeat(dtv[:, :heads], rep, axis=1)
        dec_ch = jnp.repeat(dec[:, :heads], rep, axis=1)
        y, ssm = _m2_state_step(act, dt_ch, dec_ch, ssm_in.astype(F32), mw)
        conv_new = jnp.concatenate([conv_buf.astype(F32)[:, 1:], proj[:, None, width:width + cdim]], axis=1)
    x_new = _out_proj([('row', y), ('row', proj, width, 0), ('vec', mw['norm'])], mw['w_out'], x, gate,
                      batch_kind=prompt, rows_per_batch=seq, tm=min(tm, 256), tn=MM_TN, name="m2_out_proj",
                      prologue=_gated_rmsnorm_prologue)
    return x_new, conv_new, ssm


GD_DK = 128
GD_DV = 128
GD_CHUNK = 64


def _dot_3pass(a, b):
    ah = a.astype(BF16)
    al = (a - ah.astype(F32)).astype(BF16)
    bh = b.astype(BF16)
    bl = (b - bh.astype(F32)).astype(BF16)
    return _dot(ah, bh) + (_dot(ah, bl) + _dot(al, bh))


def _l2norm_rows(x):
    return x * lax.rsqrt(jnp.sum(x * x, axis=-1, keepdims=True) + EPS)


def _rmsnorm_rows(x, w):
    return x * lax.rsqrt(jnp.mean(x * x, axis=-1, keepdims=True) + EPS) * w


GD_INV_BASE = 16


def _bdot(a, b):
    return jnp.einsum('hmk,hkn->hmn', a, b, preferred_element_type=F32)


def _bdot_nt(a, b):
    return jnp.einsum('hmk,hnk->hmn', a, b, preferred_element_type=F32)


def _split2(x):
    hi = x.astype(BF16)
    return hi, (x - hi.astype(F32)).astype(BF16)


def _bdot_3pass(a, b):
    (ah, al), (bh, bl) = a, b
    return _bdot(ah, bh) + (_bdot(ah, bl) + _bdot(al, bh))


def _unit_lower_inverse(a_strict):
    c = a_strict.shape[-1]
    row = lax.broadcasted_iota(jnp.int32, (c, c), 0)
    col = lax.broadcasted_iota(jnp.int32, (c, c), 1)
    eye = jnp.where(row == col, 1.0, 0.0)
    blk = GD_INV_BASE
    shift = int(math.log2(blk))
    p = jnp.where((row >> shift) == (col >> shift), -a_strict, 0.0)
    t = eye + p
    ps = _split2(p)
    for _ in range(shift - 1):
        p = _bdot_3pass(ps, ps)
        ps = _split2(p)
        t = t + _bdot_3pass(_split2(t), ps)
    while blk < c:
        below = jnp.logical_and((row >> (shift + 1)) == (col >> (shift + 1)), (row >> shift) != (col >> shift))
        ts = _split2(t)
        tb = _bdot_3pass(ts, _split2(jnp.where(below, a_strict, 0.0)))
        t = t - _bdot_3pass(_split2(tb), ts)
        blk *= 2
        shift += 1
    return t


def _gd_chunk_kernel(qkv_ref, z_ref, braw_ref, araw_ref, cw_ref, nega_ref, dtb_ref, nw_ref,
                     o_ref, sout_ref, pad, s_ref):
    n = pl.program_id(1)
    first = n == 0
    c = qkv_ref.shape[0]
    hv = s_ref.shape[0]
    hk = hv // 2
    rep = hv // hk

    @pl.when(first)
    def _():
        s_ref[...] = jnp.zeros_like(s_ref)

    qkv = _conv_silu_chunk(qkv_ref, cw_ref, None, pad, first)
    beta = _sigmoid(braw_ref[...])
    gl = nega_ref[...] * _softplus(araw_ref[...] + dtb_ref[...])
    incl, strict = _tri_masks(c)
    tri = jnp.where(incl, 1.0, 0.0).astype(BF16)
    gcum = _dot_exact_lhs(tri, gl)
    gcum_t = jnp.concatenate([gcum, jnp.zeros((LANES - c, LANES), F32)], axis=0).T

    heads = range(hv)
    per_value_head = lambda t: jnp.stack([t[h // rep] for h in heads])
    q3 = jnp.stack([qkv[:, i * GD_DK:(i + 1) * GD_DK] for i in range(hk)])
    k3 = jnp.stack([qkv[:, (hk + i) * GD_DK:(hk + i + 1) * GD_DK] for i in range(hk)])
    v3 = jnp.stack([qkv[:, (2 * hk + h) * GD_DV:(2 * hk + h + 1) * GD_DV] for h in heads])
    q3 = _l2norm_rows(q3) * (GD_DK ** -0.5)
    k3 = _l2norm_rows(k3)
    k3b = k3.astype(BF16)
    kk = per_value_head(_bdot_nt(k3b, k3b))
    qk = per_value_head(_bdot_nt(q3.astype(BF16), k3b))
    q_v, k_v = per_value_head(q3), per_value_head(k3)
    colv = jnp.stack([gcum[:, h:h + 1] for h in heads])
    rowv = jnp.stack([gcum_t[h:h + 1, :c] for h in heads])
    bcol = jnp.stack([beta[:, h:h + 1] for h in heads])
    glast = colv[:, c - 1:c, :]
    ecol = jnp.exp(colv)
    dec = jnp.where(incl, jnp.exp(jnp.where(incl, colv - rowv, 0.0)), 0.0)
    a = jnp.where(strict, (bcol * kk) * dec, 0.0)
    tinv = _unit_lower_inverse(a)
    rhs = jnp.concatenate([v3 * bcol, (k_v * bcol) * ecol], axis=-1)
    th, tl = _split2(tinv)
    rb = rhs.astype(BF16)
    sol = _bdot(th, rb) + _bdot(tl, rb)
    u, w = sol[:, :, :GD_DV], sol[:, :, GD_DV:]
    s = s_ref[...]
    sb = s.astype(BF16)
    v_new = u - _bdot(w.astype(BF16), sb)
    o = _bdot((q_v * ecol).astype(BF16), sb) + _bdot((qk * dec).astype(BF16), v_new.astype(BF16))
    zpad = jnp.zeros((hv, LANES - c, GD_DV), F32)
    kd_t = jnp.swapaxes(jnp.concatenate([k_v * jnp.exp(glast - colv), zpad], axis=1), 1, 2)
    vn_pad = jnp.concatenate([v_new, zpad], axis=1)
    s_ref[...] = s * jnp.exp(glast) + _bdot(kd_t.astype(BF16), vn_pad.astype(BF16))
    on = _rmsnorm_rows(o, nw_ref[...])
    for h in heads:
        sl = slice(h * GD_DV, (h + 1) * GD_DV)
        o_ref[:, sl] = (on[h] * _silu(z_ref[:, sl])).astype(o_ref.dtype)

    @pl.when(n == pl.num_programs(1) - 1)
    def _():
        sout_ref[...] = s_ref[...]


def _gd_chunked(proj, gates, gw, batch, seq):
    c = GD_CHUNK
    nch = seq // c
    cdim, width, hv = gw['cdim'], gw['width'], gw['hv']
    row = lambda w, off: pl.BlockSpec((c, w), lambda b, n: (b * nch + n, off))
    par = lambda r, w: pl.BlockSpec((r, w), lambda b, n: (0, 0))
    return pl.pallas_call(
        _gd_chunk_kernel,
        out_shape=(jax.ShapeDtypeStruct((batch * seq, width), BF16),
                   jax.ShapeDtypeStruct((batch, hv, GD_DK, GD_DV), F32)),
        grid=(batch, nch),
        in_specs=[row(cdim, 0), row(width, cdim // width), row(LANES, 0), row(LANES, 1),
                  par(CONV_W, cdim), par(1, LANES), par(1, LANES), par(1, GD_DV)],
        out_specs=(pl.BlockSpec((c, width), lambda b, n: (b * nch + n, 0)),
                   pl.BlockSpec((None, hv, GD_DK, GD_DV), lambda b, n: (b, 0, 0, 0))),
        scratch_shapes=[pltpu.VMEM((c + SUBLANES, cdim), F32), pltpu.VMEM((hv, GD_DK, GD_DV), F32)],
        compiler_params=_params(("parallel", "arbitrary")), name="gd_chunked")(
            proj, proj, gates, gates, gw['conv_w'], gw['neg_a'], gw['dt_bias'], gw['norm'])


def _gd_pre_step_kernel(qkv_ref, braw_ref, araw_ref, taps_ref, cw_ref, nega_ref, dtb_ref,
                        q_ref, k_ref, v_ref, beta_ref, eg_ref):
    hk = q_ref.shape[1] // GD_DK
    act = _conv_silu_step(qkv_ref[...], taps_ref, cw_ref, None)
    for kh in range(hk):
        sl = slice(kh * GD_DK, (kh + 1) * GD_DK)
        q_ref[:, sl] = _l2norm_rows(act[:, kh * GD_DK:(kh + 1) * GD_DK]) * (GD_DK ** -0.5)
        k_ref[:, sl] = _l2norm_rows(act[:, (hk + kh) * GD_DK:(hk + kh + 1) * GD_DK])
    v_ref[...] = act[:, 2 * hk * GD_DK:]
    beta_ref[...] = _sigmoid(braw_ref[...])
    eg_ref[...] = jnp.exp(nega_ref[...] * _softplus(araw_ref[...] + dtb_ref[...]))


def _gd_pre_step(proj, gates, taps, gw):
    rows = proj.shape[0]
    cdim, width, hv = gw['cdim'], gw['width'], gw['hv']
    qk_w = (cdim - width) // 2
    sd = lambda w: jax.ShapeDtypeStruct((rows, w), F32)
    return pl.pallas_call(
        _gd_pre_step_kernel, out_shape=(sd(qk_w), sd(qk_w), sd(width), sd(LANES), sd(LANES)),
        name="gd_pre_step", compiler_params=pltpu.CompilerParams(vmem_limit_bytes=V7X_VMEM_LIMIT_BYTES))(
            proj[:, :cdim], gates[:, :LANES], gates[:, LANES:], taps,
            gw['conv_w'], gw['neg_a'], gw['dt_bias'])


def _gd_state_step_kernel(q_ref, k_ref, v_ref, beta_ref, eg_ref, z_ref, nw_ref, s_ref, o_ref, sout_ref):
    rows = q_ref.shape[0]
    nk = q_ref.shape[1] // GD_DK
    rep = (v_ref.shape[1] // GD_DV) // nk
    nw = nw_ref[...]
    zrows = jnp.zeros((SUBLANES - 2, GD_DK), F32)
    for kh in range(nk):
        ksl = slice(kh * GD_DK, (kh + 1) * GD_DK)
        q8, k8 = q_ref[:, ksl], k_ref[:, ksl]
        k_t = _pad_to_square_t(k8, GD_DK)
        for b in range(rows):
            qb, kb = q8[b:b + 1, :], k8[b:b + 1, :]
            kq = jnp.concatenate([kb, qb, zrows], axis=0).astype(BF16)
            qk = jnp.sum(qb * kb, axis=-1, keepdims=True)
            for r in range(rep):
                h = kh * rep + r
                vsl = slice(h * GD_DV, (h + 1) * GD_DV)
                s = s_ref[b, h]
                ks_qs = _dot(kq, s.astype(BF16))
                eg = eg_ref[b:b + 1, vsl]
                beta = beta_ref[b:b + 1, vsl]
                v_new = beta * (v_ref[b:b + 1, vsl] - eg * ks_qs[0:1, :])
                o = eg * ks_qs[1:2, :] + qk * v_new
                sout_ref[b, h] = s * eg[:, 0:1] + k_t[:, b:b + 1] * v_new
                o_ref[b:b + 1, vsl] = _rmsnorm_rows(o, nw) * _silu(z_ref[b:b + 1, vsl])


def _gd_state_step(proj, qn, kn, v, beta_ch, eg_ch, state, gw, heads_per_step=4):
    rows = qn.shape[0]
    cdim, width, hv = gw['cdim'], gw['width'], gw['hv']
    steps = hv // heads_per_step
    kw = qn.shape[1] // steps
    vw = width // steps
    blk = lambda w, base=0: pl.BlockSpec((rows, w), lambda g, base=base: (0, base + g))
    sspec = pl.BlockSpec((rows, heads_per_step, GD_DK, GD_DV), lambda g: (0, g, 0, 0))
    return pl.pallas_call(
        _gd_state_step_kernel,
        out_shape=(jax.ShapeDtypeStruct((rows, width), F32), jax.ShapeDtypeStruct(state.shape, F32)),
        grid=(steps,),
        in_specs=[blk(kw), blk(kw), blk(vw), blk(vw), blk(vw), blk(vw, cdim // vw),
                  pl.BlockSpec((1, GD_DV), lambda g: (0, 0)), sspec],
        out_specs=(blk(vw), sspec),
        compiler_params=_params(("parallel",)), name="gd_state_step")(
            qn, kn, v, beta_ch, eg_ch, proj, gw['norm'], state)


def _gd_weights(w_in, conv_w, a_log, dt_bias, norm_w, w_out):
    f = F32
    hv = a_log.shape[0]
    cdim = conv_w.shape[1]
    width = w_out.shape[0]
    pad = LANES - hv
    base = cdim + width
    zeros = jnp.zeros((w_in.shape[0], pad), w_in.dtype)
    w_gate = jnp.concatenate([w_in[:, base:base + hv], zeros, w_in[:, base + hv:], zeros], axis=1)
    return dict(
        w_in=_weight_bf16(w_in, base), w_gate=w_gate.astype(f), conv_w=conv_w.astype(f),
        neg_a=jnp.pad(-jnp.exp(a_log.astype(f)), (0, pad)).reshape(1, LANES),
        dt_bias=jnp.pad(dt_bias.astype(f), (0, pad)).reshape(1, LANES),
        norm=norm_w.astype(f).reshape(1, -1), w_out=_weight_bf16(w_out), cdim=cdim, width=width, hv=hv)


def _gd_layer(x, mods, state, gw, *, prompt, batch, seq, tm):
    g, scale, shift, gate = mods
    cdim, width, hv = gw['cdim'], gw['width'], gw['hv']
    proj = _in_proj(x, g, scale, shift, gw['w_in'], batch_kind=prompt, rows_per_batch=seq, tm=tm, tn=MM_TN,
                    name="gd_in_proj")
    gates = _gate_proj(x, g, scale, shift, gw['w_gate'], batch_kind=prompt, rows_per_batch=seq, tm=tm,
                       name="gd_gate_proj")
    if prompt:
        a, ssm = _gd_chunked(proj, gates, gw, batch, seq)
        conv_new = proj.reshape(batch, seq, -1)[:, seq - (CONV_W - 1):, :cdim]
        x_new = _out_proj([('row', a)], gw['w_out'], x, gate, batch_kind=True, rows_per_batch=seq, tm=tm, tn=MM_TN,
                          name="gd_out_proj")
    else:
        conv_buf, ssm_in = state
        taps = jnp.swapaxes(conv_buf.astype(F32), 0, 1)
        qn, kn, v, beta, eg = _gd_pre_step(proj, gates, taps, gw)
        beta_ch = jnp.repeat(beta[:, :hv], GD_DV, axis=1)
        eg_ch = jnp.repeat(eg[:, :hv], GD_DV, axis=1)
        a, ssm = _gd_state_step(proj, qn, kn, v, beta_ch, eg_ch, ssm_in.astype(F32), gw)
        conv_new = jnp.concatenate([conv_buf.astype(F32)[:, 1:], proj[:, None, :cdim]], axis=1)
        x_new = _out_proj([('row', a)], gw['w_out'], x, gate, batch_kind=False, rows_per_batch=seq, tm=tm, tn=MM_TN,
                          name="gd_out_proj", prologue=_cast_prologue)
    return x_new, conv_new, ssm


AT_DIM = 128
IDX_DIM = 128
TOPK_MAX = 256
REL_BUCKETS = 32
REL_MAX_DIST = 128
AT_TILE = 256
INT32_MIN = -2 ** 31
_NEG_BITS = int(np.float32(NEG).view(np.int32))
NEG_SORT_KEY = _NEG_BITS ^ 0x7FFFFFFF if _NEG_BITS < 0 else _NEG_BITS


def _bucket_starts():
    d = np.arange(0, REL_MAX_DIST + 1)
    exact = REL_BUCKETS // 2
    far = exact + (np.log(np.maximum(d, 1).astype(np.float32) / exact) / math.log(REL_MAX_DIST / exact)
                   * (REL_BUCKETS - exact)).astype(np.int32)
    bucket = np.where(d < exact, d, np.minimum(far, REL_BUCKETS - 1))
    assert np.all(np.diff(bucket) >= 0) and bucket[-1] == REL_BUCKETS - 1
    return [int(np.argmax(bucket >= b)) for b in range(REL_BUCKETS)]


def _bias_from_dist(dist, value_of_bucket):
    starts = _bucket_starts()
    val = value_of_bucket(REL_BUCKETS - 1)
    for b in range(REL_BUCKETS - 2, -1, -1):
        val = jnp.where(dist < starts[b + 1], value_of_bucket(b), val)
    return val


def _sort_key(x):
    x = jnp.where(x == 0.0, 0.0, x)
    b = pltpu.bitcast(x, jnp.int32)
    return jnp.where(b < 0, b ^ jnp.int32(0x7FFFFFFF), b)


def _kth_largest_key(count_ge, shape, k):
    def body(it, ans):
        cand = ans | jnp.left_shift(jnp.int32(1), 31 - it)
        cnt = count_ge(cand ^ jnp.int32(INT32_MIN))
        return jnp.where(cnt >= k, cand, ans)

    ans = lax.fori_loop(0, 32, body, jnp.zeros(shape, jnp.int32))
    return ans ^ jnp.int32(INT32_MIN)


def _relbias_tiles_kernel(rb_ref, o_ref):
    delta = pl.program_id(0) * AT_TILE
    h = pl.program_id(1)
    i = lax.broadcasted_iota(jnp.int32, (AT_TILE, AT_TILE), 0)
    j = lax.broadcasted_iota(jnp.int32, (AT_TILE, AT_TILE), 1)
    o_ref[...] = _bias_from_dist(delta + i - j, lambda b: rb_ref[b, h]) - rb_ref[REL_BUCKETS - 1, h]


def _relbias_tiles(rel_bias):
    heads = rel_bias.shape[1]
    ntile = 2
    assert ntile * AT_TILE - (AT_TILE - 1) >= REL_MAX_DIST
    return pl.pallas_call(
        _relbias_tiles_kernel, out_shape=jax.ShapeDtypeStruct((ntile, heads, AT_TILE, AT_TILE), F32),
        grid=(ntile, heads),
        in_specs=[pl.BlockSpec(memory_space=pltpu.SMEM)],
        out_specs=pl.BlockSpec((None, None, AT_TILE, AT_TILE), lambda d, h: (d, h, 0, 0)),
        compiler_params=_params(("parallel", "parallel")), name="at_relbias_tiles")(rel_bias.astype(F32))


def _at_index_kernel(qi_ref, wi_ref, ki_ref, o_ref, keys, cnt, *, k_sel, score_scale):
    qb = pl.program_id(1)
    tq = qi_ref.shape[0]
    nkb = keys.shape[0]
    tk = keys.shape[2]
    nh = qi_ref.shape[1] // IDX_DIM
    wsc = wi_ref[...] * score_scale
    qpos = qb * tq + lax.broadcasted_iota(jnp.int32, (tq, tk), 0)
    kloc = lax.broadcasted_iota(jnp.int32, (tq, tk), 1)
    neg_key = _sort_key(jnp.full((tq, tk), NEG, F32))

    for kb in range(nkb):
        @pl.when(kb <= qb)
        def _():
            kblk = ki_ref[kb * tk:(kb + 1) * tk, :].astype(BF16)
            sc = jnp.zeros((tq, tk), F32)
            for h in range(nh):
                d = _dot_nt(qi_ref[:, h * IDX_DIM:(h + 1) * IDX_DIM].astype(BF16), kblk)
                sc = sc + wsc[:, h:h + 1] * jnp.maximum(d, 0.0)
            adm = kb * tk + kloc <= qpos
            keys[kb] = _sort_key(jnp.where(adm, sc, NEG))

        @pl.when(kb > qb)
        def _():
            keys[kb] = neg_key

    def count_ge(t):
        cnt[...] = jnp.where(keys[0] >= t, 1, 0)
        for kb in range(1, nkb):
            @pl.when(kb <= qb)
            def _():
                cnt[...] += jnp.where(keys[kb] >= t, 1, 0)
        beyond = (nkb - 1 - qb) * tk
        return jnp.sum(cnt[...], axis=1, keepdims=True) + jnp.where(t <= NEG_SORT_KEY, beyond, 0)

    thr = _kth_largest_key(count_ge, (tq, 1), k_sel)
    n_ge = count_ge(thr)
    has_ties = jnp.max(n_ge) > k_sel

    @pl.when(jnp.logical_not(has_ties))
    def _():
        for kb in range(nkb):
            adm = kb * tk + kloc <= qpos
            sel = jnp.logical_and(keys[kb] >= thr, adm)
            o_ref[kb] = jnp.where(sel, 0.0, MASKED).T.astype(o_ref.dtype)

    @pl.when(has_ties)
    def _():
        acc = jnp.zeros((tq, tk), jnp.int32)
        for kb in range(nkb):
            acc = acc + jnp.where(keys[kb] > thr, 1, 0)
        room = (k_sel - jnp.sum(acc, axis=1, keepdims=True)).astype(F32)
        upper = jnp.where(lax.broadcasted_iota(jnp.int32, (tk, tk), 0) <= lax.broadcasted_iota(jnp.int32, (tk, tk), 1),
                          1.0, 0.0).astype(BF16)
        seen = jnp.zeros((tq, 1), F32)
        for kb in range(nkb):
            key = keys[kb]
            eq = key == thr
            eqf = jnp.where(eq, 1.0, 0.0)
            rank = seen + _dot(eqf.astype(BF16), upper)
            seen = seen + jnp.sum(eqf, axis=1, keepdims=True)
            adm = kb * tk + kloc <= qpos
            sel = jnp.logical_and(jnp.logical_or(key > thr, jnp.logical_and(eq, rank <= room)), adm)
            o_ref[kb] = jnp.where(sel, 0.0, MASKED).T.astype(o_ref.dtype)


def _at_index(proj, aw, batch, seq, k_sel):
    tq = tk = AT_TILE
    nq = seq // tq
    width = aw['width']
    nh = aw['idx_heads']
    qio = (4 * width) // (nh * IDX_DIM)
    kio = (4 * width + nh * IDX_DIM) // IDX_DIM
    kern = functools.partial(_at_index_kernel, k_sel=k_sel, score_scale=IDX_DIM ** -0.5 * nh ** -0.5)
    return pl.pallas_call(
        kern, out_shape=jax.ShapeDtypeStruct((batch * nq, seq // tk, tq, tk), BF16), grid=(batch, nq),
        in_specs=[pl.BlockSpec((tq, nh * IDX_DIM), lambda b, q: (b * nq + q, qio)),
                  pl.BlockSpec((tq, LANES), lambda b, q: (b * nq + q, kio + 1)),
                  pl.BlockSpec((seq, IDX_DIM), lambda b, q: (b, kio))],
        out_specs=pl.BlockSpec((None, seq // tk, tq, tk), lambda b, q: (b * nq + q, 0, 0, 0)),
        scratch_shapes=[pltpu.VMEM((seq // tk, tq, tk), jnp.int32), pltpu.VMEM((tq, tk), jnp.int32)],
        compiler_params=_params(("parallel", "parallel")), name="at_index")(proj, proj, proj)


AT_HEAD_GROUP = 8
MASKED = 2.0 * NEG


def _at_attend_kernel(q_ref, k_ref, vt_ref, z_ref, mask_ref, bias_ref, o_ref, acc, m_scr, l_scr):
    qb = pl.program_id(2)
    t = q_ref.shape[0]
    hg = q_ref.shape[1] // AT_DIM
    acc[...] = jnp.zeros_like(acc)
    m_scr[...] = jnp.full(m_scr.shape, NEG, F32)
    l_scr[...] = jnp.zeros_like(l_scr)

    heads = [slice(h * AT_DIM, (h + 1) * AT_DIM) for h in range(hg)]
    q3t = jnp.stack([q_ref[:, sl].T for sl in heads]).astype(BF16)

    def key_tile(kb, bias):
        rows = pl.ds(pl.multiple_of(kb * t, t), t)
        k3 = jnp.stack([k_ref[rows, sl] for sl in heads])
        s_t = _bdot(k3, q3t) + mask_ref[kb].astype(F32)
        if bias is not None:
            s_t = s_t + bias
        m_old = m_scr[...]
        m_new = jnp.maximum(m_old, jnp.max(s_t, axis=1, keepdims=True))
        alpha = jnp.exp(m_old - m_new)
        p_t = jnp.exp(s_t - m_new)
        l_scr[...] = alpha * l_scr[...] + jnp.sum(p_t, axis=1, keepdims=True)
        acc[...] = alpha * acc[...] + _bdot(vt_ref[:, kb], p_t.astype(BF16))
        m_scr[...] = m_new

    def far_tile(kb, carry):
        key_tile(kb, None)
        return carry

    lax.fori_loop(0, jnp.maximum(qb - 1, 0), far_tile, 0)

    @pl.when(qb >= 1)
    def _():
        key_tile(qb - 1, bias_ref[1])

    key_tile(qb, bias_ref[0])
    o_t = acc[...] / l_scr[...]
    for h, sl in enumerate(heads):
        o_ref[:, sl] = (o_t[h].T * _silu(z_ref[:, sl])).astype(o_ref.dtype)


def _at_attend(proj, proj_bf, maskadd_t, tiles_t, aw, batch, seq):
    t = AT_TILE
    nq = seq // t
    width = aw['width']
    heads = width // AT_DIM
    hg = AT_HEAD_GROUP
    gw = hg * AT_DIM
    ng = width // gw
    v_t = proj_bf[:, 2 * width:3 * width].reshape(batch, nq, t, heads, AT_DIM).transpose(0, 3, 1, 4, 2)
    return pl.pallas_call(
        _at_attend_kernel, out_shape=jax.ShapeDtypeStruct((batch * seq, width), BF16), grid=(batch, ng, nq),
        in_specs=[pl.BlockSpec((t, gw), lambda b, g, q: (b * nq + q, g)),
                  pl.BlockSpec((seq, gw), lambda b, g, q: (b, ng + g)),
                  pl.BlockSpec((None, hg, nq, AT_DIM, t), lambda b, g, q: (b, g, 0, 0, 0)),
                  pl.BlockSpec((t, gw), lambda b, g, q: (b * nq + q, 3 * ng + g)),
                  pl.BlockSpec((None, nq, t, t), lambda b, g, q: (b * nq + q, 0, 0, 0)),
                  pl.BlockSpec((2, hg, t, t), lambda b, g, q: (0, g, 0, 0))],
        out_specs=pl.BlockSpec((t, gw), lambda b, g, q: (b * nq + q, g)),
        scratch_shapes=[pltpu.VMEM((hg, AT_DIM, t), F32), pltpu.VMEM((hg, 1, t), F32), pltpu.VMEM((hg, 1, t), F32)],
        compiler_params=_params(("parallel", "parallel", "arbitrary")), name="at_attend")(
            proj, proj_bf, v_t, proj, maskadd_t, tiles_t)


def _at_weights(w_in, rel_bias, w_out):
    width = w_out.shape[0]
    heads = rel_bias.shape[1]
    idx_heads = (w_in.shape[1] - 4 * width - IDX_DIM) // (IDX_DIM + 1)
    col_scale = jnp.where(jnp.arange(w_in.shape[1]) < width, AT_DIM ** -0.5, 1.0).astype(F32)
    return dict(w_in=_weight_bf16(w_in, col_scale=col_scale), rel_bias=rel_bias.astype(F32),
                w_out=_weight_bf16(w_out), width=width, heads=heads, idx_heads=idx_heads)


def _at_rows_kernel(k_ref, v_ref, ki_ref, ko_ref, vo_ref, kio_ref):
    nh = ko_ref.shape[1]
    heads = lambda x: jnp.stack([x[:, h * AT_DIM:(h + 1) * AT_DIM] for h in range(nh)], axis=1)
    ko_ref[...] = heads(k_ref[...])
    vo_ref[...] = heads(v_ref[...])
    kio_ref[...] = ki_ref[...]


def _at_rows(proj, aw, lead, tm=256):
    width, heads = aw['width'], aw['heads']
    m = proj.shape[0]
    tm = min(tm, m)
    kio = (4 * width + aw['idx_heads'] * IDX_DIM) // IDX_DIM
    row4 = pl.BlockSpec((tm, heads, AT_DIM), lambda i: (i, 0, 0))
    k, v, ki = pl.pallas_call(
        _at_rows_kernel,
        out_shape=(jax.ShapeDtypeStruct((m, heads, AT_DIM), F32), jax.ShapeDtypeStruct((m, heads, AT_DIM), F32),
                   jax.ShapeDtypeStruct((m, IDX_DIM), F32)),
        grid=(m // tm,),
        in_specs=[pl.BlockSpec((tm, width), lambda i: (i, 1)), pl.BlockSpec((tm, width), lambda i: (i, 2)),
                  pl.BlockSpec((tm, IDX_DIM), lambda i: (i, kio))],
        out_specs=(row4, row4, pl.BlockSpec((tm, IDX_DIM), lambda i: (i, 0))),
        compiler_params=_params(("parallel",)), name="at_rows")(proj, proj, proj)
    return (k.reshape(lead + (heads, AT_DIM)), v.reshape(lead + (heads, AT_DIM)), ki.reshape(lead + (IDX_DIM,)))


def _at_layer_prompt(x, mods, aw, final_g, *, batch, seq, tm):
    g, scale, shift, gate = mods
    proj, proj_bf = _in_proj(x, g, scale, shift, aw['w_in'], batch_kind=True, rows_per_batch=seq, tm=tm, tn=MM_TN,
                             name="at_in_proj", bf16_copy=True)
    k_sel = min(TOPK_MAX, seq // 4)
    maskadd = _at_index(proj, aw, batch, seq, k_sel)
    tiles = _relbias_tiles(aw['rel_bias'])
    a = _at_attend(proj, proj_bf, maskadd, jnp.swapaxes(tiles, 2, 3), aw, batch, seq)
    d = x.shape[1]
    y = _fused_matmul([('row', a)], aw['w_out'], [('tile', x), ('batchcol', gate), ('col', final_g.reshape(1, d))],
                      prologue=None, epilogue=_residual_norm_epilogue, out_dtype=F32, tm=min(tm, MM_TM_WIDE_ROWS),
                      tn=d, rows_per_batch=seq, name="at_out_proj_norm")
    return (y,) + _at_rows(proj, aw, (batch, seq))


AT_PAGES_PER_STEP = 16


def _at_page_scores_kernel(pt_ref, qi_ref, w_ref, kidx_ref, o_ref, kbuf, sem):
    b, j = pl.program_id(0), pl.program_id(1)
    nj = pl.num_programs(1)
    npg = kbuf.shape[1]
    step = b * nj + j
    last_step = pl.num_programs(0) * nj - 1

    def page_copy(s, i, slot):
        sb, sj = s // nj, s % nj
        return pltpu.make_async_copy(kidx_ref.at[pt_ref[sb, sj * npg + i]], kbuf.at[slot, i], sem.at[slot])

    def start_all(s, slot):
        for i in range(npg):
            page_copy(s, i, slot).start()

    slot = step % 2

    @pl.when(step == 0)
    def _():
        start_all(step, slot)

    @pl.when(step < last_step)
    def _():
        start_all(step + 1, 1 - slot)

    for i in range(npg):
        page_copy(step, i, slot).wait()
    qi = qi_ref[...].astype(BF16)
    w = w_ref[...]
    for i in range(npg):
        d = _dot_nt(qi, kbuf[slot, i].astype(BF16))
        o_ref[i:i + 1, :] = jnp.sum(w * jnp.maximum(d, 0.0), axis=0, keepdims=True)


def _at_page_scores(qi3, w3, cache_kidx, page_table):
    rows, nh, _ = qi3.shape
    npages = page_table.shape[1]
    page = cache_kidx.shape[1]
    npg = math.gcd(npages, AT_PAGES_PER_STEP)
    grid_spec = pltpu.PrefetchScalarGridSpec(
        num_scalar_prefetch=1, grid=(rows, npages // npg),
        in_specs=[pl.BlockSpec((None, nh, IDX_DIM), lambda b, j, pt: (b, 0, 0)),
                  pl.BlockSpec((None, nh, IDX_DIM), lambda b, j, pt: (b, 0, 0)),
                  pl.BlockSpec(memory_space=pl.ANY)],
        out_specs=pl.BlockSpec((None, npg, page), lambda b, j, pt: (b, j, 0)),
        scratch_shapes=[pltpu.VMEM((2, npg, page, IDX_DIM), F32), pltpu.SemaphoreType.DMA((2,))])
    return pl.pallas_call(
        _at_page_scores_kernel, out_shape=jax.ShapeDtypeStruct((rows, npages, page), F32), grid_spec=grid_spec,
        compiler_params=_params(("arbitrary", "arbitrary")), name="at_page_scores")(page_table, qi3, w3, cache_kidx)


def _at_sample_select_kernel(sc_ref, qi_ref, w_ref, kin_ref, gidx_ref, newadd_ref, rank_scr, *, k_sel):
    npages, page = sc_ref.shape
    upper = jnp.where(lax.broadcasted_iota(jnp.int32, (page, page), 0) <= lax.broadcasted_iota(jnp.int32, (page, page), 1),
                      1.0, 0.0).astype(BF16)
    lower = jnp.where(lax.broadcasted_iota(jnp.int32, (npages, npages), 1) < lax.broadcasted_iota(jnp.int32, (npages, npages), 0),
                      1.0, 0.0).astype(BF16)

    def total(x):
        return jnp.sum(jnp.sum(x, axis=1, keepdims=True), axis=0, keepdims=True)

    def position_rank(flags):
        row_cnt = jnp.broadcast_to(jnp.sum(flags, axis=1, keepdims=True), (npages, page))
        return _dot(lower, row_cnt.astype(BF16)) + _dot(flags.astype(BF16), upper)

    keys = _sort_key(sc_ref[...])
    dots = jnp.sum(qi_ref[...] * kin_ref[...], axis=1, keepdims=True)
    s_new = jnp.sum(w_ref[:, 0:1] * jnp.maximum(dots, 0.0), axis=0, keepdims=True)
    key_new = _sort_key(s_new)

    def count_ge(t):
        return total(jnp.where(keys >= t, 1, 0)) + jnp.where(key_new >= t, 1, 0)

    thr = _kth_largest_key(count_ge, (1, 1), k_sel)
    n_gt = total(jnp.where(keys > thr, 1.0, 0.0)) + jnp.where(key_new > thr, 1.0, 0.0)
    room = k_sel - n_gt
    eq = keys == thr
    eqf = jnp.where(eq, 1.0, 0.0)
    sel = jnp.logical_or(keys > thr, jnp.logical_and(eq, position_rank(eqf) <= room))
    sel_new = jnp.logical_or(key_new > thr, jnp.logical_and(key_new == thr, total(eqf) + 1.0 <= room))
    newadd_ref[...] = jnp.broadcast_to(jnp.where(sel_new, 0.0, NEG), (1, page))

    self_f = jnp.where(sel, 1.0, 0.0)
    rank_scr[...] = jnp.where(sel, position_rank(self_f) - 1.0, -1.0)
    jidx = lax.broadcasted_iota(jnp.int32, (k_sel, page), 0).astype(F32)
    lane = lax.broadcasted_iota(jnp.int32, (page, LANES), 1)
    pick = jnp.where(lane == 0, lax.broadcasted_iota(jnp.int32, (page, LANES), 0).astype(F32),
                     jnp.where(lane <= 2, 1.0, 0.0)).astype(BF16)
    out_lane = lax.broadcasted_iota(jnp.int32, (k_sel, LANES), 1)

    def add_pages(g, acc):
        ranks = rank_scr[pl.ds(pl.multiple_of(g * SUBLANES, SUBLANES), SUBLANES), :]
        for i in range(SUBLANES):
            onehot = jnp.where(ranks[i:i + 1, :] == jidx, 1.0, 0.0).astype(BF16)
            slot = lax.convert_element_type(g * SUBLANES + i, F32)
            acc = acc + _dot(onehot, pick) * jnp.where(out_lane == 1, slot, 1.0)
        return acc

    assert npages % SUBLANES == 0
    gidx_ref[...] = lax.fori_loop(0, npages // SUBLANES, add_pages, jnp.zeros((k_sel, LANES), F32))


def _at_sample_select(scores, qi3, w3, ki_new, k_sel):
    rows, npages, page = scores.shape
    nh = qi3.shape[1]
    assert page == LANES and page >= REL_MAX_DIST
    kern = functools.partial(_at_sample_select_kernel, k_sel=k_sel)
    return pl.pallas_call(
        kern, out_shape=(jax.ShapeDtypeStruct((rows, k_sel, LANES), F32), jax.ShapeDtypeStruct((rows, 1, page), F32)),
        grid=(rows,),
        in_specs=[pl.BlockSpec((None, npages, page), lambda b: (b, 0, 0)),
                  pl.BlockSpec((None, nh, IDX_DIM), lambda b: (b, 0, 0)),
                  pl.BlockSpec((None, nh, IDX_DIM), lambda b: (b, 0, 0)),
                  pl.BlockSpec((None, 1, IDX_DIM), lambda b: (b, 0, 0))],
        out_specs=(pl.BlockSpec((None, k_sel, LANES), lambda b: (b, 0, 0)),
                   pl.BlockSpec((None, 1, page), lambda b: (b, 0, 0))),
        scratch_shapes=[pltpu.VMEM((npages, page), F32)],
        compiler_params=_params(("parallel",)), name="at_sample_select")(
            scores, qi3, w3, ki_new.reshape(rows, 1, IDX_DIM))


def _at_gather_attend_kernel(pt_ref, slot_ref, off_ref, q_ref, ck_ref, cv_ref, pos_ref, newadd_ref, rbt_ref,
                             kn_ref, vn_ref, z_ref, o_ref, kg, vg, sem, *, past):
    b = pl.program_id(0)
    nsel, nh, d = kg.shape
    ncol = nsel * nh

    def row_copies(j):
        page = pt_ref[b, slot_ref[b, j]]
        off = off_ref[b, j]
        return (pltpu.make_async_copy(ck_ref.at[page, off], kg.at[j], sem.at[0]),
                pltpu.make_async_copy(cv_ref.at[page, off], vg.at[j], sem.at[1]))

    def start(j, carry):
        for prio, c in enumerate(row_copies(j)):
            c.start(priority=prio)
        return carry

    def wait(j, carry):
        for c in row_copies(j):
            c.wait()
        return carry

    lax.fori_loop(0, nsel, start, 0, unroll=SUBLANES)
    qs = q_ref[...].astype(BF16)
    pos = pos_ref[...]
    own = (lax.broadcasted_iota(jnp.int32, (nh, ncol), 1) & (nh - 1)) == lax.broadcasted_iota(jnp.int32, (nh, ncol), 0)
    keep = jnp.logical_and(own, pos >= 0)
    bias = _bias_from_dist(jnp.maximum(past - pos, 0), lambda bk: rbt_ref[:, bk:bk + 1])
    nadd = newadd_ref[:, 0:1]
    s_n = jnp.sum(qs.astype(F32) * kn_ref[...], axis=1, keepdims=True) + rbt_ref[:, 0:1] + nadd
    lax.fori_loop(0, nsel, wait, 0, unroll=SUBLANES)
    s = jnp.where(keep, _dot_nt(qs, kg[...].reshape(ncol, d).astype(BF16)) + bias, NEG)
    m = jnp.maximum(jnp.max(s, axis=1, keepdims=True), s_n)
    pr = jnp.where(keep, jnp.exp(s - m), 0.0)
    p_n = jnp.where(nadd < 0.0, 0.0, jnp.exp(s_n - m))
    l = jnp.sum(pr, axis=1, keepdims=True) + p_n
    o = _dot(pr.astype(BF16), vg[...].reshape(ncol, d).astype(BF16)) + p_n * vn_ref[...]
    o_ref[...] = o / l * _silu(z_ref[...])


def _at_gather_attend(q3, cache_k, cache_v, page_table, slot, off, pos_cols, newadd, rbt, kn3, vn3, z3):
    rows, nh, d = q3.shape
    assert nh & (nh - 1) == 0
    nsel = slot.shape[1]
    ncol = nsel * nh
    past = page_table.shape[1] * cache_k.shape[1]
    row3 = pl.BlockSpec((None, nh, d), lambda b, *_: (b, 0, 0))
    grid_spec = pltpu.PrefetchScalarGridSpec(
        num_scalar_prefetch=3, grid=(rows,),
        in_specs=[row3, pl.BlockSpec(memory_space=pl.ANY), pl.BlockSpec(memory_space=pl.ANY),
                  pl.BlockSpec((None, 1, ncol), lambda b, *_: (b, 0, 0)),
                  pl.BlockSpec((None, 1, newadd.shape[-1]), lambda b, *_: (b, 0, 0)),
                  pl.BlockSpec(rbt.shape, lambda b, *_: (0, 0)),
                  row3, row3, row3],
        out_specs=row3,
        scratch_shapes=[pltpu.VMEM((nsel, nh, d), F32), pltpu.VMEM((nsel, nh, d), F32), pltpu.SemaphoreType.DMA((2,))])
    kern = functools.partial(_at_gather_attend_kernel, past=past)
    return pl.pallas_call(
        kern, out_shape=jax.ShapeDtypeStruct((rows, nh, d), F32), grid_spec=grid_spec,
        compiler_params=_params(("arbitrary",)), name="at_gather_attend")(
            page_table, slot, off, q3, cache_k, cache_v, pos_cols, newadd, rbt, kn3, vn3, z3)


def _at_layer_sample(x, mods, cache_k, cache_v, cache_kidx, page_table, aw):
    g, scale, shift, gate = mods
    rows = x.shape[0]
    width, heads, nh = aw['width'], aw['heads'], aw['idx_heads']
    proj = _in_proj(x, g, scale, shift, aw['w_in'], batch_kind=False, rows_per_batch=1, tm=SUBLANES, tn=MM_TN,
                    name="at_in_proj")
    npool, page = cache_k.shape[:2]
    npages = page_table.shape[1]
    past = npages * page
    k_sel = min(TOPK_MAX, (past + 1) // 4)
    o = 4 * width
    qi3 = proj[:, o:o + nh * IDX_DIM].reshape(rows, nh, IDX_DIM)
    ki_new = proj[:, o + nh * IDX_DIM:o + nh * IDX_DIM + IDX_DIM]
    wi = proj[:, o + nh * IDX_DIM + IDX_DIM:o + nh * IDX_DIM + IDX_DIM + nh] * (IDX_DIM ** -0.5 * nh ** -0.5)
    w3 = jnp.broadcast_to(wi[:, :, None], (rows, nh, IDX_DIM))
    scores = _at_page_scores(qi3, w3, cache_kidx.astype(F32), page_table)
    rbt = aw['rel_bias'].T
    gidx, newadd = _at_sample_select(scores, qi3, w3, ki_new, k_sel)
    off = gidx[:, :, 0].astype(jnp.int32)
    slot = gidx[:, :, 1].astype(jnp.int32)
    pos = jnp.where(gidx[:, :, 2] > 0.5, slot * page + off, -1)
    pos_cols = jnp.repeat(pos, heads, axis=-1).reshape(rows, 1, k_sel * heads)
    r3 = lambda t: t.reshape(rows, heads, AT_DIM)
    a = _at_gather_attend(r3(proj[:, :width]), cache_k.astype(F32), cache_v.astype(F32), page_table, slot, off,
                          pos_cols, newadd, rbt, r3(proj[:, width:2 * width]), r3(proj[:, 2 * width:3 * width]),
                          r3(proj[:, 3 * width:4 * width]))
    x_new = _out_proj([('row', a.reshape(rows, width))], aw['w_out'], x, gate, batch_kind=False, rows_per_batch=1,
                      tm=SUBLANES, tn=MM_TN, name="at_out_proj", prologue=_cast_prologue)
    return (x_new,) + _at_rows(proj, aw, (rows, 1))


def kernel(x_prompt, x_sample, state_s5_re, state_s5_im, state_m2_conv, state_m2_ssm, state_gd_conv, state_gd_ssm, cache_k, cache_v, cache_kidx, page_table, c_prompt, c_sample, norm_g, w_mod, b_mod, final_g, s5_w_in, s5_lam_re, s5_lam_im, s5_log_dt, s5_b_re, s5_b_im, s5_c_re, s5_c_im, s5_d, s5_w_glu, s5_b_glu, s5_w_out, m2_w_in, m2_conv_w, m2_conv_b, m2_dt_bias, m2_a_log, m2_d, m2_norm, m2_w_out, gd_w_in, gd_conv_w, gd_a_log, gd_dt_bias, gd_norm, gd_w_out, at_w_in, rel_bias, at_w_out):
    f = F32
    bp, seq, d = x_prompt.shape
    bs = x_sample.shape[0]
    depth = norm_g.shape[0]
    xp = x_prompt.astype(f).reshape(bp * seq, d)
    xs = x_sample.astype(f).reshape(bs, d)

    pad_rows = (-(bs + bp)) % SUBLANES
    c_all = jnp.concatenate([c_sample.astype(f), c_prompt.astype(f), jnp.zeros((pad_rows, d), f)], axis=0)
    mod = _modulation(c_all, w_mod, b_mod)

    def mods(i, prompt):
        g = norm_g[i].astype(f).reshape(1, d)
        rows = mod[i, bs:bs + bp] if prompt else mod[i, :bs]
        shift, scale, gate = rows[:, :d], rows[:, d:2 * d], rows[:, 2 * d:]
        if prompt:
            return g, scale[:, None, :], shift[:, None, :], gate[:, None, :]
        return g, scale, shift, gate

    s5w = _s5_weights(s5_w_in, s5_lam_re, s5_lam_im, s5_log_dt, s5_b_re, s5_b_im, s5_c_re, s5_c_im, s5_d,
                      s5_w_glu, s5_b_glu, s5_w_out)
    tm_p = MM_TM

    xp, s5_re_p, s5_im_p = _s5_layer(xp, mods(0, True), None, s5w, prompt=True, batch=bp, seq=seq, tm=tm_p)
    xs, s5_re_s, s5_im_s = _s5_layer(xs, mods(0, False), (state_s5_re, state_s5_im), s5w, prompt=False,
                                     batch=bs, seq=1, tm=SUBLANES)
    groups, nstate = state_s5_re.shape[1:]
    s5_re_p, s5_im_p = s5_re_p.reshape(bp, groups, nstate), s5_im_p.reshape(bp, groups, nstate)
    s5_re_s, s5_im_s = s5_re_s.reshape(bs, groups, nstate), s5_im_s.reshape(bs, groups, nstate)

    m2w = _m2_weights(m2_w_in, m2_conv_w, m2_conv_b, m2_dt_bias, m2_a_log, m2_d, m2_norm, m2_w_out)
    xp, m2_conv_p, m2_ssm_p = _m2_layer(xp, mods(1, True), None, m2w, prompt=True, batch=bp, seq=seq, tm=tm_p)
    xs, m2_conv_s, m2_ssm_s = _m2_layer(xs, mods(1, False), (state_m2_conv, state_m2_ssm), m2w, prompt=False,
                                        batch=bs, seq=1, tm=SUBLANES)
    m2_ssm_p = m2_ssm_p.reshape((bp,) + state_m2_ssm.shape[1:])
    m2_ssm_s = m2_ssm_s.reshape(state_m2_ssm.shape)

    gdw = _gd_weights(gd_w_in, gd_conv_w, gd_a_log, gd_dt_bias, gd_norm, gd_w_out)
    xp, gd_conv_p, gd_ssm_p = _gd_layer(xp, mods(2, True), None, gdw, prompt=True, batch=bp, seq=seq, tm=tm_p)
    xs, gd_conv_s, gd_ssm_s = _gd_layer(xs, mods(2, False), (state_gd_conv, state_gd_ssm), gdw, prompt=False,
                                        batch=bs, seq=1, tm=SUBLANES)

    atw = _at_weights(at_w_in, rel_bias, at_w_out)
    yp, k_rows_p, v_rows_p, kidx_rows_p = _at_layer_prompt(xp, mods(3, True), atw, final_g.astype(f), batch=bp,
                                                           seq=seq, tm=tm_p)
    xs, k_rows_s, v_rows_s, kidx_rows_s = _at_layer_sample(xs, mods(3, False), cache_k, cache_v, cache_kidx,
                                                           page_table, atw)

    y_prompt = yp.reshape(x_prompt.shape).astype(x_prompt.dtype)
    y_sample = _final_norm(xs, final_g).reshape(x_sample.shape).astype(x_sample.dtype)
    return (y_prompt, y_sample, s5_re_p, s5_im_p, s5_re_s, s5_im_s, m2_conv_p, m2_ssm_p, m2_conv_s, m2_ssm_s,
            gd_conv_p, gd_ssm_p, gd_conv_s, gd_ssm_s,
            k_rows_p, v_rows_p, kidx_rows_p, k_rows_s, v_rows_s, kidx_rows_s)
```
